```python
import math
import jax, jax.numpy as jnp
from jax import lax
import numpy as np

D_MODEL = 1024
BATCH = 32
SEQ = 256
DEPTH = 2
DEC_BATCH = 2
DEC_SEQ = 2048
PAST_LEN = 512

GRID_W = 64
HEAD_DIM = 64
NA_HEADS = 6
NA_WIN_ROWS = 8
NA_WIN_COLS = 16
RET_HEADS = 6
RET_CHUNK = 128
DIFF_HEADS = 4
DIFF_D = 32
DIFF_DV = 64
NA_W = NA_HEADS * HEAD_DIM
RET_W = RET_HEADS * HEAD_DIM
DIFF_W = DIFF_HEADS * DIFF_DV
MIX_W = NA_W + RET_W + DIFF_W
IN_SPLITS = (NA_W, NA_W, NA_W, RET_W, RET_W, RET_W, RET_W, DIFF_W, DIFF_W, DIFF_W)
IN_COLS = sum(IN_SPLITS)
N_GROUPS = 4
EXPERTS_PER_GROUP = 8
N_EXPERTS = N_GROUPS * EXPERTS_PER_GROUP
TOP_K_IN_GROUP = 2
EXPERT_FF = 256
Q_BLOCK = 128
ROPE_BASE = 10000.0
LN_EPS = 1e-5
NEG_INF = -1e30
ALPHA = (2.0 * DEPTH) ** 0.25
BETA = (8.0 * DEPTH) ** -0.25

kernel_name = 'hybrid_na_retention_diffattn_hmoe_flow_step'


def layer_norm(x, w, b):
    xf = x.astype(jnp.float32)
    mu = jnp.mean(xf, axis=-1, keepdims=True)
    var = jnp.mean(jnp.square(xf - mu), axis=-1, keepdims=True)
    return ((xf - mu) * lax.rsqrt(var + LN_EPS)).astype(x.dtype) * w + b


def head_layer_norm(o):
    of = o.astype(jnp.float32)
    mu = jnp.mean(of, axis=-1, keepdims=True)
    var = jnp.mean(jnp.square(of - mu), axis=-1, keepdims=True)
    return ((of - mu) * lax.rsqrt(var + LN_EPS)).astype(o.dtype)


def head_rms_norm(o):
    of = o.astype(jnp.float32)
    return (of * lax.rsqrt(jnp.mean(jnp.square(of), axis=-1, keepdims=True) + LN_EPS)).astype(o.dtype)


def modulation(cond, w_mod, b_mod):
    m = jax.nn.silu(cond) @ w_mod + b_mod
    return jnp.split(m[:, None, :], 6, axis=-1)


def axial_rope(x, pos_row, pos_col):
    half = x.shape[-1] // 2

    def rot(xa, pos):
        n = xa.shape[-1] // 2
        freqs = ROPE_BASE ** (-jnp.arange(n, dtype=jnp.float32) / n)
        ang = pos.astype(jnp.float32)[:, None] * freqs[None, :]
        ang = ang.reshape((ang.shape[0],) + (1,) * (xa.ndim - 3) + (n,))
        cos = jnp.cos(ang).astype(xa.dtype)
        sin = jnp.sin(ang).astype(xa.dtype)
        x1, x2 = xa[..., :n], xa[..., n:]
        return jnp.concatenate([x1 * cos - x2 * sin, x1 * sin + x2 * cos], axis=-1)

    return jnp.concatenate([rot(x[..., :half], pos_row), rot(x[..., half:], pos_col)], axis=-1)


def project(h, w_in):
    B, T, _ = h.shape
    p = h @ w_in
    bounds = np.cumsum((0,) + IN_SPLITS)
    parts = [p[..., int(bounds[i]):int(bounds[i + 1])] for i in range(len(IN_SPLITS))]
    na_q, na_k, na_v = [t.reshape(B, T, NA_HEADS, HEAD_DIM) for t in parts[0:3]]
    ret_q, ret_k, ret_v = [t.reshape(B, T, RET_HEADS, HEAD_DIM) for t in parts[3:6]]
    ret_k = ret_k * (HEAD_DIM ** -0.5)
    ret_g = parts[6]
    diff_q, diff_k = [t.reshape(B, T, DIFF_HEADS, 2, DIFF_D) for t in parts[7:9]]
    diff_v = parts[9].reshape(B, T, DIFF_HEADS, DIFF_DV)
    return na_q, na_k, na_v, ret_q, ret_k, ret_v, ret_g, diff_q, diff_k, diff_v


def _query_blocks(q):
    B, T = q.shape[:2]
    return jnp.moveaxis(q.reshape((B, T // Q_BLOCK, Q_BLOCK) + q.shape[2:]), 1, 0)


def _merge_blocks(o):
    nb, B = o.shape[:2]
    return jnp.moveaxis(o, 0, 1).reshape((B, nb * Q_BLOCK) + o.shape[3:])


def dense_attn(q, k, v):
    scale = q.shape[-1] ** -0.5

    def block(qb):
        s = jnp.einsum('bqhd,bkhd->bhqk', qb, k).astype(jnp.float32) * scale
        p = jax.nn.softmax(s, axis=-1).astype(v.dtype)
        return jnp.einsum('bhqk,bkhd->bqhd', p, v)

    return _merge_blocks(lax.map(block, _query_blocks(q)))


def diff_attn(q, k, v, lam):
    scale = q.shape[-1] ** -0.5

    def block(qb):
        s = jnp.einsum('bqhid,bkhid->bihqk', qb, k).astype(jnp.float32) * scale
        p = jax.nn.softmax(s, axis=-1)
        a = (p[:, 0] - lam * p[:, 1]).astype(v.dtype)
        return jnp.einsum('bhqk,bkhv->bqhv', a, v)

    return _merge_blocks(lax.map(block, _query_blocks(q)))


def diff_lambda_value(lam_p, lam_init):
    lp = lam_p.astype(jnp.float32)
    return jnp.exp(jnp.sum(lp[0] * lp[1])) - jnp.exp(jnp.sum(lp[2] * lp[3])) + lam_init


def retention_dir(q, k, v, log_g, s0, strict):
    B, T, H, dk = q.shape
    dv = v.shape[-1]
    n = T // RET_CHUNK
    idx = jnp.arange(RET_CHUNK, dtype=jnp.float32)
    dist = idx[:, None] - idx[None, :]
    mask = (dist > 0) if strict else (dist >= 0)
    intra = jnp.where(mask[None], jnp.exp(jnp.where(mask, dist, 0.0)[None] * log_g[:, None, None]), 0.0).astype(q.dtype)
    q_dec = jnp.exp((idx[:, None] + 1.0) * log_g[None, :]).astype(q.dtype)
    k_dec = jnp.exp((RET_CHUNK - 1.0 - idx[:, None]) * log_g[None, :]).astype(q.dtype)
    c_dec = jnp.exp(RET_CHUNK * log_g).astype(q.dtype)

    def step(S, inp):
        qb, kb, vb = inp
        sc = jnp.einsum('bihd,bjhd->bhij', qb, kb) * intra[None]
        inner = jnp.einsum('bhij,bjhv->bihv', sc, vb)
        cross = jnp.einsum('bihd,bhdv->bihv', qb, S) * q_dec[None, :, :, None]
        S_new = S * c_dec[None, :, None, None] + jnp.einsum('bjhd,bjhv->bhdv', kb * k_dec[None, :, :, None], vb)
        return S_new, inner + cross

    def split(t):
        return jnp.moveaxis(t.reshape((B, n, RET_CHUNK) + t.shape[2:]), 1, 0)

    S_fin, out = lax.scan(step, s0, (split(q), split(k), split(v)))
    return jnp.moveaxis(out, 0, 1).reshape(B, T, H, dv), S_fin


def retention_bidir(q, k, v, decay_logit, s0_fwd, s0_bwd):
    log_g = jax.nn.log_sigmoid(decay_logit.astype(jnp.float32))
    o_f, s_f = retention_dir(q, k, v, log_g[0], s0_fwd, False)
    flip = lambda t: jnp.flip(t, axis=1)
    o_b, s_b = retention_dir(flip(q), flip(k), flip(v), log_g[1], s0_bwd, True)
    return o_f + flip(o_b), s_f, s_b


def na_latent(q, k, v, k_ctx, v_ctx, rel_bias):
    B, T, H, d = q.shape
    rows = T // GRID_W
    wr = min(NA_WIN_ROWS, rows)
    scale = d ** -0.5
    r = jnp.arange(rows)
    row_start = jnp.clip(r - wr // 2, 0, rows - wr)
    row_idx = row_start[:, None] + jnp.arange(wr)[None, :]
    cidx = jnp.arange(GRID_W)
    col_start = jnp.clip(cidx - NA_WIN_COLS // 2, 0, GRID_W - NA_WIN_COLS)
    col_ok = (cidx[None, :] >= col_start[:, None]) & (cidx[None, :] < col_start[:, None] + NA_WIN_COLS)
    qg = q.reshape(B, rows, GRID_W, H, d)
    kg = k.reshape(B, rows, GRID_W, H, d)[:, row_idx]
    vg = v.reshape(B, rows, GRID_W, H, d)[:, row_idx]
    s_win = jnp.einsum('brqhd,brwkhd->bhrqwk', qg, kg).astype(jnp.float32) * scale
    dr = row_idx - r[:, None] + (NA_WIN_ROWS - 1)
    dc = jnp.clip(cidx[None, :] - cidx[:, None] + (NA_WIN_COLS - 1), 0, 2 * NA_WIN_COLS - 2)
    bias = rel_bias[:, dr[:, None, :, None], dc[None, :, None, :]]
    s_win = jnp.where(col_ok[:, None, :], s_win + bias[None].astype(jnp.float32), NEG_INF)
    s_ctx = jnp.einsum('brqhd,bkhd->bhrqk', qg, k_ctx).astype(jnp.float32) * scale
    n_win = wr * GRID_W
    s = jnp.concatenate([s_win.reshape(B, H, rows, GRID_W, n_win), s_ctx], axis=-1)
    p = jax.nn.softmax(s, axis=-1).astype(v.dtype)
    p_win = p[..., :n_win].reshape(B, H, rows, GRID_W, wr, GRID_W)
    o = jnp.einsum('bhrqwk,brwkhd->brqhd', p_win, vg) + jnp.einsum('bhrqk,bkhd->brqhd', p[..., n_win:], v_ctx)
    return o.reshape(B, T, H * d)


def hier_moe(h, router_group, router_expert, w_gate, w_up, w_down):
    B, T, D = h.shape
    ht = h.reshape(B * T, D)
    g_logits = (ht @ router_group).astype(jnp.float32)
    g_prob = jax.nn.softmax(g_logits, axis=-1)
    g_sel = jnp.argmax(g_logits, axis=-1)
    g_w = jnp.take_along_axis(g_prob, g_sel[:, None], axis=-1)
    e_logits = (ht @ router_expert).astype(jnp.float32).reshape(-1, N_GROUPS, EXPERTS_PER_GROUP)
    e_in = jnp.take_along_axis(e_logits, g_sel[:, None, None], axis=1)[:, 0]
    top_v, top_i = lax.top_k(e_in, TOP_K_IN_GROUP)
    top_w = jax.nn.softmax(top_v, axis=-1) * g_w
    expert_id = g_sel[:, None] * EXPERTS_PER_GROUP + top_i
    gates = jnp.sum(jax.nn.one_hot(expert_id, N_EXPERTS, dtype=jnp.float32) * top_w[..., None], axis=1).astype(h.dtype)
    a = jnp.einsum('nd,edf->nef', ht, w_gate)
    u = jnp.einsum('nd,edf->nef', ht, w_up)
    y = jnp.einsum('nef,efd->nd', jax.nn.silu(a) * u * gates[:, :, None], w_down)
    return y.reshape(B, T, D)


def merge_mixers(na_o, ret_o, ret_g, diff_o, p, lam_init):
    B, T = na_o.shape[:2]
    ret_o = head_layer_norm(ret_o).reshape(B, T, RET_W) * p['ret_gn_w'] + p['ret_gn_b']
    ret_o = ret_o * jax.nn.silu(ret_g)
    diff_o = head_rms_norm(diff_o).reshape(B, T, DIFF_W) * p['diff_norm_w'] * (1.0 - lam_init)
    mix = jnp.concatenate([na_o.reshape(B, T, NA_W), ret_o, diff_o], axis=-1)
    return mix @ p['w_out']


def post_norm_residual(x, y, gate, w, b):
    return layer_norm(ALPHA * x + gate * y, w, b)


def channel_sublayer(x, shift, scale, gate, p):
    h = x * (1 + scale) + shift
    y = hier_moe(h, p['router_group'], p['router_expert'], p['exp_w_gate'], p['exp_w_up'], p['exp_w_down'])
    return post_norm_residual(x, y, gate, p['ln2_w'], p['ln2_b'])


def context_layer(x, c_ctx, p, l):
    B, T = x.shape[:2]
    shift1, scale1, gate1, shift2, scale2, gate2 = modulation(c_ctx[None, :], p['w_mod'], p['b_mod'])
    h = x * (1 + scale1) + shift1
    na_q, na_k, na_v, ret_q, ret_k, ret_v, ret_g, diff_q, diff_k, diff_v = project(h, p['w_in'])
    na_o = dense_attn(na_q, na_k, na_v)
    zero = jnp.zeros((B, RET_HEADS, HEAD_DIM, HEAD_DIM), ret_q.dtype)
    ret_o, s_fwd, s_bwd = retention_bidir(ret_q, ret_k, ret_v, p['ret_decay'], zero, zero)
    lam_init = 0.8 - 0.6 * math.exp(-0.3 * l)
    diff_o = diff_attn(diff_q, diff_k, diff_v, diff_lambda_value(p['diff_lambda'], lam_init))
    y = merge_mixers(na_o, ret_o, ret_g, diff_o, p, lam_init)
    x = post_norm_residual(x, y, gate1, p['ln1_w'], p['ln1_b'])
    x = channel_sublayer(x, shift2, scale2, gate2, p)
    ret_state = jnp.stack([s_fwd, s_bwd], axis=1)
    return x, na_k, na_v, ret_state, diff_k.reshape(B, T, DIFF_HEADS, 2 * DIFF_D), diff_v


def latent_layer(x, c, na_k_ctx, na_v_ctx, ret_state, diff_k_ctx, diff_v_ctx, p, l):
    B, T = x.shape[:2]
    shift1, scale1, gate1, shift2, scale2, gate2 = modulation(c, p['w_mod'], p['b_mod'])
    h = x * (1 + scale1) + shift1
    na_q, na_k, na_v, ret_q, ret_k, ret_v, ret_g, diff_q, diff_k, diff_v = project(h, p['w_in'])
    na_o = na_latent(na_q, na_k, na_v, na_k_ctx, na_v_ctx, p['na_rel_bias'])
    s0 = ret_state.astype(ret_q.dtype)
    ret_o, _, _ = retention_bidir(ret_q, ret_k, ret_v, p['ret_decay'], s0[:, 0], s0[:, 1])
    t = jnp.arange(T)
    row, col = t // GRID_W, t % GRID_W
    q_r = axial_rope(diff_q, row, col)
    k_r = axial_rope(diff_k, row, col)
    k_all = jnp.concatenate([diff_k_ctx.reshape(B, diff_k_ctx.shape[1], DIFF_HEADS, 2, DIFF_D), k_r], axis=1)
    v_all = jnp.concatenate([diff_v_ctx, diff_v], axis=1)
    lam_init = 0.8 - 0.6 * math.exp(-0.3 * l)
    diff_o = diff_attn(q_r, k_all, v_all, diff_lambda_value(p['diff_lambda'], lam_init))
    y = merge_mixers(na_o, ret_o, ret_g, diff_o, p, lam_init)
    x = post_norm_residual(x, y, gate1, p['ln1_w'], p['ln1_b'])
    return channel_sublayer(x, shift2, scale2, gate2, p)


def setup_inputs(seed: int = 0) -> dict:
    key = jax.random.key(seed)
    keys = jax.random.split(key, 28)

    def nrm(i, shape, s):
        return jax.random.normal(keys[i], shape, jnp.float32) * s

    D = D_MODEL
    ret_base = jnp.asarray(np.log(2.0 ** (5 + np.arange(RET_HEADS)) - 1.0), dtype=jnp.float32)
    return {
        'x_prompt': nrm(0, (BATCH, SEQ, D), 1.0),
        'x_sample': nrm(1, (DEC_BATCH, DEC_SEQ, D), 1.0),
        'c': nrm(2, (DEC_BATCH, D), 1.0),
        'cache_na_k': nrm(3, (DEC_BATCH, DEPTH, PAST_LEN, NA_HEADS, HEAD_DIM), 1.0),
        'cache_na_v': nrm(4, (DEC_BATCH, DEPTH, PAST_LEN, NA_HEADS, HEAD_DIM), 1.0),
        'state_ret': nrm(5, (DEC_BATCH, DEPTH, 2, RET_HEADS, HEAD_DIM, HEAD_DIM), 1.0),
        'cache_diff_k': nrm(6, (DEC_BATCH, DEPTH, PAST_LEN, DIFF_HEADS, 2 * DIFF_D), 1.0),
        'cache_diff_v': nrm(7, (DEC_BATCH, DEPTH, PAST_LEN, DIFF_HEADS, DIFF_DV), 1.0),
        'c_ctx': nrm(8, (D,), 1.0),
        'w_mod': nrm(9, (DEPTH, D, 6 * D), 0.5 * D ** -0.5),
        'b_mod': nrm(10, (DEPTH, 6 * D), 0.01),
        'w_in': nrm(11, (DEPTH, D, IN_COLS), D ** -0.5),
        'na_rel_bias': nrm(12, (DEPTH, NA_HEADS, 2 * NA_WIN_ROWS - 1, 2 * NA_WIN_COLS - 1), 0.1),
        'ret_decay': jnp.broadcast_to(ret_base, (DEPTH, 2, RET_HEADS)) + nrm(13, (DEPTH, 2, RET_HEADS), 0.1),
        'ret_gn_w': 1.0 + nrm(14, (DEPTH, RET_W), 0.05),
        'ret_gn_b': nrm(15, (DEPTH, RET_W), 0.01),
        'diff_lambda': nrm(16, (DEPTH, 4, DIFF_D), 0.1),
        'diff_norm_w': 1.0 + nrm(17, (DEPTH, DIFF_W), 0.05),
        'w_out': nrm(18, (DEPTH, MIX_W, D), BETA * MIX_W ** -0.5),
        'ln1_w': 1.0 + nrm(19, (DEPTH, D), 0.05),
        'ln1_b': nrm(20, (DEPTH, D), 0.01),
        'router_group': nrm(21, (DEPTH, D, N_GROUPS), D ** -0.5),
        'router_expert': nrm(22, (DEPTH, D, N_EXPERTS), D ** -0.5),
        'exp_w_gate': nrm(23, (DEPTH, N_EXPERTS, D, EXPERT_FF), D ** -0.5),
        'exp_w_up': nrm(24, (DEPTH, N_EXPERTS, D, EXPERT_FF), D ** -0.5),
        'exp_w_down': nrm(25, (DEPTH, N_EXPERTS, EXPERT_FF, D), BETA * EXPERT_FF ** -0.5),
        'ln2_w': 1.0 + nrm(26, (DEPTH, D), 0.05),
        'ln2_b': nrm(27, (DEPTH, D), 0.01),
    }


def reference(x_prompt, x_sample, c, cache_na_k, cache_na_v, state_ret, cache_diff_k, cache_diff_v, c_ctx,
              w_mod, b_mod, w_in, na_rel_bias, ret_decay, ret_gn_w, ret_gn_b, diff_lambda, diff_norm_w, w_out,
              ln1_w, ln1_b, router_group, router_expert, exp_w_gate, exp_w_up, exp_w_down, ln2_w, ln2_b):
    y_prompt = x_prompt
    y_sample = x_sample
    na_ks, na_vs, ret_states, diff_ks, diff_vs = [], [], [], [], []
    for l in range(DEPTH):
        p = dict(w_mod=w_mod[l], b_mod=b_mod[l], w_in=w_in[l], na_rel_bias=na_rel_bias[l],
                 ret_decay=ret_decay[l], ret_gn_w=ret_gn_w[l], ret_gn_b=ret_gn_b[l],
                 diff_lambda=diff_lambda[l], diff_norm_w=diff_norm_w[l], w_out=w_out[l],
                 ln1_w=ln1_w[l], ln1_b=ln1_b[l], router_group=router_group[l], router_expert=router_expert[l],
                 exp_w_gate=exp_w_gate[l], exp_w_up=exp_w_up[l], exp_w_down=exp_w_down[l],
                 ln2_w=ln2_w[l], ln2_b=ln2_b[l])
        y_prompt, nk, nv, rs, dk, dv = context_layer(y_prompt, c_ctx, p, l)
        na_ks.append(nk)
        na_vs.append(nv)
        ret_states.append(rs)
        diff_ks.append(dk)
        diff_vs.append(dv)
        y_sample = latent_layer(y_sample, c, cache_na_k[:, l], cache_na_v[:, l], state_ret[:, l],
                                cache_diff_k[:, l], cache_diff_v[:, l], p, l)
    new_na_k = jnp.stack(na_ks, axis=1)
    new_na_v = jnp.stack(na_vs, axis=1)
    new_ret_state = jnp.stack(ret_states, axis=1)
    new_diff_k = jnp.stack(diff_ks, axis=1)
    new_diff_v = jnp.stack(diff_vs, axis=1)
    return (y_prompt, y_sample, new_na_k, new_na_v, new_ret_state, new_diff_k, new_diff_v)
```

```python
import functools
import math

import numpy as np
import jax
import jax.numpy as jnp
from jax import lax
from jax.experimental import pallas as pl
from jax.experimental.pallas import tpu as pltpu

D_MODEL = 1024
BATCH = 32
SEQ = 256
DEPTH = 2
DEC_BATCH = 2
DEC_SEQ = 2048
PAST_LEN = 512
GRID_W = 64
HEAD_DIM = 64
NA_HEADS = 6
NA_WIN_ROWS = 8
NA_WIN_COLS = 16
RET_HEADS = 6
DIFF_HEADS = 4
DIFF_D = 32
DIFF_DV = 64
NA_W = NA_HEADS * HEAD_DIM
RET_W = RET_HEADS * HEAD_DIM
DIFF_W = DIFF_HEADS * DIFF_DV
MIX_W = NA_W + RET_W + DIFF_W
IN_COLS = 3 * NA_W + 4 * RET_W + 3 * DIFF_W
N_GROUPS = 4
EXPERTS_PER_GROUP = 8
N_EXPERTS = N_GROUPS * EXPERTS_PER_GROUP
EXPERT_FF = 256
ROPE_BASE = 10000.0
LN_EPS = 1e-5
NEG_INF = -1e30
ALPHA = (2.0 * DEPTH) ** 0.25

F32 = jnp.float32
BF16 = jnp.bfloat16

Q_NA, Q_RET, Q_GATE, Q_DIFF = 0, NA_W, NA_W + RET_W, NA_W + 2 * RET_W
Q_COLS = NA_W + 2 * RET_W + DIFF_W
KV_NAK, KV_NAV = 0, NA_W
KV_RETK, KV_RETV = 2 * NA_W, 2 * NA_W + RET_W
KV_DK, KV_DV = 2 * NA_W + 2 * RET_W, 2 * NA_W + 2 * RET_W + DIFF_W
KV_COLS = 2 * NA_W + 2 * RET_W + 2 * DIFF_W
MIX_RET, MIX_DIFF = NA_W, NA_W + RET_W

NA_Q_TILE = 256
NA_KEY_ROWS = 12
NA_KEYS = NA_KEY_ROWS * GRID_W
RET_KEY_CHUNK = 512
GRID_ROWS = DEC_SEQ // GRID_W

VMEM_LIMIT = 60 * 1024 * 1024


def _cparams(sem):
    return pltpu.CompilerParams(dimension_semantics=sem, vmem_limit_bytes=VMEM_LIMIT)


def _dot(a, b):
    return jnp.dot(a, b, preferred_element_type=F32)


def _dot_nt(a, b):
    return lax.dot_general(a, b, (((1,), (1,)), ((), ())), preferred_element_type=F32)


def _dot_tn(a, b):
    return lax.dot_general(a, b, (((0,), (0,)), ((), ())), preferred_element_type=F32)


def _silu(x):
    return x / (1.0 + jnp.exp(-x))


def _layer_norm_rows(z, w, b):
    mu = jnp.mean(z, axis=-1, keepdims=True)
    zc = z - mu
    var = jnp.mean(zc * zc, axis=-1, keepdims=True)
    return zc * lax.rsqrt(var + LN_EPS) * w + b


def _mod_kernel(cond_ref, w_ref, b_ref, o_ref):
    c = cond_ref[...]
    s = _silu(c)
    o_ref[...] = jnp.dot(s, w_ref[...], preferred_element_type=F32,
                         precision=lax.Precision.HIGHEST) + b_ref[...]


def _modulation(cond8, w_mod, b_mod):
    nj = 6
    out = pl.pallas_call(
        _mod_kernel,
        grid=(DEPTH, nj),
        in_specs=[
            pl.BlockSpec((8, D_MODEL), lambda l, j: (0, 0)),
            pl.BlockSpec((None, D_MODEL, D_MODEL), lambda l, j: (l, 0, j)),
            pl.BlockSpec((None, 1, D_MODEL), lambda l, j: (l, 0, j)),
        ],
        out_specs=pl.BlockSpec((None, 8, D_MODEL), lambda l, j: (l, 0, j)),
        out_shape=jax.ShapeDtypeStruct((DEPTH, 8, 6 * D_MODEL), F32),
        compiler_params=_cparams(("arbitrary", "arbitrary")),
        name="modulation",
    )(cond8, w_mod, b_mod.reshape(DEPTH, 1, 6 * D_MODEL))
    return out.reshape(DEPTH, 8, 6, D_MODEL)


def _proj_kernel(x_ref, mod_ref, w_ref, q_ref, kv_ref, *cache_refs):
    m = mod_ref[...]
    h = x_ref[...] * (1.0 + m[1:2]) + m[0:1]
    p = _dot(h.astype(BF16), w_ref[...])
    o = 0
    na_q = p[:, o:o + NA_W]; o += NA_W
    na_k = p[:, o:o + NA_W]; o += NA_W
    na_v = p[:, o:o + NA_W]; o += NA_W
    ret_q = p[:, o:o + RET_W]; o += RET_W
    ret_k = p[:, o:o + RET_W] * (HEAD_DIM ** -0.5); o += RET_W
    ret_v = p[:, o:o + RET_W]; o += RET_W
    ret_g = p[:, o:o + RET_W]; o += RET_W
    dq = p[:, o:o + DIFF_W]; o += DIFF_W
    dk = p[:, o:o + DIFF_W]; o += DIFF_W
    dv = p[:, o:o + DIFF_W]
    q_ref[:, Q_NA:Q_NA + NA_W] = na_q.astype(BF16)
    q_ref[:, Q_RET:Q_RET + RET_W] = ret_q.astype(BF16)
    q_ref[:, Q_GATE:Q_GATE + RET_W] = ret_g.astype(BF16)
    q_ref[:, Q_DIFF:Q_DIFF + DIFF_W] = dq.astype(BF16)
    kv_ref[:, KV_NAK:KV_NAK + NA_W] = na_k.astype(BF16)
    kv_ref[:, KV_NAV:KV_NAV + NA_W] = na_v.astype(BF16)
    kv_ref[:, KV_RETK:KV_RETK + RET_W] = ret_k.astype(BF16)
    kv_ref[:, KV_RETV:KV_RETV + RET_W] = ret_v.astype(BF16)
    kv_ref[:, KV_DK:KV_DK + DIFF_W] = dk.astype(BF16)
    kv_ref[:, KV_DV:KV_DV + DIFF_W] = dv.astype(BF16)
    if cache_refs:
        nak_ref, nav_ref, dk_ref, dv_ref = cache_refs
        nak_ref[...] = na_k
        nav_ref[...] = na_v
        dk_ref[...] = dk
        dv_ref[...] = dv


def _project(x2d, mod_l, w_in_bf, mod_row_fn, emit_caches, tm=512):
    n = x2d.shape[0]
    row = lambda i: (i, 0)
    out_shape = [jax.ShapeDtypeStruct((n, Q_COLS), BF16), jax.ShapeDtypeStruct((n, KV_COLS), BF16)]
    out_specs = [pl.BlockSpec((tm, Q_COLS), row), pl.BlockSpec((tm, KV_COLS), row)]
    if emit_caches:
        for w in (NA_W, NA_W, DIFF_W, DIFF_W):
            out_shape.append(jax.ShapeDtypeStruct((n, w), F32))
            out_specs.append(pl.BlockSpec((tm, w), row))
    return pl.pallas_call(
        _proj_kernel,
        grid=(n // tm,),
        in_specs=[
            pl.BlockSpec((tm, D_MODEL), row),
            pl.BlockSpec((None, 6, D_MODEL), lambda i: (mod_row_fn(i * tm), 0, 0)),
            pl.BlockSpec((D_MODEL, IN_COLS), lambda i: (0, 0)),
        ],
        out_specs=out_specs,
        out_shape=out_shape,
        compiler_params=_cparams(("arbitrary",)),
        name="proj_ctx" if emit_caches else "proj_lat",
    )(x2d, mod_l, w_in_bf)


def _softmax_pv(score_list, v_list):
    m = None
    for s in score_list:
        mi = jnp.max(s, axis=-1, keepdims=True)
        m = mi if m is None else jnp.maximum(m, mi)
    l = None
    o = None
    for s, v in zip(score_list, v_list):
        e = jnp.exp(s - m)
        li = jnp.sum(e, axis=-1, keepdims=True)
        oi = _dot(e.astype(BF16), v)
        l = li if l is None else l + li
        o = oi if o is None else o + oi
    return o / l


def _ret_finish(o, gate_bf, gnw, gnb):
    mu = jnp.mean(o, axis=-1, keepdims=True)
    oc = o - mu
    var = jnp.mean(oc * oc, axis=-1, keepdims=True)
    on = oc * lax.rsqrt(var + LN_EPS)
    return (on * gnw + gnb) * _silu(gate_bf.astype(F32))


def _diff_finish(o, dnw, one_minus_lam_init):
    ms = jnp.mean(o * o, axis=-1, keepdims=True)
    return o * lax.rsqrt(ms + LN_EPS) * dnw * one_minus_lam_init


def _out_proj_post_norm(mix_ref, wout_ref, x_ref, mod_ref, lnw_ref, lnb_ref, x1_ref):
    y = _dot(mix_ref[...].astype(BF16), wout_ref[...])
    m = mod_ref[...]
    z = ALPHA * x_ref[...] + m[2:3] * y
    x1_ref[...] = _layer_norm_rows(z, lnw_ref[...], lnb_ref[...])


def _ctx_mix_kernel(lg_ref, lam_ref, q_ref, kv_ref, x_ref, mod_ref, wout_ref, gnw_ref, gnb_ref,
                    dnw_ref, lnw_ref, lnb_ref, x1_ref, st_ref, mix_ref, dm_ref, *, one_minus_lam_init):
    T = SEQ

    @pl.when(pl.program_id(0) == 0)
    def _():
        i = lax.broadcasted_iota(jnp.int32, (T, T), 0).astype(F32)
        j = lax.broadcasted_iota(jnp.int32, (T, T), 1).astype(F32)
        d = i - j
        for h in range(RET_HEADS):
            dm_ref[h] = jnp.exp(jnp.where(d >= 0, d * lg_ref[0, h], (-d) * lg_ref[1, h]))

    for h in range(NA_HEADS):
        c = h * HEAD_DIM
        q = q_ref[:, Q_NA + c:Q_NA + c + HEAD_DIM]
        k = kv_ref[:, KV_NAK + c:KV_NAK + c + HEAD_DIM]
        v = kv_ref[:, KV_NAV + c:KV_NAV + c + HEAD_DIM]
        s = _dot_nt(q, k) * (HEAD_DIM ** -0.5)
        mix_ref[:, c:c + HEAD_DIM] = _softmax_pv([s], [v])

    jj = lax.broadcasted_iota(jnp.int32, (T, HEAD_DIM), 0).astype(F32)
    for h in range(RET_HEADS):
        c = h * HEAD_DIM
        q = q_ref[:, Q_RET + c:Q_RET + c + HEAD_DIM]
        g = q_ref[:, Q_GATE + c:Q_GATE + c + HEAD_DIM]
        k = kv_ref[:, KV_RETK + c:KV_RETK + c + HEAD_DIM]
        v = kv_ref[:, KV_RETV + c:KV_RETV + c + HEAD_DIM]
        sc = _dot_nt(q, k) * dm_ref[h]
        o = _dot(sc.astype(BF16), v)
        kf = k.astype(F32)
        k_fwd = (kf * jnp.exp((T - 1.0 - jj) * lg_ref[0, h])).astype(BF16)
        k_bwd = (kf * jnp.exp(jj * lg_ref[1, h])).astype(BF16)
        st_ref[0, h] = _dot_tn(k_fwd, v)
        st_ref[1, h] = _dot_tn(k_bwd, v)
        mix_ref[:, MIX_RET + c:MIX_RET + c + HEAD_DIM] = _ret_finish(
            o, g, gnw_ref[:, c:c + HEAD_DIM], gnb_ref[:, c:c + HEAD_DIM])

    lam = lam_ref[0]
    for h in range(DIFF_HEADS):
        c = h * DIFF_DV
        v = kv_ref[:, KV_DV + c:KV_DV + c + DIFF_DV]
        parts = []
        for i in range(2):
            q = q_ref[:, Q_DIFF + c + i * DIFF_D:Q_DIFF + c + (i + 1) * DIFF_D]
            k = kv_ref[:, KV_DK + c + i * DIFF_D:KV_DK + c + (i + 1) * DIFF_D]
            s = _dot_nt(q, k) * (DIFF_D ** -0.5)
            parts.append(_softmax_pv([s], [v]))
        o = parts[0] - lam * parts[1]
        mix_ref[:, MIX_DIFF + c:MIX_DIFF + c + DIFF_DV] = _diff_finish(
            o, dnw_ref[:, c:c + DIFF_DV], one_minus_lam_init)

    _out_proj_post_norm(mix_ref, wout_ref, x_ref, mod_ref, lnw_ref, lnb_ref, x1_ref)


def _ctx_mix(q_arr, kv_arr, x2d, mod_l, w_out_bf, log_g, lam, gnw, gnb, dnw, lnw, lnb, lam_init):
    row = lambda b: (b, 0)
    const2 = lambda b: (0, 0)
    smem = pl.BlockSpec(memory_space=pltpu.SMEM)
    return pl.pallas_call(
        functools.partial(_ctx_mix_kernel, one_minus_lam_init=1.0 - lam_init),
        grid=(BATCH,),
        in_specs=[
            smem, smem,
            pl.BlockSpec((SEQ, Q_COLS), row),
            pl.BlockSpec((SEQ, KV_COLS), row),
            pl.BlockSpec((SEQ, D_MODEL), row),
            pl.BlockSpec((None, 6, D_MODEL), lambda b: (0, 0, 0)),
            pl.BlockSpec((MIX_W, D_MODEL), const2),
            pl.BlockSpec((1, RET_W), const2),
            pl.BlockSpec((1, RET_W), const2),
            pl.BlockSpec((1, DIFF_W), const2),
            pl.BlockSpec((1, D_MODEL), const2),
            pl.BlockSpec((1, D_MODEL), const2),
        ],
        out_specs=[
            pl.BlockSpec((SEQ, D_MODEL), row),
            pl.BlockSpec((None, 2, RET_HEADS, HEAD_DIM, HEAD_DIM), lambda b: (b, 0, 0, 0, 0)),
        ],
        out_shape=[
            jax.ShapeDtypeStruct((BATCH * SEQ, D_MODEL), F32),
            jax.ShapeDtypeStruct((BATCH, 2, RET_HEADS, HEAD_DIM, HEAD_DIM), F32),
        ],
        scratch_shapes=[
            pltpu.VMEM((SEQ, MIX_W), F32),
            pltpu.VMEM((RET_HEADS, SEQ, SEQ), F32),
        ],
        compiler_params=_cparams(("arbitrary",)),
        name="ctx_mix",
    )(log_g, lam, q_arr, kv_arr, x2d, mod_l, w_out_bf, gnw, gnb, dnw, lnw, lnb)


def _rope(x, cos, sin_signed):
    n, w = x.shape
    lane = lax.broadcasted_iota(jnp.int32, (n, w), 1)
    first = (lane % 16) < 8
    partner = jnp.where(first, pltpu.roll(x, w - 8, 1), pltpu.roll(x, 8, 1))
    return x * cos + partner * sin_signed


def _lat_mix_kernel(lg_ref, lam_ref, q_ref, kv_ref, x_ref, mod_ref, wout_ref, gnw_ref, gnb_ref,
                    dnw_ref, lnw_ref, lnb_ref, cnak_ref, cnav_ref, st0_ref, cdk_ref, cdv_ref,
                    bias_ref, cos_ref, sin_ref, x1_ref, mix_ref, kr_ref, *, one_minus_lam_init):
    TQ = NA_Q_TILE
    T = DEC_SEQ
    qt = pl.program_id(1)
    q0 = pl.multiple_of(qt * TQ, TQ)

    @pl.when(qt == 0)
    def _():
        kr = _rope(kv_ref[:, KV_DK:KV_DK + DIFF_W].astype(F32), cos_ref[...], sin_ref[...])
        kr_ref[...] = kr.astype(BF16)

    ks = jnp.clip(qt * (TQ // GRID_W) - NA_WIN_ROWS // 2, 0, GRID_ROWS - NA_KEY_ROWS)
    k0 = pl.multiple_of(ks * GRID_W, 256)
    for h in range(NA_HEADS):
        c = h * HEAD_DIM
        q = q_ref[:, Q_NA + c:Q_NA + c + HEAD_DIM]
        kw = kv_ref[pl.ds(k0, NA_KEYS), KV_NAK + c:KV_NAK + c + HEAD_DIM]
        vw = kv_ref[pl.ds(k0, NA_KEYS), KV_NAV + c:KV_NAV + c + HEAD_DIM]
        kc = cnak_ref[:, c:c + HEAD_DIM].astype(BF16)
        vc = cnav_ref[:, c:c + HEAD_DIM].astype(BF16)
        s_win = _dot_nt(q, kw) * (HEAD_DIM ** -0.5) + bias_ref[h]
        s_ctx = _dot_nt(q, kc) * (HEAD_DIM ** -0.5)
        mix_ref[:, c:c + HEAD_DIM] = _softmax_pv([s_win, s_ctx], [vw, vc])

    KC = RET_KEY_CHUNK
    d0 = (lax.broadcasted_iota(jnp.int32, (TQ, KC), 0) + q0
          - lax.broadcasted_iota(jnp.int32, (TQ, KC), 1)).astype(F32)
    ii = (lax.broadcasted_iota(jnp.int32, (TQ, HEAD_DIM), 0) + q0).astype(F32)
    for h in range(RET_HEADS):
        c = h * HEAD_DIM
        lf = lg_ref[0, h]
        lb = lg_ref[1, h]
        q = q_ref[:, Q_RET + c:Q_RET + c + HEAD_DIM]
        g = q_ref[:, Q_GATE + c:Q_GATE + c + HEAD_DIM]
        o = None
        for kc in range(T // KC):
            k = kv_ref[kc * KC:(kc + 1) * KC, KV_RETK + c:KV_RETK + c + HEAD_DIM]
            v = kv_ref[kc * KC:(kc + 1) * KC, KV_RETV + c:KV_RETV + c + HEAD_DIM]
            d = d0 - float(kc * KC)
            decay = jnp.exp(jnp.where(d >= 0, d * lf, (-d) * lb))
            sc = _dot_nt(q, k) * decay
            oc = _dot(sc.astype(BF16), v)
            o = oc if o is None else o + oc
        qf = q.astype(F32)
        q_fwd = (qf * jnp.exp((ii + 1.0) * lf)).astype(BF16)
        q_bwd = (qf * jnp.exp((T - ii) * lb)).astype(BF16)
        o = o + _dot(q_fwd, st0_ref[0, h].astype(BF16)) + _dot(q_bwd, st0_ref[1, h].astype(BF16))
        mix_ref[:, MIX_RET + c:MIX_RET + c + HEAD_DIM] = _ret_finish(
            o, g, gnw_ref[:, c:c + HEAD_DIM], gnb_ref[:, c:c + HEAD_DIM])

    lam = lam_ref[0]
    qr = _rope(q_ref[:, Q_DIFF:Q_DIFF + DIFF_W].astype(F32),
               cos_ref[pl.ds(q0, TQ), :], sin_ref[pl.ds(q0, TQ), :]).astype(BF16)
    for h in range(DIFF_HEADS):
        c = h * DIFF_DV
        vl = kv_ref[:, KV_DV + c:KV_DV + c + DIFF_DV]
        vc = cdv_ref[:, c:c + DIFF_DV].astype(BF16)
        parts = []
        for i in range(2):
            lo = c + i * DIFF_D
            q = qr[:, lo:lo + DIFF_D]
            kl = kr_ref[:, lo:lo + DIFF_D]
            kc = cdk_ref[:, lo:lo + DIFF_D].astype(BF16)
            s_c = _dot_nt(q, kc) * (DIFF_D ** -0.5)
            s_l = _dot_nt(q, kl) * (DIFF_D ** -0.5)
            parts.append(_softmax_pv([s_c, s_l], [vc, vl]))
        o = parts[0] - lam * parts[1]
        mix_ref[:, MIX_DIFF + c:MIX_DIFF + c + DIFF_DV] = _diff_finish(
            o, dnw_ref[:, c:c + DIFF_DV], one_minus_lam_init)

    _out_proj_post_norm(mix_ref, wout_ref, x_ref, mod_ref, lnw_ref, lnb_ref, x1_ref)


def _lat_mix(q_arr, kv_arr, x2d, mod_l, w_out_bf, log_g, lam, gnw, gnb, dnw, lnw, lnb,
             cache_na_k, cache_na_v, state_ret, cache_diff_k, cache_diff_v, bias_tab, cos_tab,
             sin_tab, layer, lam_init):
    nq = DEC_SEQ // NA_Q_TILE
    const2 = lambda b, t: (0, 0)
    smem = pl.BlockSpec(memory_space=pltpu.SMEM)
    qrow = lambda b, t: (b * nq + t, 0)

    def variant(b, t):
        return (jnp.where(t == 0, 0, jnp.where(t == nq - 1, 2, 1)), 0, 0, 0)

    return pl.pallas_call(
        functools.partial(_lat_mix_kernel, one_minus_lam_init=1.0 - lam_init),
        grid=(DEC_BATCH, nq),
        in_specs=[
            smem, smem,
            pl.BlockSpec((NA_Q_TILE, Q_COLS), qrow),
            pl.BlockSpec((DEC_SEQ, KV_COLS), lambda b, t: (b, 0), pipeline_mode=pl.Buffered(1)),
            pl.BlockSpec((NA_Q_TILE, D_MODEL), qrow),
            pl.BlockSpec((None, 6, D_MODEL), lambda b, t: (b + 1, 0, 0)),
            pl.BlockSpec((MIX_W, D_MODEL), const2),
            pl.BlockSpec((1, RET_W), const2),
            pl.BlockSpec((1, RET_W), const2),
            pl.BlockSpec((1, DIFF_W), const2),
            pl.BlockSpec((1, D_MODEL), const2),
            pl.BlockSpec((1, D_MODEL), const2),
            pl.BlockSpec((None, None, PAST_LEN, NA_W), lambda b, t: (b, layer, 0, 0)),
            pl.BlockSpec((None, None, PAST_LEN, NA_W), lambda b, t: (b, layer, 0, 0)),
            pl.BlockSpec((None, None, 2, RET_HEADS, HEAD_DIM, HEAD_DIM),
                         lambda b, t: (b, layer, 0, 0, 0, 0)),
            pl.BlockSpec((None, None, PAST_LEN, DIFF_W), lambda b, t: (b, layer, 0, 0)),
            pl.BlockSpec((None, None, PAST_LEN, DIFF_W), lambda b, t: (b, layer, 0, 0)),
            pl.BlockSpec((None, NA_HEADS, NA_Q_TILE, NA_KEYS), variant, pipeline_mode=pl.Buffered(1)),
            pl.BlockSpec((DEC_SEQ, DIFF_W), const2),
            pl.BlockSpec((DEC_SEQ, DIFF_W), const2),
        ],
        out_specs=pl.BlockSpec((NA_Q_TILE, D_MODEL), qrow),
        out_shape=jax.ShapeDtypeStruct((DEC_BATCH * DEC_SEQ, D_MODEL), F32),
        scratch_shapes=[
            pltpu.VMEM((NA_Q_TILE, MIX_W), F32),
            pltpu.VMEM((DEC_SEQ, DIFF_W), BF16),
        ],
        compiler_params=_cparams(("arbitrary", "arbitrary")),
        name="lat_mix",
    )(log_g, lam, q_arr, kv_arr, x2d, mod_l, w_out_bf, gnw, gnb, dnw, lnw, lnb,
      cache_na_k, cache_na_v, state_ret, cache_diff_k, cache_diff_v, bias_tab, cos_tab, sin_tab)


MOE_TM = 1024
MOE_EPS = 4


def _route_transposed(lt):
    shape = lt.shape
    r = lax.broadcasted_iota(jnp.int32, shape, 0).astype(F32)
    ninf = -jnp.inf
    is_g = jnp.where(r >= N_EXPERTS, jnp.where(r < N_EXPERTS + N_GROUPS, 1.0, 0.0), 0.0) > 0.5
    gl = jnp.where(is_g, lt, ninf)
    gmax = jnp.max(gl, axis=0, keepdims=True)
    gsel = jnp.min(jnp.where(gl == gmax, r - N_EXPERTS, 1e9), axis=0, keepdims=True)
    gsum = jnp.sum(jnp.where(is_g, jnp.exp(gl - gmax), 0.0), axis=0, keepdims=True)
    gw = 1.0 / gsum
    lo = gsel * EXPERTS_PER_GROUP
    is_e = jnp.where(r >= lo, jnp.where(r < lo + EXPERTS_PER_GROUP, 1.0, 0.0), 0.0) > 0.5
    el = jnp.where(is_e, lt, ninf)
    v1 = jnp.max(el, axis=0, keepdims=True)
    i1 = jnp.min(jnp.where(el == v1, r, 1e9), axis=0, keepdims=True)
    el2 = jnp.where(r == i1, ninf, el)
    v2 = jnp.max(el2, axis=0, keepdims=True)
    i2 = jnp.min(jnp.where(el2 == v2, r, 1e9), axis=0, keepdims=True)
    t = jnp.exp(v2 - v1)
    w1 = gw / (1.0 + t)
    w2 = gw * t / (1.0 + t)
    return jnp.where(r == i1, w1, 0.0) + jnp.where(r == i2, w2, 0.0)


def _moe_kernel(x_ref, mod_ref, wr_ref, wg_ref, wu_ref, wd_ref, lnw_ref, lnb_ref, out_ref,
                h_ref, gates_ref, acc_ref):
    j = pl.program_id(1)
    tm = x_ref.shape[0]

    @pl.when(j == 0)
    def _():
        m = mod_ref[...]
        h = x_ref[...] * (1.0 + m[4:5]) + m[3:4]
        h_ref[...] = h.astype(BF16)
        lt = lax.dot_general(wr_ref[...], h, (((1,), (1,)), ((), ())),
                             preferred_element_type=F32, precision=lax.Precision.HIGHEST)
        gates_ref[...] = _route_transposed(lt).T
        acc_ref[...] = jnp.zeros_like(acc_ref)

    r = lax.broadcasted_iota(jnp.int32, (128, MOE_EPS * EXPERT_FF), 0)
    cexp = lax.broadcasted_iota(jnp.int32, (128, MOE_EPS * EXPERT_FF), 1) // EXPERT_FF
    sel = jnp.where(r == j * MOE_EPS + cexp, 1.0, 0.0).astype(F32)
    gexp = jnp.dot(gates_ref[...], sel, preferred_element_type=F32,
                   precision=lax.Precision.HIGHEST)
    hb = h_ref[...]
    acc = acc_ref[...]
    for e in range(MOE_EPS):
        a = _dot(hb, wg_ref[e])
        u = _dot(hb, wu_ref[e])
        hm = _silu(a) * u * gexp[:, e * EXPERT_FF:(e + 1) * EXPERT_FF]
        acc = acc + _dot(hm.astype(BF16), wd_ref[e])
    acc_ref[...] = acc

    @pl.when(j == pl.num_programs(1) - 1)
    def _():
        m = mod_ref[...]
        z = ALPHA * x_ref[...] + m[5:6] * acc_ref[...]
        out_ref[...] = _layer_norm_rows(z, lnw_ref[...], lnb_ref[...])


def _moe(x2d, mod_l, mod_row_fn, wr_t, wg_bf, wu_bf, wd_bf, lnw, lnb, name):
    n = x2d.shape[0]
    tm = MOE_TM
    row = lambda i, j: (i, 0)
    const2 = lambda i, j: (0, 0)
    return pl.pallas_call(
        _moe_kernel,
        grid=(n // tm, N_EXPERTS // MOE_EPS),
        in_specs=[
            pl.BlockSpec((tm, D_MODEL), row),
            pl.BlockSpec((None, 6, D_MODEL), lambda i, j: (mod_row_fn(i * tm), 0, 0)),
            pl.BlockSpec((128, D_MODEL), const2),
            pl.BlockSpec((MOE_EPS, D_MODEL, EXPERT_FF), lambda i, j: (j, 0, 0)),
            pl.BlockSpec((MOE_EPS, D_MODEL, EXPERT_FF), lambda i, j: (j, 0, 0)),
            pl.BlockSpec((MOE_EPS, EXPERT_FF, D_MODEL), lambda i, j: (j, 0, 0)),
            pl.BlockSpec((1, D_MODEL), const2),
            pl.BlockSpec((1, D_MODEL), const2),
        ],
        out_specs=pl.BlockSpec((tm, D_MODEL), row),
        out_shape=jax.ShapeDtypeStruct((n, D_MODEL), F32),
        scratch_shapes=[
            pltpu.VMEM((tm, D_MODEL), BF16),
            pltpu.VMEM((tm, 128), F32),
            pltpu.VMEM((tm, D_MODEL), F32),
        ],
        compiler_params=_cparams(("arbitrary", "arbitrary")),
        name=name,
    )(x2d, mod_l, wr_t, wg_bf, wu_bf, wd_bf, lnw, lnb)


def _na_bias_tables(rel_bias):
    q_rows = NA_Q_TILE // GRID_W
    qi = np.arange(NA_Q_TILE)
    ki = np.arange(NA_KEYS)
    tabs = []
    for r0, ks in ((0, 0), (8, 4), (GRID_ROWS - q_rows, GRID_ROWS - NA_KEY_ROWS)):
        q_row = r0 + qi // GRID_W
        q_col = qi % GRID_W
        k_row = ks + ki // GRID_W
        k_col = ki % GRID_W
        row_start = np.clip(q_row - NA_WIN_ROWS // 2, 0, GRID_ROWS - NA_WIN_ROWS)
        row_ok = (k_row[None, :] >= row_start[:, None]) & (k_row[None, :] < row_start[:, None] + NA_WIN_ROWS)
        col_start = np.clip(q_col - NA_WIN_COLS // 2, 0, GRID_W - NA_WIN_COLS)
        col_ok = (k_col[None, :] >= col_start[:, None]) & (k_col[None, :] < col_start[:, None] + NA_WIN_COLS)
        dr = np.clip(k_row[None, :] - q_row[:, None] + (NA_WIN_ROWS - 1), 0, 2 * NA_WIN_ROWS - 2)
        dc = np.clip(k_col[None, :] - q_col[:, None] + (NA_WIN_COLS - 1), 0, 2 * NA_WIN_COLS - 2)
        b = rel_bias[:, dr, dc].astype(F32)
        tabs.append(jnp.where(jnp.asarray(row_ok & col_ok)[None], b, NEG_INF))
    return jnp.stack(tabs, axis=0)


def _rope_tables():
    n = DIFF_D // 4
    lane = np.arange(DIFF_W)
    d = lane % DIFF_D
    use_col = d >= DIFF_D // 2
    e = d % (DIFF_D // 2)
    f = e % n
    first = e < n
    t = jnp.arange(DEC_SEQ)
    pos = jnp.where(jnp.asarray(use_col)[None, :], (t % GRID_W)[:, None], (t // GRID_W)[:, None])
    freqs = ROPE_BASE ** (-jnp.arange(n, dtype=F32) / n)
    ang = pos.astype(F32) * freqs[jnp.asarray(f)][None, :]
    sign = jnp.asarray(np.where(first, -1.0, 1.0), dtype=F32)
    return jnp.cos(ang), jnp.sin(ang) * sign[None, :]


def kernel(x_prompt, x_sample, c, cache_na_k, cache_na_v, state_ret, cache_diff_k, cache_diff_v, c_ctx,
           w_mod, b_mod, w_in, na_rel_bias, ret_decay, ret_gn_w, ret_gn_b, diff_lambda, diff_norm_w,
           w_out, ln1_w, ln1_b, router_group, router_expert, exp_w_gate, exp_w_up, exp_w_down,
           ln2_w, ln2_b):
    n_ctx = BATCH * SEQ
    n_lat = DEC_BATCH * DEC_SEQ
    x_ctx = x_prompt.reshape(n_ctx, D_MODEL)
    x_lat = x_sample.reshape(n_lat, D_MODEL)

    cond8 = jnp.zeros((8, D_MODEL), F32).at[0].set(c_ctx).at[1:1 + DEC_BATCH].set(c)
    mod = _modulation(cond8, w_mod, b_mod)

    cache_na_k = cache_na_k.reshape(DEC_BATCH, DEPTH, PAST_LEN, NA_W)
    cache_na_v = cache_na_v.reshape(DEC_BATCH, DEPTH, PAST_LEN, NA_W)
    cache_diff_k = cache_diff_k.reshape(DEC_BATCH, DEPTH, PAST_LEN, DIFF_W)
    cache_diff_v = cache_diff_v.reshape(DEC_BATCH, DEPTH, PAST_LEN, DIFF_W)
    cos_tab, sin_tab = _rope_tables()

    ctx_row = lambda r: 0
    lat_row = lambda r: 1 + r // DEC_SEQ

    na_ks, na_vs, ret_states, diff_ks, diff_vs = [], [], [], [], []
    for l in range(DEPTH):
        lam_init = 0.8 - 0.6 * math.exp(-0.3 * l)
        lp = diff_lambda[l].astype(F32)
        lam = (jnp.exp(jnp.sum(lp[0] * lp[1])) - jnp.exp(jnp.sum(lp[2] * lp[3])) + lam_init).reshape(1)
        log_g = jax.nn.log_sigmoid(ret_decay[l].astype(F32))
        w_in_bf = w_in[l].astype(BF16)
        w_out_bf = w_out[l].astype(BF16)
        gnw = ret_gn_w[l].reshape(1, RET_W)
        gnb = ret_gn_b[l].reshape(1, RET_W)
        dnw = diff_norm_w[l].reshape(1, DIFF_W)
        l1w = ln1_w[l].reshape(1, D_MODEL)
        l1b = ln1_b[l].reshape(1, D_MODEL)
        l2w = ln2_w[l].reshape(1, D_MODEL)
        l2b = ln2_b[l].reshape(1, D_MODEL)
        wr_t = jnp.zeros((128, D_MODEL), F32)
        wr_t = wr_t.at[:N_EXPERTS].set(router_expert[l].T).at[N_EXPERTS:N_EXPERTS + N_GROUPS].set(router_group[l].T)
        wg_bf = exp_w_gate[l].astype(BF16)
        wu_bf = exp_w_up[l].astype(BF16)
        wd_bf = exp_w_down[l].astype(BF16)
        bias_tab = _na_bias_tables(na_rel_bias[l])
        mod_l = mod[l]

        q_c, kv_c, nak, nav, dk, dv = _project(x_ctx, mod_l, w_in_bf, ctx_row, True)
        x1_c, st = _ctx_mix(q_c, kv_c, x_ctx, mod_l, w_out_bf, log_g, lam, gnw, gnb, dnw, l1w, l1b, lam_init)
        x_ctx = _moe(x1_c, mod_l, ctx_row, wr_t, wg_bf, wu_bf, wd_bf, l2w, l2b, "moe_ctx")
        na_ks.append(nak.reshape(BATCH, SEQ, NA_HEADS, HEAD_DIM))
        na_vs.append(nav.reshape(BATCH, SEQ, NA_HEADS, HEAD_DIM))
        ret_states.append(st)
        diff_ks.append(dk.reshape(BATCH, SEQ, DIFF_HEADS, 2 * DIFF_D))
        diff_vs.append(dv.reshape(BATCH, SEQ, DIFF_HEADS, DIFF_DV))

        q_l, kv_l = _project(x_lat, mod_l, w_in_bf, lat_row, False)
        x1_l = _lat_mix(q_l, kv_l, x_lat, mod_l, w_out_bf, log_g, lam, gnw, gnb, dnw, l1w, l1b,
                        cache_na_k, cache_na_v, state_ret, cache_diff_k, cache_diff_v,
                        bias_tab, cos_tab, sin_tab, l, lam_init)
        x_lat = _moe(x1_l, mod_l, lat_row, wr_t, wg_bf, wu_bf, wd_bf, l2w, l2b, "moe_lat")

    return (x_ctx.reshape(BATCH, SEQ, D_MODEL), x_lat.reshape(DEC_BATCH, DEC_SEQ, D_MODEL),
            jnp.stack(na_ks, axis=1), jnp.stack(na_vs, axis=1), jnp.stack(ret_states, axis=1),
            jnp.stack(diff_ks, axis=1), jnp.stack(diff_vs, axis=1))
```

```python
import functools
import math

import numpy as np
import jax
import jax.numpy as jnp
from jax import lax
from jax.experimental import pallas as pl
from jax.experimental.pallas import tpu as pltpu

D_MODEL = 1024
BATCH = 32
SEQ = 256
DEPTH = 2
DEC_BATCH = 2
DEC_SEQ = 2048
PAST_LEN = 512
GRID_W = 64
HEAD_DIM = 64
NA_HEADS = 6
NA_WIN_ROWS = 8
NA_WIN_COLS = 16
RET_HEADS = 6
DIFF_HEADS = 4
DIFF_D = 32
DIFF_DV = 64
NA_W = NA_HEADS * HEAD_DIM
RET_W = RET_HEADS * HEAD_DIM
DIFF_W = DIFF_HEADS * DIFF_DV
MIX_W = NA_W + RET_W + DIFF_W
IN_COLS = 3 * NA_W + 4 * RET_W + 3 * DIFF_W
N_GROUPS = 4
EXPERTS_PER_GROUP = 8
N_EXPERTS = N_GROUPS * EXPERTS_PER_GROUP
EXPERT_FF = 256
ROPE_BASE = 10000.0
LN_EPS = 1e-5
NEG_INF = -1e30
ALPHA = (2.0 * DEPTH) ** 0.25

F32 = jnp.float32
BF16 = jnp.bfloat16

Q_NA, Q_RET, Q_GATE, Q_DIFF = 0, NA_W, NA_W + RET_W, NA_W + 2 * RET_W
Q_COLS = NA_W + 2 * RET_W + DIFF_W
KV_NAK, KV_NAV = 0, NA_W
KV_RETK, KV_RETV = 2 * NA_W, 2 * NA_W + RET_W
KV_DK, KV_DV = 2 * NA_W + 2 * RET_W, 2 * NA_W + 2 * RET_W + DIFF_W
KV_COLS = 2 * NA_W + 2 * RET_W + 2 * DIFF_W
MIX_RET, MIX_DIFF = NA_W, NA_W + RET_W

NA_Q_TILE = 256
NA_KEY_ROWS = 12
NA_KEYS = NA_KEY_ROWS * GRID_W
RET_KEY_CHUNK = 512
GRID_ROWS = DEC_SEQ // GRID_W

VMEM_LIMIT = 60 * 1024 * 1024


def _cparams(sem):
    return pltpu.CompilerParams(dimension_semantics=sem, vmem_limit_bytes=VMEM_LIMIT)


def _dot(a, b):
    return jnp.dot(a, b, preferred_element_type=F32)


def _dot_nt(a, b):
    return lax.dot_general(a, b, (((1,), (1,)), ((), ())), preferred_element_type=F32)


def _dot_tn(a, b):
    return lax.dot_general(a, b, (((0,), (0,)), ((), ())), preferred_element_type=F32)


def _silu(x):
    return x / (1.0 + jnp.exp(-x))


def _layer_norm_rows(z, w, b):
    mu = jnp.mean(z, axis=-1, keepdims=True)
    zc = z - mu
    var = jnp.mean(zc * zc, axis=-1, keepdims=True)
    return zc * lax.rsqrt(var + LN_EPS) * w + b


def _mod_kernel(cond_ref, w_ref, b_ref, o_ref):
    c = cond_ref[...]
    s = _silu(c)
    o_ref[...] = jnp.dot(s, w_ref[...], preferred_element_type=F32,
                         precision=lax.Precision.HIGHEST) + b_ref[...]


def _modulation(cond8, w_mod, b_mod):
    nj = 6
    out = pl.pallas_call(
        _mod_kernel,
        grid=(DEPTH, nj),
        in_specs=[
            pl.BlockSpec((8, D_MODEL), lambda l, j: (0, 0)),
            pl.BlockSpec((None, D_MODEL, D_MODEL), lambda l, j: (l, 0, j)),
            pl.BlockSpec((None, 1, D_MODEL), lambda l, j: (l, 0, j)),
        ],
        out_specs=pl.BlockSpec((None, 8, D_MODEL), lambda l, j: (l, 0, j)),
        out_shape=jax.ShapeDtypeStruct((DEPTH, 8, 6 * D_MODEL), F32),
        compiler_params=_cparams(("arbitrary", "arbitrary")),
        name="modulation",
    )(cond8, w_mod, b_mod.reshape(DEPTH, 1, 6 * D_MODEL))
    return out.reshape(DEPTH, 8, 6, D_MODEL)


def _proj_kernel(x_ref, mod_ref, w_ref, q_ref, kv_ref, *cache_refs):
    m = mod_ref[...]
    h = x_ref[...] * (1.0 + m[1:2]) + m[0:1]
    p = _dot(h.astype(BF16), w_ref[...])
    o = 0
    na_q = p[:, o:o + NA_W]; o += NA_W
    na_k = p[:, o:o + NA_W]; o += NA_W
    na_v = p[:, o:o + NA_W]; o += NA_W
    ret_q = p[:, o:o + RET_W]; o += RET_W
    ret_k = p[:, o:o + RET_W] * (HEAD_DIM ** -0.5); o += RET_W
    ret_v = p[:, o:o + RET_W]; o += RET_W
    ret_g = p[:, o:o + RET_W]; o += RET_W
    dq = p[:, o:o + DIFF_W]; o += DIFF_W
    dk = p[:, o:o + DIFF_W]; o += DIFF_W
    dv = p[:, o:o + DIFF_W]
    q_ref[:, Q_NA:Q_NA + NA_W] = na_q.astype(BF16)
    q_ref[:, Q_RET:Q_RET + RET_W] = ret_q.astype(BF16)
    q_ref[:, Q_GATE:Q_GATE + RET_W] = ret_g.astype(BF16)
    q_ref[:, Q_DIFF:Q_DIFF + DIFF_W] = dq.astype(BF16)
    kv_ref[:, KV_NAK:KV_NAK + NA_W] = na_k.astype(BF16)
    kv_ref[:, KV_NAV:KV_NAV + NA_W] = na_v.astype(BF16)
    kv_ref[:, KV_RETK:KV_RETK + RET_W] = ret_k.astype(BF16)
    kv_ref[:, KV_RETV:KV_RETV + RET_W] = ret_v.astype(BF16)
    kv_ref[:, KV_DK:KV_DK + DIFF_W] = dk.astype(BF16)
    kv_ref[:, KV_DV:KV_DV + DIFF_W] = dv.astype(BF16)
    if cache_refs:
        nak_ref, nav_ref, dk_ref, dv_ref = cache_refs
        nak_ref[...] = na_k
        nav_ref[...] = na_v
        dk_ref[...] = dk
        dv_ref[...] = dv


def _project(x2d, mod_l, w_in_bf, mod_row_fn, emit_caches, tm=512):
    n = x2d.shape[0]
    row = lambda i: (i, 0)
    out_shape = [jax.ShapeDtypeStruct((n, Q_COLS), BF16), jax.ShapeDtypeStruct((n, KV_COLS), BF16)]
    out_specs = [pl.BlockSpec((tm, Q_COLS), row), pl.BlockSpec((tm, KV_COLS), row)]
    if emit_caches:
        for w in (NA_W, NA_W, DIFF_W, DIFF_W):
            out_shape.append(jax.ShapeDtypeStruct((n, w), F32))
            out_specs.append(pl.BlockSpec((tm, w), row))
    return pl.pallas_call(
        _proj_kernel,
        grid=(n // tm,),
        in_specs=[
            pl.BlockSpec((tm, D_MODEL), row),
            pl.BlockSpec((None, 6, D_MODEL), lambda i: (mod_row_fn(i * tm), 0, 0)),
            pl.BlockSpec((D_MODEL, IN_COLS), lambda i: (0, 0)),
        ],
        out_specs=out_specs,
        out_shape=out_shape,
        compiler_params=_cparams(("arbitrary",)),
        name="proj_ctx" if emit_caches else "proj_lat",
    )(x2d, mod_l, w_in_bf)


def _softmax_pv(score_list, v_list):
    m = None
    for s in score_list:
        mi = jnp.max(s, axis=-1, keepdims=True)
        m = mi if m is None else jnp.maximum(m, mi)
    l = None
    o = None
    for s, v in zip(score_list, v_list):
        e = jnp.exp(s - m)
        li = jnp.sum(e, axis=-1, keepdims=True)
        oi = _dot(e.astype(BF16), v)
        l = li if l is None else l + li
        o = oi if o is None else o + oi
    return o / l


def _ret_finish(o, gate_bf, gnw, gnb):
    mu = jnp.mean(o, axis=-1, keepdims=True)
    oc = o - mu
    var = jnp.mean(oc * oc, axis=-1, keepdims=True)
    on = oc * lax.rsqrt(var + LN_EPS)
    return (on * gnw + gnb) * _silu(gate_bf.astype(F32))


def _diff_finish(o, dnw, one_minus_lam_init):
    ms = jnp.mean(o * o, axis=-1, keepdims=True)
    return o * lax.rsqrt(ms + LN_EPS) * dnw * one_minus_lam_init


def _out_proj_post_norm(mix_ref, wout_ref, x_ref, mod_ref, lnw_ref, lnb_ref, x1_ref):
    y = _dot(mix_ref[...].astype(BF16), wout_ref[...])
    m = mod_ref[...]
    z = ALPHA * x_ref[...] + m[2:3] * y
    x1_ref[...] = _layer_norm_rows(z, lnw_ref[...], lnb_ref[...])


def _ctx_mix_kernel(lg_ref, lam_ref, q_ref, kv_ref, x_ref, mod_ref, wout_ref, gnw_ref, gnb_ref,
                    dnw_ref, lnw_ref, lnb_ref, x1_ref, st_ref, mix_ref, dm_ref, *, one_minus_lam_init):
    T = SEQ

    @pl.when(pl.program_id(0) == 0)
    def _():
        i = lax.broadcasted_iota(jnp.int32, (T, T), 0).astype(F32)
        j = lax.broadcasted_iota(jnp.int32, (T, T), 1).astype(F32)
        d = i - j
        for h in range(RET_HEADS):
            dm_ref[h] = jnp.exp(jnp.where(d >= 0, d * lg_ref[0, h], (-d) * lg_ref[1, h]))

    for h in range(NA_HEADS):
        c = h * HEAD_DIM
        q = q_ref[:, Q_NA + c:Q_NA + c + HEAD_DIM]
        k = kv_ref[:, KV_NAK + c:KV_NAK + c + HEAD_DIM]
        v = kv_ref[:, KV_NAV + c:KV_NAV + c + HEAD_DIM]
        s = _dot_nt(q, k) * (HEAD_DIM ** -0.5)
        mix_ref[:, c:c + HEAD_DIM] = _softmax_pv([s], [v])

    jj = lax.broadcasted_iota(jnp.int32, (T, HEAD_DIM), 0).astype(F32)
    for h in range(RET_HEADS):
        c = h * HEAD_DIM
        q = q_ref[:, Q_RET + c:Q_RET + c + HEAD_DIM]
        g = q_ref[:, Q_GATE + c:Q_GATE + c + HEAD_DIM]
        k = kv_ref[:, KV_RETK + c:KV_RETK + c + HEAD_DIM]
        v = kv_ref[:, KV_RETV + c:KV_RETV + c + HEAD_DIM]
        sc = _dot_nt(q, k) * dm_ref[h]
        o = _dot(sc.astype(BF16), v)
        kf = k.astype(F32)
        k_fwd = (kf * jnp.exp((T - 1.0 - jj) * lg_ref[0, h])).astype(BF16)
        k_bwd = (kf * jnp.exp(jj * lg_ref[1, h])).astype(BF16)
        st_ref[0, h] = _dot_tn(k_fwd, v)
        st_ref[1, h] = _dot_tn(k_bwd, v)
        mix_ref[:, MIX_RET + c:MIX_RET + c + HEAD_DIM] = _ret_finish(
            o, g, gnw_ref[:, c:c + HEAD_DIM], gnb_ref[:, c:c + HEAD_DIM])

    lam = lam_ref[0]
    for h in range(DIFF_HEADS):
        c = h * DIFF_DV
        v = kv_ref[:, KV_DV + c:KV_DV + c + DIFF_DV]
        parts = []
        for i in range(2):
            q = q_ref[:, Q_DIFF + c + i * DIFF_D:Q_DIFF + c + (i + 1) * DIFF_D]
            k = kv_ref[:, KV_DK + c + i * DIFF_D:KV_DK + c + (i + 1) * DIFF_D]
            s = _dot_nt(q, k) * (DIFF_D ** -0.5)
            parts.append(_softmax_pv([s], [v]))
        o = parts[0] - lam * parts[1]
        mix_ref[:, MIX_DIFF + c:MIX_DIFF + c + DIFF_DV] = _diff_finish(
            o, dnw_ref[:, c:c + DIFF_DV], one_minus_lam_init)

    _out_proj_post_norm(mix_ref, wout_ref, x_ref, mod_ref, lnw_ref, lnb_ref, x1_ref)


def _ctx_mix(q_arr, kv_arr, x2d, mod_l, w_out_bf, log_g, lam, gnw, gnb, dnw, lnw, lnb, lam_init):
    row = lambda b: (b, 0)
    const2 = lambda b: (0, 0)
    smem = pl.BlockSpec(memory_space=pltpu.SMEM)
    return pl.pallas_call(
        functools.partial(_ctx_mix_kernel, one_minus_lam_init=1.0 - lam_init),
        grid=(BATCH,),
        in_specs=[
            smem, smem,
            pl.BlockSpec((SEQ, Q_COLS), row),
            pl.BlockSpec((SEQ, KV_COLS), row),
            pl.BlockSpec((SEQ, D_MODEL), row),
            pl.BlockSpec((None, 6, D_MODEL), lambda b: (0, 0, 0)),
            pl.BlockSpec((MIX_W, D_MODEL), const2),
            pl.BlockSpec((1, RET_W), const2),
            pl.BlockSpec((1, RET_W), const2),
            pl.BlockSpec((1, DIFF_W), const2),
            pl.BlockSpec((1, D_MODEL), const2),
            pl.BlockSpec((1, D_MODEL), const2),
        ],
        out_specs=[
            pl.BlockSpec((SEQ, D_MODEL), row),
            pl.BlockSpec((None, 2, RET_HEADS, HEAD_DIM, HEAD_DIM), lambda b: (b, 0, 0, 0, 0)),
        ],
        out_shape=[
            jax.ShapeDtypeStruct((BATCH * SEQ, D_MODEL), F32),
            jax.ShapeDtypeStruct((BATCH, 2, RET_HEADS, HEAD_DIM, HEAD_DIM), F32),
        ],
        scratch_shapes=[
            pltpu.VMEM((SEQ, MIX_W), F32),
            pltpu.VMEM((RET_HEADS, SEQ, SEQ), F32),
        ],
        compiler_params=_cparams(("arbitrary",)),
        name="ctx_mix",
    )(log_g, lam, q_arr, kv_arr, x2d, mod_l, w_out_bf, gnw, gnb, dnw, lnw, lnb)


def _rope(x, cos, sin_signed):
    n, w = x.shape
    lane = lax.broadcasted_iota(jnp.int32, (n, w), 1)
    first = (lane % 16) < 8
    partner = jnp.where(first, pltpu.roll(x, w - 8, 1), pltpu.roll(x, 8, 1))
    return x * cos + partner * sin_signed


def _lat_mix_kernel(lg_ref, lam_ref, q_ref, kv_ref, x_ref, mod_ref, wout_ref, gnw_ref, gnb_ref,
                    dnw_ref, lnw_ref, lnb_ref, cnak_ref, cnav_ref, st0_ref, cdk_ref, cdv_ref,
                    bias_ref, cos_ref, sin_ref, x1_ref, mix_ref, kr_ref, *, one_minus_lam_init):
    TQ = NA_Q_TILE
    T = DEC_SEQ
    qt = pl.program_id(1)
    q0 = pl.multiple_of(qt * TQ, TQ)

    @pl.when(qt == 0)
    def _():
        kr = _rope(kv_ref[:, KV_DK:KV_DK + DIFF_W].astype(F32), cos_ref[...], sin_ref[...])
        kr_ref[...] = kr.astype(BF16)

    ks = jnp.clip(qt * (TQ // GRID_W) - NA_WIN_ROWS // 2, 0, GRID_ROWS - NA_KEY_ROWS)
    k0 = pl.multiple_of(ks * GRID_W, 256)
    for h in range(NA_HEADS):
        c = h * HEAD_DIM
        q = q_ref[:, Q_NA + c:Q_NA + c + HEAD_DIM]
        kw = kv_ref[pl.ds(k0, NA_KEYS), KV_NAK + c:KV_NAK + c + HEAD_DIM]
        vw = kv_ref[pl.ds(k0, NA_KEYS), KV_NAV + c:KV_NAV + c + HEAD_DIM]
        kc = cnak_ref[:, c:c + HEAD_DIM].astype(BF16)
        vc = cnav_ref[:, c:c + HEAD_DIM].astype(BF16)
        s_win = _dot_nt(q, kw) * (HEAD_DIM ** -0.5) + bias_ref[h]
        s_ctx = _dot_nt(q, kc) * (HEAD_DIM ** -0.5)
        mix_ref[:, c:c + HEAD_DIM] = _softmax_pv([s_win, s_ctx], [vw, vc])

    KC = RET_KEY_CHUNK
    d0 = (lax.broadcasted_iota(jnp.int32, (TQ, KC), 0) + q0
          - lax.broadcasted_iota(jnp.int32, (TQ, KC), 1)).astype(F32)
    ii = (lax.broadcasted_iota(jnp.int32, (TQ, HEAD_DIM), 0) + q0).astype(F32)
    for h in range(RET_HEADS):
        c = h * HEAD_DIM
        lf = lg_ref[0, h]
        lb = lg_ref[1, h]
        q = q_ref[:, Q_RET + c:Q_RET + c + HEAD_DIM]
        g = q_ref[:, Q_GATE + c:Q_GATE + c + HEAD_DIM]
        o = None
        for kc in range(T // KC):
            k = kv_ref[kc * KC:(kc + 1) * KC, KV_RETK + c:KV_RETK + c + HEAD_DIM]
            v = kv_ref[kc * KC:(kc + 1) * KC, KV_RETV + c:KV_RETV + c + HEAD_DIM]
            d = d0 - float(kc * KC)
            decay = jnp.exp(jnp.where(d >= 0, d * lf, (-d) * lb))
            sc = _dot_nt(q, k) * decay
            oc = _dot(sc.astype(BF16), v)
            o = oc if o is None else o + oc
        qf = q.astype(F32)
        q_fwd = (qf * jnp.exp((ii + 1.0) * lf)).astype(BF16)
        q_bwd = (qf * jnp.exp((T - ii) * lb)).astype(BF16)
        o = o + _dot(q_fwd, st0_ref[0, h].astype(BF16)) + _dot(q_bwd, st0_ref[1, h].astype(BF16))
        mix_ref[:, MIX_RET + c:MIX_RET + c + HEAD_DIM] = _ret_finish(
            o, g, gnw_ref[:, c:c + HEAD_DIM], gnb_ref[:, c:c + HEAD_DIM])

    lam = lam_ref[0]
    qr = _rope(q_ref[:, Q_DIFF:Q_DIFF + DIFF_W].astype(F32),
               cos_ref[pl.ds(q0, TQ), :], sin_ref[pl.ds(q0, TQ), :]).astype(BF16)
    for h in range(DIFF_HEADS):
        c = h * DIFF_DV
        vl = kv_ref[:, KV_DV + c:KV_DV + c + DIFF_DV]
        vc = cdv_ref[:, c:c + DIFF_DV].astype(BF16)
        parts = []
        for i in range(2):
            lo = c + i * DIFF_D
            q = qr[:, lo:lo + DIFF_D]
            kl = kr_ref[:, lo:lo + DIFF_D]
            kc = cdk_ref[:, lo:lo + DIFF_D].astype(BF16)
            s_c = _dot_nt(q, kc) * (DIFF_D ** -0.5)
            s_l = _dot_nt(q, kl) * (DIFF_D ** -0.5)
            parts.append(_softmax_pv([s_c, s_l], [vc, vl]))
        o = parts[0] - lam * parts[1]
        mix_ref[:, MIX_DIFF + c:MIX_DIFF + c + DIFF_DV] = _diff_finish(
            o, dnw_ref[:, c:c + DIFF_DV], one_minus_lam_init)

    _out_proj_post_norm(mix_ref, wout_ref, x_ref, mod_ref, lnw_ref, lnb_ref, x1_ref)


def _lat_mix(q_arr, kv_arr, x2d, mod_l, w_out_bf, log_g, lam, gnw, gnb, dnw, lnw, lnb,
             cache_na_k, cache_na_v, state_ret, cache_diff_k, cache_diff_v, bias_tab, cos_tab,
             sin_tab, layer, lam_init):
    nq = DEC_SEQ // NA_Q_TILE
    const2 = lambda b, t: (0, 0)
    smem = pl.BlockSpec(memory_space=pltpu.SMEM)
    qrow = lambda b, t: (b * nq + t, 0)

    def variant(b, t):
        return (jnp.where(t == 0, 0, jnp.where(t == nq - 1, 2, 1)), 0, 0, 0)

    return pl.pallas_call(
        functools.partial(_lat_mix_kernel, one_minus_lam_init=1.0 - lam_init),
        grid=(DEC_BATCH, nq),
        in_specs=[
            smem, smem,
            pl.BlockSpec((NA_Q_TILE, Q_COLS), qrow),
            pl.BlockSpec((DEC_SEQ, KV_COLS), lambda b, t: (b, 0), pipeline_mode=pl.Buffered(1)),
            pl.BlockSpec((NA_Q_TILE, D_MODEL), qrow),
            pl.BlockSpec((None, 6, D_MODEL), lambda b, t: (b + 1, 0, 0)),
            pl.BlockSpec((MIX_W, D_MODEL), const2),
            pl.BlockSpec((1, RET_W), const2),
            pl.BlockSpec((1, RET_W), const2),
            pl.BlockSpec((1, DIFF_W), const2),
            pl.BlockSpec((1, D_MODEL), const2),
            pl.BlockSpec((1, D_MODEL), const2),
            pl.BlockSpec((None, None, PAST_LEN, NA_W), lambda b, t: (b, layer, 0, 0)),
            pl.BlockSpec((None, None, PAST_LEN, NA_W), lambda b, t: (b, layer, 0, 0)),
            pl.BlockSpec((None, None, 2, RET_HEADS, HEAD_DIM, HEAD_DIM),
                         lambda b, t: (b, layer, 0, 0, 0, 0)),
            pl.BlockSpec((None, None, PAST_LEN, DIFF_W), lambda b, t: (b, layer, 0, 0)),
            pl.BlockSpec((None, None, PAST_LEN, DIFF_W), lambda b, t: (b, layer, 0, 0)),
            pl.BlockSpec((None, NA_HEADS, NA_Q_TILE, NA_KEYS), variant, pipeline_mode=pl.Buffered(1)),
            pl.BlockSpec((DEC_SEQ, DIFF_W), const2),
            pl.BlockSpec((DEC_SEQ, DIFF_W), const2),
        ],
        out_specs=pl.BlockSpec((NA_Q_TILE, D_MODEL), qrow),
        out_shape=jax.ShapeDtypeStruct((DEC_BATCH * DEC_SEQ, D_MODEL), F32),
        scratch_shapes=[
            pltpu.VMEM((NA_Q_TILE, MIX_W), F32),
            pltpu.VMEM((DEC_SEQ, DIFF_W), BF16),
        ],
        compiler_params=_cparams(("arbitrary", "arbitrary")),
        name="lat_mix",
    )(log_g, lam, q_arr, kv_arr, x2d, mod_l, w_out_bf, gnw, gnb, dnw, lnw, lnb,
      cache_na_k, cache_na_v, state_ret, cache_diff_k, cache_diff_v, bias_tab, cos_tab, sin_tab)


MOE_TM = 1024
MOE_EPS = 4


def _route_transposed(lt):
    shape = lt.shape
    r = lax.broadcasted_iota(jnp.int32, shape, 0).astype(F32)
    ninf = -jnp.inf
    is_g = jnp.where(r >= N_EXPERTS, jnp.where(r < N_EXPERTS + N_GROUPS, 1.0, 0.0), 0.0) > 0.5
    gl = jnp.where(is_g, lt, ninf)
    gmax = jnp.max(gl, axis=0, keepdims=True)
    gsel = jnp.min(jnp.where(gl == gmax, r - N_EXPERTS, 1e9), axis=0, keepdims=True)
    gsum = jnp.sum(jnp.where(is_g, jnp.exp(gl - gmax), 0.0), axis=0, keepdims=True)
    gw = 1.0 / gsum
    lo = gsel * EXPERTS_PER_GROUP
    is_e = jnp.where(r >= lo, jnp.where(r < lo + EXPERTS_PER_GROUP, 1.0, 0.0), 0.0) > 0.5
    el = jnp.where(is_e, lt, ninf)
    v1 = jnp.max(el, axis=0, keepdims=True)
    i1 = jnp.min(jnp.where(el == v1, r, 1e9), axis=0, keepdims=True)
    el2 = jnp.where(r == i1, ninf, el)
    v2 = jnp.max(el2, axis=0, keepdims=True)
    i2 = jnp.min(jnp.where(el2 == v2, r, 1e9), axis=0, keepdims=True)
    t = jnp.exp(v2 - v1)
    w1 = gw / (1.0 + t)
    w2 = gw * t / (1.0 + t)
    return jnp.where(r == i1, w1, 0.0) + jnp.where(r == i2, w2, 0.0)


def _moe_kernel(x_ref, mod_ref, wr_ref, wg_ref, wu_ref, wd_ref, lnw_ref, lnb_ref, out_ref,
                h_ref, gates_ref, acc_ref):
    j = pl.program_id(1)
    tm = x_ref.shape[0]

    @pl.when(j == 0)
    def _():
        m = mod_ref[...]
        h = x_ref[...] * (1.0 + m[4:5]) + m[3:4]
        h_ref[...] = h.astype(BF16)
        lt = lax.dot_general(wr_ref[...], h, (((1,), (1,)), ((), ())),
                             preferred_element_type=F32, precision=lax.Precision.HIGHEST)
        gates_ref[...] = _route_transposed(lt).T
        acc_ref[...] = jnp.zeros_like(acc_ref)

    r = lax.broadcasted_iota(jnp.int32, (128, MOE_EPS * EXPERT_FF), 0)
    cexp = lax.broadcasted_iota(jnp.int32, (128, MOE_EPS * EXPERT_FF), 1) // EXPERT_FF
    sel = jnp.where(r == j * MOE_EPS + cexp, 1.0, 0.0).astype(F32)
    gexp = jnp.dot(gates_ref[...], sel, preferred_element_type=F32,
                   precision=lax.Precision.HIGHEST)
    hb = h_ref[...]
    acc = acc_ref[...]
    for e in range(MOE_EPS):
        a = _dot(hb, wg_ref[e])
        u = _dot(hb, wu_ref[e])
        hm = _silu(a) * u * gexp[:, e * EXPERT_FF:(e + 1) * EXPERT_FF]
        acc = acc + _dot(hm.astype(BF16), wd_ref[e])
    acc_ref[...] = acc

    @pl.when(j == pl.num_programs(1) - 1)
    def _():
        m = mod_ref[...]
        z = ALPHA * x_ref[...] + m[5:6] * acc_ref[...]
        out_ref[...] = _layer_norm_rows(z, lnw_ref[...], lnb_ref[...])


def _moe(x2d, mod_l, mod_row_fn, wr_t, wg_bf, wu_bf, wd_bf, lnw, lnb, name):
    n = x2d.shape[0]
    tm = MOE_TM
    row = lambda i, j: (i, 0)
    const2 = lambda i, j: (0, 0)
    return pl.pallas_call(
        _moe_kernel,
        grid=(n // tm, N_EXPERTS // MOE_EPS),
        in_specs=[
            pl.BlockSpec((tm, D_MODEL), row),
            pl.BlockSpec((None, 6, D_MODEL), lambda i, j: (mod_row_fn(i * tm), 0, 0)),
            pl.BlockSpec((128, D_MODEL), const2),
            pl.BlockSpec((MOE_EPS, D_MODEL, EXPERT_FF), lambda i, j: (j, 0, 0)),
            pl.BlockSpec((MOE_EPS, D_MODEL, EXPERT_FF), lambda i, j: (j, 0, 0)),
            pl.BlockSpec((MOE_EPS, EXPERT_FF, D_MODEL), lambda i, j: (j, 0, 0)),
            pl.BlockSpec((1, D_MODEL), const2),
            pl.BlockSpec((1, D_MODEL), const2),
        ],
        out_specs=pl.BlockSpec((tm, D_MODEL), row),
        out_shape=jax.ShapeDtypeStruct((n, D_MODEL), F32),
        scratch_shapes=[
            pltpu.VMEM((tm, D_MODEL), BF16),
            pltpu.VMEM((tm, 128), F32),
            pltpu.VMEM((tm, D_MODEL), F32),
        ],
        compiler_params=_cparams(("arbitrary", "arbitrary")),
        name=name,
    )(x2d, mod_l, wr_t, wg_bf, wu_bf, wd_bf, lnw, lnb)


def _na_bias_tables(rel_bias):
    q_rows = NA_Q_TILE // GRID_W
    n_dr = 2 * NA_WIN_ROWS - 1
    pad_c = GRID_W - NA_WIN_COLS
    padded = jnp.pad(rel_bias.astype(F32), ((0, 0), (0, 0), (pad_c, pad_c)), mode="edge")
    col_tab = jnp.stack([padded[:, :, GRID_W - 1 - qc:2 * GRID_W - 1 - qc] for qc in range(GRID_W)], axis=2)
    pad_r = NA_KEY_ROWS - q_rows
    col_tab = jnp.pad(col_tab, ((0, 0), (pad_r, pad_r), (0, 0), (0, 0)))
    qi = np.arange(NA_Q_TILE)
    ki = np.arange(NA_KEYS)
    tabs = []
    for r0, ks in ((0, 0), (8, 4), (GRID_ROWS - q_rows, GRID_ROWS - NA_KEY_ROWS)):
        rows = []
        for a in range(q_rows):
            start = ks - r0 - a + (NA_WIN_ROWS - 1) + pad_r
            blk = col_tab[:, start:start + NA_KEY_ROWS]
            rows.append(blk.transpose(0, 2, 1, 3).reshape(NA_HEADS, GRID_W, NA_KEYS))
        b = jnp.concatenate(rows, axis=1)
        q_row = r0 + qi // GRID_W
        q_col = qi % GRID_W
        k_row = ks + ki // GRID_W
        k_col = ki % GRID_W
        row_start = np.clip(q_row - NA_WIN_ROWS // 2, 0, GRID_ROWS - NA_WIN_ROWS)
        row_ok = (k_row[None, :] >= row_start[:, None]) & (k_row[None, :] < row_start[:, None] + NA_WIN_ROWS)
        col_start = np.clip(q_col - NA_WIN_COLS // 2, 0, GRID_W - NA_WIN_COLS)
        col_ok = (k_col[None, :] >= col_start[:, None]) & (k_col[None, :] < col_start[:, None] + NA_WIN_COLS)
        dr = k_row[None, :] - q_row[:, None] + (NA_WIN_ROWS - 1)
        assert np.all((dr[row_ok] >= 0) & (dr[row_ok] < n_dr))
        tabs.append(jnp.where(jnp.asarray(row_ok & col_ok)[None], b, NEG_INF))
    return jnp.stack(tabs, axis=0)


def _rope_tables():
    n = DIFF_D // 4
    lane = np.arange(DIFF_W)
    d = lane % DIFF_D
    use_col = d >= DIFF_D // 2
    e = d % (DIFF_D // 2)
    f = e % n
    first = e < n
    t = jnp.arange(DEC_SEQ)
    pos = jnp.where(jnp.asarray(use_col)[None, :], (t % GRID_W)[:, None], (t // GRID_W)[:, None])
    freqs = ROPE_BASE ** (-jnp.arange(n, dtype=F32) / n)
    ang = pos.astype(F32) * freqs[jnp.asarray(f)][None, :]
    sign = jnp.asarray(np.where(first, -1.0, 1.0), dtype=F32)
    return jnp.cos(ang), jnp.sin(ang) * sign[None, :]


def kernel(x_prompt, x_sample, c, cache_na_k, cache_na_v, state_ret, cache_diff_k, cache_diff_v, c_ctx,
           w_mod, b_mod, w_in, na_rel_bias, ret_decay, ret_gn_w, ret_gn_b, diff_lambda, diff_norm_w,
           w_out, ln1_w, ln1_b, router_group, router_expert, exp_w_gate, exp_w_up, exp_w_down,
           ln2_w, ln2_b):
    n_ctx = BATCH * SEQ
    n_lat = DEC_BATCH * DEC_SEQ
    x_ctx = x_prompt.reshape(n_ctx, D_MODEL)
    x_lat = x_sample.reshape(n_lat, D_MODEL)

    cond8 = jnp.zeros((8, D_MODEL), F32).at[0].set(c_ctx).at[1:1 + DEC_BATCH].set(c)
    mod = _modulation(cond8, w_mod, b_mod)

    cache_na_k = cache_na_k.reshape(DEC_BATCH, DEPTH, PAST_LEN, NA_W)
    cache_na_v = cache_na_v.reshape(DEC_BATCH, DEPTH, PAST_LEN, NA_W)
    cache_diff_k = cache_diff_k.reshape(DEC_BATCH, DEPTH, PAST_LEN, DIFF_W)
    cache_diff_v = cache_diff_v.reshape(DEC_BATCH, DEPTH, PAST_LEN, DIFF_W)
    cos_tab, sin_tab = _rope_tables()

    ctx_row = lambda r: 0
    lat_row = lambda r: 1 + r // DEC_SEQ

    na_ks, na_vs, ret_states, diff_ks, diff_vs = [], [], [], [], []
    for l in range(DEPTH):
        lam_init = 0.8 - 0.6 * math.exp(-0.3 * l)
        lp = diff_lambda[l].astype(F32)
        lam = (jnp.exp(jnp.sum(lp[0] * lp[1])) - jnp.exp(jnp.sum(lp[2] * lp[3])) + lam_init).reshape(1)
        log_g = jax.nn.log_sigmoid(ret_decay[l].astype(F32))
        w_in_bf = w_in[l].astype(BF16)
        w_out_bf = w_out[l].astype(BF16)
        gnw = ret_gn_w[l].reshape(1, RET_W)
        gnb = ret_gn_b[l].reshape(1, RET_W)
        dnw = diff_norm_w[l].reshape(1, DIFF_W)
        l1w = ln1_w[l].reshape(1, D_MODEL)
        l1b = ln1_b[l].reshape(1, D_MODEL)
        l2w = ln2_w[l].reshape(1, D_MODEL)
        l2b = ln2_b[l].reshape(1, D_MODEL)
        wr_t = jnp.zeros((128, D_MODEL), F32)
        wr_t = wr_t.at[:N_EXPERTS].set(router_expert[l].T).at[N_EXPERTS:N_EXPERTS + N_GROUPS].set(router_group[l].T)
        wg_bf = exp_w_gate[l].astype(BF16)
        wu_bf = exp_w_up[l].astype(BF16)
        wd_bf = exp_w_down[l].astype(BF16)
        bias_tab = _na_bias_tables(na_rel_bias[l])
        mod_l = mod[l]

        q_c, kv_c, nak, nav, dk, dv = _project(x_ctx, mod_l, w_in_bf, ctx_row, True)
        x1_c, st = _ctx_mix(q_c, kv_c, x_ctx, mod_l, w_out_bf, log_g, lam, gnw, gnb, dnw, l1w, l1b, lam_init)
        x_ctx = _moe(x1_c, mod_l, ctx_row, wr_t, wg_bf, wu_bf, wd_bf, l2w, l2b, "moe_ctx")
        na_ks.append(nak.reshape(BATCH, SEQ, NA_HEADS, HEAD_DIM))
        na_vs.append(nav.reshape(BATCH, SEQ, NA_HEADS, HEAD_DIM))
        ret_states.append(st)
        diff_ks.append(dk.reshape(BATCH, SEQ, DIFF_HEADS, 2 * DIFF_D))
        diff_vs.append(dv.reshape(BATCH, SEQ, DIFF_HEADS, DIFF_DV))

        q_l, kv_l = _project(x_lat, mod_l, w_in_bf, lat_row, False)
        x1_l = _lat_mix(q_l, kv_l, x_lat, mod_l, w_out_bf, log_g, lam, gnw, gnb, dnw, l1w, l1b,
                        cache_na_k, cache_na_v, state_ret, cache_diff_k, cache_diff_v,
                        bias_tab, cos_tab, sin_tab, l, lam_init)
        x_lat = _moe(x1_l, mod_l, lat_row, wr_t, wg_bf, wu_bf, wd_bf, l2w, l2b, "moe_lat")

    return (x_ctx.reshape(BATCH, SEQ, D_MODEL), x_lat.reshape(DEC_BATCH, DEC_SEQ, D_MODEL),
            jnp.stack(na_ks, axis=1), jnp.stack(na_vs, axis=1), jnp.stack(ret_states, axis=1),
            jnp.stack(diff_ks, axis=1), jnp.stack(diff_vs, axis=1))
```

```python
import functools
import math

import numpy as np
import jax
import jax.numpy as jnp
from jax import lax
from jax.experimental import pallas as pl
from jax.experimental.pallas import tpu as pltpu

D_MODEL = 1024
BATCH = 32
SEQ = 256
DEPTH = 2
DEC_BATCH = 2
DEC_SEQ = 2048
PAST_LEN = 512
GRID_W = 64
HEAD_DIM = 64
NA_HEADS = 6
NA_WIN_ROWS = 8
NA_WIN_COLS = 16
RET_HEADS = 6
DIFF_HEADS = 4
DIFF_D = 32
DIFF_DV = 64
NA_W = NA_HEADS * HEAD_DIM
RET_W = RET_HEADS * HEAD_DIM
DIFF_W = DIFF_HEADS * DIFF_DV
MIX_W = NA_W + RET_W + DIFF_W
IN_COLS = 3 * NA_W + 4 * RET_W + 3 * DIFF_W
N_GROUPS = 4
EXPERTS_PER_GROUP = 8
N_EXPERTS = N_GROUPS * EXPERTS_PER_GROUP
EXPERT_FF = 256
ROPE_BASE = 10000.0
LN_EPS = 1e-5
NEG_INF = -1e30
ALPHA = (2.0 * DEPTH) ** 0.25

F32 = jnp.float32
BF16 = jnp.bfloat16

Q_NA, Q_RET, Q_GATE, Q_DIFF = 0, NA_W, NA_W + RET_W, NA_W + 2 * RET_W
Q_COLS = NA_W + 2 * RET_W + DIFF_W
KV_NAK, KV_NAV = 0, NA_W
KV_RETK, KV_RETV = 2 * NA_W, 2 * NA_W + RET_W
KV_DK, KV_DV = 2 * NA_W + 2 * RET_W, 2 * NA_W + 2 * RET_W + DIFF_W
KV_COLS = 2 * NA_W + 2 * RET_W + 2 * DIFF_W
MIX_RET, MIX_DIFF = NA_W, NA_W + RET_W

NA_Q_TILE = 256
NA_KEY_ROWS = 12
NA_KEYS = NA_KEY_ROWS * GRID_W
RET_KEY_CHUNK = 512
GRID_ROWS = DEC_SEQ // GRID_W

VMEM_LIMIT = 60 * 1024 * 1024


def _cparams(sem):
    return pltpu.CompilerParams(dimension_semantics=sem, vmem_limit_bytes=VMEM_LIMIT)


def _dot(a, b):
    return jnp.dot(a, b, preferred_element_type=F32)


def _dot_nt(a, b):
    return lax.dot_general(a, b, (((1,), (1,)), ((), ())), preferred_element_type=F32)


def _dot_tn(a, b):
    return lax.dot_general(a, b, (((0,), (0,)), ((), ())), preferred_element_type=F32)


def _silu(x):
    return x / (1.0 + jnp.exp(-x))


def _layer_norm_rows(z, w, b):
    mu = jnp.mean(z, axis=-1, keepdims=True)
    zc = z - mu
    var = jnp.mean(zc * zc, axis=-1, keepdims=True)
    return zc * lax.rsqrt(var + LN_EPS) * w + b


def _mod_kernel(cond_ref, w_ref, b_ref, o_ref):
    c = cond_ref[...]
    s = _silu(c)
    o_ref[...] = jnp.dot(s, w_ref[...], preferred_element_type=F32,
                         precision=lax.Precision.HIGHEST) + b_ref[...]


def _modulation(cond8, w_mod, b_mod):
    nj = 6
    out = pl.pallas_call(
        _mod_kernel,
        grid=(DEPTH, nj),
        in_specs=[
            pl.BlockSpec((8, D_MODEL), lambda l, j: (0, 0)),
            pl.BlockSpec((None, D_MODEL, D_MODEL), lambda l, j: (l, 0, j)),
            pl.BlockSpec((None, 1, D_MODEL), lambda l, j: (l, 0, j)),
        ],
        out_specs=pl.BlockSpec((None, 8, D_MODEL), lambda l, j: (l, 0, j)),
        out_shape=jax.ShapeDtypeStruct((DEPTH, 8, 6 * D_MODEL), F32),
        compiler_params=_cparams(("arbitrary", "arbitrary")),
        name="modulation",
    )(cond8, w_mod, b_mod.reshape(DEPTH, 1, 6 * D_MODEL))
    return out.reshape(DEPTH, 8, 6, D_MODEL)


def _proj_kernel(x_ref, mod_ref, w_ref, q_ref, kv_ref, *cache_refs):
    m = mod_ref[...]
    h = x_ref[...] * (1.0 + m[1:2]) + m[0:1]
    p = _dot(h.astype(BF16), w_ref[...])
    o = 0
    na_q = p[:, o:o + NA_W]; o += NA_W
    na_k = p[:, o:o + NA_W]; o += NA_W
    na_v = p[:, o:o + NA_W]; o += NA_W
    ret_q = p[:, o:o + RET_W]; o += RET_W
    ret_k = p[:, o:o + RET_W] * (HEAD_DIM ** -0.5); o += RET_W
    ret_v = p[:, o:o + RET_W]; o += RET_W
    ret_g = p[:, o:o + RET_W]; o += RET_W
    dq = p[:, o:o + DIFF_W]; o += DIFF_W
    dk = p[:, o:o + DIFF_W]; o += DIFF_W
    dv = p[:, o:o + DIFF_W]
    q_ref[:, Q_NA:Q_NA + NA_W] = na_q.astype(BF16)
    q_ref[:, Q_RET:Q_RET + RET_W] = ret_q.astype(BF16)
    q_ref[:, Q_GATE:Q_GATE + RET_W] = ret_g.astype(BF16)
    q_ref[:, Q_DIFF:Q_DIFF + DIFF_W] = dq.astype(BF16)
    kv_ref[:, KV_NAK:KV_NAK + NA_W] = na_k.astype(BF16)
    kv_ref[:, KV_NAV:KV_NAV + NA_W] = na_v.astype(BF16)
    kv_ref[:, KV_RETK:KV_RETK + RET_W] = ret_k.astype(BF16)
    kv_ref[:, KV_RETV:KV_RETV + RET_W] = ret_v.astype(BF16)
    kv_ref[:, KV_DK:KV_DK + DIFF_W] = dk.astype(BF16)
    kv_ref[:, KV_DV:KV_DV + DIFF_W] = dv.astype(BF16)
    if cache_refs:
        nak_ref, nav_ref, dk_ref, dv_ref = cache_refs
        nak_ref[...] = na_k
        nav_ref[...] = na_v
        dk_ref[...] = dk
        dv_ref[...] = dv


def _project(x2d, mod_l, w_in_bf, mod_row_fn, emit_caches, tm=512):
    n = x2d.shape[0]
    row = lambda i: (i, 0)
    out_shape = [jax.ShapeDtypeStruct((n, Q_COLS), BF16), jax.ShapeDtypeStruct((n, KV_COLS), BF16)]
    out_specs = [pl.BlockSpec((tm, Q_COLS), row), pl.BlockSpec((tm, KV_COLS), row)]
    if emit_caches:
        for w in (NA_W, NA_W, DIFF_W, DIFF_W):
            out_shape.append(jax.ShapeDtypeStruct((n, w), F32))
            out_specs.append(pl.BlockSpec((tm, w), row))
    return pl.pallas_call(
        _proj_kernel,
        grid=(n // tm,),
        in_specs=[
            pl.BlockSpec((tm, D_MODEL), row),
            pl.BlockSpec((None, 6, D_MODEL), lambda i: (mod_row_fn(i * tm), 0, 0)),
            pl.BlockSpec((D_MODEL, IN_COLS), lambda i: (0, 0)),
        ],
        out_specs=out_specs,
        out_shape=out_shape,
        compiler_params=_cparams(("arbitrary",)),
        name="proj_ctx" if emit_caches else "proj_lat",
    )(x2d, mod_l, w_in_bf)


def _softmax_pv(score_list, v_list):
    m = None
    for s in score_list:
        mi = jnp.max(s, axis=-1, keepdims=True)
        m = mi if m is None else jnp.maximum(m, mi)
    l = None
    o = None
    for s, v in zip(score_list, v_list):
        e = jnp.exp(s - m)
        li = jnp.sum(e, axis=-1, keepdims=True)
        oi = _dot(e.astype(BF16), v)
        l = li if l is None else l + li
        o = oi if o is None else o + oi
    return o / l


def _ret_finish(o, gate_bf, gnw, gnb):
    mu = jnp.mean(o, axis=-1, keepdims=True)
    oc = o - mu
    var = jnp.mean(oc * oc, axis=-1, keepdims=True)
    on = oc * lax.rsqrt(var + LN_EPS)
    return (on * gnw + gnb) * _silu(gate_bf.astype(F32))


def _diff_finish(o, dnw, one_minus_lam_init):
    ms = jnp.mean(o * o, axis=-1, keepdims=True)
    return o * lax.rsqrt(ms + LN_EPS) * dnw * one_minus_lam_init


def _out_proj_post_norm(mix_ref, wout_ref, x_ref, mod_ref, lnw_ref, lnb_ref, x1_ref):
    y = _dot(mix_ref[...].astype(BF16), wout_ref[...])
    m = mod_ref[...]
    z = ALPHA * x_ref[...] + m[2:3] * y
    x1_ref[...] = _layer_norm_rows(z, lnw_ref[...], lnb_ref[...])


def _ctx_mix_kernel(lg_ref, lam_ref, q_ref, kv_ref, x_ref, mod_ref, wout_ref, gnw_ref, gnb_ref,
                    dnw_ref, lnw_ref, lnb_ref, x1_ref, st_ref, mix_ref, dm_ref, *, one_minus_lam_init):
    T = SEQ

    @pl.when(pl.program_id(0) == 0)
    def _():
        i = lax.broadcasted_iota(jnp.int32, (T, T), 0).astype(F32)
        j = lax.broadcasted_iota(jnp.int32, (T, T), 1).astype(F32)
        d = i - j
        for h in range(RET_HEADS):
            dm_ref[h] = jnp.exp(jnp.where(d >= 0, d * lg_ref[0, h], (-d) * lg_ref[1, h]))

    for h in range(NA_HEADS):
        c = h * HEAD_DIM
        q = q_ref[:, Q_NA + c:Q_NA + c + HEAD_DIM]
        k = kv_ref[:, KV_NAK + c:KV_NAK + c + HEAD_DIM]
        v = kv_ref[:, KV_NAV + c:KV_NAV + c + HEAD_DIM]
        s = _dot_nt(q, k) * (HEAD_DIM ** -0.5)
        mix_ref[:, c:c + HEAD_DIM] = _softmax_pv([s], [v])

    jj = lax.broadcasted_iota(jnp.int32, (T, HEAD_DIM), 0).astype(F32)
    for h in range(RET_HEADS):
        c = h * HEAD_DIM
        q = q_ref[:, Q_RET + c:Q_RET + c + HEAD_DIM]
        g = q_ref[:, Q_GATE + c:Q_GATE + c + HEAD_DIM]
        k = kv_ref[:, KV_RETK + c:KV_RETK + c + HEAD_DIM]
        v = kv_ref[:, KV_RETV + c:KV_RETV + c + HEAD_DIM]
        sc = _dot_nt(q, k) * dm_ref[h]
        o = _dot(sc.astype(BF16), v)
        kf = k.astype(F32)
        k_fwd = (kf * jnp.exp((T - 1.0 - jj) * lg_ref[0, h])).astype(BF16)
        k_bwd = (kf * jnp.exp(jj * lg_ref[1, h])).astype(BF16)
        st_ref[0, h] = _dot_tn(k_fwd, v)
        st_ref[1, h] = _dot_tn(k_bwd, v)
        mix_ref[:, MIX_RET + c:MIX_RET + c + HEAD_DIM] = _ret_finish(
            o, g, gnw_ref[:, c:c + HEAD_DIM], gnb_ref[:, c:c + HEAD_DIM])

    lam = lam_ref[0]
    for h in range(DIFF_HEADS):
        c = h * DIFF_DV
        v = kv_ref[:, KV_DV + c:KV_DV + c + DIFF_DV]
        parts = []
        for i in range(2):
            q = q_ref[:, Q_DIFF + c + i * DIFF_D:Q_DIFF + c + (i + 1) * DIFF_D]
            k = kv_ref[:, KV_DK + c + i * DIFF_D:KV_DK + c + (i + 1) * DIFF_D]
            s = _dot_nt(q, k) * (DIFF_D ** -0.5)
            parts.append(_softmax_pv([s], [v]))
        o = parts[0] - lam * parts[1]
        mix_ref[:, MIX_DIFF + c:MIX_DIFF + c + DIFF_DV] = _diff_finish(
            o, dnw_ref[:, c:c + DIFF_DV], one_minus_lam_init)

    _out_proj_post_norm(mix_ref, wout_ref, x_ref, mod_ref, lnw_ref, lnb_ref, x1_ref)


def _ctx_mix(q_arr, kv_arr, x2d, mod_l, w_out_bf, log_g, lam, gnw, gnb, dnw, lnw, lnb, lam_init):
    row = lambda b: (b, 0)
    const2 = lambda b: (0, 0)
    smem = pl.BlockSpec(memory_space=pltpu.SMEM)
    return pl.pallas_call(
        functools.partial(_ctx_mix_kernel, one_minus_lam_init=1.0 - lam_init),
        grid=(BATCH,),
        in_specs=[
            smem, smem,
            pl.BlockSpec((SEQ, Q_COLS), row),
            pl.BlockSpec((SEQ, KV_COLS), row),
            pl.BlockSpec((SEQ, D_MODEL), row),
            pl.BlockSpec((None, 6, D_MODEL), lambda b: (0, 0, 0)),
            pl.BlockSpec((MIX_W, D_MODEL), const2),
            pl.BlockSpec((1, RET_W), const2),
            pl.BlockSpec((1, RET_W), const2),
            pl.BlockSpec((1, DIFF_W), const2),
            pl.BlockSpec((1, D_MODEL), const2),
            pl.BlockSpec((1, D_MODEL), const2),
        ],
        out_specs=[
            pl.BlockSpec((SEQ, D_MODEL), row),
            pl.BlockSpec((None, 2, RET_HEADS, HEAD_DIM, HEAD_DIM), lambda b: (b, 0, 0, 0, 0)),
        ],
        out_shape=[
            jax.ShapeDtypeStruct((BATCH * SEQ, D_MODEL), F32),
            jax.ShapeDtypeStruct((BATCH, 2, RET_HEADS, HEAD_DIM, HEAD_DIM), F32),
        ],
        scratch_shapes=[
            pltpu.VMEM((SEQ, MIX_W), F32),
            pltpu.VMEM((RET_HEADS, SEQ, SEQ), F32),
        ],
        compiler_params=_cparams(("arbitrary",)),
        name="ctx_mix",
    )(log_g, lam, q_arr, kv_arr, x2d, mod_l, w_out_bf, gnw, gnb, dnw, lnw, lnb)


def _rope(x, cos, sin_signed):
    n, w = x.shape
    lane = lax.broadcasted_iota(jnp.int32, (n, w), 1)
    first = (lane % 16) < 8
    partner = jnp.where(first, pltpu.roll(x, w - 8, 1), pltpu.roll(x, 8, 1))
    return x * cos + partner * sin_signed


def _lat_mix_kernel(lg_ref, lam_ref, q_ref, kv_ref, x_ref, mod_ref, wout_ref, gnw_ref, gnb_ref,
                    dnw_ref, lnw_ref, lnb_ref, cnak_ref, cnav_ref, st0_ref, cdk_ref, cdv_ref,
                    bias_ref, cos_ref, sin_ref, x1_ref, mix_ref, kr_ref, *, one_minus_lam_init):
    TQ = NA_Q_TILE
    T = DEC_SEQ
    qt = pl.program_id(1)
    q0 = pl.multiple_of(qt * TQ, TQ)

    @pl.when(qt == 0)
    def _():
        kr = _rope(kv_ref[:, KV_DK:KV_DK + DIFF_W].astype(F32), cos_ref[...], sin_ref[...])
        kr_ref[...] = kr.astype(BF16)

    ks = jnp.clip(qt * (TQ // GRID_W) - NA_WIN_ROWS // 2, 0, GRID_ROWS - NA_KEY_ROWS)
    k0 = pl.multiple_of(ks * GRID_W, 256)
    for h in range(NA_HEADS):
        c = h * HEAD_DIM
        q = q_ref[:, Q_NA + c:Q_NA + c + HEAD_DIM]
        kw = kv_ref[pl.ds(k0, NA_KEYS), KV_NAK + c:KV_NAK + c + HEAD_DIM]
        vw = kv_ref[pl.ds(k0, NA_KEYS), KV_NAV + c:KV_NAV + c + HEAD_DIM]
        kc = cnak_ref[:, c:c + HEAD_DIM].astype(BF16)
        vc = cnav_ref[:, c:c + HEAD_DIM].astype(BF16)
        s_win = _dot_nt(q, kw) * (HEAD_DIM ** -0.5) + bias_ref[h]
        s_ctx = _dot_nt(q, kc) * (HEAD_DIM ** -0.5)
        mix_ref[:, c:c + HEAD_DIM] = _softmax_pv([s_win, s_ctx], [vw, vc])

    KC = RET_KEY_CHUNK
    d0 = (lax.broadcasted_iota(jnp.int32, (TQ, KC), 0) + q0
          - lax.broadcasted_iota(jnp.int32, (TQ, KC), 1)).astype(F32)
    ii = (lax.broadcasted_iota(jnp.int32, (TQ, HEAD_DIM), 0) + q0).astype(F32)
    for h in range(RET_HEADS):
        c = h * HEAD_DIM
        lf = lg_ref[0, h]
        lb = lg_ref[1, h]
        q = q_ref[:, Q_RET + c:Q_RET + c + HEAD_DIM]
        g = q_ref[:, Q_GATE + c:Q_GATE + c + HEAD_DIM]
        o = None
        for kc in range(T // KC):
            k = kv_ref[kc * KC:(kc + 1) * KC, KV_RETK + c:KV_RETK + c + HEAD_DIM]
            v = kv_ref[kc * KC:(kc + 1) * KC, KV_RETV + c:KV_RETV + c + HEAD_DIM]
            d = d0 - float(kc * KC)
            decay = jnp.exp(jnp.where(d >= 0, d * lf, (-d) * lb))
            sc = _dot_nt(q, k) * decay
            oc = _dot(sc.astype(BF16), v)
            o = oc if o is None else o + oc
        qf = q.astype(F32)
        q_fwd = (qf * jnp.exp((ii + 1.0) * lf)).astype(BF16)
        q_bwd = (qf * jnp.exp((T - ii) * lb)).astype(BF16)
        o = o + _dot(q_fwd, st0_ref[0, h].astype(BF16)) + _dot(q_bwd, st0_ref[1, h].astype(BF16))
        mix_ref[:, MIX_RET + c:MIX_RET + c + HEAD_DIM] = _ret_finish(
            o, g, gnw_ref[:, c:c + HEAD_DIM], gnb_ref[:, c:c + HEAD_DIM])

    lam = lam_ref[0]
    qr = _rope(q_ref[:, Q_DIFF:Q_DIFF + DIFF_W].astype(F32),
               cos_ref[pl.ds(q0, TQ), :], sin_ref[pl.ds(q0, TQ), :]).astype(BF16)
    for h in range(DIFF_HEADS):
        c = h * DIFF_DV
        vl = kv_ref[:, KV_DV + c:KV_DV + c + DIFF_DV]
        vc = cdv_ref[:, c:c + DIFF_DV].astype(BF16)
        parts = []
        for i in range(2):
            lo = c + i * DIFF_D
            q = qr[:, lo:lo + DIFF_D]
            kl = kr_ref[:, lo:lo + DIFF_D]
            kc = cdk_ref[:, lo:lo + DIFF_D].astype(BF16)
            s_c = _dot_nt(q, kc) * (DIFF_D ** -0.5)
            s_l = _dot_nt(q, kl) * (DIFF_D ** -0.5)
            parts.append(_softmax_pv([s_c, s_l], [vc, vl]))
        o = parts[0] - lam * parts[1]
        mix_ref[:, MIX_DIFF + c:MIX_DIFF + c + DIFF_DV] = _diff_finish(
            o, dnw_ref[:, c:c + DIFF_DV], one_minus_lam_init)

    _out_proj_post_norm(mix_ref, wout_ref, x_ref, mod_ref, lnw_ref, lnb_ref, x1_ref)


def _lat_mix(q_arr, kv_arr, x2d, mod_l, w_out_bf, log_g, lam, gnw, gnb, dnw, lnw, lnb,
             cache_na_k, cache_na_v, state_ret, cache_diff_k, cache_diff_v, bias_tab, cos_tab,
             sin_tab, layer, lam_init):
    nq = DEC_SEQ // NA_Q_TILE
    const2 = lambda b, t: (0, 0)
    smem = pl.BlockSpec(memory_space=pltpu.SMEM)
    qrow = lambda b, t: (b * nq + t, 0)

    def variant(b, t):
        return (jnp.where(t == 0, 0, jnp.where(t == nq - 1, 2, 1)), 0, 0, 0)

    return pl.pallas_call(
        functools.partial(_lat_mix_kernel, one_minus_lam_init=1.0 - lam_init),
        grid=(DEC_BATCH, nq),
        in_specs=[
            smem, smem,
            pl.BlockSpec((NA_Q_TILE, Q_COLS), qrow),
            pl.BlockSpec((DEC_SEQ, KV_COLS), lambda b, t: (b, 0), pipeline_mode=pl.Buffered(1)),
            pl.BlockSpec((NA_Q_TILE, D_MODEL), qrow),
            pl.BlockSpec((None, 6, D_MODEL), lambda b, t: (b + 1, 0, 0)),
            pl.BlockSpec((MIX_W, D_MODEL), const2),
            pl.BlockSpec((1, RET_W), const2),
            pl.BlockSpec((1, RET_W), const2),
            pl.BlockSpec((1, DIFF_W), const2),
            pl.BlockSpec((1, D_MODEL), const2),
            pl.BlockSpec((1, D_MODEL), const2),
            pl.BlockSpec((None, None, PAST_LEN, NA_W), lambda b, t: (b, layer, 0, 0)),
            pl.BlockSpec((None, None, PAST_LEN, NA_W), lambda b, t: (b, layer, 0, 0)),
            pl.BlockSpec((None, None, 2, RET_HEADS, HEAD_DIM, HEAD_DIM),
                         lambda b, t: (b, layer, 0, 0, 0, 0)),
            pl.BlockSpec((None, None, PAST_LEN, DIFF_W), lambda b, t: (b, layer, 0, 0)),
            pl.BlockSpec((None, None, PAST_LEN, DIFF_W), lambda b, t: (b, layer, 0, 0)),
            pl.BlockSpec((None, NA_HEADS, NA_Q_TILE, NA_KEYS), variant, pipeline_mode=pl.Buffered(1)),
            pl.BlockSpec((DEC_SEQ, DIFF_W), const2),
            pl.BlockSpec((DEC_SEQ, DIFF_W), const2),
        ],
        out_specs=pl.BlockSpec((NA_Q_TILE, D_MODEL), qrow),
        out_shape=jax.ShapeDtypeStruct((DEC_BATCH * DEC_SEQ, D_MODEL), F32),
        scratch_shapes=[
            pltpu.VMEM((NA_Q_TILE, MIX_W), F32),
            pltpu.VMEM((DEC_SEQ, DIFF_W), BF16),
        ],
        compiler_params=_cparams(("arbitrary", "arbitrary")),
        name="lat_mix",
    )(log_g, lam, q_arr, kv_arr, x2d, mod_l, w_out_bf, gnw, gnb, dnw, lnw, lnb,
      cache_na_k, cache_na_v, state_ret, cache_diff_k, cache_diff_v, bias_tab, cos_tab, sin_tab)


MOE_PART = 2048
MOE_TILE = 512
MOE_EB = 4
MOE_CH = 64
MOE_STEPS = N_EXPERTS // MOE_EB


def _route_transposed(lt):
    shape = lt.shape
    r = lax.broadcasted_iota(jnp.int32, shape, 0).astype(F32)
    ninf = -jnp.inf
    is_g = jnp.where(r >= N_EXPERTS, jnp.where(r < N_EXPERTS + N_GROUPS, 1.0, 0.0), 0.0) > 0.5
    gl = jnp.where(is_g, lt, ninf)
    gmax = jnp.max(gl, axis=0, keepdims=True)
    gsel = jnp.min(jnp.where(gl == gmax, r - N_EXPERTS, 1e9), axis=0, keepdims=True)
    gsum = jnp.sum(jnp.where(is_g, jnp.exp(gl - gmax), 0.0), axis=0, keepdims=True)
    gw = 1.0 / gsum
    lo = gsel * EXPERTS_PER_GROUP
    is_e = jnp.where(r >= lo, jnp.where(r < lo + EXPERTS_PER_GROUP, 1.0, 0.0), 0.0) > 0.5
    el = jnp.where(is_e, lt, ninf)
    v1 = jnp.max(el, axis=0, keepdims=True)
    i1 = jnp.min(jnp.where(el == v1, r, 1e9), axis=0, keepdims=True)
    el2 = jnp.where(r == i1, ninf, el)
    v2 = jnp.max(el2, axis=0, keepdims=True)
    i2 = jnp.min(jnp.where(el2 == v2, r, 1e9), axis=0, keepdims=True)
    t = jnp.exp(v2 - v1)
    w1 = gw / (1.0 + t)
    w2 = gw * t / (1.0 + t)
    first = r == i1
    second = r == i2
    gates = jnp.where(first, w1, 0.0) + jnp.where(second, w2, 0.0)
    member = jnp.where(first, 1.0, jnp.where(second, 1.0, 0.0))
    return gates, member


def _moe_kernel(x_ref, mod_ref, wr_ref, wg_ref, wu_ref, wd_ref, lnw_ref, lnb_ref, out_ref,
                h_ref, rank_ref, gate_ref):
    s = pl.program_id(1)
    n_tiles = MOE_PART // MOE_TILE

    @pl.when(s == 0)
    def _():
        m = mod_ref[...]
        h = x_ref[...] * (1.0 + m[4:5]) + m[3:4]
        h_ref[...] = h.astype(BF16)
        lt = lax.dot_general(wr_ref[...], h, (((1,), (1,)), ((), ())),
                             preferred_element_type=F32, precision=lax.Precision.HIGHEST)
        gates, member = _route_transposed(lt)
        before = jnp.where(lax.broadcasted_iota(jnp.int32, (MOE_TILE, MOE_TILE), 0)
                           < lax.broadcasted_iota(jnp.int32, (MOE_TILE, MOE_TILE), 1), 1.0, 0.0).astype(BF16)
        rank_ref[...] = jnp.full(rank_ref.shape, -1.0, F32)
        gate_ref[...] = jnp.zeros(gate_ref.shape, F32)
        for t in range(n_tiles):
            c0 = t * MOE_TILE
            mem_t = member[0:N_EXPERTS, c0:c0 + MOE_TILE]
            cnt = _dot(mem_t.astype(BF16), before)
            rank = jnp.where(mem_t > 0.5, cnt, -1.0)
            for st in range(MOE_STEPS):
                rank_ref[st, 0:MOE_EB, c0:c0 + MOE_TILE] = rank[st * MOE_EB:(st + 1) * MOE_EB]
                gate_ref[st, 0:MOE_EB, c0:c0 + MOE_TILE] = gates[st * MOE_EB:(st + 1) * MOE_EB, c0:c0 + MOE_TILE]
        out_ref[...] = jnp.zeros(out_ref.shape, F32)

    ranks = rank_ref[s, 0:MOE_EB, :]
    gts = gate_ref[s, 0:MOE_EB, :]
    n_chunks = (jnp.max(ranks) * (1.0 / MOE_CH)).astype(jnp.int32) + 1

    def chunk_body(k, carry):
        slot = (lax.broadcasted_iota(jnp.int32, (MOE_CH, MOE_TILE), 0) + k * MOE_CH).astype(F32)
        for t in range(n_tiles):
            c0 = t * MOE_TILE
            onehots = []
            row_gate = []
            for i in range(MOE_EB):
                hit = ranks[i:i + 1, c0:c0 + MOE_TILE] == slot
                onehots.append(jnp.where(hit, 1.0, 0.0).astype(BF16))
                row_gate.append(jnp.sum(jnp.where(hit, gts[i:i + 1, c0:c0 + MOE_TILE], 0.0),
                                        axis=1, keepdims=True))
            sel = jnp.concatenate(onehots, axis=0)
            xs = _dot(sel, h_ref[c0:c0 + MOE_TILE, :]).astype(BF16)
            ys = []
            for i in range(MOE_EB):
                xi = xs[i * MOE_CH:(i + 1) * MOE_CH]
                a = _dot(xi, wg_ref[i])
                u = _dot(xi, wu_ref[i])
                hm = (_silu(a) * u * row_gate[i]).astype(BF16)
                ys.append(_dot(hm, wd_ref[i]).astype(BF16))
            y = jnp.concatenate(ys, axis=0)
            out_ref[c0:c0 + MOE_TILE, :] += _dot_tn(sel, y)
        return carry

    lax.fori_loop(0, n_chunks, chunk_body, 0)

    @pl.when(s == MOE_STEPS - 1)
    def _():
        m = mod_ref[...]
        z = ALPHA * x_ref[...] + m[5:6] * out_ref[...]
        out_ref[...] = _layer_norm_rows(z, lnw_ref[...], lnb_ref[...])


def _moe(x2d, mod_l, mod_row_fn, wr_t, wg_bf, wu_bf, wd_bf, lnw, lnb, layer, name):
    n = x2d.shape[0]
    row = lambda p, s: (p, 0)
    const2 = lambda p, s: (0, 0)
    wspec = lambda shape: pl.BlockSpec((None, MOE_EB) + shape, lambda p, s: (layer, s, 0, 0))
    return pl.pallas_call(
        _moe_kernel,
        grid=(n // MOE_PART, MOE_STEPS),
        in_specs=[
            pl.BlockSpec((MOE_PART, D_MODEL), row, pipeline_mode=pl.Buffered(1)),
            pl.BlockSpec((None, 6, D_MODEL), lambda p, s: (mod_row_fn(p * MOE_PART), 0, 0)),
            pl.BlockSpec((128, D_MODEL), const2),
            wspec((D_MODEL, EXPERT_FF)),
            wspec((D_MODEL, EXPERT_FF)),
            wspec((EXPERT_FF, D_MODEL)),
            pl.BlockSpec((1, D_MODEL), const2),
            pl.BlockSpec((1, D_MODEL), const2),
        ],
        out_specs=pl.BlockSpec((MOE_PART, D_MODEL), row),
        out_shape=jax.ShapeDtypeStruct((n, D_MODEL), F32),
        scratch_shapes=[
            pltpu.VMEM((MOE_PART, D_MODEL), BF16),
            pltpu.VMEM((MOE_STEPS, 8, MOE_PART), F32),
            pltpu.VMEM((MOE_STEPS, 8, MOE_PART), F32),
        ],
        compiler_params=_cparams(("arbitrary", "arbitrary")),
        name=name,
    )(x2d, mod_l, wr_t, wg_bf, wu_bf, wd_bf, lnw, lnb)


def _na_bias_tables(rel_bias):
    q_rows = NA_Q_TILE // GRID_W
    n_dr = 2 * NA_WIN_ROWS - 1
    pad_c = GRID_W - NA_WIN_COLS
    padded = jnp.pad(rel_bias.astype(F32), ((0, 0), (0, 0), (pad_c, pad_c)), mode="edge")
    col_tab = jnp.stack([padded[:, :, GRID_W - 1 - qc:2 * GRID_W - 1 - qc] for qc in range(GRID_W)], axis=2)
    pad_r = NA_KEY_ROWS - q_rows
    col_tab = jnp.pad(col_tab, ((0, 0), (pad_r, pad_r), (0, 0), (0, 0)))
    qi = np.arange(NA_Q_TILE)
    ki = np.arange(NA_KEYS)
    tabs = []
    for r0, ks in ((0, 0), (8, 4), (GRID_ROWS - q_rows, GRID_ROWS - NA_KEY_ROWS)):
        rows = []
        for a in range(q_rows):
            start = ks - r0 - a + (NA_WIN_ROWS - 1) + pad_r
            blk = col_tab[:, start:start + NA_KEY_ROWS]
            rows.append(blk.transpose(0, 2, 1, 3).reshape(NA_HEADS, GRID_W, NA_KEYS))
        b = jnp.concatenate(rows, axis=1)
        q_row = r0 + qi // GRID_W
        q_col = qi % GRID_W
        k_row = ks + ki // GRID_W
        k_col = ki % GRID_W
        row_start = np.clip(q_row - NA_WIN_ROWS // 2, 0, GRID_ROWS - NA_WIN_ROWS)
        row_ok = (k_row[None, :] >= row_start[:, None]) & (k_row[None, :] < row_start[:, None] + NA_WIN_ROWS)
        col_start = np.clip(q_col - NA_WIN_COLS // 2, 0, GRID_W - NA_WIN_COLS)
        col_ok = (k_col[None, :] >= col_start[:, None]) & (k_col[None, :] < col_start[:, None] + NA_WIN_COLS)
        dr = k_row[None, :] - q_row[:, None] + (NA_WIN_ROWS - 1)
        assert np.all((dr[row_ok] >= 0) & (dr[row_ok] < n_dr))
        tabs.append(jnp.where(jnp.asarray(row_ok & col_ok)[None], b, NEG_INF))
    return jnp.stack(tabs, axis=0)


def _rope_tables():
    n = DIFF_D // 4
    lane = np.arange(DIFF_W)
    d = lane % DIFF_D
    use_col = d >= DIFF_D // 2
    e = d % (DIFF_D // 2)
    f = e % n
    first = e < n
    t = jnp.arange(DEC_SEQ)
    pos = jnp.where(jnp.asarray(use_col)[None, :], (t % GRID_W)[:, None], (t // GRID_W)[:, None])
    freqs = ROPE_BASE ** (-jnp.arange(n, dtype=F32) / n)
    ang = pos.astype(F32) * freqs[jnp.asarray(f)][None, :]
    sign = jnp.asarray(np.where(first, -1.0, 1.0), dtype=F32)
    return jnp.cos(ang), jnp.sin(ang) * sign[None, :]


def kernel(x_prompt, x_sample, c, cache_na_k, cache_na_v, state_ret, cache_diff_k, cache_diff_v, c_ctx,
           w_mod, b_mod, w_in, na_rel_bias, ret_decay, ret_gn_w, ret_gn_b, diff_lambda, diff_norm_w,
           w_out, ln1_w, ln1_b, router_group, router_expert, exp_w_gate, exp_w_up, exp_w_down,
           ln2_w, ln2_b):
    n_ctx = BATCH * SEQ
    n_lat = DEC_BATCH * DEC_SEQ
    x_ctx = x_prompt.reshape(n_ctx, D_MODEL)
    x_lat = x_sample.reshape(n_lat, D_MODEL)

    cond8 = jnp.zeros((8, D_MODEL), F32).at[0].set(c_ctx).at[1:1 + DEC_BATCH].set(c)
    mod = _modulation(cond8, w_mod, b_mod)

    cache_na_k = cache_na_k.reshape(DEC_BATCH, DEPTH, PAST_LEN, NA_W)
    cache_na_v = cache_na_v.reshape(DEC_BATCH, DEPTH, PAST_LEN, NA_W)
    cache_diff_k = cache_diff_k.reshape(DEC_BATCH, DEPTH, PAST_LEN, DIFF_W)
    cache_diff_v = cache_diff_v.reshape(DEC_BATCH, DEPTH, PAST_LEN, DIFF_W)
    cos_tab, sin_tab = _rope_tables()

    ctx_row = lambda r: 0
    lat_row = lambda r: 1 + r // DEC_SEQ

    wg_bf = exp_w_gate.astype(BF16)
    wu_bf = exp_w_up.astype(BF16)
    wd_bf = exp_w_down.astype(BF16)

    na_ks, na_vs, ret_states, diff_ks, diff_vs = [], [], [], [], []
    for l in range(DEPTH):
        lam_init = 0.8 - 0.6 * math.exp(-0.3 * l)
        lp = diff_lambda[l].astype(F32)
        lam = (jnp.exp(jnp.sum(lp[0] * lp[1])) - jnp.exp(jnp.sum(lp[2] * lp[3])) + lam_init).reshape(1)
        log_g = jax.nn.log_sigmoid(ret_decay[l].astype(F32))
        w_in_bf = w_in[l].astype(BF16)
        w_out_bf = w_out[l].astype(BF16)
        gnw = ret_gn_w[l].reshape(1, RET_W)
        gnb = ret_gn_b[l].reshape(1, RET_W)
        dnw = diff_norm_w[l].reshape(1, DIFF_W)
        l1w = ln1_w[l].reshape(1, D_MODEL)
        l1b = ln1_b[l].reshape(1, D_MODEL)
        l2w = ln2_w[l].reshape(1, D_MODEL)
        l2b = ln2_b[l].reshape(1, D_MODEL)
        wr_t = jnp.zeros((128, D_MODEL), F32)
        wr_t = wr_t.at[:N_EXPERTS].set(router_expert[l].T).at[N_EXPERTS:N_EXPERTS + N_GROUPS].set(router_group[l].T)
        bias_tab = _na_bias_tables(na_rel_bias[l])
        mod_l = mod[l]

        q_c, kv_c, nak, nav, dk, dv = _project(x_ctx, mod_l, w_in_bf, ctx_row, True)
        x1_c, st = _ctx_mix(q_c, kv_c, x_ctx, mod_l, w_out_bf, log_g, lam, gnw, gnb, dnw, l1w, l1b, lam_init)
        x_ctx = _moe(x1_c, mod_l, ctx_row, wr_t, wg_bf, wu_bf, wd_bf, l2w, l2b, l, "moe_ctx")
        na_ks.append(nak.reshape(BATCH, SEQ, NA_HEADS, HEAD_DIM))
        na_vs.append(nav.reshape(BATCH, SEQ, NA_HEADS, HEAD_DIM))
        ret_states.append(st)
        diff_ks.append(dk.reshape(BATCH, SEQ, DIFF_HEADS, 2 * DIFF_D))
        diff_vs.append(dv.reshape(BATCH, SEQ, DIFF_HEADS, DIFF_DV))

        q_l, kv_l = _project(x_lat, mod_l, w_in_bf, lat_row, False)
        x1_l = _lat_mix(q_l, kv_l, x_lat, mod_l, w_out_bf, log_g, lam, gnw, gnb, dnw, l1w, l1b,
                        cache_na_k, cache_na_v, state_ret, cache_diff_k, cache_diff_v,
                        bias_tab, cos_tab, sin_tab, l, lam_init)
        x_lat = _moe(x1_l, mod_l, lat_row, wr_t, wg_bf, wu_bf, wd_bf, l2w, l2b, l, "moe_lat")

    return (x_ctx.reshape(BATCH, SEQ, D_MODEL), x_lat.reshape(DEC_BATCH, DEC_SEQ, D_MODEL),
            jnp.stack(na_ks, axis=1), jnp.stack(na_vs, axis=1), jnp.stack(ret_states, axis=1),
            jnp.stack(diff_ks, axis=1), jnp.stack(diff_vs, axis=1))
```

```python
import functools
import math

import numpy as np
import jax
import jax.numpy as jnp
from jax import lax
from jax.experimental import pallas as pl
from jax.experimental.pallas import tpu as pltpu

D_MODEL = 1024
BATCH = 32
SEQ = 256
DEPTH = 2
DEC_BATCH = 2
DEC_SEQ = 2048
PAST_LEN = 512
GRID_W = 64
HEAD_DIM = 64
NA_HEADS = 6
NA_WIN_ROWS = 8
NA_WIN_COLS = 16
RET_HEADS = 6
DIFF_HEADS = 4
DIFF_D = 32
DIFF_DV = 64
NA_W = NA_HEADS * HEAD_DIM
RET_W = RET_HEADS * HEAD_DIM
DIFF_W = DIFF_HEADS * DIFF_DV
MIX_W = NA_W + RET_W + DIFF_W
IN_COLS = 3 * NA_W + 4 * RET_W + 3 * DIFF_W
N_GROUPS = 4
EXPERTS_PER_GROUP = 8
N_EXPERTS = N_GROUPS * EXPERTS_PER_GROUP
EXPERT_FF = 256
ROPE_BASE = 10000.0
LN_EPS = 1e-5
NEG_INF = -1e30
ALPHA = (2.0 * DEPTH) ** 0.25
LOG2E = math.log2(math.e)

F32 = jnp.float32
BF16 = jnp.bfloat16

Q_NA, Q_RET, Q_GATE, Q_DIFF = 0, NA_W, NA_W + RET_W, NA_W + 2 * RET_W
Q_COLS = NA_W + 2 * RET_W + DIFF_W
KV_NAK, KV_NAV = 0, NA_W
KV_RETK, KV_RETV = 2 * NA_W, 2 * NA_W + RET_W
KV_DK, KV_DV = 2 * NA_W + 2 * RET_W, 2 * NA_W + 2 * RET_W + DIFF_W
KV_COLS = 2 * NA_W + 2 * RET_W + 2 * DIFF_W
MIX_RET, MIX_DIFF = NA_W, NA_W + RET_W

NA_Q_TILE = 256
NA_KEY_ROWS = 12
NA_KEYS = NA_KEY_ROWS * GRID_W
RET_KEY_CHUNK = 512
GRID_ROWS = DEC_SEQ // GRID_W

VMEM_LIMIT = 60 * 1024 * 1024


def _cparams(sem):
    return pltpu.CompilerParams(dimension_semantics=sem, vmem_limit_bytes=VMEM_LIMIT)


def _dot(a, b):
    return jnp.dot(a, b, preferred_element_type=F32)


def _dot_nt(a, b):
    return lax.dot_general(a, b, (((1,), (1,)), ((), ())), preferred_element_type=F32)


def _dot_tn(a, b):
    return lax.dot_general(a, b, (((0,), (0,)), ((), ())), preferred_element_type=F32)


def _silu(x):
    return x / (1.0 + jnp.exp(-x))


def _layer_norm_rows(z, w, b):
    mu = jnp.mean(z, axis=-1, keepdims=True)
    zc = z - mu
    var = jnp.mean(zc * zc, axis=-1, keepdims=True)
    return zc * lax.rsqrt(var + LN_EPS) * w + b


def _mod_kernel(cond_ref, w_ref, b_ref, o_ref):
    c = cond_ref[...]
    s = _silu(c)
    o_ref[...] = jnp.dot(s, w_ref[...], preferred_element_type=F32,
                         precision=lax.Precision.HIGHEST) + b_ref[...]


def _modulation(cond8, w_mod, b_mod):
    nj = 6
    out = pl.pallas_call(
        _mod_kernel,
        grid=(DEPTH, nj),
        in_specs=[
            pl.BlockSpec((8, D_MODEL), lambda l, j: (0, 0)),
            pl.BlockSpec((None, D_MODEL, D_MODEL), lambda l, j: (l, 0, j)),
            pl.BlockSpec((None, 1, D_MODEL), lambda l, j: (l, 0, j)),
        ],
        out_specs=pl.BlockSpec((None, 8, D_MODEL), lambda l, j: (l, 0, j)),
        out_shape=jax.ShapeDtypeStruct((DEPTH, 8, 6 * D_MODEL), F32),
        compiler_params=_cparams(("arbitrary", "arbitrary")),
        name="modulation",
    )(cond8, w_mod, b_mod.reshape(DEPTH, 1, 6 * D_MODEL))
    return out.reshape(DEPTH, 8, 6, D_MODEL)


def _proj_kernel(x_ref, mod_ref, w_ref, *refs):
    if len(refs) == 2:
        q_ref, kv_ref = refs
        cache_refs = None
    else:
        q_ref, kv_ref = refs[4:6]
        cache_refs = refs[6:]
    m = mod_ref[...]
    h = x_ref[...] * (1.0 + m[1:2]) + m[0:1]
    p = _dot(h.astype(BF16), w_ref[...])
    o = 0
    na_q = p[:, o:o + NA_W] * (HEAD_DIM ** -0.5 * LOG2E); o += NA_W
    na_k = p[:, o:o + NA_W]; o += NA_W
    na_v = p[:, o:o + NA_W]; o += NA_W
    ret_q = p[:, o:o + RET_W]; o += RET_W
    ret_k = p[:, o:o + RET_W] * (HEAD_DIM ** -0.5); o += RET_W
    ret_v = p[:, o:o + RET_W]; o += RET_W
    ret_g = p[:, o:o + RET_W]; o += RET_W
    dq = p[:, o:o + DIFF_W] * (DIFF_D ** -0.5 * LOG2E); o += DIFF_W
    dk = p[:, o:o + DIFF_W]; o += DIFF_W
    dv = p[:, o:o + DIFF_W]
    q_ref[:, Q_NA:Q_NA + NA_W] = na_q.astype(BF16)
    q_ref[:, Q_RET:Q_RET + RET_W] = ret_q.astype(BF16)
    q_ref[:, Q_GATE:Q_GATE + RET_W] = ret_g.astype(BF16)
    q_ref[:, Q_DIFF:Q_DIFF + DIFF_W] = dq.astype(BF16)
    kv_ref[:, KV_NAK:KV_NAK + NA_W] = na_k.astype(BF16)
    kv_ref[:, KV_NAV:KV_NAV + NA_W] = na_v.astype(BF16)
    kv_ref[:, KV_RETK:KV_RETK + RET_W] = ret_k.astype(BF16)
    kv_ref[:, KV_RETV:KV_RETV + RET_W] = ret_v.astype(BF16)
    kv_ref[:, KV_DK:KV_DK + DIFF_W] = dk.astype(BF16)
    kv_ref[:, KV_DV:KV_DV + DIFF_W] = dv.astype(BF16)
    if cache_refs is not None:
        nb = x_ref.shape[0] // SEQ
        for ref, val in zip(cache_refs, (na_k, na_v, dk, dv)):
            ref[...] = val.reshape(nb, SEQ, val.shape[-1])


def _project(x2d, mod_l, w_in_bf, mod_row_fn, caches=None, layer=0, tm=512):
    n = x2d.shape[0]
    row = lambda i: (i, 0)
    out_shape = [jax.ShapeDtypeStruct((n, Q_COLS), BF16), jax.ShapeDtypeStruct((n, KV_COLS), BF16)]
    out_specs = [pl.BlockSpec((tm, Q_COLS), row), pl.BlockSpec((tm, KV_COLS), row)]
    in_specs = [
        pl.BlockSpec((tm, D_MODEL), row),
        pl.BlockSpec((None, 6, D_MODEL), lambda i: (mod_row_fn(i * tm), 0, 0)),
        pl.BlockSpec((D_MODEL, IN_COLS), lambda i: (0, 0)),
    ]
    args = [x2d, mod_l, w_in_bf]
    aliases = {}
    if caches is not None:
        for k, arr in enumerate(caches):
            in_specs.append(pl.BlockSpec(memory_space=pl.ANY))
            args.append(arr)
            aliases[3 + k] = 2 + k
            out_shape.append(jax.ShapeDtypeStruct(arr.shape, F32))
            out_specs.append(pl.BlockSpec((tm // SEQ, None, SEQ, arr.shape[-1]), lambda i: (i, layer, 0, 0)))
    return pl.pallas_call(
        _proj_kernel,
        grid=(n // tm,),
        in_specs=in_specs,
        out_specs=out_specs,
        out_shape=out_shape,
        input_output_aliases=aliases,
        compiler_params=_cparams(("arbitrary",)),
        name="proj_lat" if caches is None else "proj_ctx",
    )(*args)


def _softmax_pv(score_list, v_list):
    m = None
    for s in score_list:
        mi = jnp.max(s, axis=-1, keepdims=True)
        m = mi if m is None else jnp.maximum(m, mi)
    l = None
    o = None
    for s, v in zip(score_list, v_list):
        e = jnp.exp2(s - m)
        li = jnp.sum(e, axis=-1, keepdims=True)
        oi = _dot(e.astype(BF16), v)
        l = li if l is None else l + li
        o = oi if o is None else o + oi
    return o / l


def _ret_finish(o, gate_bf, gnw, gnb):
    mu = jnp.mean(o, axis=-1, keepdims=True)
    oc = o - mu
    var = jnp.mean(oc * oc, axis=-1, keepdims=True)
    on = oc * lax.rsqrt(var + LN_EPS)
    return (on * gnw + gnb) * _silu(gate_bf.astype(F32))


def _diff_finish(o, dnw, one_minus_lam_init):
    ms = jnp.mean(o * o, axis=-1, keepdims=True)
    return o * lax.rsqrt(ms + LN_EPS) * dnw * one_minus_lam_init


def _out_proj_post_norm(mix_ref, wout_ref, x_ref, mod_ref, lnw_ref, lnb_ref, x1_ref):
    y = _dot(mix_ref[...].astype(BF16), wout_ref[...])
    m = mod_ref[...]
    z = ALPHA * x_ref[...] + m[2:3] * y
    x1_ref[...] = _layer_norm_rows(z, lnw_ref[...], lnb_ref[...])


def _ctx_mix_kernel(lg_ref, lam_ref, q_ref, kv_ref, x_ref, mod_ref, wout_ref, gnw_ref, gnb_ref,
                    dnw_ref, lnw_ref, lnb_ref, st_alias_ref, x1_ref, st_ref, mix_ref, dm_ref, *,
                    one_minus_lam_init):
    T = SEQ

    @pl.when(pl.program_id(0) == 0)
    def _():
        i = lax.broadcasted_iota(jnp.int32, (T, T), 0).astype(F32)
        j = lax.broadcasted_iota(jnp.int32, (T, T), 1).astype(F32)
        d = i - j
        for h in range(RET_HEADS):
            dm_ref[h] = jnp.exp2(jnp.where(d >= 0, d * lg_ref[0, h], (-d) * lg_ref[1, h]))

    for h in range(NA_HEADS):
        c = h * HEAD_DIM
        q = q_ref[:, Q_NA + c:Q_NA + c + HEAD_DIM]
        k = kv_ref[:, KV_NAK + c:KV_NAK + c + HEAD_DIM]
        v = kv_ref[:, KV_NAV + c:KV_NAV + c + HEAD_DIM]
        mix_ref[:, c:c + HEAD_DIM] = _softmax_pv([_dot_nt(q, k)], [v])

    jj = lax.broadcasted_iota(jnp.int32, (T, HEAD_DIM), 0).astype(F32)
    for h in range(RET_HEADS):
        c = h * HEAD_DIM
        q = q_ref[:, Q_RET + c:Q_RET + c + HEAD_DIM]
        g = q_ref[:, Q_GATE + c:Q_GATE + c + HEAD_DIM]
        k = kv_ref[:, KV_RETK + c:KV_RETK + c + HEAD_DIM]
        v = kv_ref[:, KV_RETV + c:KV_RETV + c + HEAD_DIM]
        sc = _dot_nt(q, k) * dm_ref[h]
        o = _dot(sc.astype(BF16), v)
        kf = k.astype(F32)
        k_fwd = (kf * jnp.exp2((T - 1.0 - jj) * lg_ref[0, h])).astype(BF16)
        k_bwd = (kf * jnp.exp2(jj * lg_ref[1, h])).astype(BF16)
        st_ref[0, h] = _dot_tn(k_fwd, v)
        st_ref[1, h] = _dot_tn(k_bwd, v)
        mix_ref[:, MIX_RET + c:MIX_RET + c + HEAD_DIM] = _ret_finish(
            o, g, gnw_ref[:, c:c + HEAD_DIM], gnb_ref[:, c:c + HEAD_DIM])

    lam = lam_ref[0]
    for h in range(DIFF_HEADS):
        c = h * DIFF_DV
        v = kv_ref[:, KV_DV + c:KV_DV + c + DIFF_DV]
        parts = []
        for i in range(2):
            q = q_ref[:, Q_DIFF + c + i * DIFF_D:Q_DIFF + c + (i + 1) * DIFF_D]
            k = kv_ref[:, KV_DK + c + i * DIFF_D:KV_DK + c + (i + 1) * DIFF_D]
            parts.append(_softmax_pv([_dot_nt(q, k)], [v]))
        o = parts[0] - lam * parts[1]
        mix_ref[:, MIX_DIFF + c:MIX_DIFF + c + DIFF_DV] = _diff_finish(
            o, dnw_ref[:, c:c + DIFF_DV], one_minus_lam_init)

    _out_proj_post_norm(mix_ref, wout_ref, x_ref, mod_ref, lnw_ref, lnb_ref, x1_ref)


def _ctx_mix(q_arr, kv_arr, x2d, mod_l, w_out_bf, log_g, lam, gnw, gnb, dnw, lnw, lnb, states, layer,
             lam_init):
    row = lambda b: (b, 0)
    const2 = lambda b: (0, 0)
    smem = pl.BlockSpec(memory_space=pltpu.SMEM)
    return pl.pallas_call(
        functools.partial(_ctx_mix_kernel, one_minus_lam_init=1.0 - lam_init),
        grid=(BATCH,),
        in_specs=[
            smem, smem,
            pl.BlockSpec((SEQ, Q_COLS), row),
            pl.BlockSpec((SEQ, KV_COLS), row),
            pl.BlockSpec((SEQ, D_MODEL), row),
            pl.BlockSpec((None, 6, D_MODEL), lambda b: (0, 0, 0)),
            pl.BlockSpec((MIX_W, D_MODEL), const2),
            pl.BlockSpec((1, RET_W), const2),
            pl.BlockSpec((1, RET_W), const2),
            pl.BlockSpec((1, DIFF_W), const2),
            pl.BlockSpec((1, D_MODEL), const2),
            pl.BlockSpec((1, D_MODEL), const2),
            pl.BlockSpec(memory_space=pl.ANY),
        ],
        out_specs=[
            pl.BlockSpec((SEQ, D_MODEL), row),
            pl.BlockSpec((None, None, 2, RET_HEADS, HEAD_DIM, HEAD_DIM), lambda b: (b, layer, 0, 0, 0, 0)),
        ],
        out_shape=[
            jax.ShapeDtypeStruct((BATCH * SEQ, D_MODEL), F32),
            jax.ShapeDtypeStruct(states.shape, F32),
        ],
        input_output_aliases={12: 1},
        scratch_shapes=[
            pltpu.VMEM((SEQ, MIX_W), F32),
            pltpu.VMEM((RET_HEADS, SEQ, SEQ), F32),
        ],
        compiler_params=_cparams(("arbitrary",)),
        name="ctx_mix",
    )(log_g, lam, q_arr, kv_arr, x2d, mod_l, w_out_bf, gnw, gnb, dnw, lnw, lnb, states)


def _rope(x, cos, sin_signed):
    n, w = x.shape
    lane = lax.broadcasted_iota(jnp.int32, (n, w), 1)
    first = (lane % 16) < 8
    partner = jnp.where(first, pltpu.roll(x, w - 8, 1), pltpu.roll(x, 8, 1))
    return x * cos + partner * sin_signed


def _lat_mix_kernel(lg_ref, lam_ref, q_ref, kv_ref, x_ref, mod_ref, wout_ref, gnw_ref, gnb_ref,
                    dnw_ref, lnw_ref, lnb_ref, cnak_ref, cnav_ref, st0_ref, cdk_ref, cdv_ref,
                    bias_ref, cos_ref, sin_ref, x1_ref, mix_ref, kr_ref, *, one_minus_lam_init):
    TQ = NA_Q_TILE
    T = DEC_SEQ
    qt = pl.program_id(1)
    q0 = pl.multiple_of(qt * TQ, TQ)

    @pl.when(qt == 0)
    def _():
        kr = _rope(kv_ref[:, KV_DK:KV_DK + DIFF_W].astype(F32), cos_ref[...], sin_ref[...])
        kr_ref[...] = kr.astype(BF16)

    ks = jnp.clip(qt * (TQ // GRID_W) - NA_WIN_ROWS // 2, 0, GRID_ROWS - NA_KEY_ROWS)
    k0 = pl.multiple_of(ks * GRID_W, 256)
    for h in range(NA_HEADS):
        c = h * HEAD_DIM
        q = q_ref[:, Q_NA + c:Q_NA + c + HEAD_DIM]
        kw = kv_ref[pl.ds(k0, NA_KEYS), KV_NAK + c:KV_NAK + c + HEAD_DIM]
        vw = kv_ref[pl.ds(k0, NA_KEYS), KV_NAV + c:KV_NAV + c + HEAD_DIM]
        kc = cnak_ref[:, c:c + HEAD_DIM].astype(BF16)
        vc = cnav_ref[:, c:c + HEAD_DIM].astype(BF16)
        s_win = _dot_nt(q, kw) + bias_ref[h]
        s_ctx = _dot_nt(q, kc)
        mix_ref[:, c:c + HEAD_DIM] = _softmax_pv([s_win, s_ctx], [vw, vc])

    KC = RET_KEY_CHUNK
    d0 = (lax.broadcasted_iota(jnp.int32, (TQ, KC), 0) + q0
          - lax.broadcasted_iota(jnp.int32, (TQ, KC), 1)).astype(F32)
    ii = (lax.broadcasted_iota(jnp.int32, (TQ, HEAD_DIM), 0) + q0).astype(F32)
    for h in range(RET_HEADS):
        c = h * HEAD_DIM
        lf = lg_ref[0, h]
        lb = lg_ref[1, h]
        q = q_ref[:, Q_RET + c:Q_RET + c + HEAD_DIM]
        g = q_ref[:, Q_GATE + c:Q_GATE + c + HEAD_DIM]
        o = None
        for kc in range(T // KC):
            k = kv_ref[kc * KC:(kc + 1) * KC, KV_RETK + c:KV_RETK + c + HEAD_DIM]
            v = kv_ref[kc * KC:(kc + 1) * KC, KV_RETV + c:KV_RETV + c + HEAD_DIM]
            d = d0 - float(kc * KC)
            decay = jnp.exp2(jnp.where(d >= 0, d * lf, (-d) * lb))
            sc = _dot_nt(q, k) * decay
            oc = _dot(sc.astype(BF16), v)
            o = oc if o is None else o + oc
        qf = q.astype(F32)
        q_fwd = (qf * jnp.exp2((ii + 1.0) * lf)).astype(BF16)
        q_bwd = (qf * jnp.exp2((T - ii) * lb)).astype(BF16)
        o = o + _dot(q_fwd, st0_ref[0, h].astype(BF16)) + _dot(q_bwd, st0_ref[1, h].astype(BF16))
        mix_ref[:, MIX_RET + c:MIX_RET + c + HEAD_DIM] = _ret_finish(
            o, g, gnw_ref[:, c:c + HEAD_DIM], gnb_ref[:, c:c + HEAD_DIM])

    lam = lam_ref[0]
    qr = _rope(q_ref[:, Q_DIFF:Q_DIFF + DIFF_W].astype(F32),
               cos_ref[pl.ds(q0, TQ), :], sin_ref[pl.ds(q0, TQ), :]).astype(BF16)
    for h in range(DIFF_HEADS):
        c = h * DIFF_DV
        vl = kv_ref[:, KV_DV + c:KV_DV + c + DIFF_DV]
        vc = cdv_ref[:, c:c + DIFF_DV].astype(BF16)
        parts = []
        for i in range(2):
            lo = c + i * DIFF_D
            q = qr[:, lo:lo + DIFF_D]
            kl = kr_ref[:, lo:lo + DIFF_D]
            kc = cdk_ref[:, lo:lo + DIFF_D].astype(BF16)
            parts.append(_softmax_pv([_dot_nt(q, kc), _dot_nt(q, kl)], [vc, vl]))
        o = parts[0] - lam * parts[1]
        mix_ref[:, MIX_DIFF + c:MIX_DIFF + c + DIFF_DV] = _diff_finish(
            o, dnw_ref[:, c:c + DIFF_DV], one_minus_lam_init)

    _out_proj_post_norm(mix_ref, wout_ref, x_ref, mod_ref, lnw_ref, lnb_ref, x1_ref)


def _lat_mix(q_arr, kv_arr, x2d, mod_l, w_out_bf, log_g, lam, gnw, gnb, dnw, lnw, lnb,
             cache_na_k, cache_na_v, state_ret, cache_diff_k, cache_diff_v, bias_tab, cos_tab,
             sin_tab, layer, lam_init):
    nq = DEC_SEQ // NA_Q_TILE
    const2 = lambda b, t: (0, 0)
    smem = pl.BlockSpec(memory_space=pltpu.SMEM)
    qrow = lambda b, t: (b * nq + t, 0)

    def variant(b, t):
        return (jnp.where(t == 0, 0, jnp.where(t == nq - 1, 2, 1)), 0, 0, 0)

    return pl.pallas_call(
        functools.partial(_lat_mix_kernel, one_minus_lam_init=1.0 - lam_init),
        grid=(DEC_BATCH, nq),
        in_specs=[
            smem, smem,
            pl.BlockSpec((NA_Q_TILE, Q_COLS), qrow),
            pl.BlockSpec((DEC_SEQ, KV_COLS), lambda b, t: (b, 0), pipeline_mode=pl.Buffered(1)),
            pl.BlockSpec((NA_Q_TILE, D_MODEL), qrow),
            pl.BlockSpec((None, 6, D_MODEL), lambda b, t: (b + 1, 0, 0)),
            pl.BlockSpec((MIX_W, D_MODEL), const2),
            pl.BlockSpec((1, RET_W), const2),
            pl.BlockSpec((1, RET_W), const2),
            pl.BlockSpec((1, DIFF_W), const2),
            pl.BlockSpec((1, D_MODEL), const2),
            pl.BlockSpec((1, D_MODEL), const2),
            pl.BlockSpec((None, None, PAST_LEN, NA_W), lambda b, t: (b, layer, 0, 0)),
            pl.BlockSpec((None, None, PAST_LEN, NA_W), lambda b, t: (b, layer, 0, 0)),
            pl.BlockSpec((None, None, 2, RET_HEADS, HEAD_DIM, HEAD_DIM),
                         lambda b, t: (b, layer, 0, 0, 0, 0)),
            pl.BlockSpec((None, None, PAST_LEN, DIFF_W), lambda b, t: (b, layer, 0, 0)),
            pl.BlockSpec((None, None, PAST_LEN, DIFF_W), lambda b, t: (b, layer, 0, 0)),
            pl.BlockSpec((None, NA_HEADS, NA_Q_TILE, NA_KEYS), variant, pipeline_mode=pl.Buffered(1)),
            pl.BlockSpec((DEC_SEQ, DIFF_W), const2),
            pl.BlockSpec((DEC_SEQ, DIFF_W), const2),
        ],
        out_specs=pl.BlockSpec((NA_Q_TILE, D_MODEL), qrow),
        out_shape=jax.ShapeDtypeStruct((DEC_BATCH * DEC_SEQ, D_MODEL), F32),
        scratch_shapes=[
            pltpu.VMEM((NA_Q_TILE, MIX_W), F32),
            pltpu.VMEM((DEC_SEQ, DIFF_W), BF16),
        ],
        compiler_params=_cparams(("arbitrary", "arbitrary")),
        name="lat_mix",
    )(log_g, lam, q_arr, kv_arr, x2d, mod_l, w_out_bf, gnw, gnb, dnw, lnw, lnb,
      cache_na_k, cache_na_v, state_ret, cache_diff_k, cache_diff_v, bias_tab, cos_tab, sin_tab)


MOE_PART = 2048
MOE_TILE = 512
MOE_EB = 4
MOE_CH = 64
MOE_STEPS = N_EXPERTS // MOE_EB
MOE_ROUTE_ROWS = 40


def _route_transposed(lt):
    shape = lt.shape
    r = lax.broadcasted_iota(jnp.int32, shape, 0).astype(F32)
    ninf = -jnp.inf
    is_g = jnp.where(r >= N_EXPERTS, jnp.where(r < N_EXPERTS + N_GROUPS, 1.0, 0.0), 0.0) > 0.5
    gl = jnp.where(is_g, lt, ninf)
    gmax = jnp.max(gl, axis=0, keepdims=True)
    gsel = jnp.min(jnp.where(gl == gmax, r - N_EXPERTS, 1e9), axis=0, keepdims=True)
    gsum = jnp.sum(jnp.where(is_g, jnp.exp(gl - gmax), 0.0), axis=0, keepdims=True)
    gw = 1.0 / gsum
    lo = gsel * EXPERTS_PER_GROUP
    is_e = jnp.where(r >= lo, jnp.where(r < lo + EXPERTS_PER_GROUP, 1.0, 0.0), 0.0) > 0.5
    el = jnp.where(is_e, lt, ninf)
    v1 = jnp.max(el, axis=0, keepdims=True)
    i1 = jnp.min(jnp.where(el == v1, r, 1e9), axis=0, keepdims=True)
    el2 = jnp.where(r == i1, ninf, el)
    v2 = jnp.max(el2, axis=0, keepdims=True)
    i2 = jnp.min(jnp.where(el2 == v2, r, 1e9), axis=0, keepdims=True)
    t = jnp.exp(v2 - v1)
    w1 = gw / (1.0 + t)
    w2 = gw * t / (1.0 + t)
    first = r == i1
    second = r == i2
    gates = jnp.where(first, w1, 0.0) + jnp.where(second, w2, 0.0)
    member = jnp.where(first, 1.0, jnp.where(second, 1.0, 0.0))
    return gates, member


def _moe_kernel(x_ref, mod_ref, wr_ref, wg_ref, wu_ref, wd_ref, lnw_ref, lnb_ref, out_ref,
                h_ref, rank_ref, gate_ref, sel_ref, xs_ref, ys_ref):
    s = pl.program_id(1)
    n_tiles = MOE_PART // MOE_TILE

    @pl.when(s == 0)
    def _():
        m = mod_ref[...]
        h = x_ref[...] * (1.0 + m[4:5]) + m[3:4]
        h_ref[...] = h.astype(BF16)
        lt = lax.dot_general(wr_ref[...], h, (((1,), (1,)), ((), ())),
                             preferred_element_type=F32, precision=lax.Precision.HIGHEST)
        gates, member = _route_transposed(lt)
        before = jnp.where(lax.broadcasted_iota(jnp.int32, (MOE_TILE, MOE_TILE), 0)
                           < lax.broadcasted_iota(jnp.int32, (MOE_TILE, MOE_TILE), 1), 1.0, 0.0).astype(BF16)
        rank_ref[...] = jnp.full(rank_ref.shape, -1.0, F32)
        gate_ref[...] = jnp.zeros(gate_ref.shape, F32)
        for t in range(n_tiles):
            c0 = t * MOE_TILE
            mem_t = member[0:N_EXPERTS, c0:c0 + MOE_TILE]
            cnt = _dot(mem_t.astype(BF16), before)
            rank = jnp.where(mem_t > 0.5, cnt, -1.0)
            for st in range(MOE_STEPS):
                rank_ref[st, 0:MOE_EB, c0:c0 + MOE_TILE] = rank[st * MOE_EB:(st + 1) * MOE_EB]
                gate_ref[st, 0:MOE_EB, c0:c0 + MOE_TILE] = gates[st * MOE_EB:(st + 1) * MOE_EB, c0:c0 + MOE_TILE]
        out_ref[...] = jnp.zeros(out_ref.shape, F32)

    ranks = rank_ref[s, 0:MOE_EB, :]
    gts = gate_ref[s, 0:MOE_EB, :]
    n_chunks = (jnp.max(ranks) * (1.0 / MOE_CH)).astype(jnp.int32) + 1

    def chunk_body(k, carry):
        slot = (lax.broadcasted_iota(jnp.int32, (MOE_CH, MOE_TILE), 0) + k * MOE_CH).astype(F32)
        row_gate = [[] for _ in range(MOE_EB)]
        for t in range(n_tiles):
            c0 = t * MOE_TILE
            onehots = []
            for i in range(MOE_EB):
                hit = ranks[i:i + 1, c0:c0 + MOE_TILE] == slot
                onehots.append(jnp.where(hit, 1.0, 0.0).astype(BF16))
                row_gate[i].append(jnp.sum(jnp.where(hit, gts[i:i + 1, c0:c0 + MOE_TILE], 0.0),
                                           axis=1, keepdims=True))
            sel = jnp.concatenate(onehots, axis=0)
            sel_ref[t] = sel
            xs = _dot(sel, h_ref[c0:c0 + MOE_TILE, :]).astype(BF16)
            for i in range(MOE_EB):
                xs_ref[i, t * MOE_CH:(t + 1) * MOE_CH, :] = xs[i * MOE_CH:(i + 1) * MOE_CH]
        for i in range(MOE_EB):
            xi = xs_ref[i]
            a = _dot(xi, wg_ref[i])
            u = _dot(xi, wu_ref[i])
            hm = (_silu(a) * u * jnp.concatenate(row_gate[i], axis=0)).astype(BF16)
            ys_ref[i] = _dot(hm, wd_ref[i]).astype(BF16)
        for t in range(n_tiles):
            c0 = t * MOE_TILE
            y = jnp.concatenate([ys_ref[i, t * MOE_CH:(t + 1) * MOE_CH, :] for i in range(MOE_EB)], axis=0)
            out_ref[c0:c0 + MOE_TILE, :] += _dot_tn(sel_ref[t], y)
        return carry

    lax.fori_loop(0, n_chunks, chunk_body, 0)

    @pl.when(s == MOE_STEPS - 1)
    def _():
        m = mod_ref[...]
        z = ALPHA * x_ref[...] + m[5:6] * out_ref[...]
        out_ref[...] = _layer_norm_rows(z, lnw_ref[...], lnb_ref[...])


def _moe(x2d, mod_l, mod_row_fn, wr_t, wg_bf, wu_bf, wd_bf, lnw, lnb, layer, name):
    n = x2d.shape[0]
    row = lambda p, s: (p, 0)
    const2 = lambda p, s: (0, 0)
    wspec = lambda shape: pl.BlockSpec((None, MOE_EB) + shape, lambda p, s: (layer, s, 0, 0))
    return pl.pallas_call(
        _moe_kernel,
        grid=(n // MOE_PART, MOE_STEPS),
        in_specs=[
            pl.BlockSpec((MOE_PART, D_MODEL), row, pipeline_mode=pl.Buffered(1)),
            pl.BlockSpec((None, 6, D_MODEL), lambda p, s: (mod_row_fn(p * MOE_PART), 0, 0)),
            pl.BlockSpec((MOE_ROUTE_ROWS, D_MODEL), const2),
            wspec((D_MODEL, EXPERT_FF)),
            wspec((D_MODEL, EXPERT_FF)),
            wspec((EXPERT_FF, D_MODEL)),
            pl.BlockSpec((1, D_MODEL), const2),
            pl.BlockSpec((1, D_MODEL), const2),
        ],
        out_specs=pl.BlockSpec((MOE_PART, D_MODEL), row),
        out_shape=jax.ShapeDtypeStruct((n, D_MODEL), F32),
        scratch_shapes=[
            pltpu.VMEM((MOE_PART, D_MODEL), BF16),
            pltpu.VMEM((MOE_STEPS, 8, MOE_PART), F32),
            pltpu.VMEM((MOE_STEPS, 8, MOE_PART), F32),
            pltpu.VMEM((MOE_PART // MOE_TILE, MOE_EB * MOE_CH, MOE_TILE), BF16),
            pltpu.VMEM((MOE_EB, MOE_PART // MOE_TILE * MOE_CH, D_MODEL), BF16),
            pltpu.VMEM((MOE_EB, MOE_PART // MOE_TILE * MOE_CH, D_MODEL), BF16),
        ],
        compiler_params=_cparams(("arbitrary", "arbitrary")),
        name=name,
    )(x2d, mod_l, wr_t, wg_bf, wu_bf, wd_bf, lnw, lnb)


def _na_bias_tables(rel_bias):
    q_rows = NA_Q_TILE // GRID_W
    n_dr = 2 * NA_WIN_ROWS - 1
    pad_c = GRID_W - NA_WIN_COLS
    padded = jnp.pad(rel_bias.astype(F32), ((0, 0), (0, 0), (pad_c, pad_c)), mode="edge")
    col_tab = jnp.stack([padded[:, :, GRID_W - 1 - qc:2 * GRID_W - 1 - qc] for qc in range(GRID_W)], axis=2)
    pad_r = NA_KEY_ROWS - q_rows
    col_tab = jnp.pad(col_tab, ((0, 0), (pad_r, pad_r), (0, 0), (0, 0)))
    qi = np.arange(NA_Q_TILE)
    ki = np.arange(NA_KEYS)
    tabs = []
    for r0, ks in ((0, 0), (8, 4), (GRID_ROWS - q_rows, GRID_ROWS - NA_KEY_ROWS)):
        rows = []
        for a in range(q_rows):
            start = ks - r0 - a + (NA_WIN_ROWS - 1) + pad_r
            blk = col_tab[:, start:start + NA_KEY_ROWS]
            rows.append(blk.transpose(0, 2, 1, 3).reshape(NA_HEADS, GRID_W, NA_KEYS))
        b = jnp.concatenate(rows, axis=1)
        q_row = r0 + qi // GRID_W
        q_col = qi % GRID_W
        k_row = ks + ki // GRID_W
        k_col = ki % GRID_W
        row_start = np.clip(q_row - NA_WIN_ROWS // 2, 0, GRID_ROWS - NA_WIN_ROWS)
        row_ok = (k_row[None, :] >= row_start[:, None]) & (k_row[None, :] < row_start[:, None] + NA_WIN_ROWS)
        col_start = np.clip(q_col - NA_WIN_COLS // 2, 0, GRID_W - NA_WIN_COLS)
        col_ok = (k_col[None, :] >= col_start[:, None]) & (k_col[None, :] < col_start[:, None] + NA_WIN_COLS)
        dr = k_row[None, :] - q_row[:, None] + (NA_WIN_ROWS - 1)
        assert np.all((dr[row_ok] >= 0) & (dr[row_ok] < n_dr))
        tabs.append(jnp.where(jnp.asarray(row_ok & col_ok)[None], b * LOG2E, NEG_INF))
    return jnp.stack(tabs, axis=0)


def _rope_tables():
    n = DIFF_D // 4
    lane = np.arange(DIFF_W)
    d = lane % DIFF_D
    use_col = d >= DIFF_D // 2
    e = d % (DIFF_D // 2)
    f = e % n
    first = e < n
    t = jnp.arange(DEC_SEQ)
    pos = jnp.where(jnp.asarray(use_col)[None, :], (t % GRID_W)[:, None], (t // GRID_W)[:, None])
    freqs = ROPE_BASE ** (-jnp.arange(n, dtype=F32) / n)
    ang = pos.astype(F32) * freqs[jnp.asarray(f)][None, :]
    sign = jnp.asarray(np.where(first, -1.0, 1.0), dtype=F32)
    return jnp.cos(ang), jnp.sin(ang) * sign[None, :]


def kernel(x_prompt, x_sample, c, cache_na_k, cache_na_v, state_ret, cache_diff_k, cache_diff_v, c_ctx,
           w_mod, b_mod, w_in, na_rel_bias, ret_decay, ret_gn_w, ret_gn_b, diff_lambda, diff_norm_w,
           w_out, ln1_w, ln1_b, router_group, router_expert, exp_w_gate, exp_w_up, exp_w_down,
           ln2_w, ln2_b):
    n_ctx = BATCH * SEQ
    n_lat = DEC_BATCH * DEC_SEQ
    x_ctx = x_prompt.reshape(n_ctx, D_MODEL)
    x_lat = x_sample.reshape(n_lat, D_MODEL)

    cond8 = jnp.zeros((8, D_MODEL), F32).at[0].set(c_ctx).at[1:1 + DEC_BATCH].set(c)
    mod = _modulation(cond8, w_mod, b_mod)

    cache_na_k = cache_na_k.reshape(DEC_BATCH, DEPTH, PAST_LEN, NA_W)
    cache_na_v = cache_na_v.reshape(DEC_BATCH, DEPTH, PAST_LEN, NA_W)
    cache_diff_k = cache_diff_k.reshape(DEC_BATCH, DEPTH, PAST_LEN, DIFF_W)
    cache_diff_v = cache_diff_v.reshape(DEC_BATCH, DEPTH, PAST_LEN, DIFF_W)
    cos_tab, sin_tab = _rope_tables()

    ctx_row = lambda r: 0
    lat_row = lambda r: 1 + r // DEC_SEQ

    wg_bf = exp_w_gate.astype(BF16)
    wu_bf = exp_w_up.astype(BF16)
    wd_bf = exp_w_down.astype(BF16)

    caches = [jnp.zeros((BATCH, DEPTH, SEQ, w), F32) for w in (NA_W, NA_W, DIFF_W, DIFF_W)]
    states = jnp.zeros((BATCH, DEPTH, 2, RET_HEADS, HEAD_DIM, HEAD_DIM), F32)
    for l in range(DEPTH):
        lam_init = 0.8 - 0.6 * math.exp(-0.3 * l)
        lp = diff_lambda[l].astype(F32)
        lam = (jnp.exp(jnp.sum(lp[0] * lp[1])) - jnp.exp(jnp.sum(lp[2] * lp[3])) + lam_init).reshape(1)
        log_g = jax.nn.log_sigmoid(ret_decay[l].astype(F32)) * LOG2E
        w_in_bf = w_in[l].astype(BF16)
        w_out_bf = w_out[l].astype(BF16)
        gnw = ret_gn_w[l].reshape(1, RET_W)
        gnb = ret_gn_b[l].reshape(1, RET_W)
        dnw = diff_norm_w[l].reshape(1, DIFF_W)
        l1w = ln1_w[l].reshape(1, D_MODEL)
        l1b = ln1_b[l].reshape(1, D_MODEL)
        l2w = ln2_w[l].reshape(1, D_MODEL)
        l2b = ln2_b[l].reshape(1, D_MODEL)
        wr_t = jnp.zeros((MOE_ROUTE_ROWS, D_MODEL), F32)
        wr_t = wr_t.at[:N_EXPERTS].set(router_expert[l].T).at[N_EXPERTS:N_EXPERTS + N_GROUPS].set(router_group[l].T)
        bias_tab = _na_bias_tables(na_rel_bias[l])
        mod_l = mod[l]

        q_c, kv_c, *caches = _project(x_ctx, mod_l, w_in_bf, ctx_row, caches, l)
        x1_c, states = _ctx_mix(q_c, kv_c, x_ctx, mod_l, w_out_bf, log_g, lam, gnw, gnb, dnw, l1w, l1b,
                                states, l, lam_init)
        x_ctx = _moe(x1_c, mod_l, ctx_row, wr_t, wg_bf, wu_bf, wd_bf, l2w, l2b, l, "moe_ctx")

        q_l, kv_l = _project(x_lat, mod_l, w_in_bf, lat_row)
        x1_l = _lat_mix(q_l, kv_l, x_lat, mod_l, w_out_bf, log_g, lam, gnw, gnb, dnw, l1w, l1b,
                        cache_na_k, cache_na_v, state_ret, cache_diff_k, cache_diff_v,
                        bias_tab, cos_tab, sin_tab, l, lam_init)
        x_lat = _moe(x1_l, mod_l, lat_row, wr_t, wg_bf, wu_bf, wd_bf, l2w, l2b, l, "moe_lat")

    new_na_k, new_na_v, new_diff_k, new_diff_v = caches
    return (x_ctx.reshape(BATCH, SEQ, D_MODEL), x_lat.reshape(DEC_BATCH, DEC_SEQ, D_MODEL),
            new_na_k.reshape(BATCH, DEPTH, SEQ, NA_HEADS, HEAD_DIM),
            new_na_v.reshape(BATCH, DEPTH, SEQ, NA_HEADS, HEAD_DIM),
            states,
            new_diff_k.reshape(BATCH, DEPTH, SEQ, DIFF_HEADS, 2 * DIFF_D),
            new_diff_v.reshape(BATCH, DEPTH, SEQ, DIFF_HEADS, DIFF_DV))
```

```python
import functools
import math

import numpy as np
import jax
import jax.numpy as jnp
from jax import lax
from jax.experimental import pallas as pl
from jax.experimental.pallas import tpu as pltpu

D_MODEL = 1024
BATCH = 32
SEQ = 256
DEPTH = 2
DEC_BATCH = 2
DEC_SEQ = 2048
PAST_LEN = 512
GRID_W = 64
HEAD_DIM = 64
NA_HEADS = 6
NA_WIN_ROWS = 8
NA_WIN_COLS = 16
RET_HEADS = 6
DIFF_HEADS = 4
DIFF_D = 32
DIFF_DV = 64
NA_W = NA_HEADS * HEAD_DIM
RET_W = RET_HEADS * HEAD_DIM
DIFF_W = DIFF_HEADS * DIFF_DV
MIX_W = NA_W + RET_W + DIFF_W
IN_COLS = 3 * NA_W + 4 * RET_W + 3 * DIFF_W
N_GROUPS = 4
EXPERTS_PER_GROUP = 8
N_EXPERTS = N_GROUPS * EXPERTS_PER_GROUP
EXPERT_FF = 256
ROPE_BASE = 10000.0
LN_EPS = 1e-5
NEG_INF = -1e30
ALPHA = (2.0 * DEPTH) ** 0.25
LOG2E = math.log2(math.e)

F32 = jnp.float32
BF16 = jnp.bfloat16

Q_NA, Q_RET, Q_GATE, Q_DIFF = 0, NA_W, NA_W + RET_W, NA_W + 2 * RET_W
Q_COLS = NA_W + 2 * RET_W + DIFF_W
KV_NAK, KV_NAV = 0, NA_W
KV_RETK, KV_RETV = 2 * NA_W, 2 * NA_W + RET_W
KV_DK, KV_DV = 2 * NA_W + 2 * RET_W, 2 * NA_W + 2 * RET_W + DIFF_W
KV_COLS = 2 * NA_W + 2 * RET_W + 2 * DIFF_W
MIX_RET, MIX_DIFF = NA_W, NA_W + RET_W

NA_Q_TILE = 256
NA_KEY_ROWS = 12
NA_KEYS = NA_KEY_ROWS * GRID_W
CTX_PROBLEMS = NA_HEADS + 2 * DIFF_HEADS + RET_HEADS
GRID_ROWS = DEC_SEQ // GRID_W

VMEM_LIMIT = 60 * 1024 * 1024


def _cparams(sem):
    return pltpu.CompilerParams(dimension_semantics=sem, vmem_limit_bytes=VMEM_LIMIT)


def _dot(a, b):
    return jnp.dot(a, b, preferred_element_type=F32)


def _dot_nt(a, b):
    return lax.dot_general(a, b, (((1,), (1,)), ((), ())), preferred_element_type=F32)


def _dot_tn(a, b):
    return lax.dot_general(a, b, (((0,), (0,)), ((), ())), preferred_element_type=F32)


def _silu(x):
    return x / (1.0 + jnp.exp(-x))


def _layer_norm_rows(z, w, b):
    mu = jnp.mean(z, axis=-1, keepdims=True)
    zc = z - mu
    var = jnp.mean(zc * zc, axis=-1, keepdims=True)
    return zc * lax.rsqrt(var + LN_EPS) * w + b


def _mod_kernel(cond_ref, w_ref, b_ref, o_ref):
    c = cond_ref[...]
    s = _silu(c)
    o_ref[...] = jnp.dot(s, w_ref[...], preferred_element_type=F32,
                         precision=lax.Precision.HIGHEST) + b_ref[...]


def _modulation(cond8, w_mod, b_mod):
    nj = 6
    out = pl.pallas_call(
        _mod_kernel,
        grid=(DEPTH, nj),
        in_specs=[
            pl.BlockSpec((8, D_MODEL), lambda l, j: (0, 0)),
            pl.BlockSpec((None, D_MODEL, D_MODEL), lambda l, j: (l, 0, j)),
            pl.BlockSpec((None, 1, D_MODEL), lambda l, j: (l, 0, j)),
        ],
        out_specs=pl.BlockSpec((None, 8, D_MODEL), lambda l, j: (l, 0, j)),
        out_shape=jax.ShapeDtypeStruct((DEPTH, 8, 6 * D_MODEL), F32),
        compiler_params=_cparams(("arbitrary", "arbitrary")),
        name="modulation",
    )(cond8, w_mod, b_mod.reshape(DEPTH, 1, 6 * D_MODEL))
    return out.reshape(DEPTH, 8, 6, D_MODEL)


def _proj_kernel(x_ref, mod_ref, w_ref, *refs):
    if len(refs) == 2:
        q_ref, kv_ref = refs
        cache_refs = None
    else:
        q_ref, kv_ref = refs[4:6]
        cache_refs = refs[6:]
    m = mod_ref[...]
    h = x_ref[...] * (1.0 + m[1:2]) + m[0:1]
    p = _dot(h.astype(BF16), w_ref[...])
    o = 0
    na_q = p[:, o:o + NA_W] * (HEAD_DIM ** -0.5 * LOG2E); o += NA_W
    na_k = p[:, o:o + NA_W]; o += NA_W
    na_v = p[:, o:o + NA_W]; o += NA_W
    ret_q = p[:, o:o + RET_W]; o += RET_W
    ret_k = p[:, o:o + RET_W] * (HEAD_DIM ** -0.5); o += RET_W
    ret_v = p[:, o:o + RET_W]; o += RET_W
    ret_g = p[:, o:o + RET_W]; o += RET_W
    dq = p[:, o:o + DIFF_W] * (DIFF_D ** -0.5 * LOG2E); o += DIFF_W
    dk = p[:, o:o + DIFF_W]; o += DIFF_W
    dv = p[:, o:o + DIFF_W]
    q_ref[:, Q_NA:Q_NA + NA_W] = na_q.astype(BF16)
    q_ref[:, Q_RET:Q_RET + RET_W] = ret_q.astype(BF16)
    q_ref[:, Q_GATE:Q_GATE + RET_W] = ret_g.astype(BF16)
    q_ref[:, Q_DIFF:Q_DIFF + DIFF_W] = dq.astype(BF16)
    kv_ref[:, KV_NAK:KV_NAK + NA_W] = na_k.astype(BF16)
    kv_ref[:, KV_NAV:KV_NAV + NA_W] = na_v.astype(BF16)
    kv_ref[:, KV_RETK:KV_RETK + RET_W] = ret_k.astype(BF16)
    kv_ref[:, KV_RETV:KV_RETV + RET_W] = ret_v.astype(BF16)
    kv_ref[:, KV_DK:KV_DK + DIFF_W] = dk.astype(BF16)
    kv_ref[:, KV_DV:KV_DV + DIFF_W] = dv.astype(BF16)
    if cache_refs is not None:
        nb = x_ref.shape[0] // SEQ
        for ref, val in zip(cache_refs, (na_k, na_v, dk, dv)):
            ref[...] = val.reshape(nb, SEQ, val.shape[-1])


def _project(x2d, mod_l, w_in_bf, mod_row_fn, caches=None, layer=0, tm=512):
    n = x2d.shape[0]
    row = lambda i: (i, 0)
    out_shape = [jax.ShapeDtypeStruct((n, Q_COLS), BF16), jax.ShapeDtypeStruct((n, KV_COLS), BF16)]
    out_specs = [pl.BlockSpec((tm, Q_COLS), row), pl.BlockSpec((tm, KV_COLS), row)]
    in_specs = [
        pl.BlockSpec((tm, D_MODEL), row),
        pl.BlockSpec((None, 6, D_MODEL), lambda i: (mod_row_fn(i * tm), 0, 0)),
        pl.BlockSpec((D_MODEL, IN_COLS), lambda i: (0, 0)),
    ]
    args = [x2d, mod_l, w_in_bf]
    aliases = {}
    if caches is not None:
        for k, arr in enumerate(caches):
            in_specs.append(pl.BlockSpec(memory_space=pl.ANY))
            args.append(arr)
            aliases[3 + k] = 2 + k
            out_shape.append(jax.ShapeDtypeStruct(arr.shape, F32))
            out_specs.append(pl.BlockSpec((tm // SEQ, None, SEQ, arr.shape[-1]), lambda i: (i, layer, 0, 0)))
    return pl.pallas_call(
        _proj_kernel,
        grid=(n // tm,),
        in_specs=in_specs,
        out_specs=out_specs,
        out_shape=out_shape,
        input_output_aliases=aliases,
        compiler_params=_cparams(("arbitrary",)),
        name="proj_lat" if caches is None else "proj_ctx",
    )(*args)


def _softmax_pv(score_list, v_list):
    m = None
    for s in score_list:
        mi = jnp.max(s, axis=-1, keepdims=True)
        m = mi if m is None else jnp.maximum(m, mi)
    l = None
    o = None
    for s, v in zip(score_list, v_list):
        e = jnp.exp2(s - m)
        li = jnp.sum(e, axis=-1, keepdims=True)
        oi = _dot(e.astype(BF16), v)
        l = li if l is None else l + li
        o = oi if o is None else o + oi
    return o / l


def _ret_finish(o, gate_bf, gnw, gnb):
    mu = jnp.mean(o, axis=-1, keepdims=True)
    oc = o - mu
    var = jnp.mean(oc * oc, axis=-1, keepdims=True)
    on = oc * lax.rsqrt(var + LN_EPS)
    return (on * gnw + gnb) * _silu(gate_bf.astype(F32))


def _diff_finish(o, dnw, one_minus_lam_init):
    ms = jnp.mean(o * o, axis=-1, keepdims=True)
    return o * lax.rsqrt(ms + LN_EPS) * dnw * one_minus_lam_init


def _decay_table(lg_ref, dm_ref):
    T = SEQ
    i = lax.broadcasted_iota(jnp.int32, (T, T), 0).astype(F32)
    j = lax.broadcasted_iota(jnp.int32, (T, T), 1).astype(F32)
    d = i - j
    for h in range(RET_HEADS):
        dm_ref[h * T:(h + 1) * T, :] = jnp.exp2(jnp.where(d >= 0, d * lg_ref[0, h], (-d) * lg_ref[1, h]))


def _out_proj_post_norm(mix_ref, wout_ref, x_ref, mod_ref, lnw_ref, lnb_ref, x1_ref):
    y = _dot(mix_ref[...].astype(BF16), wout_ref[...])
    m = mod_ref[...]
    z = ALPHA * x_ref[...] + m[2:3] * y
    x1_ref[...] = _layer_norm_rows(z, lnw_ref[...], lnb_ref[...])


def _ctx_mix_kernel(lg_ref, lam_ref, q_ref, kv_ref, x_ref, mod_ref, wout_ref, gnw_ref, gnb_ref,
                    dnw_ref, lnw_ref, lnb_ref, st_alias_ref, x1_ref, st_ref, mix_ref, dm_ref, s_ref, e_ref,
                    o_ref, *, one_minus_lam_init):
    T = SEQ
    n_soft = NA_HEADS + 2 * DIFF_HEADS

    @pl.when(pl.program_id(0) == 0)
    def _():
        _decay_table(lg_ref, dm_ref)

    for h in range(NA_HEADS):
        c = h * HEAD_DIM
        s_ref[h * T:(h + 1) * T, :] = _dot_nt(q_ref[:, Q_NA + c:Q_NA + c + HEAD_DIM],
                                              kv_ref[:, KV_NAK + c:KV_NAK + c + HEAD_DIM])
    for n in range(2 * DIFF_HEADS):
        c = n * DIFF_D
        r0 = (NA_HEADS + n) * T
        s_ref[r0:r0 + T, :] = _dot_nt(q_ref[:, Q_DIFF + c:Q_DIFF + c + DIFF_D],
                                      kv_ref[:, KV_DK + c:KV_DK + c + DIFF_D])
    for h in range(RET_HEADS):
        c = h * HEAD_DIM
        r0 = (n_soft + h) * T
        s_ref[r0:r0 + T, :] = _dot_nt(q_ref[:, Q_RET + c:Q_RET + c + HEAD_DIM],
                                      kv_ref[:, KV_RETK + c:KV_RETK + c + HEAD_DIM])

    s = s_ref[0:n_soft * T, :]
    e = jnp.exp2(s - jnp.max(s, axis=-1, keepdims=True))
    inv_l = 1.0 / jnp.sum(e, axis=-1, keepdims=True)
    e_ref[0:n_soft * T, :] = e.astype(BF16)
    e_ref[n_soft * T:, :] = (s_ref[n_soft * T:, :] * dm_ref[...]).astype(BF16)

    for h in range(NA_HEADS):
        c = h * HEAD_DIM
        o = _dot(e_ref[h * T:(h + 1) * T, :], kv_ref[:, KV_NAV + c:KV_NAV + c + HEAD_DIM])
        mix_ref[:, c:c + HEAD_DIM] = o * inv_l[h * T:(h + 1) * T]
    lam = lam_ref[0]
    for h in range(DIFF_HEADS):
        c = h * DIFF_DV
        v = kv_ref[:, KV_DV + c:KV_DV + c + DIFF_DV]
        r0 = (NA_HEADS + 2 * h) * T
        o1 = _dot(e_ref[r0:r0 + T, :], v) * inv_l[r0:r0 + T]
        o2 = _dot(e_ref[r0 + T:r0 + 2 * T, :], v) * inv_l[r0 + T:r0 + 2 * T]
        o_ref[(RET_HEADS + h) * T:(RET_HEADS + h + 1) * T, :] = o1 - lam * o2
    jj = lax.broadcasted_iota(jnp.int32, (T, HEAD_DIM), 0).astype(F32)
    for h in range(RET_HEADS):
        c = h * HEAD_DIM
        k = kv_ref[:, KV_RETK + c:KV_RETK + c + HEAD_DIM]
        v = kv_ref[:, KV_RETV + c:KV_RETV + c + HEAD_DIM]
        r0 = (n_soft + h) * T
        o_ref[h * T:(h + 1) * T, :] = _dot(e_ref[r0:r0 + T, :], v)
        kf = k.astype(F32)
        k_fwd = (kf * jnp.exp2((T - 1.0 - jj) * lg_ref[0, h])).astype(BF16)
        k_bwd = (kf * jnp.exp2(jj * lg_ref[1, h])).astype(BF16)
        st_ref[0, h] = _dot_tn(k_fwd, v)
        st_ref[1, h] = _dot_tn(k_bwd, v)

    ro = o_ref[0:RET_HEADS * T, :]
    mu = jnp.mean(ro, axis=-1, keepdims=True)
    rc = ro - mu
    rn = rc * lax.rsqrt(jnp.mean(rc * rc, axis=-1, keepdims=True) + LN_EPS)
    do = o_ref[RET_HEADS * T:, :]
    dn = do * lax.rsqrt(jnp.mean(do * do, axis=-1, keepdims=True) + LN_EPS)
    for h in range(RET_HEADS):
        c = h * HEAD_DIM
        g = q_ref[:, Q_GATE + c:Q_GATE + c + HEAD_DIM].astype(F32)
        mix_ref[:, MIX_RET + c:MIX_RET + c + HEAD_DIM] = (
            (rn[h * T:(h + 1) * T] * gnw_ref[:, c:c + HEAD_DIM] + gnb_ref[:, c:c + HEAD_DIM]) * _silu(g))
    for h in range(DIFF_HEADS):
        c = h * DIFF_DV
        mix_ref[:, MIX_DIFF + c:MIX_DIFF + c + DIFF_DV] = (
            dn[h * T:(h + 1) * T] * dnw_ref[:, c:c + DIFF_DV] * one_minus_lam_init)

    _out_proj_post_norm(mix_ref, wout_ref, x_ref, mod_ref, lnw_ref, lnb_ref, x1_ref)


def _ctx_mix(q_arr, kv_arr, x2d, mod_l, w_out_bf, log_g, lam, gnw, gnb, dnw, lnw, lnb, states, layer,
             lam_init):
    row = lambda b: (b, 0)
    const2 = lambda b: (0, 0)
    smem = pl.BlockSpec(memory_space=pltpu.SMEM)
    return pl.pallas_call(
        functools.partial(_ctx_mix_kernel, one_minus_lam_init=1.0 - lam_init),
        grid=(BATCH,),
        in_specs=[
            smem, smem,
            pl.BlockSpec((SEQ, Q_COLS), row),
            pl.BlockSpec((SEQ, KV_COLS), row),
            pl.BlockSpec((SEQ, D_MODEL), row),
            pl.BlockSpec((None, 6, D_MODEL), lambda b: (0, 0, 0)),
            pl.BlockSpec((MIX_W, D_MODEL), const2),
            pl.BlockSpec((1, RET_W), const2),
            pl.BlockSpec((1, RET_W), const2),
            pl.BlockSpec((1, DIFF_W), const2),
            pl.BlockSpec((1, D_MODEL), const2),
            pl.BlockSpec((1, D_MODEL), const2),
            pl.BlockSpec(memory_space=pl.ANY),
        ],
        out_specs=[
            pl.BlockSpec((SEQ, D_MODEL), row),
            pl.BlockSpec((None, None, 2, RET_HEADS, HEAD_DIM, HEAD_DIM), lambda b: (b, layer, 0, 0, 0, 0)),
        ],
        out_shape=[
            jax.ShapeDtypeStruct((BATCH * SEQ, D_MODEL), F32),
            jax.ShapeDtypeStruct(states.shape, F32),
        ],
        input_output_aliases={12: 1},
        scratch_shapes=[
            pltpu.VMEM((SEQ, MIX_W), F32),
            pltpu.VMEM((RET_HEADS * SEQ, SEQ), F32),
            pltpu.VMEM((CTX_PROBLEMS * SEQ, SEQ), F32),
            pltpu.VMEM((CTX_PROBLEMS * SEQ, SEQ), BF16),
            pltpu.VMEM(((RET_HEADS + DIFF_HEADS) * SEQ, HEAD_DIM), F32),
        ],
        compiler_params=_cparams(("arbitrary",)),
        name="ctx_mix",
    )(log_g, lam, q_arr, kv_arr, x2d, mod_l, w_out_bf, gnw, gnb, dnw, lnw, lnb, states)


def _rope(x, cos, sin_signed):
    n, w = x.shape
    lane = lax.broadcasted_iota(jnp.int32, (n, w), 1)
    first = (lane % 16) < 8
    partner = jnp.where(first, pltpu.roll(x, w - 8, 1), pltpu.roll(x, 8, 1))
    return x * cos + partner * sin_signed


def _lat_mix_kernel(lg_ref, lam_ref, q_ref, kv_ref, x_ref, mod_ref, wout_ref, gnw_ref, gnb_ref,
                    dnw_ref, lnw_ref, lnb_ref, cnak_ref, cnav_ref, st0_ref, cdk_ref, cdv_ref,
                    bias_ref, cos_ref, sin_ref, x1_ref, mix_ref, kr_ref, dm_ref, sf_ref, sb_ref, *,
                    one_minus_lam_init):
    TQ = NA_Q_TILE
    T = DEC_SEQ
    n_qt = T // TQ
    qt = pl.program_id(1)
    q0 = pl.multiple_of(qt * TQ, TQ)

    @pl.when(qt == 0)
    def _():
        kr = _rope(kv_ref[:, KV_DK:KV_DK + DIFF_W].astype(F32), cos_ref[...], sin_ref[...])
        kr_ref[...] = kr.astype(BF16)
        _decay_table(lg_ref, dm_ref)
        jl = lax.broadcasted_iota(jnp.int32, (TQ, HEAD_DIM), 0).astype(F32)
        for h in range(RET_HEADS):
            c = h * HEAD_DIM
            lf = lg_ref[0, h]
            lb = lg_ref[1, h]
            dec_f = jnp.exp2((TQ - 1.0 - jl) * lf)
            dec_b = jnp.exp2(jl * lb)
            tile_f = jnp.exp2(jnp.full((HEAD_DIM, HEAD_DIM), float(TQ), F32) * lf)
            tile_b = jnp.exp2(jnp.full((HEAD_DIM, HEAD_DIM), float(TQ), F32) * lb)
            loc_f, loc_b = [], []
            for t in range(n_qt):
                kf = kv_ref[t * TQ:(t + 1) * TQ, KV_RETK + c:KV_RETK + c + HEAD_DIM].astype(F32)
                v = kv_ref[t * TQ:(t + 1) * TQ, KV_RETV + c:KV_RETV + c + HEAD_DIM]
                loc_f.append(_dot_tn((kf * dec_f).astype(BF16), v))
                loc_b.append(_dot_tn((kf * dec_b).astype(BF16), v))
            state = st0_ref[0, h]
            for t in range(n_qt):
                sf_ref[t, h] = state.astype(BF16)
                state = state * tile_f + loc_f[t]
            state = st0_ref[1, h]
            for t in reversed(range(n_qt)):
                sb_ref[t, h] = state.astype(BF16)
                state = state * tile_b + loc_b[t]

    ks = jnp.clip(qt * (TQ // GRID_W) - NA_WIN_ROWS // 2, 0, GRID_ROWS - NA_KEY_ROWS)
    k0 = pl.multiple_of(ks * GRID_W, 256)
    for h in range(NA_HEADS):
        c = h * HEAD_DIM
        q = q_ref[:, Q_NA + c:Q_NA + c + HEAD_DIM]
        kw = kv_ref[pl.ds(k0, NA_KEYS), KV_NAK + c:KV_NAK + c + HEAD_DIM]
        vw = kv_ref[pl.ds(k0, NA_KEYS), KV_NAV + c:KV_NAV + c + HEAD_DIM]
        kc = cnak_ref[:, c:c + HEAD_DIM].astype(BF16)
        vc = cnav_ref[:, c:c + HEAD_DIM].astype(BF16)
        s_win = _dot_nt(q, kw) + bias_ref[h]
        s_ctx = _dot_nt(q, kc)
        mix_ref[:, c:c + HEAD_DIM] = _softmax_pv([s_win, s_ctx], [vw, vc])

    ii = lax.broadcasted_iota(jnp.int32, (TQ, HEAD_DIM), 0).astype(F32)
    for h in range(RET_HEADS):
        c = h * HEAD_DIM
        q = q_ref[:, Q_RET + c:Q_RET + c + HEAD_DIM]
        g = q_ref[:, Q_GATE + c:Q_GATE + c + HEAD_DIM]
        k = kv_ref[pl.ds(q0, TQ), KV_RETK + c:KV_RETK + c + HEAD_DIM]
        v = kv_ref[pl.ds(q0, TQ), KV_RETV + c:KV_RETV + c + HEAD_DIM]
        sc = _dot_nt(q, k) * dm_ref[h * TQ:(h + 1) * TQ, :]
        qf = q.astype(F32)
        q_fwd = (qf * jnp.exp2((ii + 1.0) * lg_ref[0, h])).astype(BF16)
        q_bwd = (qf * jnp.exp2((TQ - ii) * lg_ref[1, h])).astype(BF16)
        o = _dot(sc.astype(BF16), v) + _dot(q_fwd, sf_ref[qt, h]) + _dot(q_bwd, sb_ref[qt, h])
        mix_ref[:, MIX_RET + c:MIX_RET + c + HEAD_DIM] = _ret_finish(
            o, g, gnw_ref[:, c:c + HEAD_DIM], gnb_ref[:, c:c + HEAD_DIM])

    lam = lam_ref[0]
    qr = _rope(q_ref[:, Q_DIFF:Q_DIFF + DIFF_W].astype(F32),
               cos_ref[pl.ds(q0, TQ), :], sin_ref[pl.ds(q0, TQ), :]).astype(BF16)
    for h in range(DIFF_HEADS):
        c = h * DIFF_DV
        vl = kv_ref[:, KV_DV + c:KV_DV + c + DIFF_DV]
        vc = cdv_ref[:, c:c + DIFF_DV].astype(BF16)
        parts = []
        for i in range(2):
            lo = c + i * DIFF_D
            q = qr[:, lo:lo + DIFF_D]
            kl = kr_ref[:, lo:lo + DIFF_D]
            kc = cdk_ref[:, lo:lo + DIFF_D].astype(BF16)
            parts.append(_softmax_pv([_dot_nt(q, kc), _dot_nt(q, kl)], [vc, vl]))
        o = parts[0] - lam * parts[1]
        mix_ref[:, MIX_DIFF + c:MIX_DIFF + c + DIFF_DV] = _diff_finish(
            o, dnw_ref[:, c:c + DIFF_DV], one_minus_lam_init)

    _out_proj_post_norm(mix_ref, wout_ref, x_ref, mod_ref, lnw_ref, lnb_ref, x1_ref)


def _lat_mix(q_arr, kv_arr, x2d, mod_l, w_out_bf, log_g, lam, gnw, gnb, dnw, lnw, lnb,
             cache_na_k, cache_na_v, state_ret, cache_diff_k, cache_diff_v, bias_tab, cos_tab,
             sin_tab, layer, lam_init):
    nq = DEC_SEQ // NA_Q_TILE
    const2 = lambda b, t: (0, 0)
    smem = pl.BlockSpec(memory_space=pltpu.SMEM)
    qrow = lambda b, t: (b * nq + t, 0)

    def variant(b, t):
        return (jnp.where(t == 0, 0, jnp.where(t == nq - 1, 2, 1)), 0, 0, 0)

    return pl.pallas_call(
        functools.partial(_lat_mix_kernel, one_minus_lam_init=1.0 - lam_init),
        grid=(DEC_BATCH, nq),
        in_specs=[
            smem, smem,
            pl.BlockSpec((NA_Q_TILE, Q_COLS), qrow),
            pl.BlockSpec((DEC_SEQ, KV_COLS), lambda b, t: (b, 0), pipeline_mode=pl.Buffered(1)),
            pl.BlockSpec((NA_Q_TILE, D_MODEL), qrow),
            pl.BlockSpec((None, 6, D_MODEL), lambda b, t: (b + 1, 0, 0)),
            pl.BlockSpec((MIX_W, D_MODEL), const2),
            pl.BlockSpec((1, RET_W), const2),
            pl.BlockSpec((1, RET_W), const2),
            pl.BlockSpec((1, DIFF_W), const2),
            pl.BlockSpec((1, D_MODEL), const2),
            pl.BlockSpec((1, D_MODEL), const2),
            pl.BlockSpec((None, None, PAST_LEN, NA_W), lambda b, t: (b, layer, 0, 0)),
            pl.BlockSpec((None, None, PAST_LEN, NA_W), lambda b, t: (b, layer, 0, 0)),
            pl.BlockSpec((None, None, 2, RET_HEADS, HEAD_DIM, HEAD_DIM),
                         lambda b, t: (b, layer, 0, 0, 0, 0)),
            pl.BlockSpec((None, None, PAST_LEN, DIFF_W), lambda b, t: (b, layer, 0, 0)),
            pl.BlockSpec((None, None, PAST_LEN, DIFF_W), lambda b, t: (b, layer, 0, 0)),
            pl.BlockSpec((None, NA_HEADS, NA_Q_TILE, NA_KEYS), variant, pipeline_mode=pl.Buffered(1)),
            pl.BlockSpec((DEC_SEQ, DIFF_W), const2),
            pl.BlockSpec((DEC_SEQ, DIFF_W), const2),
        ],
        out_specs=pl.BlockSpec((NA_Q_TILE, D_MODEL), qrow),
        out_shape=jax.ShapeDtypeStruct((DEC_BATCH * DEC_SEQ, D_MODEL), F32),
        scratch_shapes=[
            pltpu.VMEM((NA_Q_TILE, MIX_W), F32),
            pltpu.VMEM((DEC_SEQ, DIFF_W), BF16),
            pltpu.VMEM((RET_HEADS * NA_Q_TILE, NA_Q_TILE), F32),
            pltpu.VMEM((DEC_SEQ // NA_Q_TILE, RET_HEADS, HEAD_DIM, HEAD_DIM), BF16),
            pltpu.VMEM((DEC_SEQ // NA_Q_TILE, RET_HEADS, HEAD_DIM, HEAD_DIM), BF16),
        ],
        compiler_params=_cparams(("arbitrary", "arbitrary")),
        name="lat_mix",
    )(log_g, lam, q_arr, kv_arr, x2d, mod_l, w_out_bf, gnw, gnb, dnw, lnw, lnb,
      cache_na_k, cache_na_v, state_ret, cache_diff_k, cache_diff_v, bias_tab, cos_tab, sin_tab)


MOE_PART = 2048
MOE_TILE = 512
MOE_EB = 4
MOE_CH = 64
MOE_STEPS = N_EXPERTS // MOE_EB
MOE_ROUTE_ROWS = 40


def _route_transposed(lt):
    shape = lt.shape
    r = lax.broadcasted_iota(jnp.int32, shape, 0).astype(F32)
    ninf = -jnp.inf
    is_g = jnp.where(r >= N_EXPERTS, jnp.where(r < N_EXPERTS + N_GROUPS, 1.0, 0.0), 0.0) > 0.5
    gl = jnp.where(is_g, lt, ninf)
    gmax = jnp.max(gl, axis=0, keepdims=True)
    gsel = jnp.min(jnp.where(gl == gmax, r - N_EXPERTS, 1e9), axis=0, keepdims=True)
    gsum = jnp.sum(jnp.where(is_g, jnp.exp(gl - gmax), 0.0), axis=0, keepdims=True)
    gw = 1.0 / gsum
    lo = gsel * EXPERTS_PER_GROUP
    is_e = jnp.where(r >= lo, jnp.where(r < lo + EXPERTS_PER_GROUP, 1.0, 0.0), 0.0) > 0.5
    el = jnp.where(is_e, lt, ninf)
    v1 = jnp.max(el, axis=0, keepdims=True)
    i1 = jnp.min(jnp.where(el == v1, r, 1e9), axis=0, keepdims=True)
    el2 = jnp.where(r == i1, ninf, el)
    v2 = jnp.max(el2, axis=0, keepdims=True)
    i2 = jnp.min(jnp.where(el2 == v2, r, 1e9), axis=0, keepdims=True)
    t = jnp.exp(v2 - v1)
    w1 = gw / (1.0 + t)
    w2 = gw * t / (1.0 + t)
    first = r == i1
    second = r == i2
    gates = jnp.where(first, w1, 0.0) + jnp.where(second, w2, 0.0)
    member = jnp.where(first, 1.0, jnp.where(second, 1.0, 0.0))
    return gates, member


def _moe_kernel(x_ref, mod_ref, wr_ref, wg_ref, wu_ref, wd_ref, lnw_ref, lnb_ref, out_ref,
                h_ref, rank_ref, gate_ref, sel_ref, xs_ref, ys_ref):
    s = pl.program_id(1)
    n_tiles = MOE_PART // MOE_TILE

    @pl.when(s == 0)
    def _():
        m = mod_ref[...]
        h = x_ref[...] * (1.0 + m[4:5]) + m[3:4]
        h_ref[...] = h.astype(BF16)
        lt = lax.dot_general(wr_ref[...], h, (((1,), (1,)), ((), ())),
                             preferred_element_type=F32, precision=lax.Precision.HIGHEST)
        gates, member = _route_transposed(lt)
        before = jnp.where(lax.broadcasted_iota(jnp.int32, (MOE_TILE, MOE_TILE), 0)
                           < lax.broadcasted_iota(jnp.int32, (MOE_TILE, MOE_TILE), 1), 1.0, 0.0).astype(BF16)
        rank_ref[...] = jnp.full(rank_ref.shape, -1.0, F32)
        gate_ref[...] = jnp.zeros(gate_ref.shape, F32)
        for t in range(n_tiles):
            c0 = t * MOE_TILE
            mem_t = member[0:N_EXPERTS, c0:c0 + MOE_TILE]
            cnt = _dot(mem_t.astype(BF16), before)
            rank = jnp.where(mem_t > 0.5, cnt, -1.0)
            for st in range(MOE_STEPS):
                rank_ref[st, 0:MOE_EB, c0:c0 + MOE_TILE] = rank[st * MOE_EB:(st + 1) * MOE_EB]
                gate_ref[st, 0:MOE_EB, c0:c0 + MOE_TILE] = gates[st * MOE_EB:(st + 1) * MOE_EB, c0:c0 + MOE_TILE]
        out_ref[...] = jnp.zeros(out_ref.shape, F32)

    ranks = rank_ref[s, 0:MOE_EB, :]
    gts = gate_ref[s, 0:MOE_EB, :]
    n_chunks = (jnp.max(ranks) * (1.0 / MOE_CH)).astype(jnp.int32) + 1

    def chunk_body(k, carry):
        slot = (lax.broadcasted_iota(jnp.int32, (MOE_CH, MOE_TILE), 0) + k * MOE_CH).astype(F32)
        row_gate = [[] for _ in range(MOE_EB)]
        for t in range(n_tiles):
            c0 = t * MOE_TILE
            onehots = []
            for i in range(MOE_EB):
                hit = ranks[i:i + 1, c0:c0 + MOE_TILE] == slot
                onehots.append(jnp.where(hit, 1.0, 0.0).astype(BF16))
                row_gate[i].append(jnp.sum(jnp.where(hit, gts[i:i + 1, c0:c0 + MOE_TILE], 0.0),
                                           axis=1, keepdims=True))
            sel = jnp.concatenate(onehots, axis=0)
            sel_ref[t] = sel
            xs = _dot(sel, h_ref[c0:c0 + MOE_TILE, :]).astype(BF16)
            for i in range(MOE_EB):
                xs_ref[i, t * MOE_CH:(t + 1) * MOE_CH, :] = xs[i * MOE_CH:(i + 1) * MOE_CH]
        for i in range(MOE_EB):
            xi = xs_ref[i]
            a = _dot(xi, wg_ref[i])
            u = _dot(xi, wu_ref[i])
            hm = (_silu(a) * u * jnp.concatenate(row_gate[i], axis=0)).astype(BF16)
            ys_ref[i] = _dot(hm, wd_ref[i]).astype(BF16)
        for t in range(n_tiles):
            c0 = t * MOE_TILE
            y = jnp.concatenate([ys_ref[i, t * MOE_CH:(t + 1) * MOE_CH, :] for i in range(MOE_EB)], axis=0)
            out_ref[c0:c0 + MOE_TILE, :] += _dot_tn(sel_ref[t], y)
        return carry

    lax.fori_loop(0, n_chunks, chunk_body, 0)

    @pl.when(s == MOE_STEPS - 1)
    def _():
        m = mod_ref[...]
        z = ALPHA * x_ref[...] + m[5:6] * out_ref[...]
        out_ref[...] = _layer_norm_rows(z, lnw_ref[...], lnb_ref[...])


def _moe(x2d, mod_l, mod_row_fn, wr_t, wg_bf, wu_bf, wd_bf, lnw, lnb, layer, name):
    n = x2d.shape[0]
    row = lambda p, s: (p, 0)
    const2 = lambda p, s: (0, 0)
    wspec = lambda shape: pl.BlockSpec((None, MOE_EB) + shape, lambda p, s: (layer, s, 0, 0))
    return pl.pallas_call(
        _moe_kernel,
        grid=(n // MOE_PART, MOE_STEPS),
        in_specs=[
            pl.BlockSpec((MOE_PART, D_MODEL), row, pipeline_mode=pl.Buffered(1)),
            pl.BlockSpec((None, 6, D_MODEL), lambda p, s: (mod_row_fn(p * MOE_PART), 0, 0)),
            pl.BlockSpec((MOE_ROUTE_ROWS, D_MODEL), const2),
            wspec((D_MODEL, EXPERT_FF)),
            wspec((D_MODEL, EXPERT_FF)),
            wspec((EXPERT_FF, D_MODEL)),
            pl.BlockSpec((1, D_MODEL), const2),
            pl.BlockSpec((1, D_MODEL), const2),
        ],
        out_specs=pl.BlockSpec((MOE_PART, D_MODEL), row),
        out_shape=jax.ShapeDtypeStruct((n, D_MODEL), F32),
        scratch_shapes=[
            pltpu.VMEM((MOE_PART, D_MODEL), BF16),
            pltpu.VMEM((MOE_STEPS, 8, MOE_PART), F32),
            pltpu.VMEM((MOE_STEPS, 8, MOE_PART), F32),
            pltpu.VMEM((MOE_PART // MOE_TILE, MOE_EB * MOE_CH, MOE_TILE), BF16),
            pltpu.VMEM((MOE_EB, MOE_PART // MOE_TILE * MOE_CH, D_MODEL), BF16),
            pltpu.VMEM((MOE_EB, MOE_PART // MOE_TILE * MOE_CH, D_MODEL), BF16),
        ],
        compiler_params=_cparams(("arbitrary", "arbitrary")),
        name=name,
    )(x2d, mod_l, wr_t, wg_bf, wu_bf, wd_bf, lnw, lnb)


def _na_bias_tables(rel_bias):
    q_rows = NA_Q_TILE // GRID_W
    n_dr = 2 * NA_WIN_ROWS - 1
    pad_c = GRID_W - NA_WIN_COLS
    padded = jnp.pad(rel_bias.astype(F32), ((0, 0), (0, 0), (pad_c, pad_c)), mode="edge")
    col_tab = jnp.stack([padded[:, :, GRID_W - 1 - qc:2 * GRID_W - 1 - qc] for qc in range(GRID_W)], axis=2)
    pad_r = NA_KEY_ROWS - q_rows
    col_tab = jnp.pad(col_tab, ((0, 0), (pad_r, pad_r), (0, 0), (0, 0)))
    qi = np.arange(NA_Q_TILE)
    ki = np.arange(NA_KEYS)
    tabs = []
    for r0, ks in ((0, 0), (8, 4), (GRID_ROWS - q_rows, GRID_ROWS - NA_KEY_ROWS)):
        rows = []
        for a in range(q_rows):
            start = ks - r0 - a + (NA_WIN_ROWS - 1) + pad_r
            blk = col_tab[:, start:start + NA_KEY_ROWS]
            rows.append(blk.transpose(0, 2, 1, 3).reshape(NA_HEADS, GRID_W, NA_KEYS))
        b = jnp.concatenate(rows, axis=1)
        q_row = r0 + qi // GRID_W
        q_col = qi % GRID_W
        k_row = ks + ki // GRID_W
        k_col = ki % GRID_W
        row_start = np.clip(q_row - NA_WIN_ROWS // 2, 0, GRID_ROWS - NA_WIN_ROWS)
        row_ok = (k_row[None, :] >= row_start[:, None]) & (k_row[None, :] < row_start[:, None] + NA_WIN_ROWS)
        col_start = np.clip(q_col - NA_WIN_COLS // 2, 0, GRID_W - NA_WIN_COLS)
        col_ok = (k_col[None, :] >= col_start[:, None]) & (k_col[None, :] < col_start[:, None] + NA_WIN_COLS)
        dr = k_row[None, :] - q_row[:, None] + (NA_WIN_ROWS - 1)
        assert np.all((dr[row_ok] >= 0) & (dr[row_ok] < n_dr))
        tabs.append(jnp.where(jnp.asarray(row_ok & col_ok)[None], b * LOG2E, NEG_INF))
    return jnp.stack(tabs, axis=0)


def _rope_tables():
    n = DIFF_D // 4
    lane = np.arange(DIFF_W)
    d = lane % DIFF_D
    use_col = d >= DIFF_D // 2
    e = d % (DIFF_D // 2)
    f = e % n
    first = e < n
    t = jnp.arange(DEC_SEQ)
    pos = jnp.where(jnp.asarray(use_col)[None, :], (t % GRID_W)[:, None], (t // GRID_W)[:, None])
    freqs = ROPE_BASE ** (-jnp.arange(n, dtype=F32) / n)
    ang = pos.astype(F32) * freqs[jnp.asarray(f)][None, :]
    sign = jnp.asarray(np.where(first, -1.0, 1.0), dtype=F32)
    return jnp.cos(ang), jnp.sin(ang) * sign[None, :]


def kernel(x_prompt, x_sample, c, cache_na_k, cache_na_v, state_ret, cache_diff_k, cache_diff_v, c_ctx,
           w_mod, b_mod, w_in, na_rel_bias, ret_decay, ret_gn_w, ret_gn_b, diff_lambda, diff_norm_w,
           w_out, ln1_w, ln1_b, router_group, router_expert, exp_w_gate, exp_w_up, exp_w_down,
           ln2_w, ln2_b):
    n_ctx = BATCH * SEQ
    n_lat = DEC_BATCH * DEC_SEQ
    x_ctx = x_prompt.reshape(n_ctx, D_MODEL)
    x_lat = x_sample.reshape(n_lat, D_MODEL)

    cond8 = jnp.zeros((8, D_MODEL), F32).at[0].set(c_ctx).at[1:1 + DEC_BATCH].set(c)
    mod = _modulation(cond8, w_mod, b_mod)

    cache_na_k = cache_na_k.reshape(DEC_BATCH, DEPTH, PAST_LEN, NA_W)
    cache_na_v = cache_na_v.reshape(DEC_BATCH, DEPTH, PAST_LEN, NA_W)
    cache_diff_k = cache_diff_k.reshape(DEC_BATCH, DEPTH, PAST_LEN, DIFF_W)
    cache_diff_v = cache_diff_v.reshape(DEC_BATCH, DEPTH, PAST_LEN, DIFF_W)
    cos_tab, sin_tab = _rope_tables()

    ctx_row = lambda r: 0
    lat_row = lambda r: 1 + r // DEC_SEQ

    wg_bf = exp_w_gate.astype(BF16)
    wu_bf = exp_w_up.astype(BF16)
    wd_bf = exp_w_down.astype(BF16)

    caches = [jnp.zeros((BATCH, DEPTH, SEQ, w), F32) for w in (NA_W, NA_W, DIFF_W, DIFF_W)]
    states = jnp.zeros((BATCH, DEPTH, 2, RET_HEADS, HEAD_DIM, HEAD_DIM), F32)
    for l in range(DEPTH):
        lam_init = 0.8 - 0.6 * math.exp(-0.3 * l)
        lp = diff_lambda[l].astype(F32)
        lam = (jnp.exp(jnp.sum(lp[0] * lp[1])) - jnp.exp(jnp.sum(lp[2] * lp[3])) + lam_init).reshape(1)
        log_g = jax.nn.log_sigmoid(ret_decay[l].astype(F32)) * LOG2E
        w_in_bf = w_in[l].astype(BF16)
        w_out_bf = w_out[l].astype(BF16)
        gnw = ret_gn_w[l].reshape(1, RET_W)
        gnb = ret_gn_b[l].reshape(1, RET_W)
        dnw = diff_norm_w[l].reshape(1, DIFF_W)
        l1w = ln1_w[l].reshape(1, D_MODEL)
        l1b = ln1_b[l].reshape(1, D_MODEL)
        l2w = ln2_w[l].reshape(1, D_MODEL)
        l2b = ln2_b[l].reshape(1, D_MODEL)
        wr_t = jnp.zeros((MOE_ROUTE_ROWS, D_MODEL), F32)
        wr_t = wr_t.at[:N_EXPERTS].set(router_expert[l].T).at[N_EXPERTS:N_EXPERTS + N_GROUPS].set(router_group[l].T)
        bias_tab = _na_bias_tables(na_rel_bias[l])
        mod_l = mod[l]

        q_c, kv_c, *caches = _project(x_ctx, mod_l, w_in_bf, ctx_row, caches, l)
        x1_c, states = _ctx_mix(q_c, kv_c, x_ctx, mod_l, w_out_bf, log_g, lam, gnw, gnb, dnw, l1w, l1b,
                                states, l, lam_init)
        x_ctx = _moe(x1_c, mod_l, ctx_row, wr_t, wg_bf, wu_bf, wd_bf, l2w, l2b, l, "moe_ctx")

        q_l, kv_l = _project(x_lat, mod_l, w_in_bf, lat_row)
        x1_l = _lat_mix(q_l, kv_l, x_lat, mod_l, w_out_bf, log_g, lam, gnw, gnb, dnw, l1w, l1b,
                        cache_na_k, cache_na_v, state_ret, cache_diff_k, cache_diff_v,
                        bias_tab, cos_tab, sin_tab, l, lam_init)
        x_lat = _moe(x1_l, mod_l, lat_row, wr_t, wg_bf, wu_bf, wd_bf, l2w, l2b, l, "moe_lat")

    new_na_k, new_na_v, new_diff_k, new_diff_v = caches
    return (x_ctx.reshape(BATCH, SEQ, D_MODEL), x_lat.reshape(DEC_BATCH, DEC_SEQ, D_MODEL),
            new_na_k.reshape(BATCH, DEPTH, SEQ, NA_HEADS, HEAD_DIM),
            new_na_v.reshape(BATCH, DEPTH, SEQ, NA_HEADS, HEAD_DIM),
            states,
            new_diff_k.reshape(BATCH, DEPTH, SEQ, DIFF_HEADS, 2 * DIFF_D),
            new_diff_v.reshape(BATCH, DEPTH, SEQ, DIFF_HEADS, DIFF_DV))
```

```python
import functools
import math

import numpy as np
import jax
import jax.numpy as jnp
from jax import lax
from jax.experimental import pallas as pl
from jax.experimental.pallas import tpu as pltpu

D_MODEL = 1024
BATCH = 32
SEQ = 256
DEPTH = 2
DEC_BATCH = 2
DEC_SEQ = 2048
PAST_LEN = 512
GRID_W = 64
HEAD_DIM = 64
NA_HEADS = 6
NA_WIN_ROWS = 8
NA_WIN_COLS = 16
RET_HEADS = 6
DIFF_HEADS = 4
DIFF_D = 32
DIFF_DV = 64
NA_W = NA_HEADS * HEAD_DIM
RET_W = RET_HEADS * HEAD_DIM
DIFF_W = DIFF_HEADS * DIFF_DV
MIX_W = NA_W + RET_W + DIFF_W
IN_COLS = 3 * NA_W + 4 * RET_W + 3 * DIFF_W
N_GROUPS = 4
EXPERTS_PER_GROUP = 8
N_EXPERTS = N_GROUPS * EXPERTS_PER_GROUP
EXPERT_FF = 256
ROPE_BASE = 10000.0
LN_EPS = 1e-5
NEG_INF = -1e30
ALPHA = (2.0 * DEPTH) ** 0.25
LOG2E = math.log2(math.e)

F32 = jnp.float32
BF16 = jnp.bfloat16

Q_NA, Q_RET, Q_GATE, Q_DIFF = 0, NA_W, NA_W + RET_W, NA_W + 2 * RET_W
Q_COLS = NA_W + 2 * RET_W + DIFF_W
KV_NAK, KV_NAV = 0, NA_W
KV_RETK, KV_RETV = 2 * NA_W, 2 * NA_W + RET_W
KV_DK, KV_DV = 2 * NA_W + 2 * RET_W, 2 * NA_W + 2 * RET_W + DIFF_W
KV_COLS = 2 * NA_W + 2 * RET_W + 2 * DIFF_W
MIX_RET, MIX_DIFF = NA_W, NA_W + RET_W

NA_Q_TILE = 256
NA_KEY_ROWS = 12
NA_KEYS = NA_KEY_ROWS * GRID_W
CTX_PROBLEMS = NA_HEADS + 2 * DIFF_HEADS + RET_HEADS
GRID_ROWS = DEC_SEQ // GRID_W

VMEM_LIMIT = 60 * 1024 * 1024


def _cparams(sem):
    return pltpu.CompilerParams(dimension_semantics=sem, vmem_limit_bytes=VMEM_LIMIT)


def _dot(a, b):
    return jnp.dot(a, b, preferred_element_type=F32)


def _dot_nt(a, b):
    return lax.dot_general(a, b, (((1,), (1,)), ((), ())), preferred_element_type=F32)


def _dot_tn(a, b):
    return lax.dot_general(a, b, (((0,), (0,)), ((), ())), preferred_element_type=F32)


def _silu(x):
    return x / (1.0 + jnp.exp(-x))


def _layer_norm_rows(z, w, b):
    mu = jnp.mean(z, axis=-1, keepdims=True)
    zc = z - mu
    var = jnp.mean(zc * zc, axis=-1, keepdims=True)
    return zc * lax.rsqrt(var + LN_EPS) * w + b


def _mod_kernel(cond_ref, w_ref, b_ref, o_ref):
    c = cond_ref[...]
    s = _silu(c)
    o_ref[...] = jnp.dot(s, w_ref[...], preferred_element_type=F32,
                         precision=lax.Precision.HIGHEST) + b_ref[...]


def _modulation(cond8, w_mod, b_mod):
    nj = 6
    out = pl.pallas_call(
        _mod_kernel,
        grid=(DEPTH, nj),
        in_specs=[
            pl.BlockSpec((8, D_MODEL), lambda l, j: (0, 0)),
            pl.BlockSpec((None, D_MODEL, D_MODEL), lambda l, j: (l, 0, j)),
            pl.BlockSpec((None, 1, D_MODEL), lambda l, j: (l, 0, j)),
        ],
        out_specs=pl.BlockSpec((None, 8, D_MODEL), lambda l, j: (l, 0, j)),
        out_shape=jax.ShapeDtypeStruct((DEPTH, 8, 6 * D_MODEL), F32),
        compiler_params=_cparams(("arbitrary", "arbitrary")),
        name="modulation",
    )(cond8, w_mod, b_mod.reshape(DEPTH, 1, 6 * D_MODEL))
    return out.reshape(DEPTH, 8, 6, D_MODEL)


def _proj_kernel(x_ref, mod_ref, w_ref, *refs):
    if len(refs) == 10:
        refs = refs[4:]
    q_ref, kv_ref = refs[:2]
    cache_refs = refs[2:]
    m = mod_ref[...]
    h = x_ref[...] * (1.0 + m[1:2]) + m[0:1]
    p = _dot(h.astype(BF16), w_ref[...])
    o = 0
    na_q = p[:, o:o + NA_W] * (HEAD_DIM ** -0.5 * LOG2E); o += NA_W
    na_k = p[:, o:o + NA_W]; o += NA_W
    na_v = p[:, o:o + NA_W]; o += NA_W
    ret_q = p[:, o:o + RET_W]; o += RET_W
    ret_k = p[:, o:o + RET_W] * (HEAD_DIM ** -0.5); o += RET_W
    ret_v = p[:, o:o + RET_W]; o += RET_W
    ret_g = p[:, o:o + RET_W]; o += RET_W
    dq = p[:, o:o + DIFF_W] * (DIFF_D ** -0.5 * LOG2E); o += DIFF_W
    dk = p[:, o:o + DIFF_W]; o += DIFF_W
    dv = p[:, o:o + DIFF_W]
    q_ref[:, Q_NA:Q_NA + NA_W] = na_q.astype(BF16)
    q_ref[:, Q_RET:Q_RET + RET_W] = ret_q.astype(BF16)
    q_ref[:, Q_GATE:Q_GATE + RET_W] = ret_g.astype(BF16)
    q_ref[:, Q_DIFF:Q_DIFF + DIFF_W] = dq.astype(BF16)
    kv_ref[:, KV_NAK:KV_NAK + NA_W] = na_k.astype(BF16)
    kv_ref[:, KV_NAV:KV_NAV + NA_W] = na_v.astype(BF16)
    kv_ref[:, KV_RETK:KV_RETK + RET_W] = ret_k.astype(BF16)
    kv_ref[:, KV_RETV:KV_RETV + RET_W] = ret_v.astype(BF16)
    kv_ref[:, KV_DK:KV_DK + DIFF_W] = dk.astype(BF16)
    kv_ref[:, KV_DV:KV_DV + DIFF_W] = dv.astype(BF16)
    nb = x_ref.shape[0] // SEQ
    for ref, val in zip(cache_refs, (na_k, na_v, dk, dv)):
        val = val.reshape(nb, SEQ, val.shape[-1])
        if len(ref.shape) == 4:
            ref[:, 0] = val
            ref[:, 1:] = jnp.zeros((nb, DEPTH - 1) + val.shape[1:], F32)
        else:
            ref[...] = val


def _project(x2d, mod, w_in_bf, mod_row_fn, layer, emit_caches=False, caches=None, tm=512):
    n = x2d.shape[0]
    row = lambda i: (i, 0)
    out_shape = [jax.ShapeDtypeStruct((n, Q_COLS), BF16), jax.ShapeDtypeStruct((n, KV_COLS), BF16)]
    out_specs = [pl.BlockSpec((tm, Q_COLS), row), pl.BlockSpec((tm, KV_COLS), row)]
    in_specs = [
        pl.BlockSpec((tm, D_MODEL), row),
        pl.BlockSpec((None, None, 6, D_MODEL), lambda i: (layer, mod_row_fn(i * tm), 0, 0)),
        pl.BlockSpec((None, D_MODEL, IN_COLS), lambda i: (layer, 0, 0)),
    ]
    args = [x2d, mod, w_in_bf]
    aliases = {}
    if emit_caches:
        for k, w in enumerate((NA_W, NA_W, DIFF_W, DIFF_W)):
            out_shape.append(jax.ShapeDtypeStruct((BATCH, DEPTH, SEQ, w), F32))
            if caches is None:
                out_specs.append(pl.BlockSpec((tm // SEQ, DEPTH, SEQ, w), lambda i: (i, 0, 0, 0)))
            else:
                in_specs.append(pl.BlockSpec(memory_space=pl.ANY))
                args.append(caches[k])
                aliases[3 + k] = 2 + k
                out_specs.append(pl.BlockSpec((tm // SEQ, None, SEQ, w), lambda i: (i, layer, 0, 0)))
    return pl.pallas_call(
        _proj_kernel,
        grid=(n // tm,),
        in_specs=in_specs,
        out_specs=out_specs,
        out_shape=out_shape,
        input_output_aliases=aliases,
        compiler_params=_cparams(("arbitrary",)),
        name="proj_ctx" if emit_caches else "proj_lat",
    )(*args)


def _softmax_pv(score_list, v_list):
    m = None
    for s in score_list:
        mi = jnp.max(s, axis=-1, keepdims=True)
        m = mi if m is None else jnp.maximum(m, mi)
    l = None
    o = None
    for s, v in zip(score_list, v_list):
        e = jnp.exp2(s - m)
        li = jnp.sum(e, axis=-1, keepdims=True)
        oi = _dot(e.astype(BF16), v)
        l = li if l is None else l + li
        o = oi if o is None else o + oi
    return o / l


def _ret_finish(o, gate_bf, gnw, gnb):
    mu = jnp.mean(o, axis=-1, keepdims=True)
    oc = o - mu
    var = jnp.mean(oc * oc, axis=-1, keepdims=True)
    on = oc * lax.rsqrt(var + LN_EPS)
    return (on * gnw + gnb) * _silu(gate_bf.astype(F32))


def _diff_finish(o, dnw, one_minus_lam_init):
    ms = jnp.mean(o * o, axis=-1, keepdims=True)
    return o * lax.rsqrt(ms + LN_EPS) * dnw * one_minus_lam_init


def _decay_table(lg_ref, dm_ref):
    T = SEQ
    i = lax.broadcasted_iota(jnp.int32, (T, T), 0).astype(F32)
    j = lax.broadcasted_iota(jnp.int32, (T, T), 1).astype(F32)
    d = i - j
    for h in range(RET_HEADS):
        dm_ref[h * T:(h + 1) * T, :] = jnp.exp2(jnp.where(d >= 0, d * lg_ref[0, h], (-d) * lg_ref[1, h]))


def _out_proj_post_norm(mix_ref, wout_ref, x_ref, mod_ref, lnw_ref, lnb_ref, x1_ref):
    y = _dot(mix_ref[...].astype(BF16), wout_ref[...])
    m = mod_ref[...]
    z = ALPHA * x_ref[...] + m[2:3] * y
    x1_ref[...] = _layer_norm_rows(z, lnw_ref[...], lnb_ref[...])


def _ctx_mix_kernel(lg_ref, lam_ref, q_ref, kv_ref, x_ref, mod_ref, wout_ref, gnw_ref, gnb_ref,
                    dnw_ref, lnw_ref, lnb_ref, *rest, one_minus_lam_init):
    if len(rest) == 8:
        rest = rest[1:]
    x1_ref, st_ref, mix_ref, dm_ref, s_ref, e_ref, o_ref = rest
    if len(st_ref.shape) == 5:
        st_ref[1:] = jnp.zeros((DEPTH - 1,) + st_ref.shape[1:], F32)
        st_ref = st_ref.at[0]
    T = SEQ
    n_soft = NA_HEADS + 2 * DIFF_HEADS

    @pl.when(pl.program_id(0) == 0)
    def _():
        _decay_table(lg_ref, dm_ref)

    for h in range(NA_HEADS):
        c = h * HEAD_DIM
        s_ref[h * T:(h + 1) * T, :] = _dot_nt(q_ref[:, Q_NA + c:Q_NA + c + HEAD_DIM],
                                              kv_ref[:, KV_NAK + c:KV_NAK + c + HEAD_DIM])
    for n in range(2 * DIFF_HEADS):
        c = n * DIFF_D
        r0 = (NA_HEADS + n) * T
        s_ref[r0:r0 + T, :] = _dot_nt(q_ref[:, Q_DIFF + c:Q_DIFF + c + DIFF_D],
                                      kv_ref[:, KV_DK + c:KV_DK + c + DIFF_D])
    for h in range(RET_HEADS):
        c = h * HEAD_DIM
        r0 = (n_soft + h) * T
        s_ref[r0:r0 + T, :] = _dot_nt(q_ref[:, Q_RET + c:Q_RET + c + HEAD_DIM],
                                      kv_ref[:, KV_RETK + c:KV_RETK + c + HEAD_DIM])

    s = s_ref[0:n_soft * T, :]
    e = jnp.exp2(s - jnp.max(s, axis=-1, keepdims=True))
    inv_l = 1.0 / jnp.sum(e, axis=-1, keepdims=True)
    e_ref[0:n_soft * T, :] = e.astype(BF16)
    e_ref[n_soft * T:, :] = (s_ref[n_soft * T:, :] * dm_ref[...]).astype(BF16)

    for h in range(NA_HEADS):
        c = h * HEAD_DIM
        o = _dot(e_ref[h * T:(h + 1) * T, :], kv_ref[:, KV_NAV + c:KV_NAV + c + HEAD_DIM])
        mix_ref[:, c:c + HEAD_DIM] = o * inv_l[h * T:(h + 1) * T]
    lam = lam_ref[0]
    for h in range(DIFF_HEADS):
        c = h * DIFF_DV
        v = kv_ref[:, KV_DV + c:KV_DV + c + DIFF_DV]
        r0 = (NA_HEADS + 2 * h) * T
        o1 = _dot(e_ref[r0:r0 + T, :], v) * inv_l[r0:r0 + T]
        o2 = _dot(e_ref[r0 + T:r0 + 2 * T, :], v) * inv_l[r0 + T:r0 + 2 * T]
        o_ref[(RET_HEADS + h) * T:(RET_HEADS + h + 1) * T, :] = o1 - lam * o2
    jj = lax.broadcasted_iota(jnp.int32, (T, HEAD_DIM), 0).astype(F32)
    for h in range(RET_HEADS):
        c = h * HEAD_DIM
        k = kv_ref[:, KV_RETK + c:KV_RETK + c + HEAD_DIM]
        v = kv_ref[:, KV_RETV + c:KV_RETV + c + HEAD_DIM]
        r0 = (n_soft + h) * T
        o_ref[h * T:(h + 1) * T, :] = _dot(e_ref[r0:r0 + T, :], v)
        kf = k.astype(F32)
        k_fwd = (kf * jnp.exp2((T - 1.0 - jj) * lg_ref[0, h])).astype(BF16)
        k_bwd = (kf * jnp.exp2(jj * lg_ref[1, h])).astype(BF16)
        st_ref[0, h] = _dot_tn(k_fwd, v)
        st_ref[1, h] = _dot_tn(k_bwd, v)

    ro = o_ref[0:RET_HEADS * T, :]
    mu = jnp.mean(ro, axis=-1, keepdims=True)
    rc = ro - mu
    rn = rc * lax.rsqrt(jnp.mean(rc * rc, axis=-1, keepdims=True) + LN_EPS)
    do = o_ref[RET_HEADS * T:, :]
    dn = do * lax.rsqrt(jnp.mean(do * do, axis=-1, keepdims=True) + LN_EPS)
    for h in range(RET_HEADS):
        c = h * HEAD_DIM
        g = q_ref[:, Q_GATE + c:Q_GATE + c + HEAD_DIM].astype(F32)
        mix_ref[:, MIX_RET + c:MIX_RET + c + HEAD_DIM] = (
            (rn[h * T:(h + 1) * T] * gnw_ref[:, c:c + HEAD_DIM] + gnb_ref[:, c:c + HEAD_DIM]) * _silu(g))
    for h in range(DIFF_HEADS):
        c = h * DIFF_DV
        mix_ref[:, MIX_DIFF + c:MIX_DIFF + c + DIFF_DV] = (
            dn[h * T:(h + 1) * T] * dnw_ref[:, c:c + DIFF_DV] * one_minus_lam_init)

    _out_proj_post_norm(mix_ref, wout_ref, x_ref, mod_ref, lnw_ref, lnb_ref, x1_ref)


def _ctx_mix(q_arr, kv_arr, x2d, mod, w_out_bf, log_g, lam, gnw, gnb, dnw, lnw, lnb, states, layer,
             lam_init):
    row = lambda b: (b, 0)
    const2 = lambda b: (0, 0)
    smem = pl.BlockSpec(memory_space=pltpu.SMEM)
    st_shape = (BATCH, DEPTH, 2, RET_HEADS, HEAD_DIM, HEAD_DIM)
    in_specs = [
        smem, smem,
        pl.BlockSpec((SEQ, Q_COLS), row),
        pl.BlockSpec((SEQ, KV_COLS), row),
        pl.BlockSpec((SEQ, D_MODEL), row),
        pl.BlockSpec((None, None, 6, D_MODEL), lambda b: (layer, 0, 0, 0)),
        pl.BlockSpec((None, MIX_W, D_MODEL), lambda b: (layer, 0, 0)),
        pl.BlockSpec((1, RET_W), const2),
        pl.BlockSpec((1, RET_W), const2),
        pl.BlockSpec((1, DIFF_W), const2),
        pl.BlockSpec((1, D_MODEL), const2),
        pl.BlockSpec((1, D_MODEL), const2),
    ]
    args = [log_g, lam, q_arr, kv_arr, x2d, mod, w_out_bf, gnw, gnb, dnw, lnw, lnb]
    if states is None:
        st_spec = pl.BlockSpec((None,) + st_shape[1:], lambda b: (b, 0, 0, 0, 0, 0))
        aliases = {}
    else:
        in_specs.append(pl.BlockSpec(memory_space=pl.ANY))
        args.append(states)
        st_spec = pl.BlockSpec((None, None) + st_shape[2:], lambda b: (b, layer, 0, 0, 0, 0))
        aliases = {12: 1}
    return pl.pallas_call(
        functools.partial(_ctx_mix_kernel, one_minus_lam_init=1.0 - lam_init),
        grid=(BATCH,),
        in_specs=in_specs,
        out_specs=[pl.BlockSpec((SEQ, D_MODEL), row), st_spec],
        out_shape=[
            jax.ShapeDtypeStruct((BATCH * SEQ, D_MODEL), F32),
            jax.ShapeDtypeStruct(st_shape, F32),
        ],
        input_output_aliases=aliases,
        scratch_shapes=[
            pltpu.VMEM((SEQ, MIX_W), F32),
            pltpu.VMEM((RET_HEADS * SEQ, SEQ), F32),
            pltpu.VMEM((CTX_PROBLEMS * SEQ, SEQ), F32),
            pltpu.VMEM((CTX_PROBLEMS * SEQ, SEQ), BF16),
            pltpu.VMEM(((RET_HEADS + DIFF_HEADS) * SEQ, HEAD_DIM), F32),
        ],
        compiler_params=_cparams(("arbitrary",)),
        name="ctx_mix",
    )(*args)


def _rope(x, cos, sin_signed):
    n, w = x.shape
    lane = lax.broadcasted_iota(jnp.int32, (n, w), 1)
    first = (lane % 16) < 8
    partner = jnp.where(first, pltpu.roll(x, w - 8, 1), pltpu.roll(x, 8, 1))
    return x * cos + partner * sin_signed


def _lat_mix_kernel(lg_ref, lam_ref, q_ref, kv_ref, x_ref, mod_ref, wout_ref, gnw_ref, gnb_ref,
                    dnw_ref, lnw_ref, lnb_ref, cnak_ref, cnav_ref, st0_ref, cdk_ref, cdv_ref,
                    bias_ref, cos_ref, sin_ref, x1_ref, mix_ref, kr_ref, dm_ref, sf_ref, sb_ref, *,
                    one_minus_lam_init):
    TQ = NA_Q_TILE
    T = DEC_SEQ
    n_qt = T // TQ
    qt = pl.program_id(1)
    q0 = pl.multiple_of(qt * TQ, TQ)

    @pl.when(qt == 0)
    def _():
        kr = _rope(kv_ref[:, KV_DK:KV_DK + DIFF_W].astype(F32), cos_ref[...], sin_ref[...])
        kr_ref[...] = kr.astype(BF16)
        _decay_table(lg_ref, dm_ref)
        jl = lax.broadcasted_iota(jnp.int32, (TQ, HEAD_DIM), 0).astype(F32)
        for h in range(RET_HEADS):
            c = h * HEAD_DIM
            lf = lg_ref[0, h]
            lb = lg_ref[1, h]
            dec_f = jnp.exp2((TQ - 1.0 - jl) * lf)
            dec_b = jnp.exp2(jl * lb)
            tile_f = jnp.exp2(jnp.full((HEAD_DIM, HEAD_DIM), float(TQ), F32) * lf)
            tile_b = jnp.exp2(jnp.full((HEAD_DIM, HEAD_DIM), float(TQ), F32) * lb)
            loc_f, loc_b = [], []
            for t in range(n_qt):
                kf = kv_ref[t * TQ:(t + 1) * TQ, KV_RETK + c:KV_RETK + c + HEAD_DIM].astype(F32)
                v = kv_ref[t * TQ:(t + 1) * TQ, KV_RETV + c:KV_RETV + c + HEAD_DIM]
                loc_f.append(_dot_tn((kf * dec_f).astype(BF16), v))
                loc_b.append(_dot_tn((kf * dec_b).astype(BF16), v))
            state = st0_ref[0, h]
            for t in range(n_qt):
                sf_ref[t, h] = state.astype(BF16)
                state = state * tile_f + loc_f[t]
            state = st0_ref[1, h]
            for t in reversed(range(n_qt)):
                sb_ref[t, h] = state.astype(BF16)
                state = state * tile_b + loc_b[t]

    ks = jnp.clip(qt * (TQ // GRID_W) - NA_WIN_ROWS // 2, 0, GRID_ROWS - NA_KEY_ROWS)
    k0 = pl.multiple_of(ks * GRID_W, 256)
    for h in range(NA_HEADS):
        c = h * HEAD_DIM
        q = q_ref[:, Q_NA + c:Q_NA + c + HEAD_DIM]
        kw = kv_ref[pl.ds(k0, NA_KEYS), KV_NAK + c:KV_NAK + c + HEAD_DIM]
        vw = kv_ref[pl.ds(k0, NA_KEYS), KV_NAV + c:KV_NAV + c + HEAD_DIM]
        kc = cnak_ref[:, c:c + HEAD_DIM].astype(BF16)
        vc = cnav_ref[:, c:c + HEAD_DIM].astype(BF16)
        s_win = _dot_nt(q, kw) + bias_ref[h]
        s_ctx = _dot_nt(q, kc)
        mix_ref[:, c:c + HEAD_DIM] = _softmax_pv([s_win, s_ctx], [vw, vc])

    ii = lax.broadcasted_iota(jnp.int32, (TQ, HEAD_DIM), 0).astype(F32)
    for h in range(RET_HEADS):
        c = h * HEAD_DIM
        q = q_ref[:, Q_RET + c:Q_RET + c + HEAD_DIM]
        g = q_ref[:, Q_GATE + c:Q_GATE + c + HEAD_DIM]
        k = kv_ref[pl.ds(q0, TQ), KV_RETK + c:KV_RETK + c + HEAD_DIM]
        v = kv_ref[pl.ds(q0, TQ), KV_RETV + c:KV_RETV + c + HEAD_DIM]
        sc = _dot_nt(q, k) * dm_ref[h * TQ:(h + 1) * TQ, :]
        qf = q.astype(F32)
        q_fwd = (qf * jnp.exp2((ii + 1.0) * lg_ref[0, h])).astype(BF16)
        q_bwd = (qf * jnp.exp2((TQ - ii) * lg_ref[1, h])).astype(BF16)
        o = _dot(sc.astype(BF16), v) + _dot(q_fwd, sf_ref[qt, h]) + _dot(q_bwd, sb_ref[qt, h])
        mix_ref[:, MIX_RET + c:MIX_RET + c + HEAD_DIM] = _ret_finish(
            o, g, gnw_ref[:, c:c + HEAD_DIM], gnb_ref[:, c:c + HEAD_DIM])

    lam = lam_ref[0]
    qr = _rope(q_ref[:, Q_DIFF:Q_DIFF + DIFF_W].astype(F32),
               cos_ref[pl.ds(q0, TQ), :], sin_ref[pl.ds(q0, TQ), :]).astype(BF16)
    for h in range(DIFF_HEADS):
        c = h * DIFF_DV
        vl = kv_ref[:, KV_DV + c:KV_DV + c + DIFF_DV]
        vc = cdv_ref[:, c:c + DIFF_DV].astype(BF16)
        parts = []
        for i in range(2):
            lo = c + i * DIFF_D
            q = qr[:, lo:lo + DIFF_D]
            kl = kr_ref[:, lo:lo + DIFF_D]
            kc = cdk_ref[:, lo:lo + DIFF_D].astype(BF16)
            parts.append(_softmax_pv([_dot_nt(q, kc), _dot_nt(q, kl)], [vc, vl]))
        o = parts[0] - lam * parts[1]
        mix_ref[:, MIX_DIFF + c:MIX_DIFF + c + DIFF_DV] = _diff_finish(
            o, dnw_ref[:, c:c + DIFF_DV], one_minus_lam_init)

    _out_proj_post_norm(mix_ref, wout_ref, x_ref, mod_ref, lnw_ref, lnb_ref, x1_ref)


def _lat_mix(q_arr, kv_arr, x2d, mod_l, w_out_bf, log_g, lam, gnw, gnb, dnw, lnw, lnb,
             cache_na_k, cache_na_v, state_ret, cache_diff_k, cache_diff_v, bias_tab, cos_tab,
             sin_tab, layer, lam_init):
    nq = DEC_SEQ // NA_Q_TILE
    const2 = lambda b, t: (0, 0)
    smem = pl.BlockSpec(memory_space=pltpu.SMEM)
    qrow = lambda b, t: (b * nq + t, 0)

    def variant(b, t):
        return (jnp.where(t == 0, 0, jnp.where(t == nq - 1, 2, 1)), 0, 0, 0)

    return pl.pallas_call(
        functools.partial(_lat_mix_kernel, one_minus_lam_init=1.0 - lam_init),
        grid=(DEC_BATCH, nq),
        in_specs=[
            smem, smem,
            pl.BlockSpec((NA_Q_TILE, Q_COLS), qrow),
            pl.BlockSpec((DEC_SEQ, KV_COLS), lambda b, t: (b, 0), pipeline_mode=pl.Buffered(1)),
            pl.BlockSpec((NA_Q_TILE, D_MODEL), qrow),
            pl.BlockSpec((None, None, 6, D_MODEL), lambda b, t: (layer, b + 1, 0, 0)),
            pl.BlockSpec((None, MIX_W, D_MODEL), lambda b, t: (layer, 0, 0)),
            pl.BlockSpec((1, RET_W), const2),
            pl.BlockSpec((1, RET_W), const2),
            pl.BlockSpec((1, DIFF_W), const2),
            pl.BlockSpec((1, D_MODEL), const2),
            pl.BlockSpec((1, D_MODEL), const2),
            pl.BlockSpec((None, None, PAST_LEN, NA_W), lambda b, t: (b, layer, 0, 0)),
            pl.BlockSpec((None, None, PAST_LEN, NA_W), lambda b, t: (b, layer, 0, 0)),
            pl.BlockSpec((None, None, 2, RET_HEADS, HEAD_DIM, HEAD_DIM),
                         lambda b, t: (b, layer, 0, 0, 0, 0)),
            pl.BlockSpec((None, None, PAST_LEN, DIFF_W), lambda b, t: (b, layer, 0, 0)),
            pl.BlockSpec((None, None, PAST_LEN, DIFF_W), lambda b, t: (b, layer, 0, 0)),
            pl.BlockSpec((None, NA_HEADS, NA_Q_TILE, NA_KEYS), variant, pipeline_mode=pl.Buffered(1)),
            pl.BlockSpec((DEC_SEQ, DIFF_W), const2),
            pl.BlockSpec((DEC_SEQ, DIFF_W), const2),
        ],
        out_specs=pl.BlockSpec((NA_Q_TILE, D_MODEL), qrow),
        out_shape=jax.ShapeDtypeStruct((DEC_BATCH * DEC_SEQ, D_MODEL), F32),
        scratch_shapes=[
            pltpu.VMEM((NA_Q_TILE, MIX_W), F32),
            pltpu.VMEM((DEC_SEQ, DIFF_W), BF16),
            pltpu.VMEM((RET_HEADS * NA_Q_TILE, NA_Q_TILE), F32),
            pltpu.VMEM((DEC_SEQ // NA_Q_TILE, RET_HEADS, HEAD_DIM, HEAD_DIM), BF16),
            pltpu.VMEM((DEC_SEQ // NA_Q_TILE, RET_HEADS, HEAD_DIM, HEAD_DIM), BF16),
        ],
        compiler_params=_cparams(("arbitrary", "arbitrary")),
        name="lat_mix",
    )(log_g, lam, q_arr, kv_arr, x2d, mod_l, w_out_bf, gnw, gnb, dnw, lnw, lnb,
      cache_na_k, cache_na_v, state_ret, cache_diff_k, cache_diff_v, bias_tab, cos_tab, sin_tab)


MOE_PART = 2048
MOE_TILE = 512
MOE_EB = 4
MOE_CH = 48
MOE_STEPS = N_EXPERTS // MOE_EB
MOE_ROUTE_ROWS = 40


def _route_transposed(lt):
    shape = lt.shape
    r = lax.broadcasted_iota(jnp.int32, shape, 0).astype(F32)
    ninf = -jnp.inf
    is_g = jnp.where(r >= N_EXPERTS, jnp.where(r < N_EXPERTS + N_GROUPS, 1.0, 0.0), 0.0) > 0.5
    gl = jnp.where(is_g, lt, ninf)
    gmax = jnp.max(gl, axis=0, keepdims=True)
    gsel = jnp.min(jnp.where(gl == gmax, r - N_EXPERTS, 1e9), axis=0, keepdims=True)
    gsum = jnp.sum(jnp.where(is_g, jnp.exp(gl - gmax), 0.0), axis=0, keepdims=True)
    gw = 1.0 / gsum
    lo = gsel * EXPERTS_PER_GROUP
    is_e = jnp.where(r >= lo, jnp.where(r < lo + EXPERTS_PER_GROUP, 1.0, 0.0), 0.0) > 0.5
    el = jnp.where(is_e, lt, ninf)
    v1 = jnp.max(el, axis=0, keepdims=True)
    i1 = jnp.min(jnp.where(el == v1, r, 1e9), axis=0, keepdims=True)
    el2 = jnp.where(r == i1, ninf, el)
    v2 = jnp.max(el2, axis=0, keepdims=True)
    i2 = jnp.min(jnp.where(el2 == v2, r, 1e9), axis=0, keepdims=True)
    t = jnp.exp(v2 - v1)
    w1 = gw / (1.0 + t)
    w2 = gw * t / (1.0 + t)
    first = r == i1
    second = r == i2
    gates = jnp.where(first, w1, 0.0) + jnp.where(second, w2, 0.0)
    member = jnp.where(first, 1.0, jnp.where(second, 1.0, 0.0))
    return gates, member


def _moe_kernel(x_ref, mod_ref, wr_ref, wg_ref, wu_ref, wd_ref, lnw_ref, lnb_ref, out_ref,
                h_ref, rank_ref, gate_ref, sel_ref, xs_ref, ys_ref):
    s = pl.program_id(1)
    n_tiles = MOE_PART // MOE_TILE

    @pl.when(s == 0)
    def _():
        m = mod_ref[...]
        h = x_ref[...] * (1.0 + m[4:5]) + m[3:4]
        h_hi = h.astype(BF16)
        h_ref[...] = h_hi
        h_lo = (h - h_hi.astype(F32)).astype(BF16)
        w = wr_ref[...]
        w_hi = w.astype(BF16)
        w_lo = (w - w_hi.astype(F32)).astype(BF16)
        lt = _dot_nt(w_hi, h_hi) + (_dot_nt(w_hi, h_lo) + _dot_nt(w_lo, h_hi))
        gates, member = _route_transposed(lt)
        before = jnp.where(lax.broadcasted_iota(jnp.int32, (MOE_TILE, MOE_TILE), 0)
                           < lax.broadcasted_iota(jnp.int32, (MOE_TILE, MOE_TILE), 1), 1.0, 0.0).astype(BF16)
        rank_ref[...] = jnp.full(rank_ref.shape, -1.0, F32)
        gate_ref[...] = jnp.zeros(gate_ref.shape, F32)
        for t in range(n_tiles):
            c0 = t * MOE_TILE
            mem_t = member[0:N_EXPERTS, c0:c0 + MOE_TILE]
            cnt = _dot(mem_t.astype(BF16), before)
            rank = jnp.where(mem_t > 0.5, cnt, -1.0)
            for st in range(MOE_STEPS):
                rank_ref[st, 0:MOE_EB, c0:c0 + MOE_TILE] = rank[st * MOE_EB:(st + 1) * MOE_EB]
                gate_ref[st, 0:MOE_EB, c0:c0 + MOE_TILE] = gates[st * MOE_EB:(st + 1) * MOE_EB, c0:c0 + MOE_TILE]
        out_ref[...] = jnp.zeros(out_ref.shape, F32)

    ranks = rank_ref[s, 0:MOE_EB, :]
    gts = gate_ref[s, 0:MOE_EB, :]
    n_chunks = ((jnp.max(ranks) + 0.5) * (1.0 / MOE_CH)).astype(jnp.int32) + 1

    def chunk_body(k, carry):
        slot = (lax.broadcasted_iota(jnp.int32, (MOE_CH, MOE_TILE), 0) + k * MOE_CH).astype(F32)
        row_gate = [[] for _ in range(MOE_EB)]
        for t in range(n_tiles):
            c0 = t * MOE_TILE
            onehots = []
            for i in range(MOE_EB):
                hit = ranks[i:i + 1, c0:c0 + MOE_TILE] == slot
                onehots.append(jnp.where(hit, 1.0, 0.0).astype(BF16))
                row_gate[i].append(jnp.sum(jnp.where(hit, gts[i:i + 1, c0:c0 + MOE_TILE], 0.0),
                                           axis=1, keepdims=True))
            sel = jnp.concatenate(onehots, axis=0)
            sel_ref[t] = sel
            xs = _dot(sel, h_ref[c0:c0 + MOE_TILE, :]).astype(BF16)
            for i in range(MOE_EB):
                xs_ref[i, t * MOE_CH:(t + 1) * MOE_CH, :] = xs[i * MOE_CH:(i + 1) * MOE_CH]
        for i in range(MOE_EB):
            xi = xs_ref[i]
            a = _dot(xi, wg_ref[i])
            u = _dot(xi, wu_ref[i])
            hm = (_silu(a) * u * jnp.concatenate(row_gate[i], axis=0)).astype(BF16)
            ys_ref[i] = _dot(hm, wd_ref[i]).astype(BF16)
        for t in range(n_tiles):
            c0 = t * MOE_TILE
            y = jnp.concatenate([ys_ref[i, t * MOE_CH:(t + 1) * MOE_CH, :] for i in range(MOE_EB)], axis=0)
            out_ref[c0:c0 + MOE_TILE, :] += _dot_tn(sel_ref[t], y)
        return carry

    lax.fori_loop(0, n_chunks, chunk_body, 0)

    @pl.when(s == MOE_STEPS - 1)
    def _():
        m = mod_ref[...]
        z = ALPHA * x_ref[...] + m[5:6] * out_ref[...]
        out_ref[...] = _layer_norm_rows(z, lnw_ref[...], lnb_ref[...])


def _moe(x2d, mod_l, mod_row_fn, wr_t, wg_bf, wu_bf, wd_bf, lnw, lnb, layer, name):
    n = x2d.shape[0]
    row = lambda p, s: (p, 0)
    const2 = lambda p, s: (0, 0)
    wspec = lambda shape: pl.BlockSpec((None, MOE_EB) + shape, lambda p, s: (layer, s, 0, 0))
    return pl.pallas_call(
        _moe_kernel,
        grid=(n // MOE_PART, MOE_STEPS),
        in_specs=[
            pl.BlockSpec((MOE_PART, D_MODEL), row, pipeline_mode=pl.Buffered(1)),
            pl.BlockSpec((None, None, 6, D_MODEL), lambda p, s: (layer, mod_row_fn(p * MOE_PART), 0, 0)),
            pl.BlockSpec((None, MOE_ROUTE_ROWS, D_MODEL), lambda p, s: (layer, 0, 0)),
            wspec((D_MODEL, EXPERT_FF)),
            wspec((D_MODEL, EXPERT_FF)),
            wspec((EXPERT_FF, D_MODEL)),
            pl.BlockSpec((1, D_MODEL), const2),
            pl.BlockSpec((1, D_MODEL), const2),
        ],
        out_specs=pl.BlockSpec((MOE_PART, D_MODEL), row),
        out_shape=jax.ShapeDtypeStruct((n, D_MODEL), F32),
        scratch_shapes=[
            pltpu.VMEM((MOE_PART, D_MODEL), BF16),
            pltpu.VMEM((MOE_STEPS, 8, MOE_PART), F32),
            pltpu.VMEM((MOE_STEPS, 8, MOE_PART), F32),
            pltpu.VMEM((MOE_PART // MOE_TILE, MOE_EB * MOE_CH, MOE_TILE), BF16),
            pltpu.VMEM((MOE_EB, MOE_PART // MOE_TILE * MOE_CH, D_MODEL), BF16),
            pltpu.VMEM((MOE_EB, MOE_PART // MOE_TILE * MOE_CH, D_MODEL), BF16),
        ],
        compiler_params=_cparams(("arbitrary", "arbitrary")),
        name=name,
    )(x2d, mod_l, wr_t, wg_bf, wu_bf, wd_bf, lnw, lnb)


def _na_bias_tables(rel_bias):
    q_rows = NA_Q_TILE // GRID_W
    n_dr = 2 * NA_WIN_ROWS - 1
    pad_c = GRID_W - NA_WIN_COLS
    padded = jnp.pad(rel_bias.astype(F32), ((0, 0), (0, 0), (pad_c, pad_c)), mode="edge")
    col_tab = jnp.stack([padded[:, :, GRID_W - 1 - qc:2 * GRID_W - 1 - qc] for qc in range(GRID_W)], axis=2)
    pad_r = NA_KEY_ROWS - q_rows
    col_tab = jnp.pad(col_tab, ((0, 0), (pad_r, pad_r), (0, 0), (0, 0)))
    qi = np.arange(NA_Q_TILE)
    ki = np.arange(NA_KEYS)
    tabs = []
    for r0, ks in ((0, 0), (8, 4), (GRID_ROWS - q_rows, GRID_ROWS - NA_KEY_ROWS)):
        rows = []
        for a in range(q_rows):
            start = ks - r0 - a + (NA_WIN_ROWS - 1) + pad_r
            blk = col_tab[:, start:start + NA_KEY_ROWS]
            rows.append(blk.transpose(0, 2, 1, 3).reshape(NA_HEADS, GRID_W, NA_KEYS))
        b = jnp.concatenate(rows, axis=1)
        q_row = r0 + qi // GRID_W
        q_col = qi % GRID_W
        k_row = ks + ki // GRID_W
        k_col = ki % GRID_W
        row_start = np.clip(q_row - NA_WIN_ROWS // 2, 0, GRID_ROWS - NA_WIN_ROWS)
        row_ok = (k_row[None, :] >= row_start[:, None]) & (k_row[None, :] < row_start[:, None] + NA_WIN_ROWS)
        col_start = np.clip(q_col - NA_WIN_COLS // 2, 0, GRID_W - NA_WIN_COLS)
        col_ok = (k_col[None, :] >= col_start[:, None]) & (k_col[None, :] < col_start[:, None] + NA_WIN_COLS)
        dr = k_row[None, :] - q_row[:, None] + (NA_WIN_ROWS - 1)
        assert np.all((dr[row_ok] >= 0) & (dr[row_ok] < n_dr))
        tabs.append(jnp.where(jnp.asarray(row_ok & col_ok)[None], b * LOG2E, NEG_INF))
    return jnp.stack(tabs, axis=0)


def _rope_tables():
    n = DIFF_D // 4
    lane = np.arange(DIFF_W)
    d = lane % DIFF_D
    use_col = d >= DIFF_D // 2
    e = d % (DIFF_D // 2)
    f = e % n
    first = e < n
    t = jnp.arange(DEC_SEQ)
    pos = jnp.where(jnp.asarray(use_col)[None, :], (t % GRID_W)[:, None], (t // GRID_W)[:, None])
    freqs = ROPE_BASE ** (-jnp.arange(n, dtype=F32) / n)
    ang = pos.astype(F32) * freqs[jnp.asarray(f)][None, :]
    sign = jnp.asarray(np.where(first, -1.0, 1.0), dtype=F32)
    return jnp.cos(ang), jnp.sin(ang) * sign[None, :]


def kernel(x_prompt, x_sample, c, cache_na_k, cache_na_v, state_ret, cache_diff_k, cache_diff_v, c_ctx,
           w_mod, b_mod, w_in, na_rel_bias, ret_decay, ret_gn_w, ret_gn_b, diff_lambda, diff_norm_w,
           w_out, ln1_w, ln1_b, router_group, router_expert, exp_w_gate, exp_w_up, exp_w_down,
           ln2_w, ln2_b):
    n_ctx = BATCH * SEQ
    n_lat = DEC_BATCH * DEC_SEQ
    x_ctx = x_prompt.reshape(n_ctx, D_MODEL)
    x_lat = x_sample.reshape(n_lat, D_MODEL)

    cond8 = jnp.concatenate([c_ctx[None, :], c, jnp.zeros((8 - 1 - DEC_BATCH, D_MODEL), F32)], axis=0)
    mod = _modulation(cond8, w_mod, b_mod)

    cache_na_k = cache_na_k.reshape(DEC_BATCH, DEPTH, PAST_LEN, NA_W)
    cache_na_v = cache_na_v.reshape(DEC_BATCH, DEPTH, PAST_LEN, NA_W)
    cache_diff_k = cache_diff_k.reshape(DEC_BATCH, DEPTH, PAST_LEN, DIFF_W)
    cache_diff_v = cache_diff_v.reshape(DEC_BATCH, DEPTH, PAST_LEN, DIFF_W)
    cos_tab, sin_tab = _rope_tables()

    ctx_row = lambda r: 0
    lat_row = lambda r: 1 + r // DEC_SEQ

    wg_bf = exp_w_gate.astype(BF16)
    wu_bf = exp_w_up.astype(BF16)
    wd_bf = exp_w_down.astype(BF16)

    w_in_bf = w_in.astype(BF16)
    w_out_bf = w_out.astype(BF16)
    wr_t = jnp.concatenate([jnp.swapaxes(router_expert, 1, 2), jnp.swapaxes(router_group, 1, 2),
                            jnp.zeros((DEPTH, MOE_ROUTE_ROWS - N_EXPERTS - N_GROUPS, D_MODEL), F32)], axis=1)
    log_g_all = jax.nn.log_sigmoid(ret_decay.astype(F32)) * LOG2E
    lp = diff_lambda.astype(F32)
    lam_dyn = jnp.exp(jnp.sum(lp[:, 0] * lp[:, 1], axis=-1)) - jnp.exp(jnp.sum(lp[:, 2] * lp[:, 3], axis=-1))

    caches = None
    states = None
    for l in range(DEPTH):
        lam_init = 0.8 - 0.6 * math.exp(-0.3 * l)
        lam = (lam_dyn[l] + lam_init).reshape(1)
        log_g = log_g_all[l]
        gnw = ret_gn_w[l].reshape(1, RET_W)
        gnb = ret_gn_b[l].reshape(1, RET_W)
        dnw = diff_norm_w[l].reshape(1, DIFF_W)
        l1w = ln1_w[l].reshape(1, D_MODEL)
        l1b = ln1_b[l].reshape(1, D_MODEL)
        l2w = ln2_w[l].reshape(1, D_MODEL)
        l2b = ln2_b[l].reshape(1, D_MODEL)
        bias_tab = _na_bias_tables(na_rel_bias[l])

        q_c, kv_c, *caches = _project(x_ctx, mod, w_in_bf, ctx_row, l, emit_caches=True, caches=caches)
        x1_c, states = _ctx_mix(q_c, kv_c, x_ctx, mod, w_out_bf, log_g, lam, gnw, gnb, dnw, l1w, l1b,
                                states, l, lam_init)
        x_ctx = _moe(x1_c, mod, ctx_row, wr_t, wg_bf, wu_bf, wd_bf, l2w, l2b, l, "moe_ctx")

        q_l, kv_l = _project(x_lat, mod, w_in_bf, lat_row, l)
        x1_l = _lat_mix(q_l, kv_l, x_lat, mod, w_out_bf, log_g, lam, gnw, gnb, dnw, l1w, l1b,
                        cache_na_k, cache_na_v, state_ret, cache_diff_k, cache_diff_v,
                        bias_tab, cos_tab, sin_tab, l, lam_init)
        x_lat = _moe(x1_l, mod, lat_row, wr_t, wg_bf, wu_bf, wd_bf, l2w, l2b, l, "moe_lat")

    new_na_k, new_na_v, new_diff_k, new_diff_v = caches
    return (x_ctx.reshape(BATCH, SEQ, D_MODEL), x_lat.reshape(DEC_BATCH, DEC_SEQ, D_MODEL),
            new_na_k.reshape(BATCH, DEPTH, SEQ, NA_HEADS, HEAD_DIM),
            new_na_v.reshape(BATCH, DEPTH, SEQ, NA_HEADS, HEAD_DIM),
            states,
            new_diff_k.reshape(BATCH, DEPTH, SEQ, DIFF_HEADS, 2 * DIFF_D),
            new_diff_v.reshape(BATCH, DEPTH, SEQ, DIFF_HEADS, DIFF_DV))
```

```python
import functools
import math

import numpy as np
import jax
import jax.numpy as jnp
from jax import lax
from jax.experimental import pallas as pl
from jax.experimental.pallas import tpu as pltpu

D_MODEL = 1024
BATCH = 32
SEQ = 256
DEPTH = 2
DEC_BATCH = 2
DEC_SEQ = 2048
PAST_LEN = 512
GRID_W = 64
HEAD_DIM = 64
NA_HEADS = 6
NA_WIN_ROWS = 8
NA_WIN_COLS = 16
RET_HEADS = 6
DIFF_HEADS = 4
DIFF_D = 32
DIFF_DV = 64
NA_W = NA_HEADS * HEAD_DIM
RET_W = RET_HEADS * HEAD_DIM
DIFF_W = DIFF_HEADS * DIFF_DV
MIX_W = NA_W + RET_W + DIFF_W
IN_COLS = 3 * NA_W + 4 * RET_W + 3 * DIFF_W
N_GROUPS = 4
EXPERTS_PER_GROUP = 8
N_EXPERTS = N_GROUPS * EXPERTS_PER_GROUP
EXPERT_FF = 256
ROPE_BASE = 10000.0
LN_EPS = 1e-5
NEG_INF = -1e30
ALPHA = (2.0 * DEPTH) ** 0.25
LOG2E = math.log2(math.e)

F32 = jnp.float32
BF16 = jnp.bfloat16

Q_NA, Q_RET, Q_GATE, Q_DIFF = 0, NA_W, NA_W + RET_W, NA_W + 2 * RET_W
Q_COLS = NA_W + 2 * RET_W + DIFF_W
KV_NAK, KV_NAV = 0, NA_W
KV_RETK, KV_RETV = 2 * NA_W, 2 * NA_W + RET_W
KV_DK, KV_DV = 2 * NA_W + 2 * RET_W, 2 * NA_W + 2 * RET_W + DIFF_W
KV_COLS = 2 * NA_W + 2 * RET_W + 2 * DIFF_W
MIX_RET, MIX_DIFF = NA_W, NA_W + RET_W

NA_Q_TILE = 256
NA_KEY_ROWS = 12
NA_KEYS = NA_KEY_ROWS * GRID_W
CTX_PROBLEMS = NA_HEADS + 2 * DIFF_HEADS + RET_HEADS
GRID_ROWS = DEC_SEQ // GRID_W
PAIR = 2 * HEAD_DIM

VMEM_LIMIT = 60 * 1024 * 1024


def _cparams(sem):
    return pltpu.CompilerParams(dimension_semantics=sem, vmem_limit_bytes=VMEM_LIMIT)


def _dot(a, b):
    return jnp.dot(a, b, preferred_element_type=F32)


def _dot_nt(a, b):
    return lax.dot_general(a, b, (((1,), (1,)), ((), ())), preferred_element_type=F32)


def _dot_tn(a, b):
    return lax.dot_general(a, b, (((0,), (0,)), ((), ())), preferred_element_type=F32)


def _silu(x):
    return x / (1.0 + jnp.exp(-x))


def _layer_norm_rows(z, w, b):
    mu = jnp.mean(z, axis=-1, keepdims=True)
    zc = z - mu
    var = jnp.mean(zc * zc, axis=-1, keepdims=True)
    return zc * lax.rsqrt(var + LN_EPS) * w + b


def _mod_kernel(cond_ref, w_ref, b_ref, o_ref):
    c = cond_ref[...]
    s = _silu(c)
    o_ref[...] = jnp.dot(s, w_ref[...], preferred_element_type=F32,
                         precision=lax.Precision.HIGHEST) + b_ref[...]


def _modulation(cond8, w_mod, b_mod):
    nj = 6
    out = pl.pallas_call(
        _mod_kernel,
        grid=(DEPTH, nj),
        in_specs=[
            pl.BlockSpec((8, D_MODEL), lambda l, j: (0, 0)),
            pl.BlockSpec((None, D_MODEL, D_MODEL), lambda l, j: (l, 0, j)),
            pl.BlockSpec((None, 1, D_MODEL), lambda l, j: (l, 0, j)),
        ],
        out_specs=pl.BlockSpec((None, 8, D_MODEL), lambda l, j: (l, 0, j)),
        out_shape=jax.ShapeDtypeStruct((DEPTH, 8, 6 * D_MODEL), F32),
        compiler_params=_cparams(("arbitrary", "arbitrary")),
        name="modulation",
    )(cond8, w_mod, b_mod.reshape(DEPTH, 1, 6 * D_MODEL))
    return out.reshape(DEPTH, 8, 6, D_MODEL)


def _proj_kernel(x_ref, mod_ref, w_ref, *refs):
    if len(refs) == 10:
        refs = refs[4:]
    q_ref, kv_ref = refs[:2]
    cache_refs = refs[2:]
    m = mod_ref[...]
    h = x_ref[...] * (1.0 + m[1:2]) + m[0:1]
    p = _dot(h.astype(BF16), w_ref[...])
    o = 0
    na_q = p[:, o:o + NA_W] * (HEAD_DIM ** -0.5 * LOG2E); o += NA_W
    na_k = p[:, o:o + NA_W]; o += NA_W
    na_v = p[:, o:o + NA_W]; o += NA_W
    ret_q = p[:, o:o + RET_W]; o += RET_W
    ret_k = p[:, o:o + RET_W] * (HEAD_DIM ** -0.5); o += RET_W
    ret_v = p[:, o:o + RET_W]; o += RET_W
    ret_g = p[:, o:o + RET_W]; o += RET_W
    dq = p[:, o:o + DIFF_W] * (DIFF_D ** -0.5 * LOG2E); o += DIFF_W
    dk = p[:, o:o + DIFF_W]; o += DIFF_W
    dv = p[:, o:o + DIFF_W]
    q_ref[:, Q_NA:Q_NA + NA_W] = na_q.astype(BF16)
    q_ref[:, Q_RET:Q_RET + RET_W] = ret_q.astype(BF16)
    q_ref[:, Q_GATE:Q_GATE + RET_W] = ret_g.astype(BF16)
    q_ref[:, Q_DIFF:Q_DIFF + DIFF_W] = dq.astype(BF16)
    kv_ref[:, KV_NAK:KV_NAK + NA_W] = na_k.astype(BF16)
    kv_ref[:, KV_NAV:KV_NAV + NA_W] = na_v.astype(BF16)
    kv_ref[:, KV_RETK:KV_RETK + RET_W] = ret_k.astype(BF16)
    kv_ref[:, KV_RETV:KV_RETV + RET_W] = ret_v.astype(BF16)
    kv_ref[:, KV_DK:KV_DK + DIFF_W] = dk.astype(BF16)
    kv_ref[:, KV_DV:KV_DV + DIFF_W] = dv.astype(BF16)
    nb = x_ref.shape[0] // SEQ
    for ref, val in zip(cache_refs, (na_k, na_v, dk, dv)):
        val = val.reshape(nb, SEQ, val.shape[-1])
        if len(ref.shape) == 4:
            ref[:, 0] = val
            ref[:, 1:] = jnp.zeros((nb, DEPTH - 1) + val.shape[1:], F32)
        else:
            ref[...] = val


def _project(x2d, mod, w_in_bf, mod_row_fn, layer, emit_caches=False, caches=None, tm=512):
    n = x2d.shape[0]
    row = lambda i: (i, 0)
    out_shape = [jax.ShapeDtypeStruct((n, Q_COLS), BF16), jax.ShapeDtypeStruct((n, KV_COLS), BF16)]
    out_specs = [pl.BlockSpec((tm, Q_COLS), row), pl.BlockSpec((tm, KV_COLS), row)]
    in_specs = [
        pl.BlockSpec((tm, D_MODEL), row),
        pl.BlockSpec((None, None, 6, D_MODEL), lambda i: (layer, mod_row_fn(i * tm), 0, 0)),
        pl.BlockSpec((None, D_MODEL, IN_COLS), lambda i: (layer, 0, 0)),
    ]
    args = [x2d, mod, w_in_bf]
    aliases = {}
    if emit_caches:
        for k, w in enumerate((NA_W, NA_W, DIFF_W, DIFF_W)):
            out_shape.append(jax.ShapeDtypeStruct((BATCH, DEPTH, SEQ, w), F32))
            if caches is None:
                out_specs.append(pl.BlockSpec((tm // SEQ, DEPTH, SEQ, w), lambda i: (i, 0, 0, 0)))
            else:
                in_specs.append(pl.BlockSpec(memory_space=pl.ANY))
                args.append(caches[k])
                aliases[3 + k] = 2 + k
                out_specs.append(pl.BlockSpec((tm // SEQ, None, SEQ, w), lambda i: (i, layer, 0, 0)))
    return pl.pallas_call(
        _proj_kernel,
        grid=(n // tm,),
        in_specs=in_specs,
        out_specs=out_specs,
        out_shape=out_shape,
        input_output_aliases=aliases,
        compiler_params=_cparams(("arbitrary",)),
        name="proj_ctx" if emit_caches else "proj_lat",
    )(*args)


def _lane_group(rows, width):
    return lax.broadcasted_iota(jnp.int32, (rows, PAIR), 1) // width


def _keep_group(x, groups, g):
    return jnp.where(groups == g, x, jnp.zeros_like(x))


def _with_ones(v_pair):
    return jnp.concatenate([v_pair, jnp.ones(v_pair.shape, v_pair.dtype)], axis=1)


def _normalise(r):
    return r[:, :PAIR] * (1.0 / r[:, PAIR:PAIR + 1])


def _attend(qm, k_list, vx_list, bias=None):
    scores = [_dot_nt(qm, k) for k in k_list]
    if bias is not None:
        scores[0] = scores[0] + bias
    m = None
    for s in scores:
        mi = jnp.max(s, axis=-1, keepdims=True)
        m = mi if m is None else jnp.maximum(m, mi)
    r = None
    for s, vx in zip(scores, vx_list):
        ri = _dot(jnp.exp2(s - m).astype(BF16), vx)
        r = ri if r is None else r + ri
    return _normalise(r)


def _group_mean_matrix():
    r = lax.broadcasted_iota(jnp.int32, (PAIR, PAIR), 0) // HEAD_DIM
    c = lax.broadcasted_iota(jnp.int32, (PAIR, PAIR), 1) // HEAD_DIM
    return jnp.where(r == c, 1.0 / HEAD_DIM, 0.0).astype(BF16)


def _group_mean(x, gm):
    hi = x.astype(BF16)
    lo = (x - hi.astype(F32)).astype(BF16)
    return _dot(hi, gm) + _dot(lo, gm)


def _head_layer_norm(o, gm):
    oc = o - _group_mean(o, gm)
    return oc * lax.rsqrt(_group_mean(oc * oc, gm) + LN_EPS)


def _head_rms_norm(o, gm):
    return o * lax.rsqrt(_group_mean(o * o, gm) + LN_EPS)


def _pair_scalar(ref, row, j, groups):
    return jnp.where(groups == 0, ref[row, 2 * j], ref[row, 2 * j + 1])


def _decay_table(lg_ref, dm_ref):
    T = SEQ
    i = lax.broadcasted_iota(jnp.int32, (T, T), 0).astype(F32)
    j = lax.broadcasted_iota(jnp.int32, (T, T), 1).astype(F32)
    d = i - j
    for h in range(RET_HEADS):
        dm_ref[h * T:(h + 1) * T, :] = jnp.exp2(jnp.where(d >= 0, d * lg_ref[0, h], (-d) * lg_ref[1, h]))


def _out_proj_post_norm(mix_ref, wout_ref, x_ref, mod_ref, lnw_ref, lnb_ref, x1_ref):
    y = _dot(mix_ref[...].astype(BF16), wout_ref[...])
    m = mod_ref[...]
    z = ALPHA * x_ref[...] + m[2:3] * y
    x1_ref[...] = _layer_norm_rows(z, lnw_ref[...], lnb_ref[...])


def _ctx_mix_kernel(lg_ref, lam_ref, q_ref, kv_ref, x_ref, mod_ref, wout_ref, gnw_ref, gnb_ref,
                    dnw_ref, lnw_ref, lnb_ref, *rest, one_minus_lam_init):
    if len(rest) == 8:
        rest = rest[1:]
    x1_ref, st_ref, mix_ref, dm_ref, s_ref, e_ref, o_ref = rest
    if len(st_ref.shape) == 5:
        st_ref[1:] = jnp.zeros((DEPTH - 1,) + st_ref.shape[1:], F32)
        st_ref = st_ref.at[0]
    T = SEQ
    n_soft = NA_HEADS + 2 * DIFF_HEADS

    @pl.when(pl.program_id(0) == 0)
    def _():
        _decay_table(lg_ref, dm_ref)

    half = _lane_group(T, HEAD_DIM)
    quarter = _lane_group(T, DIFF_D)
    n_na, n_diff, n_ret = NA_HEADS // 2, DIFF_HEADS // 2, RET_HEADS // 2

    for j in range(n_na):
        q = q_ref[:, Q_NA + j * PAIR:Q_NA + (j + 1) * PAIR]
        k = kv_ref[:, KV_NAK + j * PAIR:KV_NAK + (j + 1) * PAIR]
        for g in range(2):
            r0 = (2 * j + g) * T
            s_ref[r0:r0 + T, :] = _dot_nt(_keep_group(q, half, g), k)
    for j in range(n_diff):
        q = q_ref[:, Q_DIFF + j * PAIR:Q_DIFF + (j + 1) * PAIR]
        k = kv_ref[:, KV_DK + j * PAIR:KV_DK + (j + 1) * PAIR]
        for g in range(4):
            r0 = (NA_HEADS + 4 * j + g) * T
            s_ref[r0:r0 + T, :] = _dot_nt(_keep_group(q, quarter, g), k)
    for j in range(n_ret):
        q = q_ref[:, Q_RET + j * PAIR:Q_RET + (j + 1) * PAIR]
        k = kv_ref[:, KV_RETK + j * PAIR:KV_RETK + (j + 1) * PAIR]
        for g in range(2):
            r0 = (n_soft + 2 * j + g) * T
            s_ref[r0:r0 + T, :] = _dot_nt(_keep_group(q, half, g), k)

    s = s_ref[0:n_soft * T, :]
    e_ref[0:n_soft * T, :] = jnp.exp2(s - jnp.max(s, axis=-1, keepdims=True)).astype(BF16)
    e_ref[n_soft * T:, :] = (s_ref[n_soft * T:, :] * dm_ref[...]).astype(BF16)

    def pv(n, vx):
        return _normalise(_dot(e_ref[n * T:(n + 1) * T, :], vx))

    for j in range(n_na):
        vx = _with_ones(kv_ref[:, KV_NAV + j * PAIR:KV_NAV + (j + 1) * PAIR])
        mix_ref[:, j * PAIR:(j + 1) * PAIR] = jnp.where(half == 0, pv(2 * j, vx), pv(2 * j + 1, vx))
    lam = lam_ref[0]
    for j in range(n_diff):
        vx = _with_ones(kv_ref[:, KV_DV + j * PAIR:KV_DV + (j + 1) * PAIR])
        n0 = NA_HEADS + 4 * j
        head_a = pv(n0, vx) - lam * pv(n0 + 1, vx)
        head_b = pv(n0 + 2, vx) - lam * pv(n0 + 3, vx)
        o_ref[(n_ret + j) * T:(n_ret + j + 1) * T, :] = jnp.where(half == 0, head_a, head_b)
    jj = lax.broadcasted_iota(jnp.int32, (T, PAIR), 0).astype(F32)
    for j in range(n_ret):
        kf = kv_ref[:, KV_RETK + j * PAIR:KV_RETK + (j + 1) * PAIR].astype(F32)
        v = kv_ref[:, KV_RETV + j * PAIR:KV_RETV + (j + 1) * PAIR]
        n0 = n_soft + 2 * j
        o_ref[j * T:(j + 1) * T, :] = jnp.where(half == 0, _dot(e_ref[n0 * T:(n0 + 1) * T, :], v),
                                                _dot(e_ref[(n0 + 1) * T:(n0 + 2) * T, :], v))
        k_fwd = (kf * jnp.exp2((T - 1.0 - jj) * _pair_scalar(lg_ref, 0, j, half))).astype(BF16)
        k_bwd = (kf * jnp.exp2(jj * _pair_scalar(lg_ref, 1, j, half))).astype(BF16)
        for d, kd in enumerate((k_fwd, k_bwd)):
            st = _dot_tn(kd, v)
            st_ref[d, 2 * j] = st[0:HEAD_DIM, 0:HEAD_DIM]
            st_ref[d, 2 * j + 1] = st[HEAD_DIM:PAIR, HEAD_DIM:PAIR]

    gm = _group_mean_matrix()
    rn = _head_layer_norm(o_ref[0:n_ret * T, :], gm)
    dn = _head_rms_norm(o_ref[n_ret * T:, :], gm)
    for j in range(n_ret):
        c = j * PAIR
        g = q_ref[:, Q_GATE + c:Q_GATE + c + PAIR].astype(F32)
        mix_ref[:, MIX_RET + c:MIX_RET + c + PAIR] = (
            (rn[j * T:(j + 1) * T] * gnw_ref[:, c:c + PAIR] + gnb_ref[:, c:c + PAIR]) * _silu(g))
    for j in range(n_diff):
        c = j * PAIR
        mix_ref[:, MIX_DIFF + c:MIX_DIFF + c + PAIR] = (
            dn[j * T:(j + 1) * T] * dnw_ref[:, c:c + PAIR] * one_minus_lam_init)

    _out_proj_post_norm(mix_ref, wout_ref, x_ref, mod_ref, lnw_ref, lnb_ref, x1_ref)


def _ctx_mix(q_arr, kv_arr, x2d, mod, w_out_bf, log_g, lam, gnw, gnb, dnw, lnw, lnb, states, layer,
             lam_init):
    row = lambda b: (b, 0)
    const2 = lambda b: (0, 0)
    smem = pl.BlockSpec(memory_space=pltpu.SMEM)
    st_shape = (BATCH, DEPTH, 2, RET_HEADS, HEAD_DIM, HEAD_DIM)
    in_specs = [
        smem, smem,
        pl.BlockSpec((SEQ, Q_COLS), row),
        pl.BlockSpec((SEQ, KV_COLS), row),
        pl.BlockSpec((SEQ, D_MODEL), row),
        pl.BlockSpec((None, None, 6, D_MODEL), lambda b: (layer, 0, 0, 0)),
        pl.BlockSpec((None, MIX_W, D_MODEL), lambda b: (layer, 0, 0)),
        pl.BlockSpec((1, RET_W), const2),
        pl.BlockSpec((1, RET_W), const2),
        pl.BlockSpec((1, DIFF_W), const2),
        pl.BlockSpec((1, D_MODEL), const2),
        pl.BlockSpec((1, D_MODEL), const2),
    ]
    args = [log_g, lam, q_arr, kv_arr, x2d, mod, w_out_bf, gnw, gnb, dnw, lnw, lnb]
    if states is None:
        st_spec = pl.BlockSpec((None,) + st_shape[1:], lambda b: (b, 0, 0, 0, 0, 0))
        aliases = {}
    else:
        in_specs.append(pl.BlockSpec(memory_space=pl.ANY))
        args.append(states)
        st_spec = pl.BlockSpec((None, None) + st_shape[2:], lambda b: (b, layer, 0, 0, 0, 0))
        aliases = {12: 1}
    return pl.pallas_call(
        functools.partial(_ctx_mix_kernel, one_minus_lam_init=1.0 - lam_init),
        grid=(BATCH,),
        in_specs=in_specs,
        out_specs=[pl.BlockSpec((SEQ, D_MODEL), row), st_spec],
        out_shape=[
            jax.ShapeDtypeStruct((BATCH * SEQ, D_MODEL), F32),
            jax.ShapeDtypeStruct(st_shape, F32),
        ],
        input_output_aliases=aliases,
        scratch_shapes=[
            pltpu.VMEM((SEQ, MIX_W), F32),
            pltpu.VMEM((RET_HEADS * SEQ, SEQ), F32),
            pltpu.VMEM((CTX_PROBLEMS * SEQ, SEQ), F32),
            pltpu.VMEM((CTX_PROBLEMS * SEQ, SEQ), BF16),
            pltpu.VMEM(((RET_HEADS + DIFF_HEADS) // 2 * SEQ, PAIR), F32),
        ],
        compiler_params=_cparams(("arbitrary",)),
        name="ctx_mix",
    )(*args)


def _rope(x, cos, sin_signed):
    n, w = x.shape
    lane = lax.broadcasted_iota(jnp.int32, (n, w), 1)
    first = (lane % 16) < 8
    partner = jnp.where(first, pltpu.roll(x, w - 8, 1), pltpu.roll(x, 8, 1))
    return x * cos + partner * sin_signed


def _lat_mix_kernel(lg_ref, lam_ref, q_ref, kv_ref, x_ref, mod_ref, wout_ref, gnw_ref, gnb_ref,
                    dnw_ref, lnw_ref, lnb_ref, cnak_ref, cnav_ref, st0_ref, cdk_ref, cdv_ref,
                    bias_ref, cos_ref, sin_ref, x1_ref, mix_ref, kr_ref, dm_ref, sf_ref, sb_ref,
                    kc_na_ref, vxc_na_ref, vx_na_ref, kc_d_ref, vxc_d_ref, vx_d_ref, *,
                    one_minus_lam_init):
    TQ = NA_Q_TILE
    T = DEC_SEQ
    n_qt = T // TQ
    n_na, n_diff, n_ret = NA_HEADS // 2, DIFF_HEADS // 2, RET_HEADS // 2
    qt = pl.program_id(1)
    q0 = pl.multiple_of(qt * TQ, TQ)
    half = _lane_group(TQ, HEAD_DIM)
    quarter = _lane_group(TQ, DIFF_D)

    @pl.when(qt == 0)
    def _():
        kr = _rope(kv_ref[:, KV_DK:KV_DK + DIFF_W].astype(F32), cos_ref[...], sin_ref[...])
        kr_ref[...] = kr.astype(BF16)
        kc_na_ref[...] = cnak_ref[...].astype(BF16)
        kc_d_ref[...] = cdk_ref[...].astype(BF16)
        for j in range(n_na):
            vxc_na_ref[j] = _with_ones(cnav_ref[:, j * PAIR:(j + 1) * PAIR].astype(BF16))
            vx_na_ref[j] = _with_ones(kv_ref[:, KV_NAV + j * PAIR:KV_NAV + (j + 1) * PAIR])
        for j in range(n_diff):
            vxc_d_ref[j] = _with_ones(cdv_ref[:, j * PAIR:(j + 1) * PAIR].astype(BF16))
            vx_d_ref[j] = _with_ones(kv_ref[:, KV_DV + j * PAIR:KV_DV + (j + 1) * PAIR])
        _decay_table(lg_ref, dm_ref)
        jl = lax.broadcasted_iota(jnp.int32, (TQ, PAIR), 0).astype(F32)
        rows = lax.broadcasted_iota(jnp.int32, (PAIR, PAIR), 0) // HEAD_DIM
        cols = lax.broadcasted_iota(jnp.int32, (PAIR, PAIR), 1) // HEAD_DIM
        for j in range(n_ret):
            lf = _pair_scalar(lg_ref, 0, j, half)
            lb = _pair_scalar(lg_ref, 1, j, half)
            dec_f = jnp.exp2((TQ - 1.0 - jl) * lf)
            dec_b = jnp.exp2(jl * lb)
            tile_f = jnp.exp2(float(TQ) * _pair_scalar(lg_ref, 0, j, rows))
            tile_b = jnp.exp2(float(TQ) * _pair_scalar(lg_ref, 1, j, rows))
            loc_f, loc_b = [], []
            for t in range(n_qt):
                kf = kv_ref[t * TQ:(t + 1) * TQ, KV_RETK + j * PAIR:KV_RETK + (j + 1) * PAIR].astype(F32)
                v = kv_ref[t * TQ:(t + 1) * TQ, KV_RETV + j * PAIR:KV_RETV + (j + 1) * PAIR]
                loc_f.append(jnp.where(rows == cols, _dot_tn((kf * dec_f).astype(BF16), v), 0.0))
                loc_b.append(jnp.where(rows == cols, _dot_tn((kf * dec_b).astype(BF16), v), 0.0))
            state = st0_ref[0, j]
            for t in range(n_qt):
                sf_ref[t, j] = state.astype(BF16)
                state = state * tile_f + loc_f[t]
            state = st0_ref[1, j]
            for t in reversed(range(n_qt)):
                sb_ref[t, j] = state.astype(BF16)
                state = state * tile_b + loc_b[t]

    ks = jnp.clip(qt * (TQ // GRID_W) - NA_WIN_ROWS // 2, 0, GRID_ROWS - NA_KEY_ROWS)
    k0 = pl.multiple_of(ks * GRID_W, 256)
    for j in range(n_na):
        c = j * PAIR
        q = q_ref[:, Q_NA + c:Q_NA + c + PAIR]
        ks_list = [kv_ref[pl.ds(k0, NA_KEYS), KV_NAK + c:KV_NAK + c + PAIR], kc_na_ref[:, c:c + PAIR]]
        vx_list = [vx_na_ref[j, pl.ds(k0, NA_KEYS), :], vxc_na_ref[j]]
        o = [_attend(_keep_group(q, half, g), ks_list, vx_list, bias_ref[2 * j + g]) for g in range(2)]
        mix_ref[:, c:c + PAIR] = jnp.where(half == 0, o[0], o[1])

    gm = _group_mean_matrix()
    ii = lax.broadcasted_iota(jnp.int32, (TQ, PAIR), 0).astype(F32)
    for j in range(n_ret):
        c = j * PAIR
        q = q_ref[:, Q_RET + c:Q_RET + c + PAIR]
        k = kv_ref[pl.ds(q0, TQ), KV_RETK + c:KV_RETK + c + PAIR]
        v = kv_ref[pl.ds(q0, TQ), KV_RETV + c:KV_RETV + c + PAIR]
        inner = []
        for g in range(2):
            h = 2 * j + g
            sc = _dot_nt(_keep_group(q, half, g), k) * dm_ref[h * TQ:(h + 1) * TQ, :]
            inner.append(_dot(sc.astype(BF16), v))
        qf = q.astype(F32)
        q_fwd = (qf * jnp.exp2((ii + 1.0) * _pair_scalar(lg_ref, 0, j, half))).astype(BF16)
        q_bwd = (qf * jnp.exp2((TQ - ii) * _pair_scalar(lg_ref, 1, j, half))).astype(BF16)
        o = jnp.where(half == 0, inner[0], inner[1]) + _dot(q_fwd, sf_ref[qt, j]) + _dot(q_bwd, sb_ref[qt, j])
        gate = q_ref[:, Q_GATE + c:Q_GATE + c + PAIR].astype(F32)
        mix_ref[:, MIX_RET + c:MIX_RET + c + PAIR] = (
            (_head_layer_norm(o, gm) * gnw_ref[:, c:c + PAIR] + gnb_ref[:, c:c + PAIR]) * _silu(gate))

    lam = lam_ref[0]
    qr = _rope(q_ref[:, Q_DIFF:Q_DIFF + DIFF_W].astype(F32),
               cos_ref[pl.ds(q0, TQ), :], sin_ref[pl.ds(q0, TQ), :]).astype(BF16)
    for j in range(n_diff):
        c = j * PAIR
        q = qr[:, c:c + PAIR]
        ks_list = [kc_d_ref[:, c:c + PAIR], kr_ref[:, c:c + PAIR]]
        vx_list = [vxc_d_ref[j], vx_d_ref[j]]
        o = [_attend(_keep_group(q, quarter, g), ks_list, vx_list) for g in range(4)]
        blk = jnp.where(half == 0, o[0] - lam * o[1], o[2] - lam * o[3])
        mix_ref[:, MIX_DIFF + c:MIX_DIFF + c + PAIR] = (
            _head_rms_norm(blk, gm) * dnw_ref[:, c:c + PAIR] * one_minus_lam_init)

    _out_proj_post_norm(mix_ref, wout_ref, x_ref, mod_ref, lnw_ref, lnb_ref, x1_ref)


def _lat_mix(q_arr, kv_arr, x2d, mod_l, w_out_bf, log_g, lam, gnw, gnb, dnw, lnw, lnb,
             cache_na_k, cache_na_v, state_ret, cache_diff_k, cache_diff_v, bias_tab, cos_tab,
             sin_tab, layer, lam_init):
    nq = DEC_SEQ // NA_Q_TILE
    const2 = lambda b, t: (0, 0)
    smem = pl.BlockSpec(memory_space=pltpu.SMEM)
    qrow = lambda b, t: (b * nq + t, 0)

    def variant(b, t):
        return (jnp.where(t == 0, 0, jnp.where(t == nq - 1, 2, 1)), 0, 0, 0)

    return pl.pallas_call(
        functools.partial(_lat_mix_kernel, one_minus_lam_init=1.0 - lam_init),
        grid=(DEC_BATCH, nq),
        in_specs=[
            smem, smem,
            pl.BlockSpec((NA_Q_TILE, Q_COLS), qrow),
            pl.BlockSpec((DEC_SEQ, KV_COLS), lambda b, t: (b, 0), pipeline_mode=pl.Buffered(1)),
            pl.BlockSpec((NA_Q_TILE, D_MODEL), qrow),
            pl.BlockSpec((None, None, 6, D_MODEL), lambda b, t: (layer, b + 1, 0, 0)),
            pl.BlockSpec((None, MIX_W, D_MODEL), lambda b, t: (layer, 0, 0)),
            pl.BlockSpec((1, RET_W), const2),
            pl.BlockSpec((1, RET_W), const2),
            pl.BlockSpec((1, DIFF_W), const2),
            pl.BlockSpec((1, D_MODEL), const2),
            pl.BlockSpec((1, D_MODEL), const2),
            pl.BlockSpec((None, None, PAST_LEN, NA_W), lambda b, t: (b, layer, 0, 0),
                         pipeline_mode=pl.Buffered(1)),
            pl.BlockSpec((None, None, PAST_LEN, NA_W), lambda b, t: (b, layer, 0, 0),
                         pipeline_mode=pl.Buffered(1)),
            pl.BlockSpec((None, None, 2, RET_HEADS // 2, PAIR, PAIR),
                         lambda b, t: (b, layer, 0, 0, 0, 0)),
            pl.BlockSpec((None, None, PAST_LEN, DIFF_W), lambda b, t: (b, layer, 0, 0),
                         pipeline_mode=pl.Buffered(1)),
            pl.BlockSpec((None, None, PAST_LEN, DIFF_W), lambda b, t: (b, layer, 0, 0),
                         pipeline_mode=pl.Buffered(1)),
            pl.BlockSpec((None, NA_HEADS, NA_Q_TILE, NA_KEYS), variant, pipeline_mode=pl.Buffered(1)),
            pl.BlockSpec((DEC_SEQ, DIFF_W), const2),
            pl.BlockSpec((DEC_SEQ, DIFF_W), const2),
        ],
        out_specs=pl.BlockSpec((NA_Q_TILE, D_MODEL), qrow),
        out_shape=jax.ShapeDtypeStruct((DEC_BATCH * DEC_SEQ, D_MODEL), F32),
        scratch_shapes=[
            pltpu.VMEM((NA_Q_TILE, MIX_W), F32),
            pltpu.VMEM((DEC_SEQ, DIFF_W), BF16),
            pltpu.VMEM((RET_HEADS * NA_Q_TILE, NA_Q_TILE), F32),
            pltpu.VMEM((DEC_SEQ // NA_Q_TILE, RET_HEADS // 2, PAIR, PAIR), BF16),
            pltpu.VMEM((DEC_SEQ // NA_Q_TILE, RET_HEADS // 2, PAIR, PAIR), BF16),
            pltpu.VMEM((PAST_LEN, NA_W), BF16),
            pltpu.VMEM((NA_HEADS // 2, PAST_LEN, 2 * PAIR), BF16),
            pltpu.VMEM((NA_HEADS // 2, DEC_SEQ, 2 * PAIR), BF16),
            pltpu.VMEM((PAST_LEN, DIFF_W), BF16),
            pltpu.VMEM((DIFF_HEADS // 2, PAST_LEN, 2 * PAIR), BF16),
            pltpu.VMEM((DIFF_HEADS // 2, DEC_SEQ, 2 * PAIR), BF16),
        ],
        compiler_params=_cparams(("arbitrary", "arbitrary")),
        name="lat_mix",
    )(log_g, lam, q_arr, kv_arr, x2d, mod_l, w_out_bf, gnw, gnb, dnw, lnw, lnb,
      cache_na_k, cache_na_v, state_ret, cache_diff_k, cache_diff_v, bias_tab, cos_tab, sin_tab)


MOE_PART = 2048
MOE_TILE = 512
MOE_EB = 4
MOE_CH = 48
MOE_STEPS = N_EXPERTS // MOE_EB
MOE_ROUTE_ROWS = 40


def _route_transposed(lt):
    shape = lt.shape
    r = lax.broadcasted_iota(jnp.int32, shape, 0).astype(F32)
    ninf = -jnp.inf
    is_g = jnp.where(r >= N_EXPERTS, jnp.where(r < N_EXPERTS + N_GROUPS, 1.0, 0.0), 0.0) > 0.5
    gl = jnp.where(is_g, lt, ninf)
    gmax = jnp.max(gl, axis=0, keepdims=True)
    gsel = jnp.min(jnp.where(gl == gmax, r - N_EXPERTS, 1e9), axis=0, keepdims=True)
    gsum = jnp.sum(jnp.where(is_g, jnp.exp(gl - gmax), 0.0), axis=0, keepdims=True)
    gw = 1.0 / gsum
    lo = gsel * EXPERTS_PER_GROUP
    is_e = jnp.where(r >= lo, jnp.where(r < lo + EXPERTS_PER_GROUP, 1.0, 0.0), 0.0) > 0.5
    el = jnp.where(is_e, lt, ninf)
    v1 = jnp.max(el, axis=0, keepdims=True)
    i1 = jnp.min(jnp.where(el == v1, r, 1e9), axis=0, keepdims=True)
    el2 = jnp.where(r == i1, ninf, el)
    v2 = jnp.max(el2, axis=0, keepdims=True)
    i2 = jnp.min(jnp.where(el2 == v2, r, 1e9), axis=0, keepdims=True)
    t = jnp.exp(v2 - v1)
    w1 = gw / (1.0 + t)
    w2 = gw * t / (1.0 + t)
    first = r == i1
    second = r == i2
    gates = jnp.where(first, w1, 0.0) + jnp.where(second, w2, 0.0)
    member = jnp.where(first, 1.0, jnp.where(second, 1.0, 0.0))
    return gates, member


def _moe_kernel(x_ref, mod_ref, wr_ref, wg_ref, wu_ref, wd_ref, lnw_ref, lnb_ref, out_ref,
                h_ref, rank_ref, gate_ref, sel_ref, xs_ref, ys_ref):
    s = pl.program_id(1)
    n_tiles = MOE_PART // MOE_TILE

    @pl.when(s == 0)
    def _():
        m = mod_ref[...]
        h = x_ref[...] * (1.0 + m[4:5]) + m[3:4]
        h_hi = h.astype(BF16)
        h_ref[...] = h_hi
        h_lo = (h - h_hi.astype(F32)).astype(BF16)
        w = wr_ref[...]
        w_hi = w.astype(BF16)
        w_lo = (w - w_hi.astype(F32)).astype(BF16)
        lt = _dot_nt(w_hi, h_hi) + (_dot_nt(w_hi, h_lo) + _dot_nt(w_lo, h_hi))
        gates, member = _route_transposed(lt)
        before = jnp.where(lax.broadcasted_iota(jnp.int32, (MOE_TILE, MOE_TILE), 0)
                           < lax.broadcasted_iota(jnp.int32, (MOE_TILE, MOE_TILE), 1), 1.0, 0.0).astype(BF16)
        rank_ref[...] = jnp.full(rank_ref.shape, -1.0, F32)
        gate_ref[...] = jnp.zeros(gate_ref.shape, F32)
        for t in range(n_tiles):
            c0 = t * MOE_TILE
            mem_t = member[0:N_EXPERTS, c0:c0 + MOE_TILE]
            cnt = _dot(mem_t.astype(BF16), before)
            rank = jnp.where(mem_t > 0.5, cnt, -1.0)
            for st in range(MOE_STEPS):
                rank_ref[st, 0:MOE_EB, c0:c0 + MOE_TILE] = rank[st * MOE_EB:(st + 1) * MOE_EB]
                gate_ref[st, 0:MOE_EB, c0:c0 + MOE_TILE] = gates[st * MOE_EB:(st + 1) * MOE_EB, c0:c0 + MOE_TILE]
        out_ref[...] = jnp.zeros(out_ref.shape, F32)

    ranks = rank_ref[s, 0:MOE_EB, :]
    gts = gate_ref[s, 0:MOE_EB, :]
    n_chunks = ((jnp.max(ranks) + 0.5) * (1.0 / MOE_CH)).astype(jnp.int32) + 1

    def chunk_body(k, carry):
        slot = (lax.broadcasted_iota(jnp.int32, (MOE_CH, MOE_TILE), 0) + k * MOE_CH).astype(F32)
        row_gate = [[] for _ in range(MOE_EB)]
        for t in range(n_tiles):
            c0 = t * MOE_TILE
            onehots = []
            for i in range(MOE_EB):
                hit = ranks[i:i + 1, c0:c0 + MOE_TILE] == slot
                onehots.append(jnp.where(hit, 1.0, 0.0).astype(BF16))
                row_gate[i].append(jnp.sum(jnp.where(hit, gts[i:i + 1, c0:c0 + MOE_TILE], 0.0),
                                           axis=1, keepdims=True))
            sel = jnp.concatenate(onehots, axis=0)
            sel_ref[t] = sel
            xs = _dot(sel, h_ref[c0:c0 + MOE_TILE, :]).astype(BF16)
            for i in range(MOE_EB):
                xs_ref[i, t * MOE_CH:(t + 1) * MOE_CH, :] = xs[i * MOE_CH:(i + 1) * MOE_CH]
        for i in range(MOE_EB):
            xi = xs_ref[i]
            a = _dot(xi, wg_ref[i])
            u = _dot(xi, wu_ref[i])
            hm = (_silu(a) * u * jnp.concatenate(row_gate[i], axis=0)).astype(BF16)
            ys_ref[i] = _dot(hm, wd_ref[i]).astype(BF16)
        for t in range(n_tiles):
            c0 = t * MOE_TILE
            y = jnp.concatenate([ys_ref[i, t * MOE_CH:(t + 1) * MOE_CH, :] for i in range(MOE_EB)], axis=0)
            out_ref[c0:c0 + MOE_TILE, :] += _dot_tn(sel_ref[t], y)
        return carry

    lax.fori_loop(0, n_chunks, chunk_body, 0)

    @pl.when(s == MOE_STEPS - 1)
    def _():
        m = mod_ref[...]
        z = ALPHA * x_ref[...] + m[5:6] * out_ref[...]
        out_ref[...] = _layer_norm_rows(z, lnw_ref[...], lnb_ref[...])


def _moe(x2d, mod_l, mod_row_fn, wr_t, wg_bf, wu_bf, wd_bf, lnw, lnb, layer, name):
    n = x2d.shape[0]
    row = lambda p, s: (p, 0)
    const2 = lambda p, s: (0, 0)
    wspec = lambda shape: pl.BlockSpec((None, MOE_EB) + shape, lambda p, s: (layer, s, 0, 0))
    return pl.pallas_call(
        _moe_kernel,
        grid=(n // MOE_PART, MOE_STEPS),
        in_specs=[
            pl.BlockSpec((MOE_PART, D_MODEL), row, pipeline_mode=pl.Buffered(1)),
            pl.BlockSpec((None, None, 6, D_MODEL), lambda p, s: (layer, mod_row_fn(p * MOE_PART), 0, 0)),
            pl.BlockSpec((None, MOE_ROUTE_ROWS, D_MODEL), lambda p, s: (layer, 0, 0)),
            wspec((D_MODEL, EXPERT_FF)),
            wspec((D_MODEL, EXPERT_FF)),
            wspec((EXPERT_FF, D_MODEL)),
            pl.BlockSpec((1, D_MODEL), const2),
            pl.BlockSpec((1, D_MODEL), const2),
        ],
        out_specs=pl.BlockSpec((MOE_PART, D_MODEL), row),
        out_shape=jax.ShapeDtypeStruct((n, D_MODEL), F32),
        scratch_shapes=[
            pltpu.VMEM((MOE_PART, D_MODEL), BF16),
            pltpu.VMEM((MOE_STEPS, 8, MOE_PART), F32),
            pltpu.VMEM((MOE_STEPS, 8, MOE_PART), F32),
            pltpu.VMEM((MOE_PART // MOE_TILE, MOE_EB * MOE_CH, MOE_TILE), BF16),
            pltpu.VMEM((MOE_EB, MOE_PART // MOE_TILE * MOE_CH, D_MODEL), BF16),
            pltpu.VMEM((MOE_EB, MOE_PART // MOE_TILE * MOE_CH, D_MODEL), BF16),
        ],
        compiler_params=_cparams(("arbitrary", "arbitrary")),
        name=name,
    )(x2d, mod_l, wr_t, wg_bf, wu_bf, wd_bf, lnw, lnb)


def _na_bias_tables(rel_bias):
    q_rows = NA_Q_TILE // GRID_W
    n_dr = 2 * NA_WIN_ROWS - 1
    pad_c = GRID_W - NA_WIN_COLS
    padded = jnp.pad(rel_bias.astype(F32), ((0, 0), (0, 0), (pad_c, pad_c)), mode="edge")
    col_tab = jnp.stack([padded[:, :, GRID_W - 1 - qc:2 * GRID_W - 1 - qc] for qc in range(GRID_W)], axis=2)
    pad_r = NA_KEY_ROWS - q_rows
    col_tab = jnp.pad(col_tab, ((0, 0), (pad_r, pad_r), (0, 0), (0, 0)))
    qi = np.arange(NA_Q_TILE)
    ki = np.arange(NA_KEYS)
    tabs = []
    for r0, ks in ((0, 0), (8, 4), (GRID_ROWS - q_rows, GRID_ROWS - NA_KEY_ROWS)):
        rows = []
        for a in range(q_rows):
            start = ks - r0 - a + (NA_WIN_ROWS - 1) + pad_r
            blk = col_tab[:, start:start + NA_KEY_ROWS]
            rows.append(blk.transpose(0, 2, 1, 3).reshape(NA_HEADS, GRID_W, NA_KEYS))
        b = jnp.concatenate(rows, axis=1)
        q_row = r0 + qi // GRID_W
        q_col = qi % GRID_W
        k_row = ks + ki // GRID_W
        k_col = ki % GRID_W
        row_start = np.clip(q_row - NA_WIN_ROWS // 2, 0, GRID_ROWS - NA_WIN_ROWS)
        row_ok = (k_row[None, :] >= row_start[:, None]) & (k_row[None, :] < row_start[:, None] + NA_WIN_ROWS)
        col_start = np.clip(q_col - NA_WIN_COLS // 2, 0, GRID_W - NA_WIN_COLS)
        col_ok = (k_col[None, :] >= col_start[:, None]) & (k_col[None, :] < col_start[:, None] + NA_WIN_COLS)
        dr = k_row[None, :] - q_row[:, None] + (NA_WIN_ROWS - 1)
        assert np.all((dr[row_ok] >= 0) & (dr[row_ok] < n_dr))
        tabs.append(jnp.where(jnp.asarray(row_ok & col_ok)[None], b * LOG2E, NEG_INF))
    return jnp.stack(tabs, axis=0)


def _rope_tables():
    n = DIFF_D // 4
    lane = np.arange(DIFF_W)
    d = lane % DIFF_D
    use_col = d >= DIFF_D // 2
    e = d % (DIFF_D // 2)
    f = e % n
    first = e < n
    t = jnp.arange(DEC_SEQ)
    pos = jnp.where(jnp.asarray(use_col)[None, :], (t % GRID_W)[:, None], (t // GRID_W)[:, None])
    freqs = ROPE_BASE ** (-jnp.arange(n, dtype=F32) / n)
    ang = pos.astype(F32) * freqs[jnp.asarray(f)][None, :]
    sign = jnp.asarray(np.where(first, -1.0, 1.0), dtype=F32)
    return jnp.cos(ang), jnp.sin(ang) * sign[None, :]


def kernel(x_prompt, x_sample, c, cache_na_k, cache_na_v, state_ret, cache_diff_k, cache_diff_v, c_ctx,
           w_mod, b_mod, w_in, na_rel_bias, ret_decay, ret_gn_w, ret_gn_b, diff_lambda, diff_norm_w,
           w_out, ln1_w, ln1_b, router_group, router_expert, exp_w_gate, exp_w_up, exp_w_down,
           ln2_w, ln2_b):
    n_ctx = BATCH * SEQ
    n_lat = DEC_BATCH * DEC_SEQ
    x_ctx = x_prompt.reshape(n_ctx, D_MODEL)
    x_lat = x_sample.reshape(n_lat, D_MODEL)

    cond8 = jnp.concatenate([c_ctx[None, :], c, jnp.zeros((8 - 1 - DEC_BATCH, D_MODEL), F32)], axis=0)
    mod = _modulation(cond8, w_mod, b_mod)

    cache_na_k = cache_na_k.reshape(DEC_BATCH, DEPTH, PAST_LEN, NA_W)
    cache_na_v = cache_na_v.reshape(DEC_BATCH, DEPTH, PAST_LEN, NA_W)
    cache_diff_k = cache_diff_k.reshape(DEC_BATCH, DEPTH, PAST_LEN, DIFF_W)
    cache_diff_v = cache_diff_v.reshape(DEC_BATCH, DEPTH, PAST_LEN, DIFF_W)
    cos_tab, sin_tab = _rope_tables()
    st0 = state_ret.astype(F32).reshape(DEC_BATCH, DEPTH, 2, RET_HEADS // 2, 2, HEAD_DIM, 1, HEAD_DIM)
    st0 = (st0 * jnp.eye(2, dtype=F32)[:, None, :, None]).reshape(
        DEC_BATCH, DEPTH, 2, RET_HEADS // 2, PAIR, PAIR)

    ctx_row = lambda r: 0
    lat_row = lambda r: 1 + r // DEC_SEQ

    wg_bf = exp_w_gate.astype(BF16)
    wu_bf = exp_w_up.astype(BF16)
    wd_bf = exp_w_down.astype(BF16)

    w_in_bf = w_in.astype(BF16)
    w_out_bf = w_out.astype(BF16)
    wr_t = jnp.concatenate([jnp.swapaxes(router_expert, 1, 2), jnp.swapaxes(router_group, 1, 2),
                            jnp.zeros((DEPTH, MOE_ROUTE_ROWS - N_EXPERTS - N_GROUPS, D_MODEL), F32)], axis=1)
    log_g_all = jax.nn.log_sigmoid(ret_decay.astype(F32)) * LOG2E
    lp = diff_lambda.astype(F32)
    lam_dyn = jnp.exp(jnp.sum(lp[:, 0] * lp[:, 1], axis=-1)) - jnp.exp(jnp.sum(lp[:, 2] * lp[:, 3], axis=-1))

    caches = None
    states = None
    for l in range(DEPTH):
        lam_init = 0.8 - 0.6 * math.exp(-0.3 * l)
        lam = (lam_dyn[l] + lam_init).reshape(1)
        log_g = log_g_all[l]
        gnw = ret_gn_w[l].reshape(1, RET_W)
        gnb = ret_gn_b[l].reshape(1, RET_W)
        dnw = diff_norm_w[l].reshape(1, DIFF_W)
        l1w = ln1_w[l].reshape(1, D_MODEL)
        l1b = ln1_b[l].reshape(1, D_MODEL)
        l2w = ln2_w[l].reshape(1, D_MODEL)
        l2b = ln2_b[l].reshape(1, D_MODEL)
        bias_tab = _na_bias_tables(na_rel_bias[l])

        q_c, kv_c, *caches = _project(x_ctx, mod, w_in_bf, ctx_row, l, emit_caches=True, caches=caches)
        x1_c, states = _ctx_mix(q_c, kv_c, x_ctx, mod, w_out_bf, log_g, lam, gnw, gnb, dnw, l1w, l1b,
                                states, l, lam_init)
        x_ctx = _moe(x1_c, mod, ctx_row, wr_t, wg_bf, wu_bf, wd_bf, l2w, l2b, l, "moe_ctx")

        q_l, kv_l = _project(x_lat, mod, w_in_bf, lat_row, l)
        x1_l = _lat_mix(q_l, kv_l, x_lat, mod, w_out_bf, log_g, lam, gnw, gnb, dnw, l1w, l1b,
                        cache_na_k, cache_na_v, st0, cache_diff_k, cache_diff_v,
                        bias_tab, cos_tab, sin_tab, l, lam_init)
        x_lat = _moe(x1_l, mod, lat_row, wr_t, wg_bf, wu_bf, wd_bf, l2w, l2b, l, "moe_lat")

    new_na_k, new_na_v, new_diff_k, new_diff_v = caches
    return (x_ctx.reshape(BATCH, SEQ, D_MODEL), x_lat.reshape(DEC_BATCH, DEC_SEQ, D_MODEL),
            new_na_k.reshape(BATCH, DEPTH, SEQ, NA_HEADS, HEAD_DIM),
            new_na_v.reshape(BATCH, DEPTH, SEQ, NA_HEADS, HEAD_DIM),
            states,
            new_diff_k.reshape(BATCH, DEPTH, SEQ, DIFF_HEADS, 2 * DIFF_D),
            new_diff_v.reshape(BATCH, DEPTH, SEQ, DIFF_HEADS, DIFF_DV))
```

```python
import functools
import math

import numpy as np
import jax
import jax.numpy as jnp
from jax import lax
from jax.experimental import pallas as pl
from jax.experimental.pallas import tpu as pltpu

D_MODEL = 1024
BATCH = 32
SEQ = 256
DEPTH = 2
DEC_BATCH = 2
DEC_SEQ = 2048
PAST_LEN = 512
GRID_W = 64
HEAD_DIM = 64
NA_HEADS = 6
NA_WIN_ROWS = 8
NA_WIN_COLS = 16
RET_HEADS = 6
DIFF_HEADS = 4
DIFF_D = 32
DIFF_DV = 64
NA_W = NA_HEADS * HEAD_DIM
RET_W = RET_HEADS * HEAD_DIM
DIFF_W = DIFF_HEADS * DIFF_DV
MIX_W = NA_W + RET_W + DIFF_W
IN_COLS = 3 * NA_W + 4 * RET_W + 3 * DIFF_W
N_GROUPS = 4
EXPERTS_PER_GROUP = 8
N_EXPERTS = N_GROUPS * EXPERTS_PER_GROUP
EXPERT_FF = 256
ROPE_BASE = 10000.0
LN_EPS = 1e-5
NEG_INF = -1e30
ALPHA = (2.0 * DEPTH) ** 0.25
LOG2E = math.log2(math.e)

F32 = jnp.float32
BF16 = jnp.bfloat16

Q_NA, Q_RET, Q_GATE, Q_DIFF = 0, NA_W, NA_W + RET_W, NA_W + 2 * RET_W
Q_COLS = NA_W + 2 * RET_W + DIFF_W
KV_NAK, KV_NAV = 0, NA_W
KV_RETK, KV_RETV = 2 * NA_W, 2 * NA_W + RET_W
KV_DK, KV_DV = 2 * NA_W + 2 * RET_W, 2 * NA_W + 2 * RET_W + DIFF_W
KV_COLS = 2 * NA_W + 2 * RET_W + 2 * DIFF_W
MIX_RET, MIX_DIFF = NA_W, NA_W + RET_W

NA_Q_TILE = 256
NA_KEY_ROWS = 12
NA_KEYS = NA_KEY_ROWS * GRID_W
CTX_PROBLEMS = NA_HEADS + 2 * DIFF_HEADS + RET_HEADS
GRID_ROWS = DEC_SEQ // GRID_W
PAIR = 2 * HEAD_DIM

VMEM_LIMIT = 60 * 1024 * 1024


def _cparams(sem):
    return pltpu.CompilerParams(dimension_semantics=sem, vmem_limit_bytes=VMEM_LIMIT)


def _dot(a, b):
    return jnp.dot(a, b, preferred_element_type=F32)


def _dot_nt(a, b):
    return lax.dot_general(a, b, (((1,), (1,)), ((), ())), preferred_element_type=F32)


def _dot_tn(a, b):
    return lax.dot_general(a, b, (((0,), (0,)), ((), ())), preferred_element_type=F32)


def _silu(x):
    return x / (1.0 + jnp.exp(-x))


def _layer_norm_rows(z, w, b):
    mu = jnp.mean(z, axis=-1, keepdims=True)
    zc = z - mu
    var = jnp.mean(zc * zc, axis=-1, keepdims=True)
    return zc * lax.rsqrt(var + LN_EPS) * w + b


def _mod_kernel(cond_ref, w_ref, b_ref, o_ref):
    c = cond_ref[...]
    s = _silu(c)
    o_ref[...] = jnp.dot(s, w_ref[...], preferred_element_type=F32,
                         precision=lax.Precision.HIGHEST) + b_ref[...]


def _modulation(cond8, w_mod, b_mod):
    nj = 6
    out = pl.pallas_call(
        _mod_kernel,
        grid=(DEPTH, nj),
        in_specs=[
            pl.BlockSpec((8, D_MODEL), lambda l, j: (0, 0)),
            pl.BlockSpec((None, D_MODEL, D_MODEL), lambda l, j: (l, 0, j)),
            pl.BlockSpec((None, 1, D_MODEL), lambda l, j: (l, 0, j)),
        ],
        out_specs=pl.BlockSpec((None, 8, D_MODEL), lambda l, j: (l, 0, j)),
        out_shape=jax.ShapeDtypeStruct((DEPTH, 8, 6 * D_MODEL), F32),
        compiler_params=_cparams(("arbitrary", "arbitrary")),
        name="modulation",
    )(cond8, w_mod, b_mod.reshape(DEPTH, 1, 6 * D_MODEL))
    return out.reshape(DEPTH, 8, 6, D_MODEL)


def _proj_kernel(x_ref, mod_ref, w_ref, *refs):
    if len(refs) == 10:
        refs = refs[4:]
    q_ref, kv_ref = refs[:2]
    cache_refs = refs[2:]
    m = mod_ref[...]
    h = x_ref[...] * (1.0 + m[1:2]) + m[0:1]
    p = _dot(h.astype(BF16), w_ref[...])
    o = 0
    na_q = p[:, o:o + NA_W] * (HEAD_DIM ** -0.5 * LOG2E); o += NA_W
    na_k = p[:, o:o + NA_W]; o += NA_W
    na_v = p[:, o:o + NA_W]; o += NA_W
    ret_q = p[:, o:o + RET_W]; o += RET_W
    ret_k = p[:, o:o + RET_W] * (HEAD_DIM ** -0.5); o += RET_W
    ret_v = p[:, o:o + RET_W]; o += RET_W
    ret_g = p[:, o:o + RET_W]; o += RET_W
    dq = p[:, o:o + DIFF_W] * (DIFF_D ** -0.5 * LOG2E); o += DIFF_W
    dk = p[:, o:o + DIFF_W]; o += DIFF_W
    dv = p[:, o:o + DIFF_W]
    q_ref[:, Q_NA:Q_NA + NA_W] = na_q.astype(BF16)
    q_ref[:, Q_RET:Q_RET + RET_W] = ret_q.astype(BF16)
    q_ref[:, Q_GATE:Q_GATE + RET_W] = ret_g.astype(BF16)
    q_ref[:, Q_DIFF:Q_DIFF + DIFF_W] = dq.astype(BF16)
    kv_ref[:, KV_NAK:KV_NAK + NA_W] = na_k.astype(BF16)
    kv_ref[:, KV_NAV:KV_NAV + NA_W] = na_v.astype(BF16)
    kv_ref[:, KV_RETK:KV_RETK + RET_W] = ret_k.astype(BF16)
    kv_ref[:, KV_RETV:KV_RETV + RET_W] = ret_v.astype(BF16)
    kv_ref[:, KV_DK:KV_DK + DIFF_W] = dk.astype(BF16)
    kv_ref[:, KV_DV:KV_DV + DIFF_W] = dv.astype(BF16)
    nb = x_ref.shape[0] // SEQ
    for ref, val in zip(cache_refs, (na_k, na_v, dk, dv)):
        val = val.reshape(nb, SEQ, val.shape[-1])
        if len(ref.shape) == 4:
            ref[:, 0] = val
            ref[:, 1:] = jnp.zeros((nb, DEPTH - 1) + val.shape[1:], F32)
        else:
            ref[...] = val


def _project(x2d, mod, w_in_bf, mod_row_fn, layer, emit_caches=False, caches=None, tm=512):
    n = x2d.shape[0]
    row = lambda i: (i, 0)
    out_shape = [jax.ShapeDtypeStruct((n, Q_COLS), BF16), jax.ShapeDtypeStruct((n, KV_COLS), BF16)]
    out_specs = [pl.BlockSpec((tm, Q_COLS), row), pl.BlockSpec((tm, KV_COLS), row)]
    in_specs = [
        pl.BlockSpec((tm, D_MODEL), row),
        pl.BlockSpec((None, None, 6, D_MODEL), lambda i: (layer, mod_row_fn(i * tm), 0, 0)),
        pl.BlockSpec((None, D_MODEL, IN_COLS), lambda i: (layer, 0, 0)),
    ]
    args = [x2d, mod, w_in_bf]
    aliases = {}
    if emit_caches:
        for k, w in enumerate((NA_W, NA_W, DIFF_W, DIFF_W)):
            out_shape.append(jax.ShapeDtypeStruct((BATCH, DEPTH, SEQ, w), F32))
            if caches is None:
                out_specs.append(pl.BlockSpec((tm // SEQ, DEPTH, SEQ, w), lambda i: (i, 0, 0, 0)))
            else:
                in_specs.append(pl.BlockSpec(memory_space=pl.ANY))
                args.append(caches[k])
                aliases[3 + k] = 2 + k
                out_specs.append(pl.BlockSpec((tm // SEQ, None, SEQ, w), lambda i: (i, layer, 0, 0)))
    return pl.pallas_call(
        _proj_kernel,
        grid=(n // tm,),
        in_specs=in_specs,
        out_specs=out_specs,
        out_shape=out_shape,
        input_output_aliases=aliases,
        compiler_params=_cparams(("arbitrary",)),
        name="proj_ctx" if emit_caches else "proj_lat",
    )(*args)


def _lane_group(rows, width):
    return lax.broadcasted_iota(jnp.int32, (rows, PAIR), 1) // width


def _keep_group(x, groups, g):
    return jnp.where(groups == g, x, jnp.zeros_like(x))


def _with_ones(v_pair):
    return jnp.concatenate([v_pair, jnp.ones(v_pair.shape, v_pair.dtype)], axis=1)


def _normalise(r):
    return r[:, :PAIR] * (1.0 / r[:, PAIR:PAIR + 1])


def _attend(qm, k_list, vx_list, bias=None):
    scores = [_dot_nt(qm, k) for k in k_list]
    if bias is not None:
        scores[0] = scores[0] + bias
    m = None
    for s in scores:
        mi = jnp.max(s, axis=-1, keepdims=True)
        m = mi if m is None else jnp.maximum(m, mi)
    r = None
    for s, vx in zip(scores, vx_list):
        ri = _dot(jnp.exp2(s - m).astype(BF16), vx)
        r = ri if r is None else r + ri
    return _normalise(r)


def _group_mean_matrix():
    r = lax.broadcasted_iota(jnp.int32, (PAIR, PAIR), 0) // HEAD_DIM
    c = lax.broadcasted_iota(jnp.int32, (PAIR, PAIR), 1) // HEAD_DIM
    return jnp.where(r == c, 1.0 / HEAD_DIM, 0.0).astype(BF16)


def _group_mean(x, gm):
    hi = x.astype(BF16)
    lo = (x - hi.astype(F32)).astype(BF16)
    return _dot(hi, gm) + _dot(lo, gm)


def _head_layer_norm(o, gm):
    oc = o - _group_mean(o, gm)
    return oc * lax.rsqrt(_group_mean(oc * oc, gm) + LN_EPS)


def _head_rms_norm(o, gm):
    return o * lax.rsqrt(_group_mean(o * o, gm) + LN_EPS)


def _pair_scalar(ref, row, j, groups):
    return jnp.where(groups == 0, ref[row, 2 * j], ref[row, 2 * j + 1])


def _decay_table(lg_ref, dm_ref):
    T = SEQ
    i = lax.broadcasted_iota(jnp.int32, (T, T), 0).astype(F32)
    j = lax.broadcasted_iota(jnp.int32, (T, T), 1).astype(F32)
    d = i - j
    for h in range(RET_HEADS):
        dm_ref[h * T:(h + 1) * T, :] = jnp.exp2(jnp.where(d >= 0, d * lg_ref[0, h], (-d) * lg_ref[1, h]))


def _out_proj_post_norm(mix_ref, wout_ref, x_ref, mod_ref, lnw_ref, lnb_ref, x1_ref):
    y = _dot(mix_ref[...].astype(BF16), wout_ref[...])
    m = mod_ref[...]
    z = ALPHA * x_ref[...] + m[2:3] * y
    x1_ref[...] = _layer_norm_rows(z, lnw_ref[...], lnb_ref[...])


def _ctx_mix_kernel(lg_ref, lam_ref, q_ref, kv_ref, x_ref, mod_ref, wout_ref, gnw_ref, gnb_ref,
                    dnw_ref, lnw_ref, lnb_ref, *rest, one_minus_lam_init):
    if len(rest) == 8:
        rest = rest[1:]
    x1_ref, st_ref, mix_ref, dm_ref, s_ref, e_ref, o_ref = rest
    if len(st_ref.shape) == 5:
        st_ref[1:] = jnp.zeros((DEPTH - 1,) + st_ref.shape[1:], F32)
        st_ref = st_ref.at[0]
    T = SEQ
    n_soft = NA_HEADS + 2 * DIFF_HEADS

    @pl.when(pl.program_id(0) == 0)
    def _():
        _decay_table(lg_ref, dm_ref)

    half = _lane_group(T, HEAD_DIM)
    quarter = _lane_group(T, DIFF_D)
    n_na, n_diff, n_ret = NA_HEADS // 2, DIFF_HEADS // 2, RET_HEADS // 2

    for j in range(n_na):
        q = q_ref[:, Q_NA + j * PAIR:Q_NA + (j + 1) * PAIR]
        k = kv_ref[:, KV_NAK + j * PAIR:KV_NAK + (j + 1) * PAIR]
        for g in range(2):
            r0 = (2 * j + g) * T
            s_ref[r0:r0 + T, :] = _dot_nt(_keep_group(q, half, g), k)
    for j in range(n_diff):
        q = q_ref[:, Q_DIFF + j * PAIR:Q_DIFF + (j + 1) * PAIR]
        k = kv_ref[:, KV_DK + j * PAIR:KV_DK + (j + 1) * PAIR]
        for g in range(4):
            r0 = (NA_HEADS + 4 * j + g) * T
            s_ref[r0:r0 + T, :] = _dot_nt(_keep_group(q, quarter, g), k)
    for j in range(n_ret):
        q = q_ref[:, Q_RET + j * PAIR:Q_RET + (j + 1) * PAIR]
        k = kv_ref[:, KV_RETK + j * PAIR:KV_RETK + (j + 1) * PAIR]
        for g in range(2):
            r0 = (n_soft + 2 * j + g) * T
            s_ref[r0:r0 + T, :] = _dot_nt(_keep_group(q, half, g), k)

    s = s_ref[0:n_soft * T, :]
    e_ref[0:n_soft * T, :] = jnp.exp2(s - jnp.max(s, axis=-1, keepdims=True)).astype(BF16)
    e_ref[n_soft * T:, :] = (s_ref[n_soft * T:, :] * dm_ref[...]).astype(BF16)

    def pv(n, vx):
        return _normalise(_dot(e_ref[n * T:(n + 1) * T, :], vx))

    for j in range(n_na):
        vx = _with_ones(kv_ref[:, KV_NAV + j * PAIR:KV_NAV + (j + 1) * PAIR])
        mix_ref[:, j * PAIR:(j + 1) * PAIR] = jnp.where(half == 0, pv(2 * j, vx), pv(2 * j + 1, vx))
    lam = lam_ref[0]
    for j in range(n_diff):
        vx = _with_ones(kv_ref[:, KV_DV + j * PAIR:KV_DV + (j + 1) * PAIR])
        n0 = NA_HEADS + 4 * j
        head_a = pv(n0, vx) - lam * pv(n0 + 1, vx)
        head_b = pv(n0 + 2, vx) - lam * pv(n0 + 3, vx)
        o_ref[(n_ret + j) * T:(n_ret + j + 1) * T, :] = jnp.where(half == 0, head_a, head_b)
    jj = lax.broadcasted_iota(jnp.int32, (T, PAIR), 0).astype(F32)
    for j in range(n_ret):
        kf = kv_ref[:, KV_RETK + j * PAIR:KV_RETK + (j + 1) * PAIR].astype(F32)
        v = kv_ref[:, KV_RETV + j * PAIR:KV_RETV + (j + 1) * PAIR]
        n0 = n_soft + 2 * j
        o_ref[j * T:(j + 1) * T, :] = jnp.where(half == 0, _dot(e_ref[n0 * T:(n0 + 1) * T, :], v),
                                                _dot(e_ref[(n0 + 1) * T:(n0 + 2) * T, :], v))
        k_fwd = (kf * jnp.exp2((T - 1.0 - jj) * _pair_scalar(lg_ref, 0, j, half))).astype(BF16)
        k_bwd = (kf * jnp.exp2(jj * _pair_scalar(lg_ref, 1, j, half))).astype(BF16)
        for d, kd in enumerate((k_fwd, k_bwd)):
            st = _dot_tn(kd, v)
            st_ref[d, 2 * j] = st[0:HEAD_DIM, 0:HEAD_DIM]
            st_ref[d, 2 * j + 1] = st[HEAD_DIM:PAIR, HEAD_DIM:PAIR]

    gm = _group_mean_matrix()
    rn = _head_layer_norm(o_ref[0:n_ret * T, :], gm)
    dn = _head_rms_norm(o_ref[n_ret * T:, :], gm)
    for j in range(n_ret):
        c = j * PAIR
        g = q_ref[:, Q_GATE + c:Q_GATE + c + PAIR].astype(F32)
        mix_ref[:, MIX_RET + c:MIX_RET + c + PAIR] = (
            (rn[j * T:(j + 1) * T] * gnw_ref[:, c:c + PAIR] + gnb_ref[:, c:c + PAIR]) * _silu(g))
    for j in range(n_diff):
        c = j * PAIR
        mix_ref[:, MIX_DIFF + c:MIX_DIFF + c + PAIR] = (
            dn[j * T:(j + 1) * T] * dnw_ref[:, c:c + PAIR] * one_minus_lam_init)

    _out_proj_post_norm(mix_ref, wout_ref, x_ref, mod_ref, lnw_ref, lnb_ref, x1_ref)


def _ctx_mix(q_arr, kv_arr, x2d, mod, w_out_bf, log_g, lam, gnw, gnb, dnw, lnw, lnb, states, layer,
             lam_init):
    row = lambda b: (b, 0)
    const2 = lambda b: (0, 0)
    smem = pl.BlockSpec(memory_space=pltpu.SMEM)
    st_shape = (BATCH, DEPTH, 2, RET_HEADS, HEAD_DIM, HEAD_DIM)
    in_specs = [
        smem, smem,
        pl.BlockSpec((SEQ, Q_COLS), row),
        pl.BlockSpec((SEQ, KV_COLS), row),
        pl.BlockSpec((SEQ, D_MODEL), row),
        pl.BlockSpec((None, None, 6, D_MODEL), lambda b: (layer, 0, 0, 0)),
        pl.BlockSpec((None, MIX_W, D_MODEL), lambda b: (layer, 0, 0)),
        pl.BlockSpec((1, RET_W), const2),
        pl.BlockSpec((1, RET_W), const2),
        pl.BlockSpec((1, DIFF_W), const2),
        pl.BlockSpec((1, D_MODEL), const2),
        pl.BlockSpec((1, D_MODEL), const2),
    ]
    args = [log_g, lam, q_arr, kv_arr, x2d, mod, w_out_bf, gnw, gnb, dnw, lnw, lnb]
    if states is None:
        st_spec = pl.BlockSpec((None,) + st_shape[1:], lambda b: (b, 0, 0, 0, 0, 0))
        aliases = {}
    else:
        in_specs.append(pl.BlockSpec(memory_space=pl.ANY))
        args.append(states)
        st_spec = pl.BlockSpec((None, None) + st_shape[2:], lambda b: (b, layer, 0, 0, 0, 0))
        aliases = {12: 1}
    return pl.pallas_call(
        functools.partial(_ctx_mix_kernel, one_minus_lam_init=1.0 - lam_init),
        grid=(BATCH,),
        in_specs=in_specs,
        out_specs=[pl.BlockSpec((SEQ, D_MODEL), row), st_spec],
        out_shape=[
            jax.ShapeDtypeStruct((BATCH * SEQ, D_MODEL), F32),
            jax.ShapeDtypeStruct(st_shape, F32),
        ],
        input_output_aliases=aliases,
        scratch_shapes=[
            pltpu.VMEM((SEQ, MIX_W), F32),
            pltpu.VMEM((RET_HEADS * SEQ, SEQ), F32),
            pltpu.VMEM((CTX_PROBLEMS * SEQ, SEQ), F32),
            pltpu.VMEM((CTX_PROBLEMS * SEQ, SEQ), BF16),
            pltpu.VMEM(((RET_HEADS + DIFF_HEADS) // 2 * SEQ, PAIR), F32),
        ],
        compiler_params=_cparams(("arbitrary",)),
        name="ctx_mix",
    )(*args)


def _rope(x, cos, sin_signed):
    n, w = x.shape
    lane = lax.broadcasted_iota(jnp.int32, (n, w), 1)
    first = (lane % 16) < 8
    partner = jnp.where(first, pltpu.roll(x, w - 8, 1), pltpu.roll(x, 8, 1))
    return x * cos + partner * sin_signed


def _lat_mix_kernel(lg_ref, lam_ref, q_ref, kv_ref, x_ref, mod_ref, wout_ref, gnw_ref, gnb_ref,
                    dnw_ref, lnw_ref, lnb_ref, cnak_ref, cnav_ref, st0_ref, cdk_ref, cdv_ref,
                    bias_ref, cos_ref, sin_ref, x1_ref, mix_ref, kr_ref, dm_ref, sf_ref, sb_ref,
                    kc_na_ref, vxc_na_ref, vx_na_ref, kc_d_ref, vxc_d_ref, vx_d_ref, *,
                    one_minus_lam_init):
    TQ = NA_Q_TILE
    T = DEC_SEQ
    n_qt = T // TQ
    n_na, n_diff, n_ret = NA_HEADS // 2, DIFF_HEADS // 2, RET_HEADS // 2
    qt = pl.program_id(1)
    q0 = pl.multiple_of(qt * TQ, TQ)
    half = _lane_group(TQ, HEAD_DIM)
    quarter = _lane_group(TQ, DIFF_D)

    @pl.when(qt == 0)
    def _():
        kr = _rope(kv_ref[:, KV_DK:KV_DK + DIFF_W].astype(F32), cos_ref[...], sin_ref[...])
        kr_ref[...] = kr.astype(BF16)
        kc_na_ref[...] = cnak_ref[...].astype(BF16)
        kc_d_ref[...] = cdk_ref[...].astype(BF16)
        for j in range(n_na):
            vxc_na_ref[j] = _with_ones(cnav_ref[:, j * PAIR:(j + 1) * PAIR].astype(BF16))
            vx_na_ref[j] = _with_ones(kv_ref[:, KV_NAV + j * PAIR:KV_NAV + (j + 1) * PAIR])
        for j in range(n_diff):
            vxc_d_ref[j] = _with_ones(cdv_ref[:, j * PAIR:(j + 1) * PAIR].astype(BF16))
            vx_d_ref[j] = _with_ones(kv_ref[:, KV_DV + j * PAIR:KV_DV + (j + 1) * PAIR])
        _decay_table(lg_ref, dm_ref)
        jl = lax.broadcasted_iota(jnp.int32, (TQ, PAIR), 0).astype(F32)
        rows = lax.broadcasted_iota(jnp.int32, (PAIR, PAIR), 0) // HEAD_DIM
        cols = lax.broadcasted_iota(jnp.int32, (PAIR, PAIR), 1) // HEAD_DIM
        for j in range(n_ret):
            lf = _pair_scalar(lg_ref, 0, j, half)
            lb = _pair_scalar(lg_ref, 1, j, half)
            dec_f = jnp.exp2((TQ - 1.0 - jl) * lf)
            dec_b = jnp.exp2(jl * lb)
            tile_f = jnp.exp2(float(TQ) * _pair_scalar(lg_ref, 0, j, rows))
            tile_b = jnp.exp2(float(TQ) * _pair_scalar(lg_ref, 1, j, rows))
            loc_f, loc_b = [], []
            for t in range(n_qt):
                kf = kv_ref[t * TQ:(t + 1) * TQ, KV_RETK + j * PAIR:KV_RETK + (j + 1) * PAIR].astype(F32)
                v = kv_ref[t * TQ:(t + 1) * TQ, KV_RETV + j * PAIR:KV_RETV + (j + 1) * PAIR]
                loc_f.append(jnp.where(rows == cols, _dot_tn((kf * dec_f).astype(BF16), v), 0.0))
                loc_b.append(jnp.where(rows == cols, _dot_tn((kf * dec_b).astype(BF16), v), 0.0))
            state = st0_ref[0, j]
            for t in range(n_qt):
                sf_ref[t, j] = state.astype(BF16)
                state = state * tile_f + loc_f[t]
            state = st0_ref[1, j]
            for t in reversed(range(n_qt)):
                sb_ref[t, j] = state.astype(BF16)
                state = state * tile_b + loc_b[t]

    ks = jnp.clip(qt * (TQ // GRID_W) - NA_WIN_ROWS // 2, 0, GRID_ROWS - NA_KEY_ROWS)
    k0 = pl.multiple_of(ks * GRID_W, 256)
    for j in range(n_na):
        c = j * PAIR
        q = q_ref[:, Q_NA + c:Q_NA + c + PAIR]
        ks_list = [kv_ref[pl.ds(k0, NA_KEYS), KV_NAK + c:KV_NAK + c + PAIR], kc_na_ref[:, c:c + PAIR]]
        vx_list = [vx_na_ref[j, pl.ds(k0, NA_KEYS), :], vxc_na_ref[j]]
        qm = jnp.concatenate([_keep_group(q, half, 0), _keep_group(q, half, 1)], axis=0)
        bias = jnp.concatenate([bias_ref[2 * j], bias_ref[2 * j + 1]], axis=0)
        o = _attend(qm, ks_list, vx_list, bias)
        mix_ref[:, c:c + PAIR] = jnp.where(half == 0, o[:TQ], o[TQ:])

    gm = _group_mean_matrix()
    ii = lax.broadcasted_iota(jnp.int32, (TQ, PAIR), 0).astype(F32)
    for j in range(n_ret):
        c = j * PAIR
        q = q_ref[:, Q_RET + c:Q_RET + c + PAIR]
        k = kv_ref[pl.ds(q0, TQ), KV_RETK + c:KV_RETK + c + PAIR]
        v = kv_ref[pl.ds(q0, TQ), KV_RETV + c:KV_RETV + c + PAIR]
        inner = []
        for g in range(2):
            h = 2 * j + g
            sc = _dot_nt(_keep_group(q, half, g), k) * dm_ref[h * TQ:(h + 1) * TQ, :]
            inner.append(_dot(sc.astype(BF16), v))
        qf = q.astype(F32)
        q_fwd = (qf * jnp.exp2((ii + 1.0) * _pair_scalar(lg_ref, 0, j, half))).astype(BF16)
        q_bwd = (qf * jnp.exp2((TQ - ii) * _pair_scalar(lg_ref, 1, j, half))).astype(BF16)
        o = jnp.where(half == 0, inner[0], inner[1]) + _dot(q_fwd, sf_ref[qt, j]) + _dot(q_bwd, sb_ref[qt, j])
        gate = q_ref[:, Q_GATE + c:Q_GATE + c + PAIR].astype(F32)
        mix_ref[:, MIX_RET + c:MIX_RET + c + PAIR] = (
            (_head_layer_norm(o, gm) * gnw_ref[:, c:c + PAIR] + gnb_ref[:, c:c + PAIR]) * _silu(gate))

    lam = lam_ref[0]
    qr = _rope(q_ref[:, Q_DIFF:Q_DIFF + DIFF_W].astype(F32),
               cos_ref[pl.ds(q0, TQ), :], sin_ref[pl.ds(q0, TQ), :]).astype(BF16)
    for j in range(n_diff):
        c = j * PAIR
        q = qr[:, c:c + PAIR]
        ks_list = [kc_d_ref[:, c:c + PAIR], kr_ref[:, c:c + PAIR]]
        vx_list = [vxc_d_ref[j], vx_d_ref[j]]
        heads = []
        for a in range(2):
            qm = jnp.concatenate([_keep_group(q, quarter, 2 * a), _keep_group(q, quarter, 2 * a + 1)], axis=0)
            o = _attend(qm, ks_list, vx_list)
            heads.append(o[:TQ] - lam * o[TQ:])
        blk = jnp.where(half == 0, heads[0], heads[1])
        mix_ref[:, MIX_DIFF + c:MIX_DIFF + c + PAIR] = (
            _head_rms_norm(blk, gm) * dnw_ref[:, c:c + PAIR] * one_minus_lam_init)

    _out_proj_post_norm(mix_ref, wout_ref, x_ref, mod_ref, lnw_ref, lnb_ref, x1_ref)


def _lat_mix(q_arr, kv_arr, x2d, mod_l, w_out_bf, log_g, lam, gnw, gnb, dnw, lnw, lnb,
             cache_na_k, cache_na_v, state_ret, cache_diff_k, cache_diff_v, bias_tab, cos_tab,
             sin_tab, layer, lam_init):
    nq = DEC_SEQ // NA_Q_TILE
    const2 = lambda b, t: (0, 0)
    smem = pl.BlockSpec(memory_space=pltpu.SMEM)
    qrow = lambda b, t: (b * nq + t, 0)

    def variant(b, t):
        return (jnp.where(t == 0, 0, jnp.where(t == nq - 1, 2, 1)), 0, 0, 0)

    return pl.pallas_call(
        functools.partial(_lat_mix_kernel, one_minus_lam_init=1.0 - lam_init),
        grid=(DEC_BATCH, nq),
        in_specs=[
            smem, smem,
            pl.BlockSpec((NA_Q_TILE, Q_COLS), qrow),
            pl.BlockSpec((DEC_SEQ, KV_COLS), lambda b, t: (b, 0), pipeline_mode=pl.Buffered(1)),
            pl.BlockSpec((NA_Q_TILE, D_MODEL), qrow),
            pl.BlockSpec((None, None, 6, D_MODEL), lambda b, t: (layer, b + 1, 0, 0)),
            pl.BlockSpec((None, MIX_W, D_MODEL), lambda b, t: (layer, 0, 0)),
            pl.BlockSpec((1, RET_W), const2),
            pl.BlockSpec((1, RET_W), const2),
            pl.BlockSpec((1, DIFF_W), const2),
            pl.BlockSpec((1, D_MODEL), const2),
            pl.BlockSpec((1, D_MODEL), const2),
            pl.BlockSpec((None, None, PAST_LEN, NA_W), lambda b, t: (b, layer, 0, 0),
                         pipeline_mode=pl.Buffered(1)),
            pl.BlockSpec((None, None, PAST_LEN, NA_W), lambda b, t: (b, layer, 0, 0),
                         pipeline_mode=pl.Buffered(1)),
            pl.BlockSpec((None, None, 2, RET_HEADS // 2, PAIR, PAIR),
                         lambda b, t: (b, layer, 0, 0, 0, 0)),
            pl.BlockSpec((None, None, PAST_LEN, DIFF_W), lambda b, t: (b, layer, 0, 0),
                         pipeline_mode=pl.Buffered(1)),
            pl.BlockSpec((None, None, PAST_LEN, DIFF_W), lambda b, t: (b, layer, 0, 0),
                         pipeline_mode=pl.Buffered(1)),
            pl.BlockSpec((None, NA_HEADS, NA_Q_TILE, NA_KEYS), variant, pipeline_mode=pl.Buffered(1)),
            pl.BlockSpec((DEC_SEQ, DIFF_W), const2),
            pl.BlockSpec((DEC_SEQ, DIFF_W), const2),
        ],
        out_specs=pl.BlockSpec((NA_Q_TILE, D_MODEL), qrow),
        out_shape=jax.ShapeDtypeStruct((DEC_BATCH * DEC_SEQ, D_MODEL), F32),
        scratch_shapes=[
            pltpu.VMEM((NA_Q_TILE, MIX_W), F32),
            pltpu.VMEM((DEC_SEQ, DIFF_W), BF16),
            pltpu.VMEM((RET_HEADS * NA_Q_TILE, NA_Q_TILE), F32),
            pltpu.VMEM((DEC_SEQ // NA_Q_TILE, RET_HEADS // 2, PAIR, PAIR), BF16),
            pltpu.VMEM((DEC_SEQ // NA_Q_TILE, RET_HEADS // 2, PAIR, PAIR), BF16),
            pltpu.VMEM((PAST_LEN, NA_W), BF16),
            pltpu.VMEM((NA_HEADS // 2, PAST_LEN, 2 * PAIR), BF16),
            pltpu.VMEM((NA_HEADS // 2, DEC_SEQ, 2 * PAIR), BF16),
            pltpu.VMEM((PAST_LEN, DIFF_W), BF16),
            pltpu.VMEM((DIFF_HEADS // 2, PAST_LEN, 2 * PAIR), BF16),
            pltpu.VMEM((DIFF_HEADS // 2, DEC_SEQ, 2 * PAIR), BF16),
        ],
        compiler_params=_cparams(("arbitrary", "arbitrary")),
        name="lat_mix",
    )(log_g, lam, q_arr, kv_arr, x2d, mod_l, w_out_bf, gnw, gnb, dnw, lnw, lnb,
      cache_na_k, cache_na_v, state_ret, cache_diff_k, cache_diff_v, bias_tab, cos_tab, sin_tab)


MOE_PART = 2048
MOE_TILE = 512
MOE_EB = 8
MOE_CH = 48
MOE_STEPS = N_EXPERTS // MOE_EB
MOE_ROUTE_ROWS = 40


def _route_transposed(lt):
    shape = lt.shape
    r = lax.broadcasted_iota(jnp.int32, shape, 0).astype(F32)
    ninf = -jnp.inf
    is_g = jnp.where(r >= N_EXPERTS, jnp.where(r < N_EXPERTS + N_GROUPS, 1.0, 0.0), 0.0) > 0.5
    gl = jnp.where(is_g, lt, ninf)
    gmax = jnp.max(gl, axis=0, keepdims=True)
    gsel = jnp.min(jnp.where(gl == gmax, r - N_EXPERTS, 1e9), axis=0, keepdims=True)
    gsum = jnp.sum(jnp.where(is_g, jnp.exp(gl - gmax), 0.0), axis=0, keepdims=True)
    gw = 1.0 / gsum
    lo = gsel * EXPERTS_PER_GROUP
    is_e = jnp.where(r >= lo, jnp.where(r < lo + EXPERTS_PER_GROUP, 1.0, 0.0), 0.0) > 0.5
    el = jnp.where(is_e, lt, ninf)
    v1 = jnp.max(el, axis=0, keepdims=True)
    i1 = jnp.min(jnp.where(el == v1, r, 1e9), axis=0, keepdims=True)
    el2 = jnp.where(r == i1, ninf, el)
    v2 = jnp.max(el2, axis=0, keepdims=True)
    i2 = jnp.min(jnp.where(el2 == v2, r, 1e9), axis=0, keepdims=True)
    t = jnp.exp(v2 - v1)
    w1 = gw / (1.0 + t)
    w2 = gw * t / (1.0 + t)
    first = r == i1
    second = r == i2
    gates = jnp.where(first, w1, 0.0) + jnp.where(second, w2, 0.0)
    member = jnp.where(first, 1.0, jnp.where(second, 1.0, 0.0))
    return gates, member


def _moe_kernel(x_ref, mod_ref, wr_ref, wg_ref, wu_ref, wd_ref, lnw_ref, lnb_ref, out_ref,
                h_ref, rank_ref, gate_ref, sel_ref, xs_ref, ys_ref):
    s = pl.program_id(1)
    n_tiles = MOE_PART // MOE_TILE

    @pl.when(s == 0)
    def _():
        m = mod_ref[...]
        h = x_ref[...] * (1.0 + m[4:5]) + m[3:4]
        h_hi = h.astype(BF16)
        h_ref[...] = h_hi
        h_lo = (h - h_hi.astype(F32)).astype(BF16)
        w = wr_ref[...]
        w_hi = w.astype(BF16)
        w_lo = (w - w_hi.astype(F32)).astype(BF16)
        lt = _dot_nt(w_hi, h_hi) + (_dot_nt(w_hi, h_lo) + _dot_nt(w_lo, h_hi))
        gates, member = _route_transposed(lt)
        before = jnp.where(lax.broadcasted_iota(jnp.int32, (MOE_TILE, MOE_TILE), 0)
                           < lax.broadcasted_iota(jnp.int32, (MOE_TILE, MOE_TILE), 1), 1.0, 0.0).astype(BF16)
        rank_ref[...] = jnp.full(rank_ref.shape, -1.0, F32)
        gate_ref[...] = jnp.zeros(gate_ref.shape, F32)
        for t in range(n_tiles):
            c0 = t * MOE_TILE
            mem_t = member[0:N_EXPERTS, c0:c0 + MOE_TILE]
            cnt = _dot(mem_t.astype(BF16), before)
            rank = jnp.where(mem_t > 0.5, cnt, -1.0)
            for st in range(MOE_STEPS):
                rank_ref[st, 0:MOE_EB, c0:c0 + MOE_TILE] = rank[st * MOE_EB:(st + 1) * MOE_EB]
                gate_ref[st, 0:MOE_EB, c0:c0 + MOE_TILE] = gates[st * MOE_EB:(st + 1) * MOE_EB, c0:c0 + MOE_TILE]
        out_ref[...] = jnp.zeros(out_ref.shape, F32)

    ranks = rank_ref[s, 0:MOE_EB, :]
    gts = gate_ref[s, 0:MOE_EB, :]
    n_chunks = ((jnp.max(ranks) + 0.5) * (1.0 / MOE_CH)).astype(jnp.int32) + 1

    def chunk_body(k, carry):
        slot = (lax.broadcasted_iota(jnp.int32, (MOE_CH, MOE_TILE), 0) + k * MOE_CH).astype(F32)
        row_gate = [[] for _ in range(MOE_EB)]
        for t in range(n_tiles):
            c0 = t * MOE_TILE
            onehots = []
            for i in range(MOE_EB):
                hit = ranks[i:i + 1, c0:c0 + MOE_TILE] == slot
                onehots.append(jnp.where(hit, 1.0, 0.0).astype(BF16))
                row_gate[i].append(jnp.sum(jnp.where(hit, gts[i:i + 1, c0:c0 + MOE_TILE], 0.0),
                                           axis=1, keepdims=True))
            sel = jnp.concatenate(onehots, axis=0)
            sel_ref[t] = sel
            xs = _dot(sel, h_ref[c0:c0 + MOE_TILE, :]).astype(BF16)
            for i in range(MOE_EB):
                xs_ref[i, t * MOE_CH:(t + 1) * MOE_CH, :] = xs[i * MOE_CH:(i + 1) * MOE_CH]
        for i in range(MOE_EB):
            xi = xs_ref[i]
            a = _dot(xi, wg_ref[i])
            u = _dot(xi, wu_ref[i])
            hm = (_silu(a) * u * jnp.concatenate(row_gate[i], axis=0)).astype(BF16)
            ys_ref[i] = _dot(hm, wd_ref[i]).astype(BF16)
        for t in range(n_tiles):
            c0 = t * MOE_TILE
            y = jnp.concatenate([ys_ref[i, t * MOE_CH:(t + 1) * MOE_CH, :] for i in range(MOE_EB)], axis=0)
            out_ref[c0:c0 + MOE_TILE, :] += _dot_tn(sel_ref[t], y)
        return carry

    lax.fori_loop(0, n_chunks, chunk_body, 0)

    @pl.when(s == MOE_STEPS - 1)
    def _():
        m = mod_ref[...]
        z = ALPHA * x_ref[...] + m[5:6] * out_ref[...]
        out_ref[...] = _layer_norm_rows(z, lnw_ref[...], lnb_ref[...])


def _moe(x2d, mod_l, mod_row_fn, wr_t, wg_bf, wu_bf, wd_bf, lnw, lnb, layer, name):
    n = x2d.shape[0]
    row = lambda p, s: (p, 0)
    const2 = lambda p, s: (0, 0)
    wspec = lambda shape: pl.BlockSpec((None, MOE_EB) + shape, lambda p, s: (layer, s, 0, 0))
    return pl.pallas_call(
        _moe_kernel,
        grid=(n // MOE_PART, MOE_STEPS),
        in_specs=[
            pl.BlockSpec((MOE_PART, D_MODEL), row, pipeline_mode=pl.Buffered(1)),
            pl.BlockSpec((None, None, 6, D_MODEL), lambda p, s: (layer, mod_row_fn(p * MOE_PART), 0, 0)),
            pl.BlockSpec((None, MOE_ROUTE_ROWS, D_MODEL), lambda p, s: (layer, 0, 0)),
            wspec((D_MODEL, EXPERT_FF)),
            wspec((D_MODEL, EXPERT_FF)),
            wspec((EXPERT_FF, D_MODEL)),
            pl.BlockSpec((1, D_MODEL), const2),
            pl.BlockSpec((1, D_MODEL), const2),
        ],
        out_specs=pl.BlockSpec((MOE_PART, D_MODEL), row, pipeline_mode=pl.Buffered(1)),
        out_shape=jax.ShapeDtypeStruct((n, D_MODEL), F32),
        scratch_shapes=[
            pltpu.VMEM((MOE_PART, D_MODEL), BF16),
            pltpu.VMEM((MOE_STEPS, 8, MOE_PART), F32),
            pltpu.VMEM((MOE_STEPS, 8, MOE_PART), F32),
            pltpu.VMEM((MOE_PART // MOE_TILE, MOE_EB * MOE_CH, MOE_TILE), BF16),
            pltpu.VMEM((MOE_EB, MOE_PART // MOE_TILE * MOE_CH, D_MODEL), BF16),
            pltpu.VMEM((MOE_EB, MOE_PART // MOE_TILE * MOE_CH, D_MODEL), BF16),
        ],
        compiler_params=_cparams(("arbitrary", "arbitrary")),
        name=name,
    )(x2d, mod_l, wr_t, wg_bf, wu_bf, wd_bf, lnw, lnb)


def _na_bias_tables(rel_bias):
    q_rows = NA_Q_TILE // GRID_W
    n_dr = 2 * NA_WIN_ROWS - 1
    pad_c = GRID_W - NA_WIN_COLS
    padded = jnp.pad(rel_bias.astype(F32), ((0, 0), (0, 0), (pad_c, pad_c)), mode="edge")
    col_tab = jnp.stack([padded[:, :, GRID_W - 1 - qc:2 * GRID_W - 1 - qc] for qc in range(GRID_W)], axis=2)
    pad_r = NA_KEY_ROWS - q_rows
    col_tab = jnp.pad(col_tab, ((0, 0), (pad_r, pad_r), (0, 0), (0, 0)))
    qi = np.arange(NA_Q_TILE)
    ki = np.arange(NA_KEYS)
    tabs = []
    for r0, ks in ((0, 0), (8, 4), (GRID_ROWS - q_rows, GRID_ROWS - NA_KEY_ROWS)):
        rows = []
        for a in range(q_rows):
            start = ks - r0 - a + (NA_WIN_ROWS - 1) + pad_r
            blk = col_tab[:, start:start + NA_KEY_ROWS]
            rows.append(blk.transpose(0, 2, 1, 3).reshape(NA_HEADS, GRID_W, NA_KEYS))
        b = jnp.concatenate(rows, axis=1)
        q_row = r0 + qi // GRID_W
        q_col = qi % GRID_W
        k_row = ks + ki // GRID_W
        k_col = ki % GRID_W
        row_start = np.clip(q_row - NA_WIN_ROWS // 2, 0, GRID_ROWS - NA_WIN_ROWS)
        row_ok = (k_row[None, :] >= row_start[:, None]) & (k_row[None, :] < row_start[:, None] + NA_WIN_ROWS)
        col_start = np.clip(q_col - NA_WIN_COLS // 2, 0, GRID_W - NA_WIN_COLS)
        col_ok = (k_col[None, :] >= col_start[:, None]) & (k_col[None, :] < col_start[:, None] + NA_WIN_COLS)
        dr = k_row[None, :] - q_row[:, None] + (NA_WIN_ROWS - 1)
        assert np.all((dr[row_ok] >= 0) & (dr[row_ok] < n_dr))
        tabs.append(jnp.where(jnp.asarray(row_ok & col_ok)[None], b * LOG2E, NEG_INF))
    return jnp.stack(tabs, axis=0)


def _rope_tables():
    n = DIFF_D // 4
    lane = np.arange(DIFF_W)
    d = lane % DIFF_D
    use_col = d >= DIFF_D // 2
    e = d % (DIFF_D // 2)
    f = e % n
    first = e < n
    t = jnp.arange(DEC_SEQ)
    pos = jnp.where(jnp.asarray(use_col)[None, :], (t % GRID_W)[:, None], (t // GRID_W)[:, None])
    freqs = ROPE_BASE ** (-jnp.arange(n, dtype=F32) / n)
    ang = pos.astype(F32) * freqs[jnp.asarray(f)][None, :]
    sign = jnp.asarray(np.where(first, -1.0, 1.0), dtype=F32)
    return jnp.cos(ang), jnp.sin(ang) * sign[None, :]


def kernel(x_prompt, x_sample, c, cache_na_k, cache_na_v, state_ret, cache_diff_k, cache_diff_v, c_ctx,
           w_mod, b_mod, w_in, na_rel_bias, ret_decay, ret_gn_w, ret_gn_b, diff_lambda, diff_norm_w,
           w_out, ln1_w, ln1_b, router_group, router_expert, exp_w_gate, exp_w_up, exp_w_down,
           ln2_w, ln2_b):
    n_ctx = BATCH * SEQ
    n_lat = DEC_BATCH * DEC_SEQ
    x_ctx = x_prompt.reshape(n_ctx, D_MODEL)
    x_lat = x_sample.reshape(n_lat, D_MODEL)

    cond8 = jnp.concatenate([c_ctx[None, :], c, jnp.zeros((8 - 1 - DEC_BATCH, D_MODEL), F32)], axis=0)
    mod = _modulation(cond8, w_mod, b_mod)

    cache_na_k = cache_na_k.reshape(DEC_BATCH, DEPTH, PAST_LEN, NA_W)
    cache_na_v = cache_na_v.reshape(DEC_BATCH, DEPTH, PAST_LEN, NA_W)
    cache_diff_k = cache_diff_k.reshape(DEC_BATCH, DEPTH, PAST_LEN, DIFF_W)
    cache_diff_v = cache_diff_v.reshape(DEC_BATCH, DEPTH, PAST_LEN, DIFF_W)
    cos_tab, sin_tab = _rope_tables()
    st0 = state_ret.astype(F32).reshape(DEC_BATCH, DEPTH, 2, RET_HEADS // 2, 2, HEAD_DIM, 1, HEAD_DIM)
    st0 = (st0 * jnp.eye(2, dtype=F32)[:, None, :, None]).reshape(
        DEC_BATCH, DEPTH, 2, RET_HEADS // 2, PAIR, PAIR)

    ctx_row = lambda r: 0
    lat_row = lambda r: 1 + r // DEC_SEQ

    wg_bf = exp_w_gate.astype(BF16)
    wu_bf = exp_w_up.astype(BF16)
    wd_bf = exp_w_down.astype(BF16)

    w_in_bf = w_in.astype(BF16)
    w_out_bf = w_out.astype(BF16)
    wr_t = jnp.concatenate([jnp.swapaxes(router_expert, 1, 2), jnp.swapaxes(router_group, 1, 2),
                            jnp.zeros((DEPTH, MOE_ROUTE_ROWS - N_EXPERTS - N_GROUPS, D_MODEL), F32)], axis=1)
    log_g_all = jax.nn.log_sigmoid(ret_decay.astype(F32)) * LOG2E
    lp = diff_lambda.astype(F32)
    lam_dyn = jnp.exp(jnp.sum(lp[:, 0] * lp[:, 1], axis=-1)) - jnp.exp(jnp.sum(lp[:, 2] * lp[:, 3], axis=-1))

    caches = None
    states = None
    for l in range(DEPTH):
        lam_init = 0.8 - 0.6 * math.exp(-0.3 * l)
        lam = (lam_dyn[l] + lam_init).reshape(1)
        log_g = log_g_all[l]
        gnw = ret_gn_w[l].reshape(1, RET_W)
        gnb = ret_gn_b[l].reshape(1, RET_W)
        dnw = diff_norm_w[l].reshape(1, DIFF_W)
        l1w = ln1_w[l].reshape(1, D_MODEL)
        l1b = ln1_b[l].reshape(1, D_MODEL)
        l2w = ln2_w[l].reshape(1, D_MODEL)
        l2b = ln2_b[l].reshape(1, D_MODEL)
        bias_tab = _na_bias_tables(na_rel_bias[l])

        q_c, kv_c, *caches = _project(x_ctx, mod, w_in_bf, ctx_row, l, emit_caches=True, caches=caches)
        x1_c, states = _ctx_mix(q_c, kv_c, x_ctx, mod, w_out_bf, log_g, lam, gnw, gnb, dnw, l1w, l1b,
                                states, l, lam_init)
        x_ctx = _moe(x1_c, mod, ctx_row, wr_t, wg_bf, wu_bf, wd_bf, l2w, l2b, l, "moe_ctx")

        q_l, kv_l = _project(x_lat, mod, w_in_bf, lat_row, l)
        x1_l = _lat_mix(q_l, kv_l, x_lat, mod, w_out_bf, log_g, lam, gnw, gnb, dnw, l1w, l1b,
                        cache_na_k, cache_na_v, st0, cache_diff_k, cache_diff_v,
                        bias_tab, cos_tab, sin_tab, l, lam_init)
        x_lat = _moe(x1_l, mod, lat_row, wr_t, wg_bf, wu_bf, wd_bf, l2w, l2b, l, "moe_lat")

    new_na_k, new_na_v, new_diff_k, new_diff_v = caches
    return (x_ctx.reshape(BATCH, SEQ, D_MODEL), x_lat.reshape(DEC_BATCH, DEC_SEQ, D_MODEL),
            new_na_k.reshape(BATCH, DEPTH, SEQ, NA_HEADS, HEAD_DIM),
            new_na_v.reshape(BATCH, DEPTH, SEQ, NA_HEADS, HEAD_DIM),
            states,
            new_diff_k.reshape(BATCH, DEPTH, SEQ, DIFF_HEADS, 2 * DIFF_D),
            new_diff_v.reshape(BATCH, DEPTH, SEQ, DIFF_HEADS, DIFF_DV))
```

```python
import functools
import math

import numpy as np
import jax
import jax.numpy as jnp
from jax import lax
from jax.experimental import pallas as pl
from jax.experimental.pallas import tpu as pltpu

D_MODEL = 1024
BATCH = 32
SEQ = 256
DEPTH = 2
DEC_BATCH = 2
DEC_SEQ = 2048
PAST_LEN = 512
GRID_W = 64
HEAD_DIM = 64
NA_HEADS = 6
NA_WIN_ROWS = 8
NA_WIN_COLS = 16
RET_HEADS = 6
DIFF_HEADS = 4
DIFF_D = 32
DIFF_DV = 64
NA_W = NA_HEADS * HEAD_DIM
RET_W = RET_HEADS * HEAD_DIM
DIFF_W = DIFF_HEADS * DIFF_DV
MIX_W = NA_W + RET_W + DIFF_W
IN_COLS = 3 * NA_W + 4 * RET_W + 3 * DIFF_W
N_GROUPS = 4
EXPERTS_PER_GROUP = 8
N_EXPERTS = N_GROUPS * EXPERTS_PER_GROUP
EXPERT_FF = 256
ROPE_BASE = 10000.0
LN_EPS = 1e-5
NEG_INF = -1e30
ALPHA = (2.0 * DEPTH) ** 0.25
LOG2E = math.log2(math.e)

F32 = jnp.float32
BF16 = jnp.bfloat16

Q_NA, Q_RET, Q_GATE, Q_DIFF = 0, NA_W, NA_W + RET_W, NA_W + 2 * RET_W
Q_COLS = NA_W + 2 * RET_W + DIFF_W
KV_NAK, KV_NAV = 0, NA_W
KV_RETK, KV_RETV = 2 * NA_W, 2 * NA_W + RET_W
KV_DK, KV_DV = 2 * NA_W + 2 * RET_W, 2 * NA_W + 2 * RET_W + DIFF_W
KV_COLS = 2 * NA_W + 2 * RET_W + 2 * DIFF_W
MIX_RET, MIX_DIFF = NA_W, NA_W + RET_W

NA_Q_TILE = 256
NA_KEY_ROWS = 12
NA_KEYS = NA_KEY_ROWS * GRID_W
CTX_PROBLEMS = NA_HEADS + 2 * DIFF_HEADS + RET_HEADS
GRID_ROWS = DEC_SEQ // GRID_W
PAIR = 2 * HEAD_DIM

VMEM_LIMIT = 60 * 1024 * 1024


def _cparams(sem):
    return pltpu.CompilerParams(dimension_semantics=sem, vmem_limit_bytes=VMEM_LIMIT)


def _dot(a, b):
    return jnp.dot(a, b, preferred_element_type=F32)


def _dot_nt(a, b):
    return lax.dot_general(a, b, (((1,), (1,)), ((), ())), preferred_element_type=F32)


def _dot_tn(a, b):
    return lax.dot_general(a, b, (((0,), (0,)), ((), ())), preferred_element_type=F32)


def _silu(x):
    return x / (1.0 + jnp.exp(-x))


def _layer_norm_rows(z, w, b):
    mu = jnp.mean(z, axis=-1, keepdims=True)
    zc = z - mu
    var = jnp.mean(zc * zc, axis=-1, keepdims=True)
    return zc * lax.rsqrt(var + LN_EPS) * w + b


def _mod_kernel(cond_ref, w_ref, b_ref, o_ref):
    s = _silu(cond_ref[...])
    w = w_ref[...]
    n_cond = 1 + DEC_BATCH
    rows = [jnp.sum(s[:, r:r + 1] * w, axis=0, keepdims=True) for r in range(n_cond)]
    rows.append(jnp.zeros((8 - n_cond, w.shape[1]), F32))
    o_ref[...] = jnp.concatenate(rows, axis=0) + b_ref[...]


def _modulation(cond_t, w_mod, b_mod):
    nj = 6
    out = pl.pallas_call(
        _mod_kernel,
        grid=(DEPTH, nj),
        in_specs=[
            pl.BlockSpec((D_MODEL, 8), lambda l, j: (0, 0)),
            pl.BlockSpec((None, D_MODEL, D_MODEL), lambda l, j: (l, 0, j)),
            pl.BlockSpec((None, 1, D_MODEL), lambda l, j: (l, 0, j)),
        ],
        out_specs=pl.BlockSpec((None, 8, D_MODEL), lambda l, j: (l, 0, j)),
        out_shape=jax.ShapeDtypeStruct((DEPTH, 8, 6 * D_MODEL), F32),
        compiler_params=_cparams(("arbitrary", "arbitrary")),
        name="modulation",
    )(cond_t, w_mod, b_mod.reshape(DEPTH, 1, 6 * D_MODEL))
    return out.reshape(DEPTH, 8, 6, D_MODEL)


def _proj_kernel(x_ref, mod_ref, w_ref, *refs):
    if len(refs) == 10:
        refs = refs[4:]
    q_ref, kv_ref = refs[:2]
    cache_refs = refs[2:]
    m = mod_ref[...]
    h = x_ref[...] * (1.0 + m[1:2]) + m[0:1]
    p = _dot(h.astype(BF16), w_ref[...])
    o = 0
    na_q = p[:, o:o + NA_W] * (HEAD_DIM ** -0.5 * LOG2E); o += NA_W
    na_k = p[:, o:o + NA_W]; o += NA_W
    na_v = p[:, o:o + NA_W]; o += NA_W
    ret_q = p[:, o:o + RET_W]; o += RET_W
    ret_k = p[:, o:o + RET_W] * (HEAD_DIM ** -0.5); o += RET_W
    ret_v = p[:, o:o + RET_W]; o += RET_W
    ret_g = p[:, o:o + RET_W]; o += RET_W
    dq = p[:, o:o + DIFF_W] * (DIFF_D ** -0.5 * LOG2E); o += DIFF_W
    dk = p[:, o:o + DIFF_W]; o += DIFF_W
    dv = p[:, o:o + DIFF_W]
    q_ref[:, Q_NA:Q_NA + NA_W] = na_q.astype(BF16)
    q_ref[:, Q_RET:Q_RET + RET_W] = ret_q.astype(BF16)
    q_ref[:, Q_GATE:Q_GATE + RET_W] = ret_g.astype(BF16)
    q_ref[:, Q_DIFF:Q_DIFF + DIFF_W] = dq.astype(BF16)
    kv_ref[:, KV_NAK:KV_NAK + NA_W] = na_k.astype(BF16)
    kv_ref[:, KV_NAV:KV_NAV + NA_W] = na_v.astype(BF16)
    kv_ref[:, KV_RETK:KV_RETK + RET_W] = ret_k.astype(BF16)
    kv_ref[:, KV_RETV:KV_RETV + RET_W] = ret_v.astype(BF16)
    kv_ref[:, KV_DK:KV_DK + DIFF_W] = dk.astype(BF16)
    kv_ref[:, KV_DV:KV_DV + DIFF_W] = dv.astype(BF16)
    nb = x_ref.shape[0] // SEQ
    for ref, val in zip(cache_refs, (na_k, na_v, dk, dv)):
        val = val.reshape(nb, SEQ, val.shape[-1])
        if len(ref.shape) == 4:
            ref[:, 0] = val
            ref[:, 1:] = jnp.zeros((nb, DEPTH - 1) + val.shape[1:], F32)
        else:
            ref[...] = val


def _project(x2d, mod, w_in_bf, mod_row_fn, layer, emit_caches=False, caches=None, tm=512):
    n = x2d.shape[0]
    row = lambda i: (i, 0)
    out_shape = [jax.ShapeDtypeStruct((n, Q_COLS), BF16), jax.ShapeDtypeStruct((n, KV_COLS), BF16)]
    out_specs = [pl.BlockSpec((tm, Q_COLS), row), pl.BlockSpec((tm, KV_COLS), row)]
    in_specs = [
        pl.BlockSpec((tm, D_MODEL), row),
        pl.BlockSpec((None, None, 6, D_MODEL), lambda i: (layer, mod_row_fn(i * tm), 0, 0)),
        pl.BlockSpec((None, D_MODEL, IN_COLS), lambda i: (layer, 0, 0)),
    ]
    args = [x2d, mod, w_in_bf]
    aliases = {}
    if emit_caches:
        for k, w in enumerate((NA_W, NA_W, DIFF_W, DIFF_W)):
            out_shape.append(jax.ShapeDtypeStruct((BATCH, DEPTH, SEQ, w), F32))
            if caches is None:
                out_specs.append(pl.BlockSpec((tm // SEQ, DEPTH, SEQ, w), lambda i: (i, 0, 0, 0)))
            else:
                in_specs.append(pl.BlockSpec(memory_space=pl.ANY))
                args.append(caches[k])
                aliases[3 + k] = 2 + k
                out_specs.append(pl.BlockSpec((tm // SEQ, None, SEQ, w), lambda i: (i, layer, 0, 0)))
    return pl.pallas_call(
        _proj_kernel,
        grid=(n // tm,),
        in_specs=in_specs,
        out_specs=out_specs,
        out_shape=out_shape,
        input_output_aliases=aliases,
        compiler_params=_cparams(("arbitrary",)),
        name="proj_ctx" if emit_caches else "proj_lat",
    )(*args)


def _lane_group(rows, width):
    return lax.broadcasted_iota(jnp.int32, (rows, PAIR), 1) // width


def _keep_group(x, groups, g):
    return jnp.where(groups == g, x, jnp.zeros_like(x))


def _with_ones(v_pair):
    return jnp.concatenate([v_pair, jnp.ones(v_pair.shape, v_pair.dtype)], axis=1)


def _normalise(r):
    return r[:, :PAIR] * (1.0 / r[:, PAIR:PAIR + 1])


def _attend(qm, k_list, vx_list, bias=None):
    scores = [_dot_nt(qm, k) for k in k_list]
    if bias is not None:
        scores[0] = scores[0] + bias
    m = None
    for s in scores:
        mi = jnp.max(s, axis=-1, keepdims=True)
        m = mi if m is None else jnp.maximum(m, mi)
    r = None
    for s, vx in zip(scores, vx_list):
        ri = _dot(jnp.exp2(s - m).astype(BF16), vx)
        r = ri if r is None else r + ri
    return _normalise(r)


def _group_mean_matrix():
    r = lax.broadcasted_iota(jnp.int32, (PAIR, PAIR), 0) // HEAD_DIM
    c = lax.broadcasted_iota(jnp.int32, (PAIR, PAIR), 1) // HEAD_DIM
    return jnp.where(r == c, 1.0 / HEAD_DIM, 0.0).astype(BF16)


def _group_mean(x, gm):
    hi = x.astype(BF16)
    lo = (x - hi.astype(F32)).astype(BF16)
    return _dot(hi, gm) + _dot(lo, gm)


def _head_layer_norm(o, gm):
    oc = o - _group_mean(o, gm)
    return oc * lax.rsqrt(_group_mean(oc * oc, gm) + LN_EPS)


def _head_rms_norm(o, gm):
    return o * lax.rsqrt(_group_mean(o * o, gm) + LN_EPS)


def _pair_scalar(ref, row, j, groups):
    return jnp.where(groups == 0, ref[row, 2 * j], ref[row, 2 * j + 1])


def _decay_table(lg_ref, dm_ref):
    T = SEQ
    i = lax.broadcasted_iota(jnp.int32, (T, T), 0).astype(F32)
    j = lax.broadcasted_iota(jnp.int32, (T, T), 1).astype(F32)
    d = i - j
    for h in range(RET_HEADS):
        dm_ref[h * T:(h + 1) * T, :] = jnp.exp2(jnp.where(d >= 0, d * lg_ref[0, h], (-d) * lg_ref[1, h]))


def _out_proj_post_norm(mix_ref, wout_ref, x_ref, mod_ref, lnw_ref, lnb_ref, x1_ref):
    y = _dot(mix_ref[...].astype(BF16), wout_ref[...])
    m = mod_ref[...]
    z = ALPHA * x_ref[...] + m[2:3] * y
    x1_ref[...] = _layer_norm_rows(z, lnw_ref[...], lnb_ref[...])


def _ctx_mix_kernel(lg_ref, lam_ref, q_ref, kv_ref, x_ref, mod_ref, wout_ref, gnw_ref, gnb_ref,
                    dnw_ref, lnw_ref, lnb_ref, *rest, one_minus_lam_init):
    if len(rest) == 8:
        rest = rest[1:]
    x1_ref, st_ref, mix_ref, dm_ref, s_ref, e_ref, o_ref = rest
    if len(st_ref.shape) == 5:
        st_ref[1:] = jnp.zeros((DEPTH - 1,) + st_ref.shape[1:], F32)
        st_ref = st_ref.at[0]
    T = SEQ
    n_soft = NA_HEADS + 2 * DIFF_HEADS

    @pl.when(pl.program_id(0) == 0)
    def _():
        _decay_table(lg_ref, dm_ref)

    half = _lane_group(T, HEAD_DIM)
    quarter = _lane_group(T, DIFF_D)
    n_na, n_diff, n_ret = NA_HEADS // 2, DIFF_HEADS // 2, RET_HEADS // 2

    for j in range(n_na):
        q = q_ref[:, Q_NA + j * PAIR:Q_NA + (j + 1) * PAIR]
        k = kv_ref[:, KV_NAK + j * PAIR:KV_NAK + (j + 1) * PAIR]
        for g in range(2):
            r0 = (2 * j + g) * T
            s_ref[r0:r0 + T, :] = _dot_nt(_keep_group(q, half, g), k)
    for j in range(n_diff):
        q = q_ref[:, Q_DIFF + j * PAIR:Q_DIFF + (j + 1) * PAIR]
        k = kv_ref[:, KV_DK + j * PAIR:KV_DK + (j + 1) * PAIR]
        for g in range(4):
            r0 = (NA_HEADS + 4 * j + g) * T
            s_ref[r0:r0 + T, :] = _dot_nt(_keep_group(q, quarter, g), k)
    for j in range(n_ret):
        q = q_ref[:, Q_RET + j * PAIR:Q_RET + (j + 1) * PAIR]
        k = kv_ref[:, KV_RETK + j * PAIR:KV_RETK + (j + 1) * PAIR]
        for g in range(2):
            r0 = (n_soft + 2 * j + g) * T
            s_ref[r0:r0 + T, :] = _dot_nt(_keep_group(q, half, g), k)

    s = s_ref[0:n_soft * T, :]
    e_ref[0:n_soft * T, :] = jnp.exp2(s - jnp.max(s, axis=-1, keepdims=True)).astype(BF16)
    e_ref[n_soft * T:, :] = (s_ref[n_soft * T:, :] * dm_ref[...]).astype(BF16)

    def pv(n, vx):
        return _normalise(_dot(e_ref[n * T:(n + 1) * T, :], vx))

    for j in range(n_na):
        vx = _with_ones(kv_ref[:, KV_NAV + j * PAIR:KV_NAV + (j + 1) * PAIR])
        mix_ref[:, j * PAIR:(j + 1) * PAIR] = jnp.where(half == 0, pv(2 * j, vx), pv(2 * j + 1, vx))
    lam = lam_ref[0]
    for j in range(n_diff):
        vx = _with_ones(kv_ref[:, KV_DV + j * PAIR:KV_DV + (j + 1) * PAIR])
        n0 = NA_HEADS + 4 * j
        head_a = pv(n0, vx) - lam * pv(n0 + 1, vx)
        head_b = pv(n0 + 2, vx) - lam * pv(n0 + 3, vx)
        o_ref[(n_ret + j) * T:(n_ret + j + 1) * T, :] = jnp.where(half == 0, head_a, head_b)
    jj = lax.broadcasted_iota(jnp.int32, (T, PAIR), 0).astype(F32)
    for j in range(n_ret):
        kf = kv_ref[:, KV_RETK + j * PAIR:KV_RETK + (j + 1) * PAIR].astype(F32)
        v = kv_ref[:, KV_RETV + j * PAIR:KV_RETV + (j + 1) * PAIR]
        n0 = n_soft + 2 * j
        o_ref[j * T:(j + 1) * T, :] = jnp.where(half == 0, _dot(e_ref[n0 * T:(n0 + 1) * T, :], v),
                                                _dot(e_ref[(n0 + 1) * T:(n0 + 2) * T, :], v))
        k_fwd = (kf * jnp.exp2((T - 1.0 - jj) * _pair_scalar(lg_ref, 0, j, half))).astype(BF16)
        k_bwd = (kf * jnp.exp2(jj * _pair_scalar(lg_ref, 1, j, half))).astype(BF16)
        for d, kd in enumerate((k_fwd, k_bwd)):
            st = _dot_tn(kd, v)
            st_ref[d, 2 * j] = st[0:HEAD_DIM, 0:HEAD_DIM]
            st_ref[d, 2 * j + 1] = st[HEAD_DIM:PAIR, HEAD_DIM:PAIR]

    gm = _group_mean_matrix()
    rn = _head_layer_norm(o_ref[0:n_ret * T, :], gm)
    dn = _head_rms_norm(o_ref[n_ret * T:, :], gm)
    for j in range(n_ret):
        c = j * PAIR
        g = q_ref[:, Q_GATE + c:Q_GATE + c + PAIR].astype(F32)
        mix_ref[:, MIX_RET + c:MIX_RET + c + PAIR] = (
            (rn[j * T:(j + 1) * T] * gnw_ref[:, c:c + PAIR] + gnb_ref[:, c:c + PAIR]) * _silu(g))
    for j in range(n_diff):
        c = j * PAIR
        mix_ref[:, MIX_DIFF + c:MIX_DIFF + c + PAIR] = (
            dn[j * T:(j + 1) * T] * dnw_ref[:, c:c + PAIR] * one_minus_lam_init)

    _out_proj_post_norm(mix_ref, wout_ref, x_ref, mod_ref, lnw_ref, lnb_ref, x1_ref)


def _ctx_mix(q_arr, kv_arr, x2d, mod, w_out_bf, log_g, lam, gnw, gnb, dnw, lnw, lnb, states, layer,
             lam_init):
    row = lambda b: (b, 0)
    const2 = lambda b: (0, 0)
    smem = pl.BlockSpec(memory_space=pltpu.SMEM)
    st_shape = (BATCH, DEPTH, 2, RET_HEADS, HEAD_DIM, HEAD_DIM)
    in_specs = [
        smem, smem,
        pl.BlockSpec((SEQ, Q_COLS), row),
        pl.BlockSpec((SEQ, KV_COLS), row),
        pl.BlockSpec((SEQ, D_MODEL), row),
        pl.BlockSpec((None, None, 6, D_MODEL), lambda b: (layer, 0, 0, 0)),
        pl.BlockSpec((None, MIX_W, D_MODEL), lambda b: (layer, 0, 0)),
        pl.BlockSpec((None, 1, RET_W), lambda *_: (layer, 0, 0)),
        pl.BlockSpec((None, 1, RET_W), lambda *_: (layer, 0, 0)),
        pl.BlockSpec((None, 1, DIFF_W), lambda *_: (layer, 0, 0)),
        pl.BlockSpec((None, 1, D_MODEL), lambda *_: (layer, 0, 0)),
        pl.BlockSpec((None, 1, D_MODEL), lambda *_: (layer, 0, 0)),
    ]
    args = [log_g, lam, q_arr, kv_arr, x2d, mod, w_out_bf, gnw, gnb, dnw, lnw, lnb]
    if states is None:
        st_spec = pl.BlockSpec((None,) + st_shape[1:], lambda b: (b, 0, 0, 0, 0, 0))
        aliases = {}
    else:
        in_specs.append(pl.BlockSpec(memory_space=pl.ANY))
        args.append(states)
        st_spec = pl.BlockSpec((None, None) + st_shape[2:], lambda b: (b, layer, 0, 0, 0, 0))
        aliases = {12: 1}
    return pl.pallas_call(
        functools.partial(_ctx_mix_kernel, one_minus_lam_init=1.0 - lam_init),
        grid=(BATCH,),
        in_specs=in_specs,
        out_specs=[pl.BlockSpec((SEQ, D_MODEL), row), st_spec],
        out_shape=[
            jax.ShapeDtypeStruct((BATCH * SEQ, D_MODEL), F32),
            jax.ShapeDtypeStruct(st_shape, F32),
        ],
        input_output_aliases=aliases,
        scratch_shapes=[
            pltpu.VMEM((SEQ, MIX_W), F32),
            pltpu.VMEM((RET_HEADS * SEQ, SEQ), F32),
            pltpu.VMEM((CTX_PROBLEMS * SEQ, SEQ), F32),
            pltpu.VMEM((CTX_PROBLEMS * SEQ, SEQ), BF16),
            pltpu.VMEM(((RET_HEADS + DIFF_HEADS) // 2 * SEQ, PAIR), F32),
        ],
        compiler_params=_cparams(("arbitrary",)),
        name="ctx_mix",
    )(*args)


def _rope(x, cos, sin_signed):
    n, w = x.shape
    lane = lax.broadcasted_iota(jnp.int32, (n, w), 1)
    first = (lane % 16) < 8
    partner = jnp.where(first, pltpu.roll(x, w - 8, 1), pltpu.roll(x, 8, 1))
    return x * cos + partner * sin_signed


def _lat_mix_kernel(lg_ref, lam_ref, q_ref, kv_ref, x_ref, mod_ref, wout_ref, gnw_ref, gnb_ref,
                    dnw_ref, lnw_ref, lnb_ref, cnak_ref, cnav_ref, st0_ref, cdk_ref, cdv_ref,
                    bias_ref, cos_ref, sin_ref, x1_ref, mix_ref, kr_ref, dm_ref, sf_ref, sb_ref,
                    kc_na_ref, vxc_na_ref, vx_na_ref, kc_d_ref, vxc_d_ref, vx_d_ref, *,
                    one_minus_lam_init):
    TQ = NA_Q_TILE
    T = DEC_SEQ
    n_qt = T // TQ
    n_na, n_diff, n_ret = NA_HEADS // 2, DIFF_HEADS // 2, RET_HEADS // 2
    qt = pl.program_id(1)
    q0 = pl.multiple_of(qt * TQ, TQ)
    half = _lane_group(TQ, HEAD_DIM)
    quarter = _lane_group(TQ, DIFF_D)

    @pl.when(qt == 0)
    def _():
        kr = _rope(kv_ref[:, KV_DK:KV_DK + DIFF_W].astype(F32), cos_ref[...], sin_ref[...])
        kr_ref[...] = kr.astype(BF16)
        kc_na_ref[...] = cnak_ref[...].astype(BF16)
        kc_d_ref[...] = cdk_ref[...].astype(BF16)
        for j in range(n_na):
            vxc_na_ref[j] = _with_ones(cnav_ref[:, j * PAIR:(j + 1) * PAIR].astype(BF16))
            vx_na_ref[j] = _with_ones(kv_ref[:, KV_NAV + j * PAIR:KV_NAV + (j + 1) * PAIR])
        for j in range(n_diff):
            vxc_d_ref[j] = _with_ones(cdv_ref[:, j * PAIR:(j + 1) * PAIR].astype(BF16))
            vx_d_ref[j] = _with_ones(kv_ref[:, KV_DV + j * PAIR:KV_DV + (j + 1) * PAIR])
        _decay_table(lg_ref, dm_ref)
        jl = lax.broadcasted_iota(jnp.int32, (TQ, PAIR), 0).astype(F32)
        rows = lax.broadcasted_iota(jnp.int32, (PAIR, PAIR), 0) // HEAD_DIM
        cols = lax.broadcasted_iota(jnp.int32, (PAIR, PAIR), 1) // HEAD_DIM
        for j in range(n_ret):
            lf = _pair_scalar(lg_ref, 0, j, half)
            lb = _pair_scalar(lg_ref, 1, j, half)
            dec_f = jnp.exp2((TQ - 1.0 - jl) * lf)
            dec_b = jnp.exp2(jl * lb)
            tile_f = jnp.exp2(float(TQ) * _pair_scalar(lg_ref, 0, j, rows))
            tile_b = jnp.exp2(float(TQ) * _pair_scalar(lg_ref, 1, j, rows))
            loc_f, loc_b = [], []
            for t in range(n_qt):
                kf = kv_ref[t * TQ:(t + 1) * TQ, KV_RETK + j * PAIR:KV_RETK + (j + 1) * PAIR].astype(F32)
                v = kv_ref[t * TQ:(t + 1) * TQ, KV_RETV + j * PAIR:KV_RETV + (j + 1) * PAIR]
                loc_f.append(jnp.where(rows == cols, _dot_tn((kf * dec_f).astype(BF16), v), 0.0))
                loc_b.append(jnp.where(rows == cols, _dot_tn((kf * dec_b).astype(BF16), v), 0.0))
            state = st0_ref[0, j]
            for t in range(n_qt):
                sf_ref[t, j] = state.astype(BF16)
                state = state * tile_f + loc_f[t]
            state = st0_ref[1, j]
            for t in reversed(range(n_qt)):
                sb_ref[t, j] = state.astype(BF16)
                state = state * tile_b + loc_b[t]

    ks = jnp.clip(qt * (TQ // GRID_W) - NA_WIN_ROWS // 2, 0, GRID_ROWS - NA_KEY_ROWS)
    k0 = pl.multiple_of(ks * GRID_W, 256)
    for j in range(n_na):
        c = j * PAIR
        q = q_ref[:, Q_NA + c:Q_NA + c + PAIR]
        ks_list = [kv_ref[pl.ds(k0, NA_KEYS), KV_NAK + c:KV_NAK + c + PAIR], kc_na_ref[:, c:c + PAIR]]
        vx_list = [vx_na_ref[j, pl.ds(k0, NA_KEYS), :], vxc_na_ref[j]]
        o = [_attend(_keep_group(q, half, g), ks_list, vx_list, bias_ref[2 * j + g]) for g in range(2)]
        mix_ref[:, c:c + PAIR] = jnp.where(half == 0, o[0], o[1])

    gm = _group_mean_matrix()
    ii = lax.broadcasted_iota(jnp.int32, (TQ, PAIR), 0).astype(F32)
    for j in range(n_ret):
        c = j * PAIR
        q = q_ref[:, Q_RET + c:Q_RET + c + PAIR]
        k = kv_ref[pl.ds(q0, TQ), KV_RETK + c:KV_RETK + c + PAIR]
        v = kv_ref[pl.ds(q0, TQ), KV_RETV + c:KV_RETV + c + PAIR]
        inner = []
        for g in range(2):
            h = 2 * j + g
            sc = _dot_nt(_keep_group(q, half, g), k) * dm_ref[h * TQ:(h + 1) * TQ, :]
            inner.append(_dot(sc.astype(BF16), v))
        qf = q.astype(F32)
        q_fwd = (qf * jnp.exp2((ii + 1.0) * _pair_scalar(lg_ref, 0, j, half))).astype(BF16)
        q_bwd = (qf * jnp.exp2((TQ - ii) * _pair_scalar(lg_ref, 1, j, half))).astype(BF16)
        o = jnp.where(half == 0, inner[0], inner[1]) + _dot(q_fwd, sf_ref[qt, j]) + _dot(q_bwd, sb_ref[qt, j])
        gate = q_ref[:, Q_GATE + c:Q_GATE + c + PAIR].astype(F32)
        mix_ref[:, MIX_RET + c:MIX_RET + c + PAIR] = (
            (_head_layer_norm(o, gm) * gnw_ref[:, c:c + PAIR] + gnb_ref[:, c:c + PAIR]) * _silu(gate))

    lam = lam_ref[0]
    qr = _rope(q_ref[:, Q_DIFF:Q_DIFF + DIFF_W].astype(F32),
               cos_ref[pl.ds(q0, TQ), :], sin_ref[pl.ds(q0, TQ), :]).astype(BF16)
    for j in range(n_diff):
        c = j * PAIR
        q = qr[:, c:c + PAIR]
        ks_list = [kc_d_ref[:, c:c + PAIR], kr_ref[:, c:c + PAIR]]
        vx_list = [vxc_d_ref[j], vx_d_ref[j]]
        o = [_attend(_keep_group(q, quarter, g), ks_list, vx_list) for g in range(4)]
        blk = jnp.where(half == 0, o[0] - lam * o[1], o[2] - lam * o[3])
        mix_ref[:, MIX_DIFF + c:MIX_DIFF + c + PAIR] = (
            _head_rms_norm(blk, gm) * dnw_ref[:, c:c + PAIR] * one_minus_lam_init)

    _out_proj_post_norm(mix_ref, wout_ref, x_ref, mod_ref, lnw_ref, lnb_ref, x1_ref)


def _lat_mix(q_arr, kv_arr, x2d, mod_l, w_out_bf, log_g, lam, gnw, gnb, dnw, lnw, lnb,
             cache_na_k, cache_na_v, state_ret, cache_diff_k, cache_diff_v, bias_tab, cos_tab,
             sin_tab, layer, lam_init):
    nq = DEC_SEQ // NA_Q_TILE
    const2 = lambda b, t: (0, 0)
    smem = pl.BlockSpec(memory_space=pltpu.SMEM)
    qrow = lambda b, t: (b * nq + t, 0)

    def variant(b, t):
        return (layer, jnp.where(t == 0, 0, jnp.where(t == nq - 1, 2, 1)), 0, 0, 0)

    return pl.pallas_call(
        functools.partial(_lat_mix_kernel, one_minus_lam_init=1.0 - lam_init),
        grid=(DEC_BATCH, nq),
        in_specs=[
            smem, smem,
            pl.BlockSpec((NA_Q_TILE, Q_COLS), qrow),
            pl.BlockSpec((DEC_SEQ, KV_COLS), lambda b, t: (b, 0), pipeline_mode=pl.Buffered(1)),
            pl.BlockSpec((NA_Q_TILE, D_MODEL), qrow),
            pl.BlockSpec((None, None, 6, D_MODEL), lambda b, t: (layer, b + 1, 0, 0)),
            pl.BlockSpec((None, MIX_W, D_MODEL), lambda b, t: (layer, 0, 0)),
            pl.BlockSpec((None, 1, RET_W), lambda *_: (layer, 0, 0)),
            pl.BlockSpec((None, 1, RET_W), lambda *_: (layer, 0, 0)),
            pl.BlockSpec((None, 1, DIFF_W), lambda *_: (layer, 0, 0)),
            pl.BlockSpec((None, 1, D_MODEL), lambda *_: (layer, 0, 0)),
            pl.BlockSpec((None, 1, D_MODEL), lambda *_: (layer, 0, 0)),
            pl.BlockSpec((None, None, PAST_LEN, NA_W), lambda b, t: (b, layer, 0, 0),
                         pipeline_mode=pl.Buffered(1)),
            pl.BlockSpec((None, None, PAST_LEN, NA_W), lambda b, t: (b, layer, 0, 0),
                         pipeline_mode=pl.Buffered(1)),
            pl.BlockSpec((None, None, 2, RET_HEADS // 2, PAIR, PAIR),
                         lambda b, t: (b, layer, 0, 0, 0, 0)),
            pl.BlockSpec((None, None, PAST_LEN, DIFF_W), lambda b, t: (b, layer, 0, 0),
                         pipeline_mode=pl.Buffered(1)),
            pl.BlockSpec((None, None, PAST_LEN, DIFF_W), lambda b, t: (b, layer, 0, 0),
                         pipeline_mode=pl.Buffered(1)),
            pl.BlockSpec((None, None, NA_HEADS, NA_Q_TILE, NA_KEYS), variant, pipeline_mode=pl.Buffered(1)),
            pl.BlockSpec((DEC_SEQ, DIFF_W), const2),
            pl.BlockSpec((DEC_SEQ, DIFF_W), const2),
        ],
        out_specs=pl.BlockSpec((NA_Q_TILE, D_MODEL), qrow),
        out_shape=jax.ShapeDtypeStruct((DEC_BATCH * DEC_SEQ, D_MODEL), F32),
        scratch_shapes=[
            pltpu.VMEM((NA_Q_TILE, MIX_W), F32),
            pltpu.VMEM((DEC_SEQ, DIFF_W), BF16),
            pltpu.VMEM((RET_HEADS * NA_Q_TILE, NA_Q_TILE), F32),
            pltpu.VMEM((DEC_SEQ // NA_Q_TILE, RET_HEADS // 2, PAIR, PAIR), BF16),
            pltpu.VMEM((DEC_SEQ // NA_Q_TILE, RET_HEADS // 2, PAIR, PAIR), BF16),
            pltpu.VMEM((PAST_LEN, NA_W), BF16),
            pltpu.VMEM((NA_HEADS // 2, PAST_LEN, 2 * PAIR), BF16),
            pltpu.VMEM((NA_HEADS // 2, DEC_SEQ, 2 * PAIR), BF16),
            pltpu.VMEM((PAST_LEN, DIFF_W), BF16),
            pltpu.VMEM((DIFF_HEADS // 2, PAST_LEN, 2 * PAIR), BF16),
            pltpu.VMEM((DIFF_HEADS // 2, DEC_SEQ, 2 * PAIR), BF16),
        ],
        compiler_params=_cparams(("arbitrary", "arbitrary")),
        name="lat_mix",
    )(log_g, lam, q_arr, kv_arr, x2d, mod_l, w_out_bf, gnw, gnb, dnw, lnw, lnb,
      cache_na_k, cache_na_v, state_ret, cache_diff_k, cache_diff_v, bias_tab, cos_tab, sin_tab)


MOE_PART = 2048
MOE_TILE = 512
MOE_EB = 4
MOE_CH = 48
MOE_STEPS = N_EXPERTS // MOE_EB
MOE_ROUTE_ROWS = 40


def _route_transposed(lt):
    shape = lt.shape
    r = lax.broadcasted_iota(jnp.int32, shape, 0).astype(F32)
    ninf = -jnp.inf
    is_g = jnp.where(r >= N_EXPERTS, jnp.where(r < N_EXPERTS + N_GROUPS, 1.0, 0.0), 0.0) > 0.5
    gl = jnp.where(is_g, lt, ninf)
    gmax = jnp.max(gl, axis=0, keepdims=True)
    gsel = jnp.min(jnp.where(gl == gmax, r - N_EXPERTS, 1e9), axis=0, keepdims=True)
    gsum = jnp.sum(jnp.where(is_g, jnp.exp(gl - gmax), 0.0), axis=0, keepdims=True)
    gw = 1.0 / gsum
    lo = gsel * EXPERTS_PER_GROUP
    is_e = jnp.where(r >= lo, jnp.where(r < lo + EXPERTS_PER_GROUP, 1.0, 0.0), 0.0) > 0.5
    el = jnp.where(is_e, lt, ninf)
    v1 = jnp.max(el, axis=0, keepdims=True)
    i1 = jnp.min(jnp.where(el == v1, r, 1e9), axis=0, keepdims=True)
    el2 = jnp.where(r == i1, ninf, el)
    v2 = jnp.max(el2, axis=0, keepdims=True)
    i2 = jnp.min(jnp.where(el2 == v2, r, 1e9), axis=0, keepdims=True)
    t = jnp.exp(v2 - v1)
    w1 = gw / (1.0 + t)
    w2 = gw * t / (1.0 + t)
    first = r == i1
    second = r == i2
    gates = jnp.where(first, w1, 0.0) + jnp.where(second, w2, 0.0)
    member = jnp.where(first, 1.0, jnp.where(second, 1.0, 0.0))
    return gates, member


def _moe_kernel(x_ref, mod_ref, wr_ref, wg_ref, wu_ref, wd_ref, lnw_ref, lnb_ref, out_ref,
                h_ref, rank_ref, gate_ref, sel_ref, xs_ref, ys_ref):
    s = pl.program_id(1)
    n_tiles = MOE_PART // MOE_TILE

    @pl.when(s == 0)
    def _():
        m = mod_ref[...]
        h = x_ref[...] * (1.0 + m[4:5]) + m[3:4]
        h_hi = h.astype(BF16)
        h_ref[...] = h_hi
        h_lo = (h - h_hi.astype(F32)).astype(BF16)
        w = wr_ref[...]
        w_hi = w.astype(BF16)
        w_lo = (w - w_hi.astype(F32)).astype(BF16)
        lt = _dot_nt(w_hi, h_hi) + (_dot_nt(w_hi, h_lo) + _dot_nt(w_lo, h_hi))
        gates, member = _route_transposed(lt)
        before = jnp.where(lax.broadcasted_iota(jnp.int32, (MOE_TILE, MOE_TILE), 0)
                           < lax.broadcasted_iota(jnp.int32, (MOE_TILE, MOE_TILE), 1), 1.0, 0.0).astype(BF16)
        rank_ref[...] = jnp.full(rank_ref.shape, -1.0, F32)
        gate_ref[...] = jnp.zeros(gate_ref.shape, F32)
        for t in range(n_tiles):
            c0 = t * MOE_TILE
            mem_t = member[0:N_EXPERTS, c0:c0 + MOE_TILE]
            cnt = _dot(mem_t.astype(BF16), before)
            rank = jnp.where(mem_t > 0.5, cnt, -1.0)
            for st in range(MOE_STEPS):
                rank_ref[st, 0:MOE_EB, c0:c0 + MOE_TILE] = rank[st * MOE_EB:(st + 1) * MOE_EB]
                gate_ref[st, 0:MOE_EB, c0:c0 + MOE_TILE] = gates[st * MOE_EB:(st + 1) * MOE_EB, c0:c0 + MOE_TILE]
        out_ref[...] = jnp.zeros(out_ref.shape, F32)

    ranks = rank_ref[s, 0:MOE_EB, :]
    gts = gate_ref[s, 0:MOE_EB, :]
    n_chunks = ((jnp.max(ranks) + 0.5) * (1.0 / MOE_CH)).astype(jnp.int32) + 1

    def chunk_body(k, carry):
        slot = (lax.broadcasted_iota(jnp.int32, (MOE_CH, MOE_TILE), 0) + k * MOE_CH).astype(F32)
        row_gate = [[] for _ in range(MOE_EB)]
        for t in range(n_tiles):
            c0 = t * MOE_TILE
            onehots = []
            for i in range(MOE_EB):
                hit = ranks[i:i + 1, c0:c0 + MOE_TILE] == slot
                onehots.append(jnp.where(hit, 1.0, 0.0).astype(BF16))
                row_gate[i].append(jnp.sum(jnp.where(hit, gts[i:i + 1, c0:c0 + MOE_TILE], 0.0),
                                           axis=1, keepdims=True))
            sel = jnp.concatenate(onehots, axis=0)
            sel_ref[t] = sel
            xs = _dot(sel, h_ref[c0:c0 + MOE_TILE, :]).astype(BF16)
            for i in range(MOE_EB):
                xs_ref[i, t * MOE_CH:(t + 1) * MOE_CH, :] = xs[i * MOE_CH:(i + 1) * MOE_CH]
        for i in range(MOE_EB):
            xi = xs_ref[i]
            a = _dot(xi, wg_ref[i])
            u = _dot(xi, wu_ref[i])
            hm = (_silu(a) * u * jnp.concatenate(row_gate[i], axis=0)).astype(BF16)
            ys_ref[i] = _dot(hm, wd_ref[i]).astype(BF16)
        for t in range(n_tiles):
            c0 = t * MOE_TILE
            y = jnp.concatenate([ys_ref[i, t * MOE_CH:(t + 1) * MOE_CH, :] for i in range(MOE_EB)], axis=0)
            out_ref[c0:c0 + MOE_TILE, :] += _dot_tn(sel_ref[t], y)
        return carry

    lax.fori_loop(0, n_chunks, chunk_body, 0)

    @pl.when(s == MOE_STEPS - 1)
    def _():
        m = mod_ref[...]
        z = ALPHA * x_ref[...] + m[5:6] * out_ref[...]
        out_ref[...] = _layer_norm_rows(z, lnw_ref[...], lnb_ref[...])


def _moe(x2d, mod_l, mod_row_fn, wr_t, wg_bf, wu_bf, wd_bf, lnw, lnb, layer, name):
    n = x2d.shape[0]
    row = lambda p, s: (p, 0)
    const2 = lambda p, s: (0, 0)
    wspec = lambda shape: pl.BlockSpec((None, MOE_EB) + shape, lambda p, s: (layer, s, 0, 0))
    return pl.pallas_call(
        _moe_kernel,
        grid=(n // MOE_PART, MOE_STEPS),
        in_specs=[
            pl.BlockSpec((MOE_PART, D_MODEL), row, pipeline_mode=pl.Buffered(1)),
            pl.BlockSpec((None, None, 6, D_MODEL), lambda p, s: (layer, mod_row_fn(p * MOE_PART), 0, 0)),
            pl.BlockSpec((None, MOE_ROUTE_ROWS, D_MODEL), lambda p, s: (layer, 0, 0)),
            wspec((D_MODEL, EXPERT_FF)),
            wspec((D_MODEL, EXPERT_FF)),
            wspec((EXPERT_FF, D_MODEL)),
            pl.BlockSpec((None, 1, D_MODEL), lambda *_: (layer, 0, 0)),
            pl.BlockSpec((None, 1, D_MODEL), lambda *_: (layer, 0, 0)),
        ],
        out_specs=pl.BlockSpec((MOE_PART, D_MODEL), row),
        out_shape=jax.ShapeDtypeStruct((n, D_MODEL), F32),
        scratch_shapes=[
            pltpu.VMEM((MOE_PART, D_MODEL), BF16),
            pltpu.VMEM((MOE_STEPS, 8, MOE_PART), F32),
            pltpu.VMEM((MOE_STEPS, 8, MOE_PART), F32),
            pltpu.VMEM((MOE_PART // MOE_TILE, MOE_EB * MOE_CH, MOE_TILE), BF16),
            pltpu.VMEM((MOE_EB, MOE_PART // MOE_TILE * MOE_CH, D_MODEL), BF16),
            pltpu.VMEM((MOE_EB, MOE_PART // MOE_TILE * MOE_CH, D_MODEL), BF16),
        ],
        compiler_params=_cparams(("arbitrary", "arbitrary")),
        name=name,
    )(x2d, mod_l, wr_t, wg_bf, wu_bf, wd_bf, lnw, lnb)


def _na_bias_tables(rel_bias):
    q_rows = NA_Q_TILE // GRID_W
    n_dr = 2 * NA_WIN_ROWS - 1
    pad_c = GRID_W - NA_WIN_COLS
    padded = jnp.pad(rel_bias.astype(F32) * LOG2E, ((0, 0), (0, 0), (0, 0), (pad_c, pad_c)), mode="edge")
    col_tab = jnp.stack([padded[..., GRID_W - 1 - qc:2 * GRID_W - 1 - qc] for qc in range(GRID_W)], axis=2)
    pad_r = NA_KEY_ROWS - q_rows
    col_tab = jnp.pad(col_tab, ((0, 0), (0, 0), (0, 0), (pad_r, pad_r), (0, 0)))
    qi = np.arange(NA_Q_TILE)
    ki = np.arange(NA_KEYS)
    tabs = []
    for r0, ks in ((0, 0), (8, 4), (GRID_ROWS - q_rows, GRID_ROWS - NA_KEY_ROWS)):
        rows = []
        for a in range(q_rows):
            start = ks - r0 - a + (NA_WIN_ROWS - 1) + pad_r
            blk = col_tab[:, :, :, start:start + NA_KEY_ROWS, :]
            rows.append(blk.reshape(DEPTH, NA_HEADS, GRID_W, NA_KEYS))
        b = jnp.concatenate(rows, axis=2)
        q_row = r0 + qi // GRID_W
        q_col = qi % GRID_W
        k_row = ks + ki // GRID_W
        k_col = ki % GRID_W
        row_start = np.clip(q_row - NA_WIN_ROWS // 2, 0, GRID_ROWS - NA_WIN_ROWS)
        row_ok = (k_row[None, :] >= row_start[:, None]) & (k_row[None, :] < row_start[:, None] + NA_WIN_ROWS)
        col_start = np.clip(q_col - NA_WIN_COLS // 2, 0, GRID_W - NA_WIN_COLS)
        col_ok = (k_col[None, :] >= col_start[:, None]) & (k_col[None, :] < col_start[:, None] + NA_WIN_COLS)
        dr = k_row[None, :] - q_row[:, None] + (NA_WIN_ROWS - 1)
        assert np.all((dr[row_ok] >= 0) & (dr[row_ok] < n_dr))
        tabs.append(jnp.where(jnp.asarray(row_ok & col_ok)[None, None], b, NEG_INF))
    return jnp.stack(tabs, axis=1)


def _rope_tables():
    n = DIFF_D // 4
    lane = np.arange(DIFF_W)
    d = lane % DIFF_D
    use_col = d >= DIFF_D // 2
    e = d % (DIFF_D // 2)
    f = e % n
    first = e < n
    t = np.arange(DEC_SEQ)
    pos = np.where(use_col[None, :], (t % GRID_W)[:, None], (t // GRID_W)[:, None]).astype(np.float64)
    freqs = ROPE_BASE ** (-np.arange(n, dtype=np.float64) / n)
    ang = pos * freqs[f][None, :]
    sign = np.where(first, -1.0, 1.0)
    return jnp.asarray(np.cos(ang), F32), jnp.asarray(np.sin(ang) * sign[None, :], F32)


def kernel(x_prompt, x_sample, c, cache_na_k, cache_na_v, state_ret, cache_diff_k, cache_diff_v, c_ctx,
           w_mod, b_mod, w_in, na_rel_bias, ret_decay, ret_gn_w, ret_gn_b, diff_lambda, diff_norm_w,
           w_out, ln1_w, ln1_b, router_group, router_expert, exp_w_gate, exp_w_up, exp_w_down,
           ln2_w, ln2_b):
    n_ctx = BATCH * SEQ
    n_lat = DEC_BATCH * DEC_SEQ
    x_ctx = x_prompt.reshape(n_ctx, D_MODEL)
    x_lat = x_sample.reshape(n_lat, D_MODEL)

    cond_t = jnp.concatenate([c_ctx[:, None], c.T, jnp.zeros((D_MODEL, 8 - 1 - DEC_BATCH), F32)], axis=1)
    mod = _modulation(cond_t, w_mod, b_mod)

    cache_na_k = cache_na_k.reshape(DEC_BATCH, DEPTH, PAST_LEN, NA_W)
    cache_na_v = cache_na_v.reshape(DEC_BATCH, DEPTH, PAST_LEN, NA_W)
    cache_diff_k = cache_diff_k.reshape(DEC_BATCH, DEPTH, PAST_LEN, DIFF_W)
    cache_diff_v = cache_diff_v.reshape(DEC_BATCH, DEPTH, PAST_LEN, DIFF_W)
    cos_tab, sin_tab = _rope_tables()
    st0 = state_ret.astype(F32).reshape(DEC_BATCH, DEPTH, 2, RET_HEADS // 2, 2, HEAD_DIM, 1, HEAD_DIM)
    st0 = (st0 * jnp.eye(2, dtype=F32)[:, None, :, None]).reshape(
        DEC_BATCH, DEPTH, 2, RET_HEADS // 2, PAIR, PAIR)

    ctx_row = lambda r: 0
    lat_row = lambda r: 1 + r // DEC_SEQ

    wg_bf = exp_w_gate.astype(BF16)
    wu_bf = exp_w_up.astype(BF16)
    wd_bf = exp_w_down.astype(BF16)

    w_in_bf = w_in.astype(BF16)
    w_out_bf = w_out.astype(BF16)
    wr_t = jnp.concatenate([jnp.swapaxes(router_expert, 1, 2), jnp.swapaxes(router_group, 1, 2),
                            jnp.zeros((DEPTH, MOE_ROUTE_ROWS - N_EXPERTS - N_GROUPS, D_MODEL), F32)], axis=1)
    log_g_all = jax.nn.log_sigmoid(ret_decay.astype(F32)) * LOG2E
    lp = diff_lambda.astype(F32)
    lam_dyn = jnp.exp(jnp.sum(lp[:, 0] * lp[:, 1], axis=-1)) - jnp.exp(jnp.sum(lp[:, 2] * lp[:, 3], axis=-1))

    bias_tab = _na_bias_tables(na_rel_bias)
    gnw = ret_gn_w.reshape(DEPTH, 1, RET_W)
    gnb = ret_gn_b.reshape(DEPTH, 1, RET_W)
    dnw = diff_norm_w.reshape(DEPTH, 1, DIFF_W)
    l1w = ln1_w.reshape(DEPTH, 1, D_MODEL)
    l1b = ln1_b.reshape(DEPTH, 1, D_MODEL)
    l2w = ln2_w.reshape(DEPTH, 1, D_MODEL)
    l2b = ln2_b.reshape(DEPTH, 1, D_MODEL)

    caches = None
    states = None
    for l in range(DEPTH):
        lam_init = 0.8 - 0.6 * math.exp(-0.3 * l)
        lam = (lam_dyn[l] + lam_init).reshape(1)
        log_g = log_g_all[l]

        q_c, kv_c, *caches = _project(x_ctx, mod, w_in_bf, ctx_row, l, emit_caches=True, caches=caches)
        x1_c, states = _ctx_mix(q_c, kv_c, x_ctx, mod, w_out_bf, log_g, lam, gnw, gnb, dnw, l1w, l1b,
                                states, l, lam_init)
        x_ctx = _moe(x1_c, mod, ctx_row, wr_t, wg_bf, wu_bf, wd_bf, l2w, l2b, l, "moe_ctx")

        q_l, kv_l = _project(x_lat, mod, w_in_bf, lat_row, l)
        x1_l = _lat_mix(q_l, kv_l, x_lat, mod, w_out_bf, log_g, lam, gnw, gnb, dnw, l1w, l1b,
                        cache_na_k, cache_na_v, st0, cache_diff_k, cache_diff_v,
                        bias_tab, cos_tab, sin_tab, l, lam_init)
        x_lat = _moe(x1_l, mod, lat_row, wr_t, wg_bf, wu_bf, wd_bf, l2w, l2b, l, "moe_lat")

    new_na_k, new_na_v, new_diff_k, new_diff_v = caches
    return (x_ctx.reshape(BATCH, SEQ, D_MODEL), x_lat.reshape(DEC_BATCH, DEC_SEQ, D_MODEL),
            new_na_k.reshape(BATCH, DEPTH, SEQ, NA_HEADS, HEAD_DIM),
            new_na_v.reshape(BATCH, DEPTH, SEQ, NA_HEADS, HEAD_DIM),
            states,
            new_diff_k.reshape(BATCH, DEPTH, SEQ, DIFF_HEADS, 2 * DIFF_D),
            new_diff_v.reshape(BATCH, DEPTH, SEQ, DIFF_HEADS, DIFF_DV))
```

```python
import functools
import math

import numpy as np
import jax
import jax.numpy as jnp
from jax import lax
from jax.experimental import pallas as pl
from jax.experimental.pallas import tpu as pltpu

D_MODEL = 1024
BATCH = 32
SEQ = 256
DEPTH = 2
DEC_BATCH = 2
DEC_SEQ = 2048
PAST_LEN = 512
GRID_W = 64
HEAD_DIM = 64
NA_HEADS = 6
NA_WIN_ROWS = 8
NA_WIN_COLS = 16
RET_HEADS = 6
DIFF_HEADS = 4
DIFF_D = 32
DIFF_DV = 64
NA_W = NA_HEADS * HEAD_DIM
RET_W = RET_HEADS * HEAD_DIM
DIFF_W = DIFF_HEADS * DIFF_DV
MIX_W = NA_W + RET_W + DIFF_W
IN_COLS = 3 * NA_W + 4 * RET_W + 3 * DIFF_W
N_GROUPS = 4
EXPERTS_PER_GROUP = 8
N_EXPERTS = N_GROUPS * EXPERTS_PER_GROUP
EXPERT_FF = 256
ROPE_BASE = 10000.0
LN_EPS = 1e-5
NEG_INF = -1e30
ALPHA = (2.0 * DEPTH) ** 0.25
LOG2E = math.log2(math.e)

F32 = jnp.float32
BF16 = jnp.bfloat16

Q_NA, Q_RET, Q_GATE, Q_DIFF = 0, NA_W, NA_W + RET_W, NA_W + 2 * RET_W
Q_COLS = NA_W + 2 * RET_W + DIFF_W
KV_NAK, KV_NAV = 0, NA_W
KV_RETK, KV_RETV = 2 * NA_W, 2 * NA_W + RET_W
KV_DK, KV_DV = 2 * NA_W + 2 * RET_W, 2 * NA_W + 2 * RET_W + DIFF_W
KV_COLS = 2 * NA_W + 2 * RET_W + 2 * DIFF_W
MIX_RET, MIX_DIFF = NA_W, NA_W + RET_W

NA_Q_TILE = 256
NA_KEY_ROWS = 12
NA_KEYS = NA_KEY_ROWS * GRID_W
CTX_PROBLEMS = NA_HEADS + 2 * DIFF_HEADS + RET_HEADS
GRID_ROWS = DEC_SEQ // GRID_W
PAIR = 2 * HEAD_DIM
NA_DR_PAD = 8
NA_DR_ROWS = 32

VMEM_LIMIT = 60 * 1024 * 1024


def _cparams(sem):
    return pltpu.CompilerParams(dimension_semantics=sem, vmem_limit_bytes=VMEM_LIMIT)


def _dot(a, b):
    return jnp.dot(a, b, preferred_element_type=F32)


def _dot_nt(a, b):
    return lax.dot_general(a, b, (((1,), (1,)), ((), ())), preferred_element_type=F32)


def _dot_tn(a, b):
    return lax.dot_general(a, b, (((0,), (0,)), ((), ())), preferred_element_type=F32)


def _silu(x):
    return x / (1.0 + jnp.exp(-x))


def _layer_norm_rows(z, w, b):
    mu = jnp.mean(z, axis=-1, keepdims=True)
    zc = z - mu
    var = jnp.mean(zc * zc, axis=-1, keepdims=True)
    return zc * lax.rsqrt(var + LN_EPS) * w + b


def _mod_kernel(cond_ref, w_ref, b_ref, o_ref):
    s = _silu(cond_ref[...])
    w = w_ref[...]
    n_cond = 1 + DEC_BATCH
    rows = [jnp.sum(s[:, r:r + 1] * w, axis=0, keepdims=True) for r in range(n_cond)]
    rows.append(jnp.zeros((8 - n_cond, w.shape[1]), F32))
    o_ref[...] = jnp.concatenate(rows, axis=0) + b_ref[...]


def _modulation(cond_t, w_mod, b_mod):
    nj = 6
    out = pl.pallas_call(
        _mod_kernel,
        grid=(DEPTH, nj),
        in_specs=[
            pl.BlockSpec((D_MODEL, 8), lambda l, j: (0, 0)),
            pl.BlockSpec((None, D_MODEL, D_MODEL), lambda l, j: (l, 0, j)),
            pl.BlockSpec((None, 1, D_MODEL), lambda l, j: (l, 0, j)),
        ],
        out_specs=pl.BlockSpec((None, 8, D_MODEL), lambda l, j: (l, 0, j)),
        out_shape=jax.ShapeDtypeStruct((DEPTH, 8, 6 * D_MODEL), F32),
        compiler_params=_cparams(("arbitrary", "arbitrary")),
        name="modulation",
    )(cond_t, w_mod, b_mod.reshape(DEPTH, 1, 6 * D_MODEL))
    return out.reshape(DEPTH, 8, 6, D_MODEL)


def _proj_kernel(x_ref, mod_ref, w_ref, *refs):
    if len(refs) == 10:
        refs = refs[4:]
    q_ref, kv_ref = refs[:2]
    cache_refs = refs[2:]
    m = mod_ref[...]
    h = x_ref[...] * (1.0 + m[1:2]) + m[0:1]
    p = _dot(h.astype(BF16), w_ref[...])
    o = 0
    na_q = p[:, o:o + NA_W] * (HEAD_DIM ** -0.5 * LOG2E); o += NA_W
    na_k = p[:, o:o + NA_W]; o += NA_W
    na_v = p[:, o:o + NA_W]; o += NA_W
    ret_q = p[:, o:o + RET_W]; o += RET_W
    ret_k = p[:, o:o + RET_W] * (HEAD_DIM ** -0.5); o += RET_W
    ret_v = p[:, o:o + RET_W]; o += RET_W
    ret_g = p[:, o:o + RET_W]; o += RET_W
    dq = p[:, o:o + DIFF_W] * (DIFF_D ** -0.5 * LOG2E); o += DIFF_W
    dk = p[:, o:o + DIFF_W]; o += DIFF_W
    dv = p[:, o:o + DIFF_W]
    q_ref[:, Q_NA:Q_NA + NA_W] = na_q.astype(BF16)
    q_ref[:, Q_RET:Q_RET + RET_W] = ret_q.astype(BF16)
    q_ref[:, Q_GATE:Q_GATE + RET_W] = ret_g.astype(BF16)
    q_ref[:, Q_DIFF:Q_DIFF + DIFF_W] = dq.astype(BF16)
    kv_ref[:, KV_NAK:KV_NAK + NA_W] = na_k.astype(BF16)
    kv_ref[:, KV_NAV:KV_NAV + NA_W] = na_v.astype(BF16)
    kv_ref[:, KV_RETK:KV_RETK + RET_W] = ret_k.astype(BF16)
    kv_ref[:, KV_RETV:KV_RETV + RET_W] = ret_v.astype(BF16)
    kv_ref[:, KV_DK:KV_DK + DIFF_W] = dk.astype(BF16)
    kv_ref[:, KV_DV:KV_DV + DIFF_W] = dv.astype(BF16)
    nb = x_ref.shape[0] // SEQ
    for ref, val in zip(cache_refs, (na_k, na_v, dk, dv)):
        val = val.reshape(nb, SEQ, val.shape[-1])
        if len(ref.shape) == 4:
            ref[:, 0] = val
            ref[:, 1:] = jnp.zeros((nb, DEPTH - 1) + val.shape[1:], F32)
        else:
            ref[...] = val


def _project(x2d, mod, w_in_bf, mod_row_fn, layer, emit_caches=False, caches=None, tm=512):
    n = x2d.shape[0]
    row = lambda i: (i, 0)
    out_shape = [jax.ShapeDtypeStruct((n, Q_COLS), BF16), jax.ShapeDtypeStruct((n, KV_COLS), BF16)]
    out_specs = [pl.BlockSpec((tm, Q_COLS), row), pl.BlockSpec((tm, KV_COLS), row)]
    in_specs = [
        pl.BlockSpec((tm, D_MODEL), row),
        pl.BlockSpec((None, None, 6, D_MODEL), lambda i: (layer, mod_row_fn(i * tm), 0, 0)),
        pl.BlockSpec((None, D_MODEL, IN_COLS), lambda i: (layer, 0, 0)),
    ]
    args = [x2d, mod, w_in_bf]
    aliases = {}
    if emit_caches:
        for k, w in enumerate((NA_W, NA_W, DIFF_W, DIFF_W)):
            out_shape.append(jax.ShapeDtypeStruct((BATCH, DEPTH, SEQ, w), F32))
            if caches is None:
                out_specs.append(pl.BlockSpec((tm // SEQ, DEPTH, SEQ, w), lambda i: (i, 0, 0, 0)))
            else:
                in_specs.append(pl.BlockSpec(memory_space=pl.ANY))
                args.append(caches[k])
                aliases[3 + k] = 2 + k
                out_specs.append(pl.BlockSpec((tm // SEQ, None, SEQ, w), lambda i: (i, layer, 0, 0)))
    return pl.pallas_call(
        _proj_kernel,
        grid=(n // tm,),
        in_specs=in_specs,
        out_specs=out_specs,
        out_shape=out_shape,
        input_output_aliases=aliases,
        compiler_params=_cparams(("arbitrary",)),
        name="proj_ctx" if emit_caches else "proj_lat",
    )(*args)


def _lane_group(rows, width):
    return lax.broadcasted_iota(jnp.int32, (rows, PAIR), 1) // width


def _keep_group(x, groups, g):
    return jnp.where(groups == g, x, jnp.zeros_like(x))


def _with_ones(v_pair):
    return jnp.concatenate([v_pair, jnp.ones(v_pair.shape, v_pair.dtype)], axis=1)


def _normalise(r):
    return r[:, :PAIR] * (1.0 / r[:, PAIR:PAIR + 1])


def _attend(qm, k_list, vx_list, bias=None):
    scores = [_dot_nt(qm, k) for k in k_list]
    if bias is not None:
        scores[0] = scores[0] + bias
    m = None
    for s in scores:
        mi = jnp.max(s, axis=-1, keepdims=True)
        m = mi if m is None else jnp.maximum(m, mi)
    r = None
    for s, vx in zip(scores, vx_list):
        ri = _dot(jnp.exp2(s - m).astype(BF16), vx)
        r = ri if r is None else r + ri
    return _normalise(r)


def _group_mean_matrix():
    r = lax.broadcasted_iota(jnp.int32, (PAIR, PAIR), 0) // HEAD_DIM
    c = lax.broadcasted_iota(jnp.int32, (PAIR, PAIR), 1) // HEAD_DIM
    return jnp.where(r == c, 1.0 / HEAD_DIM, 0.0).astype(BF16)


def _group_mean(x, gm):
    hi = x.astype(BF16)
    lo = (x - hi.astype(F32)).astype(BF16)
    return _dot(hi, gm) + _dot(lo, gm)


def _head_layer_norm(o, gm):
    oc = o - _group_mean(o, gm)
    return oc * lax.rsqrt(_group_mean(oc * oc, gm) + LN_EPS)


def _head_rms_norm(o, gm):
    return o * lax.rsqrt(_group_mean(o * o, gm) + LN_EPS)


def _pair_scalar(ref, row, j, groups):
    return jnp.where(groups == 0, ref[row, 2 * j], ref[row, 2 * j + 1])


def _decay_table(lg_ref, dm_ref):
    T = SEQ
    i = lax.broadcasted_iota(jnp.int32, (T, T), 0).astype(F32)
    j = lax.broadcasted_iota(jnp.int32, (T, T), 1).astype(F32)
    d = i - j
    for h in range(RET_HEADS):
        dm_ref[h * T:(h + 1) * T, :] = jnp.exp2(jnp.where(d >= 0, d * lg_ref[0, h], (-d) * lg_ref[1, h]))


def _out_proj_post_norm(mix_ref, wout_ref, x_ref, mod_ref, lnw_ref, lnb_ref, x1_ref):
    y = _dot(mix_ref[...].astype(BF16), wout_ref[...])
    m = mod_ref[...]
    z = ALPHA * x_ref[...] + m[2:3] * y
    x1_ref[...] = _layer_norm_rows(z, lnw_ref[...], lnb_ref[...])


def _ctx_mix_kernel(lg_ref, lam_ref, q_ref, kv_ref, x_ref, mod_ref, wout_ref, gnw_ref, gnb_ref,
                    dnw_ref, lnw_ref, lnb_ref, *rest, one_minus_lam_init):
    if len(rest) == 8:
        rest = rest[1:]
    x1_ref, st_ref, mix_ref, dm_ref, s_ref, e_ref, o_ref = rest
    if len(st_ref.shape) == 5:
        st_ref[1:] = jnp.zeros((DEPTH - 1,) + st_ref.shape[1:], F32)
        st_ref = st_ref.at[0]
    T = SEQ
    n_soft = NA_HEADS + 2 * DIFF_HEADS

    @pl.when(pl.program_id(0) == 0)
    def _():
        _decay_table(lg_ref, dm_ref)

    half = _lane_group(T, HEAD_DIM)
    quarter = _lane_group(T, DIFF_D)
    n_na, n_diff, n_ret = NA_HEADS // 2, DIFF_HEADS // 2, RET_HEADS // 2

    for j in range(n_na):
        q = q_ref[:, Q_NA + j * PAIR:Q_NA + (j + 1) * PAIR]
        k = kv_ref[:, KV_NAK + j * PAIR:KV_NAK + (j + 1) * PAIR]
        for g in range(2):
            r0 = (2 * j + g) * T
            s_ref[r0:r0 + T, :] = _dot_nt(_keep_group(q, half, g), k)
    for j in range(n_diff):
        q = q_ref[:, Q_DIFF + j * PAIR:Q_DIFF + (j + 1) * PAIR]
        k = kv_ref[:, KV_DK + j * PAIR:KV_DK + (j + 1) * PAIR]
        for g in range(4):
            r0 = (NA_HEADS + 4 * j + g) * T
            s_ref[r0:r0 + T, :] = _dot_nt(_keep_group(q, quarter, g), k)
    for j in range(n_ret):
        q = q_ref[:, Q_RET + j * PAIR:Q_RET + (j + 1) * PAIR]
        k = kv_ref[:, KV_RETK + j * PAIR:KV_RETK + (j + 1) * PAIR]
        for g in range(2):
            r0 = (n_soft + 2 * j + g) * T
            s_ref[r0:r0 + T, :] = _dot_nt(_keep_group(q, half, g), k)

    s = s_ref[0:n_soft * T, :]
    e_ref[0:n_soft * T, :] = jnp.exp2(s - jnp.max(s, axis=-1, keepdims=True)).astype(BF16)
    e_ref[n_soft * T:, :] = (s_ref[n_soft * T:, :] * dm_ref[...]).astype(BF16)

    def pv(n, vx):
        return _normalise(_dot(e_ref[n * T:(n + 1) * T, :], vx))

    for j in range(n_na):
        vx = _with_ones(kv_ref[:, KV_NAV + j * PAIR:KV_NAV + (j + 1) * PAIR])
        mix_ref[:, j * PAIR:(j + 1) * PAIR] = jnp.where(half == 0, pv(2 * j, vx), pv(2 * j + 1, vx))
    lam = lam_ref[0]
    for j in range(n_diff):
        vx = _with_ones(kv_ref[:, KV_DV + j * PAIR:KV_DV + (j + 1) * PAIR])
        n0 = NA_HEADS + 4 * j
        head_a = pv(n0, vx) - lam * pv(n0 + 1, vx)
        head_b = pv(n0 + 2, vx) - lam * pv(n0 + 3, vx)
        o_ref[(n_ret + j) * T:(n_ret + j + 1) * T, :] = jnp.where(half == 0, head_a, head_b)
    jj = lax.broadcasted_iota(jnp.int32, (T, PAIR), 0).astype(F32)
    for j in range(n_ret):
        kf = kv_ref[:, KV_RETK + j * PAIR:KV_RETK + (j + 1) * PAIR].astype(F32)
        v = kv_ref[:, KV_RETV + j * PAIR:KV_RETV + (j + 1) * PAIR]
        n0 = n_soft + 2 * j
        o_ref[j * T:(j + 1) * T, :] = jnp.where(half == 0, _dot(e_ref[n0 * T:(n0 + 1) * T, :], v),
                                                _dot(e_ref[(n0 + 1) * T:(n0 + 2) * T, :], v))
        k_fwd = (kf * jnp.exp2((T - 1.0 - jj) * _pair_scalar(lg_ref, 0, j, half))).astype(BF16)
        k_bwd = (kf * jnp.exp2(jj * _pair_scalar(lg_ref, 1, j, half))).astype(BF16)
        for d, kd in enumerate((k_fwd, k_bwd)):
            st = _dot_tn(kd, v)
            st_ref[d, 2 * j] = st[0:HEAD_DIM, 0:HEAD_DIM]
            st_ref[d, 2 * j + 1] = st[HEAD_DIM:PAIR, HEAD_DIM:PAIR]

    gm = _group_mean_matrix()
    rn = _head_layer_norm(o_ref[0:n_ret * T, :], gm)
    dn = _head_rms_norm(o_ref[n_ret * T:, :], gm)
    for j in range(n_ret):
        c = j * PAIR
        g = q_ref[:, Q_GATE + c:Q_GATE + c + PAIR].astype(F32)
        mix_ref[:, MIX_RET + c:MIX_RET + c + PAIR] = (
            (rn[j * T:(j + 1) * T] * gnw_ref[:, c:c + PAIR] + gnb_ref[:, c:c + PAIR]) * _silu(g))
    for j in range(n_diff):
        c = j * PAIR
        mix_ref[:, MIX_DIFF + c:MIX_DIFF + c + PAIR] = (
            dn[j * T:(j + 1) * T] * dnw_ref[:, c:c + PAIR] * one_minus_lam_init)

    _out_proj_post_norm(mix_ref, wout_ref, x_ref, mod_ref, lnw_ref, lnb_ref, x1_ref)


def _ctx_mix(q_arr, kv_arr, x2d, mod, w_out_bf, log_g, lam, gnw, gnb, dnw, lnw, lnb, states, layer,
             lam_init):
    row = lambda b: (b, 0)
    const2 = lambda b: (0, 0)
    smem = pl.BlockSpec(memory_space=pltpu.SMEM)
    st_shape = (BATCH, DEPTH, 2, RET_HEADS, HEAD_DIM, HEAD_DIM)
    in_specs = [
        smem, smem,
        pl.BlockSpec((SEQ, Q_COLS), row),
        pl.BlockSpec((SEQ, KV_COLS), row),
        pl.BlockSpec((SEQ, D_MODEL), row),
        pl.BlockSpec((None, None, 6, D_MODEL), lambda b: (layer, 0, 0, 0)),
        pl.BlockSpec((None, MIX_W, D_MODEL), lambda b: (layer, 0, 0)),
        pl.BlockSpec((None, 1, RET_W), lambda *_: (layer, 0, 0)),
        pl.BlockSpec((None, 1, RET_W), lambda *_: (layer, 0, 0)),
        pl.BlockSpec((None, 1, DIFF_W), lambda *_: (layer, 0, 0)),
        pl.BlockSpec((None, 1, D_MODEL), lambda *_: (layer, 0, 0)),
        pl.BlockSpec((None, 1, D_MODEL), lambda *_: (layer, 0, 0)),
    ]
    args = [log_g, lam, q_arr, kv_arr, x2d, mod, w_out_bf, gnw, gnb, dnw, lnw, lnb]
    if states is None:
        st_spec = pl.BlockSpec((None,) + st_shape[1:], lambda b: (b, 0, 0, 0, 0, 0))
        aliases = {}
    else:
        in_specs.append(pl.BlockSpec(memory_space=pl.ANY))
        args.append(states)
        st_spec = pl.BlockSpec((None, None) + st_shape[2:], lambda b: (b, layer, 0, 0, 0, 0))
        aliases = {12: 1}
    return pl.pallas_call(
        functools.partial(_ctx_mix_kernel, one_minus_lam_init=1.0 - lam_init),
        grid=(BATCH,),
        in_specs=in_specs,
        out_specs=[pl.BlockSpec((SEQ, D_MODEL), row), st_spec],
        out_shape=[
            jax.ShapeDtypeStruct((BATCH * SEQ, D_MODEL), F32),
            jax.ShapeDtypeStruct(st_shape, F32),
        ],
        input_output_aliases=aliases,
        scratch_shapes=[
            pltpu.VMEM((SEQ, MIX_W), F32),
            pltpu.VMEM((RET_HEADS * SEQ, SEQ), F32),
            pltpu.VMEM((CTX_PROBLEMS * SEQ, SEQ), F32),
            pltpu.VMEM((CTX_PROBLEMS * SEQ, SEQ), BF16),
            pltpu.VMEM(((RET_HEADS + DIFF_HEADS) // 2 * SEQ, PAIR), F32),
        ],
        compiler_params=_cparams(("arbitrary",)),
        name="ctx_mix",
    )(*args)


def _rope(x, cos, sin_signed):
    n, w = x.shape
    lane = lax.broadcasted_iota(jnp.int32, (n, w), 1)
    first = (lane % 16) < 8
    partner = jnp.where(first, pltpu.roll(x, w - 8, 1), pltpu.roll(x, 8, 1))
    return x * cos + partner * sin_signed


def _lat_mix_kernel(lg_ref, lam_ref, q_ref, kv_ref, x_ref, mod_ref, wout_ref, gnw_ref, gnb_ref,
                    dnw_ref, lnw_ref, lnb_ref, cnak_ref, cnav_ref, st0_ref, cdk_ref, cdv_ref,
                    bias_ref, cos_ref, sin_ref, x1_ref, mix_ref, kr_ref, dm_ref, sf_ref, sb_ref,
                    kc_na_ref, vxc_na_ref, vx_na_ref, kc_d_ref, vxc_d_ref, vx_d_ref, *,
                    one_minus_lam_init):
    TQ = NA_Q_TILE
    T = DEC_SEQ
    n_qt = T // TQ
    n_na, n_diff, n_ret = NA_HEADS // 2, DIFF_HEADS // 2, RET_HEADS // 2
    qt = pl.program_id(1)
    q0 = pl.multiple_of(qt * TQ, TQ)
    half = _lane_group(TQ, HEAD_DIM)
    quarter = _lane_group(TQ, DIFF_D)

    @pl.when(qt == 0)
    def _():
        kr = _rope(kv_ref[:, KV_DK:KV_DK + DIFF_W].astype(F32), cos_ref[...], sin_ref[...])
        kr_ref[...] = kr.astype(BF16)
        kc_na_ref[...] = cnak_ref[...].astype(BF16)
        kc_d_ref[...] = cdk_ref[...].astype(BF16)
        for j in range(n_na):
            vxc_na_ref[j] = _with_ones(cnav_ref[:, j * PAIR:(j + 1) * PAIR].astype(BF16))
            vx_na_ref[j] = _with_ones(kv_ref[:, KV_NAV + j * PAIR:KV_NAV + (j + 1) * PAIR])
        for j in range(n_diff):
            vxc_d_ref[j] = _with_ones(cdv_ref[:, j * PAIR:(j + 1) * PAIR].astype(BF16))
            vx_d_ref[j] = _with_ones(kv_ref[:, KV_DV + j * PAIR:KV_DV + (j + 1) * PAIR])
        _decay_table(lg_ref, dm_ref)
        jl = lax.broadcasted_iota(jnp.int32, (TQ, PAIR), 0).astype(F32)
        rows = lax.broadcasted_iota(jnp.int32, (PAIR, PAIR), 0) // HEAD_DIM
        cols = lax.broadcasted_iota(jnp.int32, (PAIR, PAIR), 1) // HEAD_DIM
        for j in range(n_ret):
            lf = _pair_scalar(lg_ref, 0, j, half)
            lb = _pair_scalar(lg_ref, 1, j, half)
            dec_f = jnp.exp2((TQ - 1.0 - jl) * lf)
            dec_b = jnp.exp2(jl * lb)
            tile_f = jnp.exp2(float(TQ) * _pair_scalar(lg_ref, 0, j, rows))
            tile_b = jnp.exp2(float(TQ) * _pair_scalar(lg_ref, 1, j, rows))
            loc_f, loc_b = [], []
            for t in range(n_qt):
                kf = kv_ref[t * TQ:(t + 1) * TQ, KV_RETK + j * PAIR:KV_RETK + (j + 1) * PAIR].astype(F32)
                v = kv_ref[t * TQ:(t + 1) * TQ, KV_RETV + j * PAIR:KV_RETV + (j + 1) * PAIR]
                loc_f.append(jnp.where(rows == cols, _dot_tn((kf * dec_f).astype(BF16), v), 0.0))
                loc_b.append(jnp.where(rows == cols, _dot_tn((kf * dec_b).astype(BF16), v), 0.0))
            state = st0_ref[0, j]
            for t in range(n_qt):
                sf_ref[t, j] = state.astype(BF16)
                state = state * tile_f + loc_f[t]
            state = st0_ref[1, j]
            for t in reversed(range(n_qt)):
                sb_ref[t, j] = state.astype(BF16)
                state = state * tile_b + loc_b[t]

    q_rows = TQ // GRID_W
    r0 = qt * q_rows
    ks = jnp.clip(r0 - NA_WIN_ROWS // 2, 0, GRID_ROWS - NA_KEY_ROWS)
    k0 = pl.multiple_of(ks * GRID_W, 256)
    key_row = lax.broadcasted_iota(jnp.int32, (1, NA_KEYS), 1) // GRID_W + ks
    row_masks = []
    for a in range(q_rows):
        start = jnp.clip(r0 + a - NA_WIN_ROWS // 2, 0, GRID_ROWS - NA_WIN_ROWS)
        inside = jnp.where(key_row >= start, jnp.where(key_row < start + NA_WIN_ROWS, 1.0, 0.0), 0.0)
        row_masks.append(jnp.where(inside > 0.5, 0.0, NEG_INF))

    def window_bias(h):
        rows = []
        for a in range(q_rows):
            base = ks - r0 - a + (NA_WIN_ROWS - 1) + NA_DR_PAD
            blocks = [bias_ref[h, base + 2 * p] for p in range(NA_KEY_ROWS // 2)]
            rows.append(jnp.concatenate(blocks, axis=1) + row_masks[a])
        return jnp.concatenate(rows, axis=0)

    for j in range(n_na):
        c = j * PAIR
        q = q_ref[:, Q_NA + c:Q_NA + c + PAIR]
        ks_list = [kv_ref[pl.ds(k0, NA_KEYS), KV_NAK + c:KV_NAK + c + PAIR], kc_na_ref[:, c:c + PAIR]]
        vx_list = [vx_na_ref[j, pl.ds(k0, NA_KEYS), :], vxc_na_ref[j]]
        o = [_attend(_keep_group(q, half, g), ks_list, vx_list, window_bias(2 * j + g)) for g in range(2)]
        mix_ref[:, c:c + PAIR] = jnp.where(half == 0, o[0], o[1])

    gm = _group_mean_matrix()
    ii = lax.broadcasted_iota(jnp.int32, (TQ, PAIR), 0).astype(F32)
    for j in range(n_ret):
        c = j * PAIR
        q = q_ref[:, Q_RET + c:Q_RET + c + PAIR]
        k = kv_ref[pl.ds(q0, TQ), KV_RETK + c:KV_RETK + c + PAIR]
        v = kv_ref[pl.ds(q0, TQ), KV_RETV + c:KV_RETV + c + PAIR]
        inner = []
        for g in range(2):
            h = 2 * j + g
            sc = _dot_nt(_keep_group(q, half, g), k) * dm_ref[h * TQ:(h + 1) * TQ, :]
            inner.append(_dot(sc.astype(BF16), v))
        qf = q.astype(F32)
        q_fwd = (qf * jnp.exp2((ii + 1.0) * _pair_scalar(lg_ref, 0, j, half))).astype(BF16)
        q_bwd = (qf * jnp.exp2((TQ - ii) * _pair_scalar(lg_ref, 1, j, half))).astype(BF16)
        o = jnp.where(half == 0, inner[0], inner[1]) + _dot(q_fwd, sf_ref[qt, j]) + _dot(q_bwd, sb_ref[qt, j])
        gate = q_ref[:, Q_GATE + c:Q_GATE + c + PAIR].astype(F32)
        mix_ref[:, MIX_RET + c:MIX_RET + c + PAIR] = (
            (_head_layer_norm(o, gm) * gnw_ref[:, c:c + PAIR] + gnb_ref[:, c:c + PAIR]) * _silu(gate))

    lam = lam_ref[0]
    qr = _rope(q_ref[:, Q_DIFF:Q_DIFF + DIFF_W].astype(F32),
               cos_ref[pl.ds(q0, TQ), :], sin_ref[pl.ds(q0, TQ), :]).astype(BF16)
    for j in range(n_diff):
        c = j * PAIR
        q = qr[:, c:c + PAIR]
        ks_list = [kc_d_ref[:, c:c + PAIR], kr_ref[:, c:c + PAIR]]
        vx_list = [vxc_d_ref[j], vx_d_ref[j]]
        o = [_attend(_keep_group(q, quarter, g), ks_list, vx_list) for g in range(4)]
        blk = jnp.where(half == 0, o[0] - lam * o[1], o[2] - lam * o[3])
        mix_ref[:, MIX_DIFF + c:MIX_DIFF + c + PAIR] = (
            _head_rms_norm(blk, gm) * dnw_ref[:, c:c + PAIR] * one_minus_lam_init)

    _out_proj_post_norm(mix_ref, wout_ref, x_ref, mod_ref, lnw_ref, lnb_ref, x1_ref)


def _lat_mix(q_arr, kv_arr, x2d, mod_l, w_out_bf, log_g, lam, gnw, gnb, dnw, lnw, lnb,
             cache_na_k, cache_na_v, state_ret, cache_diff_k, cache_diff_v, bias_tab, cos_tab,
             sin_tab, layer, lam_init):
    nq = DEC_SEQ // NA_Q_TILE
    const2 = lambda b, t: (0, 0)
    smem = pl.BlockSpec(memory_space=pltpu.SMEM)
    qrow = lambda b, t: (b * nq + t, 0)

    return pl.pallas_call(
        functools.partial(_lat_mix_kernel, one_minus_lam_init=1.0 - lam_init),
        grid=(DEC_BATCH, nq),
        in_specs=[
            smem, smem,
            pl.BlockSpec((NA_Q_TILE, Q_COLS), qrow),
            pl.BlockSpec((DEC_SEQ, KV_COLS), lambda b, t: (b, 0), pipeline_mode=pl.Buffered(1)),
            pl.BlockSpec((NA_Q_TILE, D_MODEL), qrow),
            pl.BlockSpec((None, None, 6, D_MODEL), lambda b, t: (layer, b + 1, 0, 0)),
            pl.BlockSpec((None, MIX_W, D_MODEL), lambda b, t: (layer, 0, 0)),
            pl.BlockSpec((None, 1, RET_W), lambda *_: (layer, 0, 0)),
            pl.BlockSpec((None, 1, RET_W), lambda *_: (layer, 0, 0)),
            pl.BlockSpec((None, 1, DIFF_W), lambda *_: (layer, 0, 0)),
            pl.BlockSpec((None, 1, D_MODEL), lambda *_: (layer, 0, 0)),
            pl.BlockSpec((None, 1, D_MODEL), lambda *_: (layer, 0, 0)),
            pl.BlockSpec((None, None, PAST_LEN, NA_W), lambda b, t: (b, layer, 0, 0),
                         pipeline_mode=pl.Buffered(1)),
            pl.BlockSpec((None, None, PAST_LEN, NA_W), lambda b, t: (b, layer, 0, 0),
                         pipeline_mode=pl.Buffered(1)),
            pl.BlockSpec((None, None, 2, RET_HEADS // 2, PAIR, PAIR),
                         lambda b, t: (b, layer, 0, 0, 0, 0)),
            pl.BlockSpec((None, None, PAST_LEN, DIFF_W), lambda b, t: (b, layer, 0, 0),
                         pipeline_mode=pl.Buffered(1)),
            pl.BlockSpec((None, None, PAST_LEN, DIFF_W), lambda b, t: (b, layer, 0, 0),
                         pipeline_mode=pl.Buffered(1)),
            pl.BlockSpec((None, NA_HEADS, NA_DR_ROWS, GRID_W, PAIR), lambda b, t: (layer, 0, 0, 0, 0),
                         pipeline_mode=pl.Buffered(1)),
            pl.BlockSpec((DEC_SEQ, DIFF_W), const2),
            pl.BlockSpec((DEC_SEQ, DIFF_W), const2),
        ],
        out_specs=pl.BlockSpec((NA_Q_TILE, D_MODEL), qrow),
        out_shape=jax.ShapeDtypeStruct((DEC_BATCH * DEC_SEQ, D_MODEL), F32),
        scratch_shapes=[
            pltpu.VMEM((NA_Q_TILE, MIX_W), F32),
            pltpu.VMEM((DEC_SEQ, DIFF_W), BF16),
            pltpu.VMEM((RET_HEADS * NA_Q_TILE, NA_Q_TILE), F32),
            pltpu.VMEM((DEC_SEQ // NA_Q_TILE, RET_HEADS // 2, PAIR, PAIR), BF16),
            pltpu.VMEM((DEC_SEQ // NA_Q_TILE, RET_HEADS // 2, PAIR, PAIR), BF16),
            pltpu.VMEM((PAST_LEN, NA_W), BF16),
            pltpu.VMEM((NA_HEADS // 2, PAST_LEN, 2 * PAIR), BF16),
            pltpu.VMEM((NA_HEADS // 2, DEC_SEQ, 2 * PAIR), BF16),
            pltpu.VMEM((PAST_LEN, DIFF_W), BF16),
            pltpu.VMEM((DIFF_HEADS // 2, PAST_LEN, 2 * PAIR), BF16),
            pltpu.VMEM((DIFF_HEADS // 2, DEC_SEQ, 2 * PAIR), BF16),
        ],
        compiler_params=_cparams(("arbitrary", "arbitrary")),
        name="lat_mix",
    )(log_g, lam, q_arr, kv_arr, x2d, mod_l, w_out_bf, gnw, gnb, dnw, lnw, lnb,
      cache_na_k, cache_na_v, state_ret, cache_diff_k, cache_diff_v, bias_tab, cos_tab, sin_tab)


MOE_PART = 2048
MOE_TILE = 512
MOE_EB = 4
MOE_CH = 48
MOE_STEPS = N_EXPERTS // MOE_EB
MOE_ROUTE_ROWS = 40


def _route_transposed(lt):
    shape = lt.shape
    r = lax.broadcasted_iota(jnp.int32, shape, 0).astype(F32)
    ninf = -jnp.inf
    is_g = jnp.where(r >= N_EXPERTS, jnp.where(r < N_EXPERTS + N_GROUPS, 1.0, 0.0), 0.0) > 0.5
    gl = jnp.where(is_g, lt, ninf)
    gmax = jnp.max(gl, axis=0, keepdims=True)
    gsel = jnp.min(jnp.where(gl == gmax, r - N_EXPERTS, 1e9), axis=0, keepdims=True)
    gsum = jnp.sum(jnp.where(is_g, jnp.exp(gl - gmax), 0.0), axis=0, keepdims=True)
    gw = 1.0 / gsum
    lo = gsel * EXPERTS_PER_GROUP
    is_e = jnp.where(r >= lo, jnp.where(r < lo + EXPERTS_PER_GROUP, 1.0, 0.0), 0.0) > 0.5
    el = jnp.where(is_e, lt, ninf)
    v1 = jnp.max(el, axis=0, keepdims=True)
    i1 = jnp.min(jnp.where(el == v1, r, 1e9), axis=0, keepdims=True)
    el2 = jnp.where(r == i1, ninf, el)
    v2 = jnp.max(el2, axis=0, keepdims=True)
    i2 = jnp.min(jnp.where(el2 == v2, r, 1e9), axis=0, keepdims=True)
    t = jnp.exp(v2 - v1)
    w1 = gw / (1.0 + t)
    w2 = gw * t / (1.0 + t)
    first = r == i1
    second = r == i2
    gates = jnp.where(first, w1, 0.0) + jnp.where(second, w2, 0.0)
    member = jnp.where(first, 1.0, jnp.where(second, 1.0, 0.0))
    return gates, member


def _moe_kernel(x_ref, mod_ref, wr_ref, wg_ref, wu_ref, wd_ref, lnw_ref, lnb_ref, out_ref,
                h_ref, rank_ref, gate_ref, sel_ref, xs_ref, ys_ref):
    s = pl.program_id(1)
    n_tiles = MOE_PART // MOE_TILE

    @pl.when(s == 0)
    def _():
        m = mod_ref[...]
        h = x_ref[...] * (1.0 + m[4:5]) + m[3:4]
        h_hi = h.astype(BF16)
        h_ref[...] = h_hi
        h_lo = (h - h_hi.astype(F32)).astype(BF16)
        w = wr_ref[...]
        w_hi = w.astype(BF16)
        w_lo = (w - w_hi.astype(F32)).astype(BF16)
        lt = _dot_nt(w_hi, h_hi) + (_dot_nt(w_hi, h_lo) + _dot_nt(w_lo, h_hi))
        gates, member = _route_transposed(lt)
        before = jnp.where(lax.broadcasted_iota(jnp.int32, (MOE_TILE, MOE_TILE), 0)
                           < lax.broadcasted_iota(jnp.int32, (MOE_TILE, MOE_TILE), 1), 1.0, 0.0).astype(BF16)
        rank_ref[...] = jnp.full(rank_ref.shape, -1.0, F32)
        gate_ref[...] = jnp.zeros(gate_ref.shape, F32)
        for t in range(n_tiles):
            c0 = t * MOE_TILE
            mem_t = member[0:N_EXPERTS, c0:c0 + MOE_TILE]
            cnt = _dot(mem_t.astype(BF16), before)
            rank = jnp.where(mem_t > 0.5, cnt, -1.0)
            for st in range(MOE_STEPS):
                rank_ref[st, 0:MOE_EB, c0:c0 + MOE_TILE] = rank[st * MOE_EB:(st + 1) * MOE_EB]
                gate_ref[st, 0:MOE_EB, c0:c0 + MOE_TILE] = gates[st * MOE_EB:(st + 1) * MOE_EB, c0:c0 + MOE_TILE]
        out_ref[...] = jnp.zeros(out_ref.shape, F32)

    ranks = rank_ref[s, 0:MOE_EB, :]
    gts = gate_ref[s, 0:MOE_EB, :]
    n_chunks = ((jnp.max(ranks) + 0.5) * (1.0 / MOE_CH)).astype(jnp.int32) + 1

    def chunk_body(k, carry):
        slot = (lax.broadcasted_iota(jnp.int32, (MOE_CH, MOE_TILE), 0) + k * MOE_CH).astype(F32)
        row_gate = [[] for _ in range(MOE_EB)]
        for t in range(n_tiles):
            c0 = t * MOE_TILE
            onehots = []
            for i in range(MOE_EB):
                hit = ranks[i:i + 1, c0:c0 + MOE_TILE] == slot
                onehots.append(jnp.where(hit, 1.0, 0.0).astype(BF16))
                row_gate[i].append(jnp.sum(jnp.where(hit, gts[i:i + 1, c0:c0 + MOE_TILE], 0.0),
                                           axis=1, keepdims=True))
            sel = jnp.concatenate(onehots, axis=0)
            sel_ref[t] = sel
            xs = _dot(sel, h_ref[c0:c0 + MOE_TILE, :]).astype(BF16)
            for i in range(MOE_EB):
                xs_ref[i, t * MOE_CH:(t + 1) * MOE_CH, :] = xs[i * MOE_CH:(i + 1) * MOE_CH]
        for i in range(MOE_EB):
            xi = xs_ref[i]
            a = _dot(xi, wg_ref[i])
            u = _dot(xi, wu_ref[i])
            hm = (_silu(a) * u * jnp.concatenate(row_gate[i], axis=0)).astype(BF16)
            ys_ref[i] = _dot(hm, wd_ref[i]).astype(BF16)
        for t in range(n_tiles):
            c0 = t * MOE_TILE
            y = jnp.concatenate([ys_ref[i, t * MOE_CH:(t + 1) * MOE_CH, :] for i in range(MOE_EB)], axis=0)
            out_ref[c0:c0 + MOE_TILE, :] += _dot_tn(sel_ref[t], y)
        return carry

    lax.fori_loop(0, n_chunks, chunk_body, 0)

    @pl.when(s == MOE_STEPS - 1)
    def _():
        m = mod_ref[...]
        z = ALPHA * x_ref[...] + m[5:6] * out_ref[...]
        out_ref[...] = _layer_norm_rows(z, lnw_ref[...], lnb_ref[...])


def _moe(x2d, mod_l, mod_row_fn, wr_t, wg_bf, wu_bf, wd_bf, lnw, lnb, layer, name):
    n = x2d.shape[0]
    row = lambda p, s: (p, 0)
    const2 = lambda p, s: (0, 0)
    wspec = lambda shape: pl.BlockSpec((None, MOE_EB) + shape, lambda p, s: (layer, s, 0, 0))
    return pl.pallas_call(
        _moe_kernel,
        grid=(n // MOE_PART, MOE_STEPS),
        in_specs=[
            pl.BlockSpec((MOE_PART, D_MODEL), row, pipeline_mode=pl.Buffered(1)),
            pl.BlockSpec((None, None, 6, D_MODEL), lambda p, s: (layer, mod_row_fn(p * MOE_PART), 0, 0)),
            pl.BlockSpec((None, MOE_ROUTE_ROWS, D_MODEL), lambda p, s: (layer, 0, 0)),
            wspec((D_MODEL, EXPERT_FF)),
            wspec((D_MODEL, EXPERT_FF)),
            wspec((EXPERT_FF, D_MODEL)),
            pl.BlockSpec((None, 1, D_MODEL), lambda *_: (layer, 0, 0)),
            pl.BlockSpec((None, 1, D_MODEL), lambda *_: (layer, 0, 0)),
        ],
        out_specs=pl.BlockSpec((MOE_PART, D_MODEL), row),
        out_shape=jax.ShapeDtypeStruct((n, D_MODEL), F32),
        scratch_shapes=[
            pltpu.VMEM((MOE_PART, D_MODEL), BF16),
            pltpu.VMEM((MOE_STEPS, 8, MOE_PART), F32),
            pltpu.VMEM((MOE_STEPS, 8, MOE_PART), F32),
            pltpu.VMEM((MOE_PART // MOE_TILE, MOE_EB * MOE_CH, MOE_TILE), BF16),
            pltpu.VMEM((MOE_EB, MOE_PART // MOE_TILE * MOE_CH, D_MODEL), BF16),
            pltpu.VMEM((MOE_EB, MOE_PART // MOE_TILE * MOE_CH, D_MODEL), BF16),
        ],
        compiler_params=_cparams(("arbitrary", "arbitrary")),
        name=name,
    )(x2d, mod_l, wr_t, wg_bf, wu_bf, wd_bf, lnw, lnb)


def _na_bias_tables(rel_bias):
    n_dr = 2 * NA_WIN_ROWS - 1
    pad_c = GRID_W - NA_WIN_COLS
    padded = jnp.pad(rel_bias.astype(F32) * LOG2E, ((0, 0), (0, 0), (0, 0), (pad_c, pad_c)), mode="edge")
    padded = jnp.pad(padded, ((0, 0), (0, 0), (0, 0), (0, 1)))
    rep = jnp.tile(padded, (1, 1, 1, GRID_W))[..., :GRID_W * (2 * GRID_W - 1)]
    tab = rep.reshape(DEPTH, NA_HEADS, n_dr, GRID_W, 2 * GRID_W - 1)[..., GRID_W - 1:]
    col = np.arange(GRID_W)
    col_start = np.clip(col - NA_WIN_COLS // 2, 0, GRID_W - NA_WIN_COLS)
    col_ok = (col[None, :] >= col_start[:, None]) & (col[None, :] < col_start[:, None] + NA_WIN_COLS)
    tab = jnp.where(jnp.asarray(col_ok), tab, NEG_INF)
    tab = jnp.pad(tab, ((0, 0), (0, 0), (NA_DR_PAD, NA_DR_ROWS + 1 - n_dr - NA_DR_PAD), (0, 0), (0, 0)))
    return jnp.concatenate([tab[:, :, :-1], tab[:, :, 1:]], axis=-1)


def _rope_tables():
    n = DIFF_D // 4
    lane = np.arange(DIFF_W)
    d = lane % DIFF_D
    use_col = d >= DIFF_D // 2
    e = d % (DIFF_D // 2)
    f = e % n
    first = e < n
    t = np.arange(DEC_SEQ)
    pos = np.where(use_col[None, :], (t % GRID_W)[:, None], (t // GRID_W)[:, None]).astype(np.float64)
    freqs = ROPE_BASE ** (-np.arange(n, dtype=np.float64) / n)
    ang = pos * freqs[f][None, :]
    sign = np.where(first, -1.0, 1.0)
    return jnp.asarray(np.cos(ang), F32), jnp.asarray(np.sin(ang) * sign[None, :], F32)


def kernel(x_prompt, x_sample, c, cache_na_k, cache_na_v, state_ret, cache_diff_k, cache_diff_v, c_ctx,
           w_mod, b_mod, w_in, na_rel_bias, ret_decay, ret_gn_w, ret_gn_b, diff_lambda, diff_norm_w,
           w_out, ln1_w, ln1_b, router_group, router_expert, exp_w_gate, exp_w_up, exp_w_down,
           ln2_w, ln2_b):
    n_ctx = BATCH * SEQ
    n_lat = DEC_BATCH * DEC_SEQ
    x_ctx = x_prompt.reshape(n_ctx, D_MODEL)
    x_lat = x_sample.reshape(n_lat, D_MODEL)

    cond_t = jnp.concatenate([c_ctx[:, None], c.T, jnp.zeros((D_MODEL, 8 - 1 - DEC_BATCH), F32)], axis=1)
    mod = _modulation(cond_t, w_mod, b_mod)

    cache_na_k = cache_na_k.reshape(DEC_BATCH, DEPTH, PAST_LEN, NA_W)
    cache_na_v = cache_na_v.reshape(DEC_BATCH, DEPTH, PAST_LEN, NA_W)
    cache_diff_k = cache_diff_k.reshape(DEC_BATCH, DEPTH, PAST_LEN, DIFF_W)
    cache_diff_v = cache_diff_v.reshape(DEC_BATCH, DEPTH, PAST_LEN, DIFF_W)
    cos_tab, sin_tab = _rope_tables()
    st0 = state_ret.astype(F32).reshape(DEC_BATCH, DEPTH, 2, RET_HEADS // 2, 2, HEAD_DIM, 1, HEAD_DIM)
    st0 = (st0 * jnp.eye(2, dtype=F32)[:, None, :, None]).reshape(
        DEC_BATCH, DEPTH, 2, RET_HEADS // 2, PAIR, PAIR)

    ctx_row = lambda r: 0
    lat_row = lambda r: 1 + r // DEC_SEQ

    wg_bf = exp_w_gate.astype(BF16)
    wu_bf = exp_w_up.astype(BF16)
    wd_bf = exp_w_down.astype(BF16)

    w_in_bf = w_in.astype(BF16)
    w_out_bf = w_out.astype(BF16)
    wr_t = jnp.concatenate([jnp.swapaxes(router_expert, 1, 2), jnp.swapaxes(router_group, 1, 2),
                            jnp.zeros((DEPTH, MOE_ROUTE_ROWS - N_EXPERTS - N_GROUPS, D_MODEL), F32)], axis=1)
    log_g_all = jax.nn.log_sigmoid(ret_decay.astype(F32)) * LOG2E
    lp = diff_lambda.astype(F32)
    lam_dyn = jnp.exp(jnp.sum(lp[:, 0] * lp[:, 1], axis=-1)) - jnp.exp(jnp.sum(lp[:, 2] * lp[:, 3], axis=-1))

    bias_tab = _na_bias_tables(na_rel_bias)
    gnw = ret_gn_w.reshape(DEPTH, 1, RET_W)
    gnb = ret_gn_b.reshape(DEPTH, 1, RET_W)
    dnw = diff_norm_w.reshape(DEPTH, 1, DIFF_W)
    l1w = ln1_w.reshape(DEPTH, 1, D_MODEL)
    l1b = ln1_b.reshape(DEPTH, 1, D_MODEL)
    l2w = ln2_w.reshape(DEPTH, 1, D_MODEL)
    l2b = ln2_b.reshape(DEPTH, 1, D_MODEL)

    caches = None
    states = None
    for l in range(DEPTH):
        lam_init = 0.8 - 0.6 * math.exp(-0.3 * l)
        lam = (lam_dyn[l] + lam_init).reshape(1)
        log_g = log_g_all[l]

        q_c, kv_c, *caches = _project(x_ctx, mod, w_in_bf, ctx_row, l, emit_caches=True, caches=caches)
        x1_c, states = _ctx_mix(q_c, kv_c, x_ctx, mod, w_out_bf, log_g, lam, gnw, gnb, dnw, l1w, l1b,
                                states, l, lam_init)
        x_ctx = _moe(x1_c, mod, ctx_row, wr_t, wg_bf, wu_bf, wd_bf, l2w, l2b, l, "moe_ctx")

        q_l, kv_l = _project(x_lat, mod, w_in_bf, lat_row, l)
        x1_l = _lat_mix(q_l, kv_l, x_lat, mod, w_out_bf, log_g, lam, gnw, gnb, dnw, l1w, l1b,
                        cache_na_k, cache_na_v, st0, cache_diff_k, cache_diff_v,
                        bias_tab, cos_tab, sin_tab, l, lam_init)
        x_lat = _moe(x1_l, mod, lat_row, wr_t, wg_bf, wu_bf, wd_bf, l2w, l2b, l, "moe_lat")

    new_na_k, new_na_v, new_diff_k, new_diff_v = caches
    return (x_ctx.reshape(BATCH, SEQ, D_MODEL), x_lat.reshape(DEC_BATCH, DEC_SEQ, D_MODEL),
            new_na_k.reshape(BATCH, DEPTH, SEQ, NA_HEADS, HEAD_DIM),
            new_na_v.reshape(BATCH, DEPTH, SEQ, NA_HEADS, HEAD_DIM),
            states,
            new_diff_k.reshape(BATCH, DEPTH, SEQ, DIFF_HEADS, 2 * DIFF_D),
            new_diff_v.reshape(BATCH, DEPTH, SEQ, DIFF_HEADS, DIFF_DV))
```

```python
import functools
import math

import numpy as np
import jax
import jax.numpy as jnp
from jax import lax
from jax.experimental import pallas as pl
from jax.experimental.pallas import tpu as pltpu

D_MODEL = 1024
BATCH = 32
SEQ = 256
DEPTH = 2
DEC_BATCH = 2
DEC_SEQ = 2048
PAST_LEN = 512
GRID_W = 64
HEAD_DIM = 64
NA_HEADS = 6
NA_WIN_ROWS = 8
NA_WIN_COLS = 16
RET_HEADS = 6
DIFF_HEADS = 4
DIFF_D = 32
DIFF_DV = 64
NA_W = NA_HEADS * HEAD_DIM
RET_W = RET_HEADS * HEAD_DIM
DIFF_W = DIFF_HEADS * DIFF_DV
MIX_W = NA_W + RET_W + DIFF_W
IN_COLS = 3 * NA_W + 4 * RET_W + 3 * DIFF_W
N_GROUPS = 4
EXPERTS_PER_GROUP = 8
N_EXPERTS = N_GROUPS * EXPERTS_PER_GROUP
EXPERT_FF = 256
ROPE_BASE = 10000.0
LN_EPS = 1e-5
NEG_INF = -1e30
ALPHA = (2.0 * DEPTH) ** 0.25
LOG2E = math.log2(math.e)

F32 = jnp.float32
BF16 = jnp.bfloat16

Q_NA, Q_RET, Q_GATE, Q_DIFF = 0, NA_W, NA_W + RET_W, NA_W + 2 * RET_W
Q_COLS = NA_W + 2 * RET_W + DIFF_W
KV_NAK, KV_NAV = 0, NA_W
KV_RETK, KV_RETV = 2 * NA_W, 2 * NA_W + RET_W
KV_DK, KV_DV = 2 * NA_W + 2 * RET_W, 2 * NA_W + 2 * RET_W + DIFF_W
KV_COLS = 2 * NA_W + 2 * RET_W + 2 * DIFF_W
MIX_RET, MIX_DIFF = NA_W, NA_W + RET_W

NA_Q_TILE = 256
NA_KEY_ROWS = 12
NA_KEYS = NA_KEY_ROWS * GRID_W
CTX_PROBLEMS = NA_HEADS + 2 * DIFF_HEADS + RET_HEADS
GRID_ROWS = DEC_SEQ // GRID_W
PAIR = 2 * HEAD_DIM
NA_DR_PAD = 8
NA_DR_ROWS = 32

VMEM_LIMIT = 60 * 1024 * 1024


def _cparams(sem):
    return pltpu.CompilerParams(dimension_semantics=sem, vmem_limit_bytes=VMEM_LIMIT)


def _dot(a, b):
    return jnp.dot(a, b, preferred_element_type=F32)


def _dot_nt(a, b):
    return lax.dot_general(a, b, (((1,), (1,)), ((), ())), preferred_element_type=F32)


def _dot_tn(a, b):
    return lax.dot_general(a, b, (((0,), (0,)), ((), ())), preferred_element_type=F32)


def _silu(x):
    return x / (1.0 + jnp.exp(-x))


def _layer_norm_rows(z, w, b):
    mu = jnp.mean(z, axis=-1, keepdims=True)
    zc = z - mu
    var = jnp.mean(zc * zc, axis=-1, keepdims=True)
    return zc * lax.rsqrt(var + LN_EPS) * w + b


def _mod_kernel(cond_ref, w_ref, b_ref, o_ref):
    s = _silu(cond_ref[...])
    w = w_ref[...]
    n_cond = 1 + DEC_BATCH
    rows = [jnp.sum(s[:, r:r + 1] * w, axis=0, keepdims=True) for r in range(n_cond)]
    rows.append(jnp.zeros((8 - n_cond, w.shape[1]), F32))
    o_ref[...] = jnp.concatenate(rows, axis=0) + b_ref[...]


def _modulation(cond_t, w_mod, b_mod):
    nj = 6
    out = pl.pallas_call(
        _mod_kernel,
        grid=(DEPTH, nj),
        in_specs=[
            pl.BlockSpec((D_MODEL, 8), lambda l, j: (0, 0)),
            pl.BlockSpec((None, D_MODEL, D_MODEL), lambda l, j: (l, 0, j)),
            pl.BlockSpec((None, 1, D_MODEL), lambda l, j: (l, 0, j)),
        ],
        out_specs=pl.BlockSpec((None, 8, D_MODEL), lambda l, j: (l, 0, j)),
        out_shape=jax.ShapeDtypeStruct((DEPTH, 8, 6 * D_MODEL), F32),
        compiler_params=_cparams(("arbitrary", "arbitrary")),
        name="modulation",
    )(cond_t, w_mod, b_mod.reshape(DEPTH, 1, 6 * D_MODEL))
    return out.reshape(DEPTH, 8, 6, D_MODEL)


def _proj_kernel(x_ref, mod_ref, w_ref, *refs):
    if len(refs) == 10:
        refs = refs[4:]
    q_ref, kv_ref = refs[:2]
    cache_refs = refs[2:]
    m = mod_ref[...]
    h = x_ref[...] * (1.0 + m[1:2]) + m[0:1]
    p = _dot(h.astype(BF16), w_ref[...])
    o = 0
    na_q = p[:, o:o + NA_W] * (HEAD_DIM ** -0.5 * LOG2E); o += NA_W
    na_k = p[:, o:o + NA_W]; o += NA_W
    na_v = p[:, o:o + NA_W]; o += NA_W
    ret_q = p[:, o:o + RET_W]; o += RET_W
    ret_k = p[:, o:o + RET_W] * (HEAD_DIM ** -0.5); o += RET_W
    ret_v = p[:, o:o + RET_W]; o += RET_W
    ret_g = p[:, o:o + RET_W]; o += RET_W
    dq = p[:, o:o + DIFF_W] * (DIFF_D ** -0.5 * LOG2E); o += DIFF_W
    dk = p[:, o:o + DIFF_W]; o += DIFF_W
    dv = p[:, o:o + DIFF_W]
    q_ref[:, Q_NA:Q_NA + NA_W] = na_q.astype(BF16)
    q_ref[:, Q_RET:Q_RET + RET_W] = ret_q.astype(BF16)
    q_ref[:, Q_GATE:Q_GATE + RET_W] = ret_g.astype(BF16)
    q_ref[:, Q_DIFF:Q_DIFF + DIFF_W] = dq.astype(BF16)
    kv_ref[:, KV_NAK:KV_NAK + NA_W] = na_k.astype(BF16)
    kv_ref[:, KV_NAV:KV_NAV + NA_W] = na_v.astype(BF16)
    kv_ref[:, KV_RETK:KV_RETK + RET_W] = ret_k.astype(BF16)
    kv_ref[:, KV_RETV:KV_RETV + RET_W] = ret_v.astype(BF16)
    kv_ref[:, KV_DK:KV_DK + DIFF_W] = dk.astype(BF16)
    kv_ref[:, KV_DV:KV_DV + DIFF_W] = dv.astype(BF16)
    nb = x_ref.shape[0] // SEQ
    for ref, val in zip(cache_refs, (na_k, na_v, dk, dv)):
        val = val.reshape(nb, SEQ, val.shape[-1])
        if len(ref.shape) == 4:
            ref[:, 0] = val
            ref[:, 1:] = jnp.zeros((nb, DEPTH - 1) + val.shape[1:], F32)
        else:
            ref[...] = val


def _project(x2d, mod, w_in_bf, mod_row_fn, layer, emit_caches=False, caches=None, tm=512):
    n = x2d.shape[0]
    row = lambda i: (i, 0)
    out_shape = [jax.ShapeDtypeStruct((n, Q_COLS), BF16), jax.ShapeDtypeStruct((n, KV_COLS), BF16)]
    out_specs = [pl.BlockSpec((tm, Q_COLS), row), pl.BlockSpec((tm, KV_COLS), row)]
    in_specs = [
        pl.BlockSpec((tm, D_MODEL), row),
        pl.BlockSpec((None, None, 6, D_MODEL), lambda i: (layer, mod_row_fn(i * tm), 0, 0)),
        pl.BlockSpec((None, D_MODEL, IN_COLS), lambda i: (layer, 0, 0)),
    ]
    args = [x2d, mod, w_in_bf]
    aliases = {}
    if emit_caches:
        for k, w in enumerate((NA_W, NA_W, DIFF_W, DIFF_W)):
            out_shape.append(jax.ShapeDtypeStruct((BATCH, DEPTH, SEQ, w), F32))
            if caches is None:
                out_specs.append(pl.BlockSpec((tm // SEQ, DEPTH, SEQ, w), lambda i: (i, 0, 0, 0)))
            else:
                in_specs.append(pl.BlockSpec(memory_space=pl.ANY))
                args.append(caches[k])
                aliases[3 + k] = 2 + k
                out_specs.append(pl.BlockSpec((tm // SEQ, None, SEQ, w), lambda i: (i, layer, 0, 0)))
    return pl.pallas_call(
        _proj_kernel,
        grid=(n // tm,),
        in_specs=in_specs,
        out_specs=out_specs,
        out_shape=out_shape,
        input_output_aliases=aliases,
        compiler_params=_cparams(("arbitrary",)),
        name="proj_ctx" if emit_caches else "proj_lat",
    )(*args)


def _lane_group(rows, width):
    return lax.broadcasted_iota(jnp.int32, (rows, PAIR), 1) // width


def _keep_group(x, groups, g):
    return jnp.where(groups == g, x, jnp.zeros_like(x))


def _with_ones(v_pair):
    return jnp.concatenate([v_pair, jnp.ones(v_pair.shape, v_pair.dtype)], axis=1)


def _normalise(r):
    return r[:, :PAIR] * (1.0 / r[:, PAIR:PAIR + 1])


def _attend(qm, k_list, vx_list, bias=None):
    scores = [_dot_nt(qm, k) for k in k_list]
    if bias is not None:
        scores[0] = scores[0] + bias
    m = None
    for s in scores:
        mi = jnp.max(s, axis=-1, keepdims=True)
        m = mi if m is None else jnp.maximum(m, mi)
    r = None
    for s, vx in zip(scores, vx_list):
        ri = _dot(jnp.exp2(s - m).astype(BF16), vx)
        r = ri if r is None else r + ri
    return _normalise(r)


def _group_mean_matrix():
    r = lax.broadcasted_iota(jnp.int32, (PAIR, PAIR), 0) // HEAD_DIM
    c = lax.broadcasted_iota(jnp.int32, (PAIR, PAIR), 1) // HEAD_DIM
    return jnp.where(r == c, 1.0 / HEAD_DIM, 0.0).astype(BF16)


def _group_mean(x, gm):
    hi = x.astype(BF16)
    lo = (x - hi.astype(F32)).astype(BF16)
    return _dot(hi, gm) + _dot(lo, gm)


def _head_layer_norm(o, gm):
    oc = o - _group_mean(o, gm)
    return oc * lax.rsqrt(_group_mean(oc * oc, gm) + LN_EPS)


def _head_rms_norm(o, gm):
    return o * lax.rsqrt(_group_mean(o * o, gm) + LN_EPS)


def _pair_scalar(ref, row, j, groups):
    return jnp.where(groups == 0, ref[row, 2 * j], ref[row, 2 * j + 1])


def _decay_table(lg_ref, dm_ref):
    T = SEQ
    i = lax.broadcasted_iota(jnp.int32, (T, T), 0).astype(F32)
    j = lax.broadcasted_iota(jnp.int32, (T, T), 1).astype(F32)
    d = i - j
    for h in range(RET_HEADS):
        dm_ref[h * T:(h + 1) * T, :] = jnp.exp2(jnp.where(d >= 0, d * lg_ref[0, h], (-d) * lg_ref[1, h]))


def _out_proj_post_norm(mix_ref, wout_ref, x_ref, mod_ref, lnw_ref, lnb_ref, x1_ref):
    y = _dot(mix_ref[...].astype(BF16), wout_ref[...])
    m = mod_ref[...]
    z = ALPHA * x_ref[...] + m[2:3] * y
    x1_ref[...] = _layer_norm_rows(z, lnw_ref[...], lnb_ref[...])


def _ctx_mix_kernel(lg_ref, lam_ref, q_ref, kv_ref, x_ref, mod_ref, wout_ref, gnw_ref, gnb_ref,
                    dnw_ref, lnw_ref, lnb_ref, *rest, one_minus_lam_init):
    if len(rest) == 8:
        rest = rest[1:]
    x1_ref, st_ref, mix_ref, dm_ref, s_ref, e_ref, o_ref = rest
    if len(st_ref.shape) == 5:
        st_ref[1:] = jnp.zeros((DEPTH - 1,) + st_ref.shape[1:], F32)
        st_ref = st_ref.at[0]
    T = SEQ
    n_soft = NA_HEADS + 2 * DIFF_HEADS

    @pl.when(pl.program_id(0) == 0)
    def _():
        _decay_table(lg_ref, dm_ref)

    half = _lane_group(T, HEAD_DIM)
    quarter = _lane_group(T, DIFF_D)
    n_na, n_diff, n_ret = NA_HEADS // 2, DIFF_HEADS // 2, RET_HEADS // 2

    for j in range(n_na):
        q = q_ref[:, Q_NA + j * PAIR:Q_NA + (j + 1) * PAIR]
        k = kv_ref[:, KV_NAK + j * PAIR:KV_NAK + (j + 1) * PAIR]
        for g in range(2):
            r0 = (2 * j + g) * T
            s_ref[r0:r0 + T, :] = _dot_nt(_keep_group(q, half, g), k)
    for j in range(n_diff):
        q = q_ref[:, Q_DIFF + j * PAIR:Q_DIFF + (j + 1) * PAIR]
        k = kv_ref[:, KV_DK + j * PAIR:KV_DK + (j + 1) * PAIR]
        for g in range(4):
            r0 = (NA_HEADS + 4 * j + g) * T
            s_ref[r0:r0 + T, :] = _dot_nt(_keep_group(q, quarter, g), k)
    for j in range(n_ret):
        q = q_ref[:, Q_RET + j * PAIR:Q_RET + (j + 1) * PAIR]
        k = kv_ref[:, KV_RETK + j * PAIR:KV_RETK + (j + 1) * PAIR]
        for g in range(2):
            r0 = (n_soft + 2 * j + g) * T
            s_ref[r0:r0 + T, :] = _dot_nt(_keep_group(q, half, g), k)

    s = s_ref[0:n_soft * T, :]
    e_ref[0:n_soft * T, :] = jnp.exp2(s - jnp.max(s, axis=-1, keepdims=True)).astype(BF16)
    e_ref[n_soft * T:, :] = (s_ref[n_soft * T:, :] * dm_ref[...]).astype(BF16)

    def pv(n, vx):
        return _normalise(_dot(e_ref[n * T:(n + 1) * T, :], vx))

    for j in range(n_na):
        vx = _with_ones(kv_ref[:, KV_NAV + j * PAIR:KV_NAV + (j + 1) * PAIR])
        mix_ref[:, j * PAIR:(j + 1) * PAIR] = jnp.where(half == 0, pv(2 * j, vx), pv(2 * j + 1, vx))
    lam = lam_ref[0]
    for j in range(n_diff):
        vx = _with_ones(kv_ref[:, KV_DV + j * PAIR:KV_DV + (j + 1) * PAIR])
        n0 = NA_HEADS + 4 * j
        head_a = pv(n0, vx) - lam * pv(n0 + 1, vx)
        head_b = pv(n0 + 2, vx) - lam * pv(n0 + 3, vx)
        o_ref[(n_ret + j) * T:(n_ret + j + 1) * T, :] = jnp.where(half == 0, head_a, head_b)
    jj = lax.broadcasted_iota(jnp.int32, (T, PAIR), 0).astype(F32)
    for j in range(n_ret):
        kf = kv_ref[:, KV_RETK + j * PAIR:KV_RETK + (j + 1) * PAIR].astype(F32)
        v = kv_ref[:, KV_RETV + j * PAIR:KV_RETV + (j + 1) * PAIR]
        n0 = n_soft + 2 * j
        o_ref[j * T:(j + 1) * T, :] = jnp.where(half == 0, _dot(e_ref[n0 * T:(n0 + 1) * T, :], v),
                                                _dot(e_ref[(n0 + 1) * T:(n0 + 2) * T, :], v))
        k_fwd = (kf * jnp.exp2((T - 1.0 - jj) * _pair_scalar(lg_ref, 0, j, half))).astype(BF16)
        k_bwd = (kf * jnp.exp2(jj * _pair_scalar(lg_ref, 1, j, half))).astype(BF16)
        for d, kd in enumerate((k_fwd, k_bwd)):
            st = _dot_tn(kd, v)
            st_ref[d, 2 * j] = st[0:HEAD_DIM, 0:HEAD_DIM]
            st_ref[d, 2 * j + 1] = st[HEAD_DIM:PAIR, HEAD_DIM:PAIR]

    gm = _group_mean_matrix()
    rn = _head_layer_norm(o_ref[0:n_ret * T, :], gm)
    dn = _head_rms_norm(o_ref[n_ret * T:, :], gm)
    for j in range(n_ret):
        c = j * PAIR
        g = q_ref[:, Q_GATE + c:Q_GATE + c + PAIR].astype(F32)
        mix_ref[:, MIX_RET + c:MIX_RET + c + PAIR] = (
            (rn[j * T:(j + 1) * T] * gnw_ref[:, c:c + PAIR] + gnb_ref[:, c:c + PAIR]) * _silu(g))
    for j in range(n_diff):
        c = j * PAIR
        mix_ref[:, MIX_DIFF + c:MIX_DIFF + c + PAIR] = (
            dn[j * T:(j + 1) * T] * dnw_ref[:, c:c + PAIR] * one_minus_lam_init)

    _out_proj_post_norm(mix_ref, wout_ref, x_ref, mod_ref, lnw_ref, lnb_ref, x1_ref)


def _ctx_mix(q_arr, kv_arr, x2d, mod, w_out_bf, log_g, lam, gnw, gnb, dnw, lnw, lnb, states, layer,
             lam_init):
    row = lambda b: (b, 0)
    const2 = lambda b: (0, 0)
    smem = pl.BlockSpec(memory_space=pltpu.SMEM)
    st_shape = (BATCH, DEPTH, 2, RET_HEADS, HEAD_DIM, HEAD_DIM)
    in_specs = [
        smem, smem,
        pl.BlockSpec((SEQ, Q_COLS), row),
        pl.BlockSpec((SEQ, KV_COLS), row),
        pl.BlockSpec((SEQ, D_MODEL), row),
        pl.BlockSpec((None, None, 6, D_MODEL), lambda b: (layer, 0, 0, 0)),
        pl.BlockSpec((None, MIX_W, D_MODEL), lambda b: (layer, 0, 0)),
        pl.BlockSpec((None, 1, RET_W), lambda *_: (layer, 0, 0)),
        pl.BlockSpec((None, 1, RET_W), lambda *_: (layer, 0, 0)),
        pl.BlockSpec((None, 1, DIFF_W), lambda *_: (layer, 0, 0)),
        pl.BlockSpec((None, 1, D_MODEL), lambda *_: (layer, 0, 0)),
        pl.BlockSpec((None, 1, D_MODEL), lambda *_: (layer, 0, 0)),
    ]
    args = [log_g, lam, q_arr, kv_arr, x2d, mod, w_out_bf, gnw, gnb, dnw, lnw, lnb]
    if states is None:
        st_spec = pl.BlockSpec((None,) + st_shape[1:], lambda b: (b, 0, 0, 0, 0, 0))
        aliases = {}
    else:
        in_specs.append(pl.BlockSpec(memory_space=pl.ANY))
        args.append(states)
        st_spec = pl.BlockSpec((None, None) + st_shape[2:], lambda b: (b, layer, 0, 0, 0, 0))
        aliases = {12: 1}
    return pl.pallas_call(
        functools.partial(_ctx_mix_kernel, one_minus_lam_init=1.0 - lam_init),
        grid=(BATCH,),
        in_specs=in_specs,
        out_specs=[pl.BlockSpec((SEQ, D_MODEL), row), st_spec],
        out_shape=[
            jax.ShapeDtypeStruct((BATCH * SEQ, D_MODEL), F32),
            jax.ShapeDtypeStruct(st_shape, F32),
        ],
        input_output_aliases=aliases,
        scratch_shapes=[
            pltpu.VMEM((SEQ, MIX_W), F32),
            pltpu.VMEM((RET_HEADS * SEQ, SEQ), F32),
            pltpu.VMEM((CTX_PROBLEMS * SEQ, SEQ), F32),
            pltpu.VMEM((CTX_PROBLEMS * SEQ, SEQ), BF16),
            pltpu.VMEM(((RET_HEADS + DIFF_HEADS) // 2 * SEQ, PAIR), F32),
        ],
        compiler_params=_cparams(("arbitrary",)),
        name="ctx_mix",
    )(*args)


def _rope(x, cos, sin_signed):
    n, w = x.shape
    lane = lax.broadcasted_iota(jnp.int32, (n, w), 1)
    first = (lane % 16) < 8
    partner = jnp.where(first, pltpu.roll(x, w - 8, 1), pltpu.roll(x, 8, 1))
    return x * cos + partner * sin_signed


def _na_bias_blocks(rb_ref, bias_ref):
    n_dr = 2 * NA_WIN_ROWS - 1
    lane = lax.broadcasted_iota(jnp.int32, (GRID_W, PAIR), 1)
    qc = lax.broadcasted_iota(jnp.int32, (GRID_W, PAIR), 0)
    kc = lane % GRID_W
    start = jnp.clip(qc - NA_WIN_COLS // 2, 0, GRID_W - NA_WIN_COLS)
    col_ok = jnp.where(kc >= start, jnp.where(kc < start + NA_WIN_COLS, 1.0, 0.0), 0.0) > 0.5
    zero = jnp.zeros((GRID_W, PAIR), F32)
    for h in range(NA_HEADS):
        left, right = [], []
        for dr in range(n_dr):
            row = jnp.broadcast_to(rb_ref[h, dr:dr + 1, :], (GRID_W, PAIR))
            left.append(pltpu.roll(row, GRID_W + 1, 1, stride=1, stride_axis=0))
            right.append(pltpu.roll(row, 1, 1, stride=1, stride_axis=0))
        for i in range(NA_DR_ROWS):
            d0 = i - NA_DR_PAD
            a = left[d0] if 0 <= d0 < n_dr else zero
            b = right[d0 + 1] if 0 <= d0 + 1 < n_dr else zero
            bias_ref[h, i] = jnp.where(col_ok, jnp.where(lane < GRID_W, a, b), NEG_INF)


def _lat_mix_kernel(lg_ref, lam_ref, q_ref, kv_ref, x_ref, mod_ref, wout_ref, gnw_ref, gnb_ref,
                    dnw_ref, lnw_ref, lnb_ref, cnak_ref, cnav_ref, st0_ref, cdk_ref, cdv_ref,
                    rb_ref, cos_ref, sin_ref, x1_ref, mix_ref, kr_ref, dm_ref, sf_ref, sb_ref,
                    kc_na_ref, vxc_na_ref, vx_na_ref, kc_d_ref, vxc_d_ref, vx_d_ref, bias_ref, *,
                    one_minus_lam_init):
    TQ = NA_Q_TILE
    T = DEC_SEQ
    n_qt = T // TQ
    n_na, n_diff, n_ret = NA_HEADS // 2, DIFF_HEADS // 2, RET_HEADS // 2
    qt = pl.program_id(1)
    q0 = pl.multiple_of(qt * TQ, TQ)
    half = _lane_group(TQ, HEAD_DIM)
    quarter = _lane_group(TQ, DIFF_D)

    @pl.when(qt == 0)
    def _():
        kr = _rope(kv_ref[:, KV_DK:KV_DK + DIFF_W].astype(F32), cos_ref[...], sin_ref[...])
        kr_ref[...] = kr.astype(BF16)
        kc_na_ref[...] = cnak_ref[...].astype(BF16)
        kc_d_ref[...] = cdk_ref[...].astype(BF16)
        for j in range(n_na):
            vxc_na_ref[j] = _with_ones(cnav_ref[:, j * PAIR:(j + 1) * PAIR].astype(BF16))
            vx_na_ref[j] = _with_ones(kv_ref[:, KV_NAV + j * PAIR:KV_NAV + (j + 1) * PAIR])
        for j in range(n_diff):
            vxc_d_ref[j] = _with_ones(cdv_ref[:, j * PAIR:(j + 1) * PAIR].astype(BF16))
            vx_d_ref[j] = _with_ones(kv_ref[:, KV_DV + j * PAIR:KV_DV + (j + 1) * PAIR])
        _decay_table(lg_ref, dm_ref)
        _na_bias_blocks(rb_ref, bias_ref)
        jl = lax.broadcasted_iota(jnp.int32, (TQ, PAIR), 0).astype(F32)
        rows = lax.broadcasted_iota(jnp.int32, (PAIR, PAIR), 0) // HEAD_DIM
        cols = lax.broadcasted_iota(jnp.int32, (PAIR, PAIR), 1) // HEAD_DIM
        for j in range(n_ret):
            lf = _pair_scalar(lg_ref, 0, j, half)
            lb = _pair_scalar(lg_ref, 1, j, half)
            dec_f = jnp.exp2((TQ - 1.0 - jl) * lf)
            dec_b = jnp.exp2(jl * lb)
            tile_f = jnp.exp2(float(TQ) * _pair_scalar(lg_ref, 0, j, rows))
            tile_b = jnp.exp2(float(TQ) * _pair_scalar(lg_ref, 1, j, rows))
            loc_f, loc_b = [], []
            for t in range(n_qt):
                kf = kv_ref[t * TQ:(t + 1) * TQ, KV_RETK + j * PAIR:KV_RETK + (j + 1) * PAIR].astype(F32)
                v = kv_ref[t * TQ:(t + 1) * TQ, KV_RETV + j * PAIR:KV_RETV + (j + 1) * PAIR]
                loc_f.append(jnp.where(rows == cols, _dot_tn((kf * dec_f).astype(BF16), v), 0.0))
                loc_b.append(jnp.where(rows == cols, _dot_tn((kf * dec_b).astype(BF16), v), 0.0))
            state = st0_ref[0, j]
            for t in range(n_qt):
                sf_ref[t, j] = state.astype(BF16)
                state = state * tile_f + loc_f[t]
            state = st0_ref[1, j]
            for t in reversed(range(n_qt)):
                sb_ref[t, j] = state.astype(BF16)
                state = state * tile_b + loc_b[t]

    q_rows = TQ // GRID_W
    r0 = qt * q_rows
    ks = jnp.clip(r0 - NA_WIN_ROWS // 2, 0, GRID_ROWS - NA_KEY_ROWS)
    k0 = pl.multiple_of(ks * GRID_W, 256)
    key_row = lax.broadcasted_iota(jnp.int32, (1, NA_KEYS), 1) // GRID_W + ks
    row_masks = []
    for a in range(q_rows):
        start = jnp.clip(r0 + a - NA_WIN_ROWS // 2, 0, GRID_ROWS - NA_WIN_ROWS)
        inside = jnp.where(key_row >= start, jnp.where(key_row < start + NA_WIN_ROWS, 1.0, 0.0), 0.0)
        row_masks.append(jnp.where(inside > 0.5, 0.0, NEG_INF))

    def window_bias(h):
        rows = []
        for a in range(q_rows):
            base = ks - r0 - a + (NA_WIN_ROWS - 1) + NA_DR_PAD
            blocks = [bias_ref[h, base + 2 * p] for p in range(NA_KEY_ROWS // 2)]
            rows.append(jnp.concatenate(blocks, axis=1) + row_masks[a])
        return jnp.concatenate(rows, axis=0)

    for j in range(n_na):
        c = j * PAIR
        q = q_ref[:, Q_NA + c:Q_NA + c + PAIR]
        ks_list = [kv_ref[pl.ds(k0, NA_KEYS), KV_NAK + c:KV_NAK + c + PAIR], kc_na_ref[:, c:c + PAIR]]
        vx_list = [vx_na_ref[j, pl.ds(k0, NA_KEYS), :], vxc_na_ref[j]]
        o = [_attend(_keep_group(q, half, g), ks_list, vx_list, window_bias(2 * j + g)) for g in range(2)]
        mix_ref[:, c:c + PAIR] = jnp.where(half == 0, o[0], o[1])

    gm = _group_mean_matrix()
    ii = lax.broadcasted_iota(jnp.int32, (TQ, PAIR), 0).astype(F32)
    for j in range(n_ret):
        c = j * PAIR
        q = q_ref[:, Q_RET + c:Q_RET + c + PAIR]
        k = kv_ref[pl.ds(q0, TQ), KV_RETK + c:KV_RETK + c + PAIR]
        v = kv_ref[pl.ds(q0, TQ), KV_RETV + c:KV_RETV + c + PAIR]
        inner = []
        for g in range(2):
            h = 2 * j + g
            sc = _dot_nt(_keep_group(q, half, g), k) * dm_ref[h * TQ:(h + 1) * TQ, :]
            inner.append(_dot(sc.astype(BF16), v))
        qf = q.astype(F32)
        q_fwd = (qf * jnp.exp2((ii + 1.0) * _pair_scalar(lg_ref, 0, j, half))).astype(BF16)
        q_bwd = (qf * jnp.exp2((TQ - ii) * _pair_scalar(lg_ref, 1, j, half))).astype(BF16)
        o = jnp.where(half == 0, inner[0], inner[1]) + _dot(q_fwd, sf_ref[qt, j]) + _dot(q_bwd, sb_ref[qt, j])
        gate = q_ref[:, Q_GATE + c:Q_GATE + c + PAIR].astype(F32)
        mix_ref[:, MIX_RET + c:MIX_RET + c + PAIR] = (
            (_head_layer_norm(o, gm) * gnw_ref[:, c:c + PAIR] + gnb_ref[:, c:c + PAIR]) * _silu(gate))

    lam = lam_ref[0]
    qr = _rope(q_ref[:, Q_DIFF:Q_DIFF + DIFF_W].astype(F32),
               cos_ref[pl.ds(q0, TQ), :], sin_ref[pl.ds(q0, TQ), :]).astype(BF16)
    for j in range(n_diff):
        c = j * PAIR
        q = qr[:, c:c + PAIR]
        ks_list = [kc_d_ref[:, c:c + PAIR], kr_ref[:, c:c + PAIR]]
        vx_list = [vxc_d_ref[j], vx_d_ref[j]]
        o = [_attend(_keep_group(q, quarter, g), ks_list, vx_list) for g in range(4)]
        blk = jnp.where(half == 0, o[0] - lam * o[1], o[2] - lam * o[3])
        mix_ref[:, MIX_DIFF + c:MIX_DIFF + c + PAIR] = (
            _head_rms_norm(blk, gm) * dnw_ref[:, c:c + PAIR] * one_minus_lam_init)

    _out_proj_post_norm(mix_ref, wout_ref, x_ref, mod_ref, lnw_ref, lnb_ref, x1_ref)


def _lat_mix(q_arr, kv_arr, x2d, mod_l, w_out_bf, log_g, lam, gnw, gnb, dnw, lnw, lnb,
             cache_na_k, cache_na_v, state_ret, cache_diff_k, cache_diff_v, bias_tab, cos_tab,
             sin_tab, layer, lam_init):
    nq = DEC_SEQ // NA_Q_TILE
    const2 = lambda b, t: (0, 0)
    smem = pl.BlockSpec(memory_space=pltpu.SMEM)
    qrow = lambda b, t: (b * nq + t, 0)

    return pl.pallas_call(
        functools.partial(_lat_mix_kernel, one_minus_lam_init=1.0 - lam_init),
        grid=(DEC_BATCH, nq),
        in_specs=[
            smem, smem,
            pl.BlockSpec((NA_Q_TILE, Q_COLS), qrow),
            pl.BlockSpec((DEC_SEQ, KV_COLS), lambda b, t: (b, 0), pipeline_mode=pl.Buffered(1)),
            pl.BlockSpec((NA_Q_TILE, D_MODEL), qrow),
            pl.BlockSpec((None, None, 6, D_MODEL), lambda b, t: (layer, b + 1, 0, 0)),
            pl.BlockSpec((None, MIX_W, D_MODEL), lambda b, t: (layer, 0, 0)),
            pl.BlockSpec((None, 1, RET_W), lambda *_: (layer, 0, 0)),
            pl.BlockSpec((None, 1, RET_W), lambda *_: (layer, 0, 0)),
            pl.BlockSpec((None, 1, DIFF_W), lambda *_: (layer, 0, 0)),
            pl.BlockSpec((None, 1, D_MODEL), lambda *_: (layer, 0, 0)),
            pl.BlockSpec((None, 1, D_MODEL), lambda *_: (layer, 0, 0)),
            pl.BlockSpec((None, None, PAST_LEN, NA_W), lambda b, t: (b, layer, 0, 0),
                         pipeline_mode=pl.Buffered(1)),
            pl.BlockSpec((None, None, PAST_LEN, NA_W), lambda b, t: (b, layer, 0, 0),
                         pipeline_mode=pl.Buffered(1)),
            pl.BlockSpec((None, None, 2, RET_HEADS // 2, PAIR, PAIR),
                         lambda b, t: (b, layer, 0, 0, 0, 0)),
            pl.BlockSpec((None, None, PAST_LEN, DIFF_W), lambda b, t: (b, layer, 0, 0),
                         pipeline_mode=pl.Buffered(1)),
            pl.BlockSpec((None, None, PAST_LEN, DIFF_W), lambda b, t: (b, layer, 0, 0),
                         pipeline_mode=pl.Buffered(1)),
            pl.BlockSpec((None, NA_HEADS, 2 * NA_WIN_ROWS, PAIR), lambda b, t: (layer, 0, 0, 0)),
            pl.BlockSpec((DEC_SEQ, DIFF_W), const2),
            pl.BlockSpec((DEC_SEQ, DIFF_W), const2),
        ],
        out_specs=pl.BlockSpec((NA_Q_TILE, D_MODEL), qrow),
        out_shape=jax.ShapeDtypeStruct((DEC_BATCH * DEC_SEQ, D_MODEL), F32),
        scratch_shapes=[
            pltpu.VMEM((NA_Q_TILE, MIX_W), F32),
            pltpu.VMEM((DEC_SEQ, DIFF_W), BF16),
            pltpu.VMEM((RET_HEADS * NA_Q_TILE, NA_Q_TILE), F32),
            pltpu.VMEM((DEC_SEQ // NA_Q_TILE, RET_HEADS // 2, PAIR, PAIR), BF16),
            pltpu.VMEM((DEC_SEQ // NA_Q_TILE, RET_HEADS // 2, PAIR, PAIR), BF16),
            pltpu.VMEM((PAST_LEN, NA_W), BF16),
            pltpu.VMEM((NA_HEADS // 2, PAST_LEN, 2 * PAIR), BF16),
            pltpu.VMEM((NA_HEADS // 2, DEC_SEQ, 2 * PAIR), BF16),
            pltpu.VMEM((PAST_LEN, DIFF_W), BF16),
            pltpu.VMEM((DIFF_HEADS // 2, PAST_LEN, 2 * PAIR), BF16),
            pltpu.VMEM((DIFF_HEADS // 2, DEC_SEQ, 2 * PAIR), BF16),
            pltpu.VMEM((NA_HEADS, NA_DR_ROWS, GRID_W, PAIR), F32),
        ],
        compiler_params=_cparams(("arbitrary", "arbitrary")),
        name="lat_mix",
    )(log_g, lam, q_arr, kv_arr, x2d, mod_l, w_out_bf, gnw, gnb, dnw, lnw, lnb,
      cache_na_k, cache_na_v, state_ret, cache_diff_k, cache_diff_v, bias_tab, cos_tab, sin_tab)


MOE_PART = 2048
MOE_TILE = 512
MOE_EB = 4
MOE_CH = 48
MOE_STEPS = N_EXPERTS // MOE_EB
MOE_ROUTE_ROWS = 40


def _route_transposed(lt):
    shape = lt.shape
    r = lax.broadcasted_iota(jnp.int32, shape, 0).astype(F32)
    ninf = -jnp.inf
    is_g = jnp.where(r >= N_EXPERTS, jnp.where(r < N_EXPERTS + N_GROUPS, 1.0, 0.0), 0.0) > 0.5
    gl = jnp.where(is_g, lt, ninf)
    gmax = jnp.max(gl, axis=0, keepdims=True)
    gsel = jnp.min(jnp.where(gl == gmax, r - N_EXPERTS, 1e9), axis=0, keepdims=True)
    gsum = jnp.sum(jnp.where(is_g, jnp.exp(gl - gmax), 0.0), axis=0, keepdims=True)
    gw = 1.0 / gsum
    lo = gsel * EXPERTS_PER_GROUP
    is_e = jnp.where(r >= lo, jnp.where(r < lo + EXPERTS_PER_GROUP, 1.0, 0.0), 0.0) > 0.5
    el = jnp.where(is_e, lt, ninf)
    v1 = jnp.max(el, axis=0, keepdims=True)
    i1 = jnp.min(jnp.where(el == v1, r, 1e9), axis=0, keepdims=True)
    el2 = jnp.where(r == i1, ninf, el)
    v2 = jnp.max(el2, axis=0, keepdims=True)
    i2 = jnp.min(jnp.where(el2 == v2, r, 1e9), axis=0, keepdims=True)
    t = jnp.exp(v2 - v1)
    w1 = gw / (1.0 + t)
    w2 = gw * t / (1.0 + t)
    first = r == i1
    second = r == i2
    gates = jnp.where(first, w1, 0.0) + jnp.where(second, w2, 0.0)
    member = jnp.where(first, 1.0, jnp.where(second, 1.0, 0.0))
    return gates, member


def _moe_kernel(x_ref, mod_ref, wr_ref, wg_ref, wu_ref, wd_ref, lnw_ref, lnb_ref, out_ref,
                h_ref, rank_ref, gate_ref, sel_ref, xs_ref, ys_ref):
    s = pl.program_id(1)
    n_tiles = MOE_PART // MOE_TILE

    @pl.when(s == 0)
    def _():
        m = mod_ref[...]
        h = x_ref[...] * (1.0 + m[4:5]) + m[3:4]
        h_hi = h.astype(BF16)
        h_ref[...] = h_hi
        h_lo = (h - h_hi.astype(F32)).astype(BF16)
        w = wr_ref[...]
        w_hi = w.astype(BF16)
        w_lo = (w - w_hi.astype(F32)).astype(BF16)
        lt = _dot_nt(w_hi, h_hi) + (_dot_nt(w_hi, h_lo) + _dot_nt(w_lo, h_hi))
        gates, member = _route_transposed(lt)
        before = jnp.where(lax.broadcasted_iota(jnp.int32, (MOE_TILE, MOE_TILE), 0)
                           < lax.broadcasted_iota(jnp.int32, (MOE_TILE, MOE_TILE), 1), 1.0, 0.0).astype(BF16)
        rank_ref[...] = jnp.full(rank_ref.shape, -1.0, F32)
        gate_ref[...] = jnp.zeros(gate_ref.shape, F32)
        for t in range(n_tiles):
            c0 = t * MOE_TILE
            mem_t = member[0:N_EXPERTS, c0:c0 + MOE_TILE]
            cnt = _dot(mem_t.astype(BF16), before)
            rank = jnp.where(mem_t > 0.5, cnt, -1.0)
            for st in range(MOE_STEPS):
                rank_ref[st, 0:MOE_EB, c0:c0 + MOE_TILE] = rank[st * MOE_EB:(st + 1) * MOE_EB]
                gate_ref[st, 0:MOE_EB, c0:c0 + MOE_TILE] = gates[st * MOE_EB:(st + 1) * MOE_EB, c0:c0 + MOE_TILE]
        out_ref[...] = jnp.zeros(out_ref.shape, F32)

    ranks = rank_ref[s, 0:MOE_EB, :]
    gts = gate_ref[s, 0:MOE_EB, :]
    n_chunks = ((jnp.max(ranks) + 0.5) * (1.0 / MOE_CH)).astype(jnp.int32) + 1

    def chunk_body(k, carry):
        slot = (lax.broadcasted_iota(jnp.int32, (MOE_CH, MOE_TILE), 0) + k * MOE_CH).astype(F32)
        row_gate = [[] for _ in range(MOE_EB)]
        for t in range(n_tiles):
            c0 = t * MOE_TILE
            onehots = []
            for i in range(MOE_EB):
                hit = ranks[i:i + 1, c0:c0 + MOE_TILE] == slot
                onehots.append(jnp.where(hit, 1.0, 0.0).astype(BF16))
                row_gate[i].append(jnp.sum(jnp.where(hit, gts[i:i + 1, c0:c0 + MOE_TILE], 0.0),
                                           axis=1, keepdims=True))
            sel = jnp.concatenate(onehots, axis=0)
            sel_ref[t] = sel
            xs = _dot(sel, h_ref[c0:c0 + MOE_TILE, :]).astype(BF16)
            for i in range(MOE_EB):
                xs_ref[i, t * MOE_CH:(t + 1) * MOE_CH, :] = xs[i * MOE_CH:(i + 1) * MOE_CH]
        for i in range(MOE_EB):
            xi = xs_ref[i]
            a = _dot(xi, wg_ref[i])
            u = _dot(xi, wu_ref[i])
            hm = (_silu(a) * u * jnp.concatenate(row_gate[i], axis=0)).astype(BF16)
            ys_ref[i] = _dot(hm, wd_ref[i]).astype(BF16)
        for t in range(n_tiles):
            c0 = t * MOE_TILE
            y = jnp.concatenate([ys_ref[i, t * MOE_CH:(t + 1) * MOE_CH, :] for i in range(MOE_EB)], axis=0)
            out_ref[c0:c0 + MOE_TILE, :] += _dot_tn(sel_ref[t], y)
        return carry

    lax.fori_loop(0, n_chunks, chunk_body, 0)

    @pl.when(s == MOE_STEPS - 1)
    def _():
        m = mod_ref[...]
        z = ALPHA * x_ref[...] + m[5:6] * out_ref[...]
        out_ref[...] = _layer_norm_rows(z, lnw_ref[...], lnb_ref[...])


def _moe(x2d, mod_l, mod_row_fn, wr_t, wg_bf, wu_bf, wd_bf, lnw, lnb, layer, name):
    n = x2d.shape[0]
    row = lambda p, s: (p, 0)
    const2 = lambda p, s: (0, 0)
    wspec = lambda shape: pl.BlockSpec((None, MOE_EB) + shape, lambda p, s: (layer, s, 0, 0))
    return pl.pallas_call(
        _moe_kernel,
        grid=(n // MOE_PART, MOE_STEPS),
        in_specs=[
            pl.BlockSpec((MOE_PART, D_MODEL), row, pipeline_mode=pl.Buffered(1)),
            pl.BlockSpec((None, None, 6, D_MODEL), lambda p, s: (layer, mod_row_fn(p * MOE_PART), 0, 0)),
            pl.BlockSpec((None, MOE_ROUTE_ROWS, D_MODEL), lambda p, s: (layer, 0, 0)),
            wspec((D_MODEL, EXPERT_FF)),
            wspec((D_MODEL, EXPERT_FF)),
            wspec((EXPERT_FF, D_MODEL)),
            pl.BlockSpec((None, 1, D_MODEL), lambda *_: (layer, 0, 0)),
            pl.BlockSpec((None, 1, D_MODEL), lambda *_: (layer, 0, 0)),
        ],
        out_specs=pl.BlockSpec((MOE_PART, D_MODEL), row),
        out_shape=jax.ShapeDtypeStruct((n, D_MODEL), F32),
        scratch_shapes=[
            pltpu.VMEM((MOE_PART, D_MODEL), BF16),
            pltpu.VMEM((MOE_STEPS, 8, MOE_PART), F32),
            pltpu.VMEM((MOE_STEPS, 8, MOE_PART), F32),
            pltpu.VMEM((MOE_PART // MOE_TILE, MOE_EB * MOE_CH, MOE_TILE), BF16),
            pltpu.VMEM((MOE_EB, MOE_PART // MOE_TILE * MOE_CH, D_MODEL), BF16),
            pltpu.VMEM((MOE_EB, MOE_PART // MOE_TILE * MOE_CH, D_MODEL), BF16),
        ],
        compiler_params=_cparams(("arbitrary", "arbitrary")),
        name=name,
    )(x2d, mod_l, wr_t, wg_bf, wu_bf, wd_bf, lnw, lnb)


def _na_bias_tables(rel_bias):
    pad_c = GRID_W - NA_WIN_COLS
    padded = jnp.pad(rel_bias.astype(F32) * LOG2E, ((0, 0), (0, 0), (0, 0), (pad_c, pad_c)), mode="edge")
    return jnp.pad(padded, ((0, 0), (0, 0), (0, 1), (0, 1)))


def _rope_tables():
    n = DIFF_D // 4
    lane = np.arange(DIFF_W)
    d = lane % DIFF_D
    use_col = d >= DIFF_D // 2
    e = d % (DIFF_D // 2)
    f = e % n
    first = e < n
    t = np.arange(DEC_SEQ)
    pos = np.where(use_col[None, :], (t % GRID_W)[:, None], (t // GRID_W)[:, None]).astype(np.float64)
    freqs = ROPE_BASE ** (-np.arange(n, dtype=np.float64) / n)
    ang = pos * freqs[f][None, :]
    sign = np.where(first, -1.0, 1.0)
    return jnp.asarray(np.cos(ang), F32), jnp.asarray(np.sin(ang) * sign[None, :], F32)


def kernel(x_prompt, x_sample, c, cache_na_k, cache_na_v, state_ret, cache_diff_k, cache_diff_v, c_ctx,
           w_mod, b_mod, w_in, na_rel_bias, ret_decay, ret_gn_w, ret_gn_b, diff_lambda, diff_norm_w,
           w_out, ln1_w, ln1_b, router_group, router_expert, exp_w_gate, exp_w_up, exp_w_down,
           ln2_w, ln2_b):
    n_ctx = BATCH * SEQ
    n_lat = DEC_BATCH * DEC_SEQ
    x_ctx = x_prompt.reshape(n_ctx, D_MODEL)
    x_lat = x_sample.reshape(n_lat, D_MODEL)

    cond_t = jnp.concatenate([c_ctx[:, None], c.T, jnp.zeros((D_MODEL, 8 - 1 - DEC_BATCH), F32)], axis=1)
    mod = _modulation(cond_t, w_mod, b_mod)

    cache_na_k = cache_na_k.reshape(DEC_BATCH, DEPTH, PAST_LEN, NA_W)
    cache_na_v = cache_na_v.reshape(DEC_BATCH, DEPTH, PAST_LEN, NA_W)
    cache_diff_k = cache_diff_k.reshape(DEC_BATCH, DEPTH, PAST_LEN, DIFF_W)
    cache_diff_v = cache_diff_v.reshape(DEC_BATCH, DEPTH, PAST_LEN, DIFF_W)
    cos_tab, sin_tab = _rope_tables()
    st0 = state_ret.astype(F32).reshape(DEC_BATCH, DEPTH, 2, RET_HEADS // 2, 2, HEAD_DIM, 1, HEAD_DIM)
    st0 = (st0 * jnp.eye(2, dtype=F32)[:, None, :, None]).reshape(
        DEC_BATCH, DEPTH, 2, RET_HEADS // 2, PAIR, PAIR)

    ctx_row = lambda r: 0
    lat_row = lambda r: 1 + r // DEC_SEQ

    wg_bf = exp_w_gate.astype(BF16)
    wu_bf = exp_w_up.astype(BF16)
    wd_bf = exp_w_down.astype(BF16)

    w_in_bf = w_in.astype(BF16)
    w_out_bf = w_out.astype(BF16)
    wr_t = jnp.concatenate([jnp.swapaxes(router_expert, 1, 2), jnp.swapaxes(router_group, 1, 2),
                            jnp.zeros((DEPTH, MOE_ROUTE_ROWS - N_EXPERTS - N_GROUPS, D_MODEL), F32)], axis=1)
    log_g_all = jax.nn.log_sigmoid(ret_decay.astype(F32)) * LOG2E
    lp = diff_lambda.astype(F32)
    lam_dyn = jnp.exp(jnp.sum(lp[:, 0] * lp[:, 1], axis=-1)) - jnp.exp(jnp.sum(lp[:, 2] * lp[:, 3], axis=-1))

    bias_tab = _na_bias_tables(na_rel_bias)
    gnw = ret_gn_w.reshape(DEPTH, 1, RET_W)
    gnb = ret_gn_b.reshape(DEPTH, 1, RET_W)
    dnw = diff_norm_w.reshape(DEPTH, 1, DIFF_W)
    l1w = ln1_w.reshape(DEPTH, 1, D_MODEL)
    l1b = ln1_b.reshape(DEPTH, 1, D_MODEL)
    l2w = ln2_w.reshape(DEPTH, 1, D_MODEL)
    l2b = ln2_b.reshape(DEPTH, 1, D_MODEL)

    caches = None
    states = None
    for l in range(DEPTH):
        lam_init = 0.8 - 0.6 * math.exp(-0.3 * l)
        lam = (lam_dyn[l] + lam_init).reshape(1)
        log_g = log_g_all[l]

        q_c, kv_c, *caches = _project(x_ctx, mod, w_in_bf, ctx_row, l, emit_caches=True, caches=caches)
        x1_c, states = _ctx_mix(q_c, kv_c, x_ctx, mod, w_out_bf, log_g, lam, gnw, gnb, dnw, l1w, l1b,
                                states, l, lam_init)
        x_ctx = _moe(x1_c, mod, ctx_row, wr_t, wg_bf, wu_bf, wd_bf, l2w, l2b, l, "moe_ctx")

        q_l, kv_l = _project(x_lat, mod, w_in_bf, lat_row, l)
        x1_l = _lat_mix(q_l, kv_l, x_lat, mod, w_out_bf, log_g, lam, gnw, gnb, dnw, l1w, l1b,
                        cache_na_k, cache_na_v, st0, cache_diff_k, cache_diff_v,
                        bias_tab, cos_tab, sin_tab, l, lam_init)
        x_lat = _moe(x1_l, mod, lat_row, wr_t, wg_bf, wu_bf, wd_bf, l2w, l2b, l, "moe_lat")

    new_na_k, new_na_v, new_diff_k, new_diff_v = caches
    return (x_ctx.reshape(BATCH, SEQ, D_MODEL), x_lat.reshape(DEC_BATCH, DEC_SEQ, D_MODEL),
            new_na_k.reshape(BATCH, DEPTH, SEQ, NA_HEADS, HEAD_DIM),
            new_na_v.reshape(BATCH, DEPTH, SEQ, NA_HEADS, HEAD_DIM),
            states,
            new_diff_k.reshape(BATCH, DEPTH, SEQ, DIFF_HEADS, 2 * DIFF_D),
            new_diff_v.reshape(BATCH, DEPTH, SEQ, DIFF_HEADS, DIFF_DV))
```

```python
import functools
import math

import numpy as np
import jax
import jax.numpy as jnp
from jax import lax
from jax.experimental import pallas as pl
from jax.experimental.pallas import tpu as pltpu

D_MODEL = 1024
BATCH = 32
SEQ = 256
DEPTH = 2
DEC_BATCH = 2
DEC_SEQ = 2048
PAST_LEN = 512
GRID_W = 64
HEAD_DIM = 64
NA_HEADS = 6
NA_WIN_ROWS = 8
NA_WIN_COLS = 16
RET_HEADS = 6
DIFF_HEADS = 4
DIFF_D = 32
DIFF_DV = 64
NA_W = NA_HEADS * HEAD_DIM
RET_W = RET_HEADS * HEAD_DIM
DIFF_W = DIFF_HEADS * DIFF_DV
MIX_W = NA_W + RET_W + DIFF_W
IN_COLS = 3 * NA_W + 4 * RET_W + 3 * DIFF_W
N_GROUPS = 4
EXPERTS_PER_GROUP = 8
N_EXPERTS = N_GROUPS * EXPERTS_PER_GROUP
EXPERT_FF = 256
ROPE_BASE = 10000.0
LN_EPS = 1e-5
NEG_INF = -1e30
ALPHA = (2.0 * DEPTH) ** 0.25
LOG2E = math.log2(math.e)

F32 = jnp.float32
BF16 = jnp.bfloat16

Q_NA, Q_RET, Q_GATE, Q_DIFF = 0, NA_W, NA_W + RET_W, NA_W + 2 * RET_W
Q_COLS = NA_W + 2 * RET_W + DIFF_W
KV_NAK, KV_NAV = 0, NA_W
KV_RETK, KV_RETV = 2 * NA_W, 2 * NA_W + RET_W
KV_DK, KV_DV = 2 * NA_W + 2 * RET_W, 2 * NA_W + 2 * RET_W + DIFF_W
KV_COLS = 2 * NA_W + 2 * RET_W + 2 * DIFF_W
MIX_RET, MIX_DIFF = NA_W, NA_W + RET_W

NA_Q_TILE = 256
NA_KEY_ROWS = 12
NA_KEYS = NA_KEY_ROWS * GRID_W
CTX_PROBLEMS = NA_HEADS + 2 * DIFF_HEADS + RET_HEADS
GRID_ROWS = DEC_SEQ // GRID_W
PAIR = 2 * HEAD_DIM
NA_DR_PAD = 8
NA_DR_ROWS = 32

VMEM_LIMIT = 60 * 1024 * 1024


def _cparams(sem):
    return pltpu.CompilerParams(dimension_semantics=sem, vmem_limit_bytes=VMEM_LIMIT)


def _dot(a, b):
    return jnp.dot(a, b, preferred_element_type=F32)


def _dot_nt(a, b):
    return lax.dot_general(a, b, (((1,), (1,)), ((), ())), preferred_element_type=F32)


def _dot_tn(a, b):
    return lax.dot_general(a, b, (((0,), (0,)), ((), ())), preferred_element_type=F32)


def _silu(x):
    return x / (1.0 + jnp.exp(-x))


def _layer_norm_rows(z, w, b):
    mu = jnp.mean(z, axis=-1, keepdims=True)
    zc = z - mu
    var = jnp.mean(zc * zc, axis=-1, keepdims=True)
    return zc * lax.rsqrt(var + LN_EPS) * w + b


def _mod_kernel(cond_ref, w_ref, b_ref, o_ref):
    s = _silu(cond_ref[...])
    w = w_ref[...]
    n_cond = 1 + DEC_BATCH
    rows = [jnp.sum(s[:, r:r + 1] * w, axis=0, keepdims=True) for r in range(n_cond)]
    rows.append(jnp.zeros((8 - n_cond, w.shape[1]), F32))
    o_ref[...] = jnp.concatenate(rows, axis=0) + b_ref[...]


def _modulation(cond_t, w_mod, b_mod):
    nj = 6
    out = pl.pallas_call(
        _mod_kernel,
        grid=(DEPTH, nj),
        in_specs=[
            pl.BlockSpec((D_MODEL, 8), lambda l, j: (0, 0)),
            pl.BlockSpec((None, D_MODEL, D_MODEL), lambda l, j: (l, 0, j)),
            pl.BlockSpec((None, 1, D_MODEL), lambda l, j: (l, 0, j)),
        ],
        out_specs=pl.BlockSpec((None, 8, D_MODEL), lambda l, j: (l, 0, j)),
        out_shape=jax.ShapeDtypeStruct((DEPTH, 8, 6 * D_MODEL), F32),
        compiler_params=_cparams(("arbitrary", "arbitrary")),
        name="modulation",
    )(cond_t, w_mod, b_mod.reshape(DEPTH, 1, 6 * D_MODEL))
    return out.reshape(DEPTH, 8, 6, D_MODEL)


def _proj_kernel(x_ref, mod_ref, w_ref, *refs):
    if len(refs) == 10:
        refs = refs[4:]
    q_ref, kv_ref = refs[:2]
    cache_refs = refs[2:]
    m = mod_ref[...]
    h = x_ref[...] * (1.0 + m[1:2]) + m[0:1]
    p = _dot(h.astype(BF16), w_ref[...])
    o = 0
    na_q = p[:, o:o + NA_W] * (HEAD_DIM ** -0.5 * LOG2E); o += NA_W
    na_k = p[:, o:o + NA_W]; o += NA_W
    na_v = p[:, o:o + NA_W]; o += NA_W
    ret_q = p[:, o:o + RET_W]; o += RET_W
    ret_k = p[:, o:o + RET_W] * (HEAD_DIM ** -0.5); o += RET_W
    ret_v = p[:, o:o + RET_W]; o += RET_W
    ret_g = p[:, o:o + RET_W]; o += RET_W
    dq = p[:, o:o + DIFF_W] * (DIFF_D ** -0.5 * LOG2E); o += DIFF_W
    dk = p[:, o:o + DIFF_W]; o += DIFF_W
    dv = p[:, o:o + DIFF_W]
    q_ref[:, Q_NA:Q_NA + NA_W] = na_q.astype(BF16)
    q_ref[:, Q_RET:Q_RET + RET_W] = ret_q.astype(BF16)
    q_ref[:, Q_GATE:Q_GATE + RET_W] = ret_g.astype(BF16)
    q_ref[:, Q_DIFF:Q_DIFF + DIFF_W] = dq.astype(BF16)
    kv_ref[:, KV_NAK:KV_NAK + NA_W] = na_k.astype(BF16)
    kv_ref[:, KV_NAV:KV_NAV + NA_W] = na_v.astype(BF16)
    kv_ref[:, KV_RETK:KV_RETK + RET_W] = ret_k.astype(BF16)
    kv_ref[:, KV_RETV:KV_RETV + RET_W] = ret_v.astype(BF16)
    kv_ref[:, KV_DK:KV_DK + DIFF_W] = dk.astype(BF16)
    kv_ref[:, KV_DV:KV_DV + DIFF_W] = dv.astype(BF16)
    nb = x_ref.shape[0] // SEQ
    for ref, val in zip(cache_refs, (na_k, na_v, dk, dv)):
        val = val.reshape(nb, SEQ, val.shape[-1])
        if len(ref.shape) == 4:
            ref[:, 0] = val
            ref[:, 1:] = jnp.zeros((nb, DEPTH - 1) + val.shape[1:], F32)
        else:
            ref[...] = val


def _project(x2d, mod, w_in_bf, mod_row_fn, layer, emit_caches=False, caches=None, tm=512):
    n = x2d.shape[0]
    row = lambda i: (i, 0)
    out_shape = [jax.ShapeDtypeStruct((n, Q_COLS), BF16), jax.ShapeDtypeStruct((n, KV_COLS), BF16)]
    out_specs = [pl.BlockSpec((tm, Q_COLS), row), pl.BlockSpec((tm, KV_COLS), row)]
    in_specs = [
        pl.BlockSpec((tm, D_MODEL), row),
        pl.BlockSpec((None, None, 6, D_MODEL), lambda i: (layer, mod_row_fn(i * tm), 0, 0)),
        pl.BlockSpec((None, D_MODEL, IN_COLS), lambda i: (layer, 0, 0)),
    ]
    args = [x2d, mod, w_in_bf]
    aliases = {}
    if emit_caches:
        for k, w in enumerate((NA_W, NA_W, DIFF_W, DIFF_W)):
            out_shape.append(jax.ShapeDtypeStruct((BATCH, DEPTH, SEQ, w), F32))
            if caches is None:
                out_specs.append(pl.BlockSpec((tm // SEQ, DEPTH, SEQ, w), lambda i: (i, 0, 0, 0)))
            else:
                in_specs.append(pl.BlockSpec(memory_space=pl.ANY))
                args.append(caches[k])
                aliases[3 + k] = 2 + k
                out_specs.append(pl.BlockSpec((tm // SEQ, None, SEQ, w), lambda i: (i, layer, 0, 0)))
    return pl.pallas_call(
        _proj_kernel,
        grid=(n // tm,),
        in_specs=in_specs,
        out_specs=out_specs,
        out_shape=out_shape,
        input_output_aliases=aliases,
        compiler_params=_cparams(("arbitrary",)),
        name="proj_ctx" if emit_caches else "proj_lat",
    )(*args)


def _lane_group(rows, width):
    return lax.broadcasted_iota(jnp.int32, (rows, PAIR), 1) // width


def _keep_group(x, groups, g):
    return jnp.where(groups == g, x, jnp.zeros_like(x))


def _with_ones(v_pair):
    return jnp.concatenate([v_pair, jnp.ones(v_pair.shape, v_pair.dtype)], axis=1)


def _normalise(r):
    return r[:, :PAIR] * (1.0 / r[:, PAIR:PAIR + 1])


def _attend(qm, k_list, vx_list, bias=None):
    scores = [_dot_nt(qm, k) for k in k_list]
    if bias is not None:
        scores[0] = scores[0] + bias
    m = None
    for s in scores:
        mi = jnp.max(s, axis=-1, keepdims=True)
        m = mi if m is None else jnp.maximum(m, mi)
    r = None
    for s, vx in zip(scores, vx_list):
        ri = _dot(jnp.exp2(s - m).astype(BF16), vx)
        r = ri if r is None else r + ri
    return _normalise(r)


def _group_mean_matrix():
    r = lax.broadcasted_iota(jnp.int32, (PAIR, PAIR), 0) // HEAD_DIM
    c = lax.broadcasted_iota(jnp.int32, (PAIR, PAIR), 1) // HEAD_DIM
    return jnp.where(r == c, 1.0 / HEAD_DIM, 0.0).astype(BF16)


def _group_mean(x, gm):
    hi = x.astype(BF16)
    lo = (x - hi.astype(F32)).astype(BF16)
    return _dot(hi, gm) + _dot(lo, gm)


def _head_layer_norm(o, gm):
    oc = o - _group_mean(o, gm)
    return oc * lax.rsqrt(_group_mean(oc * oc, gm) + LN_EPS)


def _head_rms_norm(o, gm):
    return o * lax.rsqrt(_group_mean(o * o, gm) + LN_EPS)


def _pair_scalar(ref, row, j, groups):
    return jnp.where(groups == 0, ref[row, 2 * j], ref[row, 2 * j + 1])


def _decay_table(lg_ref, dm_ref):
    T = SEQ
    i = lax.broadcasted_iota(jnp.int32, (T, T), 0).astype(F32)
    j = lax.broadcasted_iota(jnp.int32, (T, T), 1).astype(F32)
    d = i - j
    for h in range(RET_HEADS):
        dm_ref[h * T:(h + 1) * T, :] = jnp.exp2(jnp.where(d >= 0, d * lg_ref[0, h], (-d) * lg_ref[1, h]))


def _out_proj_post_norm(mix_ref, wout_ref, x_ref, mod_ref, lnw_ref, lnb_ref, x1_ref):
    y = _dot(mix_ref[...].astype(BF16), wout_ref[...])
    m = mod_ref[...]
    z = ALPHA * x_ref[...] + m[2:3] * y
    x1_ref[...] = _layer_norm_rows(z, lnw_ref[...], lnb_ref[...])


def _ctx_mix_kernel(lg_ref, lam_ref, q_ref, kv_ref, x_ref, mod_ref, wout_ref, gnw_ref, gnb_ref,
                    dnw_ref, lnw_ref, lnb_ref, *rest, one_minus_lam_init):
    if len(rest) == 8:
        rest = rest[1:]
    x1_ref, st_ref, mix_ref, dm_ref, s_ref, e_ref, o_ref = rest
    if len(st_ref.shape) == 5:
        st_ref[1:] = jnp.zeros((DEPTH - 1,) + st_ref.shape[1:], F32)
        st_ref = st_ref.at[0]
    T = SEQ
    n_soft = NA_HEADS + 2 * DIFF_HEADS

    @pl.when(pl.program_id(0) == 0)
    def _():
        _decay_table(lg_ref, dm_ref)

    half = _lane_group(T, HEAD_DIM)
    quarter = _lane_group(T, DIFF_D)
    n_na, n_diff, n_ret = NA_HEADS // 2, DIFF_HEADS // 2, RET_HEADS // 2

    for j in range(n_na):
        q = q_ref[:, Q_NA + j * PAIR:Q_NA + (j + 1) * PAIR]
        k = kv_ref[:, KV_NAK + j * PAIR:KV_NAK + (j + 1) * PAIR]
        for g in range(2):
            r0 = (2 * j + g) * T
            s_ref[r0:r0 + T, :] = _dot_nt(_keep_group(q, half, g), k)
    for j in range(n_diff):
        q = q_ref[:, Q_DIFF + j * PAIR:Q_DIFF + (j + 1) * PAIR]
        k = kv_ref[:, KV_DK + j * PAIR:KV_DK + (j + 1) * PAIR]
        for g in range(4):
            r0 = (NA_HEADS + 4 * j + g) * T
            s_ref[r0:r0 + T, :] = _dot_nt(_keep_group(q, quarter, g), k)
    for j in range(n_ret):
        q = q_ref[:, Q_RET + j * PAIR:Q_RET + (j + 1) * PAIR]
        k = kv_ref[:, KV_RETK + j * PAIR:KV_RETK + (j + 1) * PAIR]
        for g in range(2):
            r0 = (n_soft + 2 * j + g) * T
            s_ref[r0:r0 + T, :] = _dot_nt(_keep_group(q, half, g), k)

    s = s_ref[0:n_soft * T, :]
    e_ref[0:n_soft * T, :] = jnp.exp2(s - jnp.max(s, axis=-1, keepdims=True)).astype(BF16)
    e_ref[n_soft * T:, :] = (s_ref[n_soft * T:, :] * dm_ref[...]).astype(BF16)

    def pv(n, vx):
        return _normalise(_dot(e_ref[n * T:(n + 1) * T, :], vx))

    for j in range(n_na):
        vx = _with_ones(kv_ref[:, KV_NAV + j * PAIR:KV_NAV + (j + 1) * PAIR])
        mix_ref[:, j * PAIR:(j + 1) * PAIR] = jnp.where(half == 0, pv(2 * j, vx), pv(2 * j + 1, vx))
    lam = lam_ref[0]
    for j in range(n_diff):
        vx = _with_ones(kv_ref[:, KV_DV + j * PAIR:KV_DV + (j + 1) * PAIR])
        n0 = NA_HEADS + 4 * j
        head_a = pv(n0, vx) - lam * pv(n0 + 1, vx)
        head_b = pv(n0 + 2, vx) - lam * pv(n0 + 3, vx)
        o_ref[(n_ret + j) * T:(n_ret + j + 1) * T, :] = jnp.where(half == 0, head_a, head_b)
    jj = lax.broadcasted_iota(jnp.int32, (T, PAIR), 0).astype(F32)
    for j in range(n_ret):
        kf = kv_ref[:, KV_RETK + j * PAIR:KV_RETK + (j + 1) * PAIR].astype(F32)
        v = kv_ref[:, KV_RETV + j * PAIR:KV_RETV + (j + 1) * PAIR]
        n0 = n_soft + 2 * j
        o_ref[j * T:(j + 1) * T, :] = jnp.where(half == 0, _dot(e_ref[n0 * T:(n0 + 1) * T, :], v),
                                                _dot(e_ref[(n0 + 1) * T:(n0 + 2) * T, :], v))
        k_fwd = (kf * jnp.exp2((T - 1.0 - jj) * _pair_scalar(lg_ref, 0, j, half))).astype(BF16)
        k_bwd = (kf * jnp.exp2(jj * _pair_scalar(lg_ref, 1, j, half))).astype(BF16)
        for d, kd in enumerate((k_fwd, k_bwd)):
            st = _dot_tn(kd, v)
            st_ref[d, 2 * j] = st[0:HEAD_DIM, 0:HEAD_DIM]
            st_ref[d, 2 * j + 1] = st[HEAD_DIM:PAIR, HEAD_DIM:PAIR]

    gm = _group_mean_matrix()
    rn = _head_layer_norm(o_ref[0:n_ret * T, :], gm)
    dn = _head_rms_norm(o_ref[n_ret * T:, :], gm)
    for j in range(n_ret):
        c = j * PAIR
        g = q_ref[:, Q_GATE + c:Q_GATE + c + PAIR].astype(F32)
        mix_ref[:, MIX_RET + c:MIX_RET + c + PAIR] = (
            (rn[j * T:(j + 1) * T] * gnw_ref[:, c:c + PAIR] + gnb_ref[:, c:c + PAIR]) * _silu(g))
    for j in range(n_diff):
        c = j * PAIR
        mix_ref[:, MIX_DIFF + c:MIX_DIFF + c + PAIR] = (
            dn[j * T:(j + 1) * T] * dnw_ref[:, c:c + PAIR] * one_minus_lam_init)

    _out_proj_post_norm(mix_ref, wout_ref, x_ref, mod_ref, lnw_ref, lnb_ref, x1_ref)


def _ctx_mix(q_arr, kv_arr, x2d, mod, w_out_bf, log_g, lam, gnw, gnb, dnw, lnw, lnb, states, layer,
             lam_init):
    row = lambda b: (b, 0)
    smem = pl.BlockSpec(memory_space=pltpu.SMEM)
    st_shape = (BATCH, DEPTH, 2, RET_HEADS, HEAD_DIM, HEAD_DIM)
    in_specs = [
        smem, smem,
        pl.BlockSpec((SEQ, Q_COLS), row),
        pl.BlockSpec((SEQ, KV_COLS), row),
        pl.BlockSpec((SEQ, D_MODEL), row),
        pl.BlockSpec((None, None, 6, D_MODEL), lambda b: (layer, 0, 0, 0)),
        pl.BlockSpec((None, MIX_W, D_MODEL), lambda b: (layer, 0, 0)),
        pl.BlockSpec((None, 1, RET_W), lambda *_: (layer, 0, 0)),
        pl.BlockSpec((None, 1, RET_W), lambda *_: (layer, 0, 0)),
        pl.BlockSpec((None, 1, DIFF_W), lambda *_: (layer, 0, 0)),
        pl.BlockSpec((None, 1, D_MODEL), lambda *_: (layer, 0, 0)),
        pl.BlockSpec((None, 1, D_MODEL), lambda *_: (layer, 0, 0)),
    ]
    args = [log_g, lam, q_arr, kv_arr, x2d, mod, w_out_bf, gnw, gnb, dnw, lnw, lnb]
    if states is None:
        st_spec = pl.BlockSpec((None,) + st_shape[1:], lambda b: (b, 0, 0, 0, 0, 0))
        aliases = {}
    else:
        in_specs.append(pl.BlockSpec(memory_space=pl.ANY))
        args.append(states)
        st_spec = pl.BlockSpec((None, None) + st_shape[2:], lambda b: (b, layer, 0, 0, 0, 0))
        aliases = {12: 1}
    return pl.pallas_call(
        functools.partial(_ctx_mix_kernel, one_minus_lam_init=1.0 - lam_init),
        grid=(BATCH,),
        in_specs=in_specs,
        out_specs=[pl.BlockSpec((SEQ, D_MODEL), row), st_spec],
        out_shape=[
            jax.ShapeDtypeStruct((BATCH * SEQ, D_MODEL), F32),
            jax.ShapeDtypeStruct(st_shape, F32),
        ],
        input_output_aliases=aliases,
        scratch_shapes=[
            pltpu.VMEM((SEQ, MIX_W), F32),
            pltpu.VMEM((RET_HEADS * SEQ, SEQ), F32),
            pltpu.VMEM((CTX_PROBLEMS * SEQ, SEQ), F32),
            pltpu.VMEM((CTX_PROBLEMS * SEQ, SEQ), BF16),
            pltpu.VMEM(((RET_HEADS + DIFF_HEADS) // 2 * SEQ, PAIR), F32),
        ],
        compiler_params=_cparams(("arbitrary",)),
        name="ctx_mix",
    )(*args)


def _rope(x, cos, sin_signed):
    n, w = x.shape
    lane = lax.broadcasted_iota(jnp.int32, (n, w), 1)
    first = (lane % 16) < 8
    partner = jnp.where(first, pltpu.roll(x, w - 8, 1), pltpu.roll(x, 8, 1))
    return x * cos + partner * sin_signed


def _na_bias_blocks(rb_ref, bias_ref):
    n_dr = 2 * NA_WIN_ROWS - 1
    lane = lax.broadcasted_iota(jnp.int32, (GRID_W, PAIR), 1)
    qc = lax.broadcasted_iota(jnp.int32, (GRID_W, PAIR), 0)
    kc = lane % GRID_W
    start = jnp.clip(qc - NA_WIN_COLS // 2, 0, GRID_W - NA_WIN_COLS)
    col_ok = jnp.where(kc >= start, jnp.where(kc < start + NA_WIN_COLS, 1.0, 0.0), 0.0) > 0.5
    zero = jnp.zeros((GRID_W, PAIR), F32)
    for h in range(NA_HEADS):
        left, right = [], []
        for dr in range(n_dr):
            row = jnp.broadcast_to(rb_ref[h, dr:dr + 1, :], (GRID_W, PAIR))
            left.append(pltpu.roll(row, GRID_W + 1, 1, stride=1, stride_axis=0))
            right.append(pltpu.roll(row, 1, 1, stride=1, stride_axis=0))
        for i in range(NA_DR_ROWS):
            d0 = i - NA_DR_PAD
            a = left[d0] if 0 <= d0 < n_dr else zero
            b = right[d0 + 1] if 0 <= d0 + 1 < n_dr else zero
            bias_ref[h, i] = jnp.where(col_ok, jnp.where(lane < GRID_W, a, b), NEG_INF)


def _lat_mix_kernel(lg_ref, lam_ref, q_ref, kv_ref, x_ref, mod_ref, wout_ref, gnw_ref, gnb_ref,
                    dnw_ref, lnw_ref, lnb_ref, cnak_ref, cnav_ref, st0_ref, cdk_ref, cdv_ref,
                    rb_ref, cos_ref, sin_ref, x1_ref, mix_ref, kr_ref, dm_ref, sf_ref, sb_ref,
                    kc_na_ref, vxc_na_ref, vx_na_ref, kc_d_ref, vxc_d_ref, vx_d_ref, bias_ref, *,
                    one_minus_lam_init):
    TQ = NA_Q_TILE
    T = DEC_SEQ
    n_qt = T // TQ
    n_na, n_diff, n_ret = NA_HEADS // 2, DIFF_HEADS // 2, RET_HEADS // 2
    qt = pl.program_id(1)
    q0 = pl.multiple_of(qt * TQ, TQ)
    half = _lane_group(TQ, HEAD_DIM)
    quarter = _lane_group(TQ, DIFF_D)

    @pl.when(qt == 0)
    def _():
        kr = _rope(kv_ref[:, KV_DK:KV_DK + DIFF_W].astype(F32), cos_ref[...], sin_ref[...])
        kr_ref[...] = kr.astype(BF16)
        kc_na_ref[...] = cnak_ref[...].astype(BF16)
        kc_d_ref[...] = cdk_ref[...].astype(BF16)
        for j in range(n_na):
            vxc_na_ref[j] = _with_ones(cnav_ref[:, j * PAIR:(j + 1) * PAIR].astype(BF16))
            vx_na_ref[j] = _with_ones(kv_ref[:, KV_NAV + j * PAIR:KV_NAV + (j + 1) * PAIR])
        for j in range(n_diff):
            vxc_d_ref[j] = _with_ones(cdv_ref[:, j * PAIR:(j + 1) * PAIR].astype(BF16))
            vx_d_ref[j] = _with_ones(kv_ref[:, KV_DV + j * PAIR:KV_DV + (j + 1) * PAIR])
        _decay_table(lg_ref, dm_ref)
        _na_bias_blocks(rb_ref, bias_ref)
        jl = lax.broadcasted_iota(jnp.int32, (TQ, PAIR), 0).astype(F32)
        rows = lax.broadcasted_iota(jnp.int32, (PAIR, PAIR), 0) // HEAD_DIM
        cols = lax.broadcasted_iota(jnp.int32, (PAIR, PAIR), 1) // HEAD_DIM
        for j in range(n_ret):
            lf = _pair_scalar(lg_ref, 0, j, half)
            lb = _pair_scalar(lg_ref, 1, j, half)
            dec_f = jnp.exp2((TQ - 1.0 - jl) * lf)
            dec_b = jnp.exp2(jl * lb)
            tile_f = jnp.exp2(float(TQ) * _pair_scalar(lg_ref, 0, j, rows))
            tile_b = jnp.exp2(float(TQ) * _pair_scalar(lg_ref, 1, j, rows))
            loc_f, loc_b = [], []
            for t in range(n_qt):
                kf = kv_ref[t * TQ:(t + 1) * TQ, KV_RETK + j * PAIR:KV_RETK + (j + 1) * PAIR].astype(F32)
                v = kv_ref[t * TQ:(t + 1) * TQ, KV_RETV + j * PAIR:KV_RETV + (j + 1) * PAIR]
                loc_f.append(jnp.where(rows == cols, _dot_tn((kf * dec_f).astype(BF16), v), 0.0))
                loc_b.append(jnp.where(rows == cols, _dot_tn((kf * dec_b).astype(BF16), v), 0.0))
            state = st0_ref[0, j]
            for t in range(n_qt):
                sf_ref[t, j] = state.astype(BF16)
                state = state * tile_f + loc_f[t]
            state = st0_ref[1, j]
            for t in reversed(range(n_qt)):
                sb_ref[t, j] = state.astype(BF16)
                state = state * tile_b + loc_b[t]

    q_rows = TQ // GRID_W
    r0 = qt * q_rows
    ks = jnp.clip(r0 - NA_WIN_ROWS // 2, 0, GRID_ROWS - NA_KEY_ROWS)
    k0 = pl.multiple_of(ks * GRID_W, 256)
    key_row = lax.broadcasted_iota(jnp.int32, (1, NA_KEYS), 1) // GRID_W + ks
    row_masks = []
    for a in range(q_rows):
        start = jnp.clip(r0 + a - NA_WIN_ROWS // 2, 0, GRID_ROWS - NA_WIN_ROWS)
        inside = jnp.where(key_row >= start, jnp.where(key_row < start + NA_WIN_ROWS, 1.0, 0.0), 0.0)
        row_masks.append(jnp.where(inside > 0.5, 0.0, NEG_INF))

    def window_bias(h):
        rows = []
        for a in range(q_rows):
            base = ks - r0 - a + (NA_WIN_ROWS - 1) + NA_DR_PAD
            blocks = [bias_ref[h, base + 2 * p] for p in range(NA_KEY_ROWS // 2)]
            rows.append(jnp.concatenate(blocks, axis=1) + row_masks[a])
        return jnp.concatenate(rows, axis=0)

    for j in range(n_na):
        c = j * PAIR
        q = q_ref[:, Q_NA + c:Q_NA + c + PAIR]
        ks_list = [kv_ref[pl.ds(k0, NA_KEYS), KV_NAK + c:KV_NAK + c + PAIR], kc_na_ref[:, c:c + PAIR]]
        vx_list = [vx_na_ref[j, pl.ds(k0, NA_KEYS), :], vxc_na_ref[j]]
        o = [_attend(_keep_group(q, half, g), ks_list, vx_list, window_bias(2 * j + g)) for g in range(2)]
        mix_ref[:, c:c + PAIR] = jnp.where(half == 0, o[0], o[1])

    gm = _group_mean_matrix()
    ii = lax.broadcasted_iota(jnp.int32, (TQ, PAIR), 0).astype(F32)
    for j in range(n_ret):
        c = j * PAIR
        q = q_ref[:, Q_RET + c:Q_RET + c + PAIR]
        k = kv_ref[pl.ds(q0, TQ), KV_RETK + c:KV_RETK + c + PAIR]
        v = kv_ref[pl.ds(q0, TQ), KV_RETV + c:KV_RETV + c + PAIR]
        inner = []
        for g in range(2):
            h = 2 * j + g
            sc = _dot_nt(_keep_group(q, half, g), k) * dm_ref[h * TQ:(h + 1) * TQ, :]
            inner.append(_dot(sc.astype(BF16), v))
        qf = q.astype(F32)
        q_fwd = (qf * jnp.exp2((ii + 1.0) * _pair_scalar(lg_ref, 0, j, half))).astype(BF16)
        q_bwd = (qf * jnp.exp2((TQ - ii) * _pair_scalar(lg_ref, 1, j, half))).astype(BF16)
        o = jnp.where(half == 0, inner[0], inner[1]) + _dot(q_fwd, sf_ref[qt, j]) + _dot(q_bwd, sb_ref[qt, j])
        gate = q_ref[:, Q_GATE + c:Q_GATE + c + PAIR].astype(F32)
        mix_ref[:, MIX_RET + c:MIX_RET + c + PAIR] = (
            (_head_layer_norm(o, gm) * gnw_ref[:, c:c + PAIR] + gnb_ref[:, c:c + PAIR]) * _silu(gate))

    lam = lam_ref[0]
    qr = _rope(q_ref[:, Q_DIFF:Q_DIFF + DIFF_W].astype(F32),
               cos_ref[pl.ds(q0, TQ), :], sin_ref[pl.ds(q0, TQ), :]).astype(BF16)
    for j in range(n_diff):
        c = j * PAIR
        q = qr[:, c:c + PAIR]
        ks_list = [kc_d_ref[:, c:c + PAIR], kr_ref[:, c:c + PAIR]]
        vx_list = [vxc_d_ref[j], vx_d_ref[j]]
        o = [_attend(_keep_group(q, quarter, g), ks_list, vx_list) for g in range(4)]
        blk = jnp.where(half == 0, o[0] - lam * o[1], o[2] - lam * o[3])
        mix_ref[:, MIX_DIFF + c:MIX_DIFF + c + PAIR] = (
            _head_rms_norm(blk, gm) * dnw_ref[:, c:c + PAIR] * one_minus_lam_init)

    _out_proj_post_norm(mix_ref, wout_ref, x_ref, mod_ref, lnw_ref, lnb_ref, x1_ref)


def _lat_mix(q_arr, kv_arr, x2d, mod_l, w_out_bf, log_g, lam, gnw, gnb, dnw, lnw, lnb,
             cache_na_k, cache_na_v, state_ret, cache_diff_k, cache_diff_v, bias_tab, cos_tab,
             sin_tab, layer, lam_init):
    nq = DEC_SEQ // NA_Q_TILE
    const2 = lambda b, t: (0, 0)
    smem = pl.BlockSpec(memory_space=pltpu.SMEM)
    qrow = lambda b, t: (b * nq + t, 0)

    return pl.pallas_call(
        functools.partial(_lat_mix_kernel, one_minus_lam_init=1.0 - lam_init),
        grid=(DEC_BATCH, nq),
        in_specs=[
            smem, smem,
            pl.BlockSpec((NA_Q_TILE, Q_COLS), qrow),
            pl.BlockSpec((DEC_SEQ, KV_COLS), lambda b, t: (b, 0), pipeline_mode=pl.Buffered(1)),
            pl.BlockSpec((NA_Q_TILE, D_MODEL), qrow),
            pl.BlockSpec((None, None, 6, D_MODEL), lambda b, t: (layer, b + 1, 0, 0)),
            pl.BlockSpec((None, MIX_W, D_MODEL), lambda b, t: (layer, 0, 0)),
            pl.BlockSpec((None, 1, RET_W), lambda *_: (layer, 0, 0)),
            pl.BlockSpec((None, 1, RET_W), lambda *_: (layer, 0, 0)),
            pl.BlockSpec((None, 1, DIFF_W), lambda *_: (layer, 0, 0)),
            pl.BlockSpec((None, 1, D_MODEL), lambda *_: (layer, 0, 0)),
            pl.BlockSpec((None, 1, D_MODEL), lambda *_: (layer, 0, 0)),
            pl.BlockSpec((None, None, PAST_LEN, NA_W), lambda b, t: (b, layer, 0, 0),
                         pipeline_mode=pl.Buffered(1)),
            pl.BlockSpec((None, None, PAST_LEN, NA_W), lambda b, t: (b, layer, 0, 0),
                         pipeline_mode=pl.Buffered(1)),
            pl.BlockSpec((None, None, 2, RET_HEADS // 2, PAIR, PAIR),
                         lambda b, t: (b, layer, 0, 0, 0, 0)),
            pl.BlockSpec((None, None, PAST_LEN, DIFF_W), lambda b, t: (b, layer, 0, 0),
                         pipeline_mode=pl.Buffered(1)),
            pl.BlockSpec((None, None, PAST_LEN, DIFF_W), lambda b, t: (b, layer, 0, 0),
                         pipeline_mode=pl.Buffered(1)),
            pl.BlockSpec((None, NA_HEADS, 2 * NA_WIN_ROWS, PAIR), lambda b, t: (layer, 0, 0, 0)),
            pl.BlockSpec((DEC_SEQ, DIFF_W), const2),
            pl.BlockSpec((DEC_SEQ, DIFF_W), const2),
        ],
        out_specs=pl.BlockSpec((NA_Q_TILE, D_MODEL), qrow),
        out_shape=jax.ShapeDtypeStruct((DEC_BATCH * DEC_SEQ, D_MODEL), F32),
        scratch_shapes=[
            pltpu.VMEM((NA_Q_TILE, MIX_W), F32),
            pltpu.VMEM((DEC_SEQ, DIFF_W), BF16),
            pltpu.VMEM((RET_HEADS * NA_Q_TILE, NA_Q_TILE), F32),
            pltpu.VMEM((DEC_SEQ // NA_Q_TILE, RET_HEADS // 2, PAIR, PAIR), BF16),
            pltpu.VMEM((DEC_SEQ // NA_Q_TILE, RET_HEADS // 2, PAIR, PAIR), BF16),
            pltpu.VMEM((PAST_LEN, NA_W), BF16),
            pltpu.VMEM((NA_HEADS // 2, PAST_LEN, 2 * PAIR), BF16),
            pltpu.VMEM((NA_HEADS // 2, DEC_SEQ, 2 * PAIR), BF16),
            pltpu.VMEM((PAST_LEN, DIFF_W), BF16),
            pltpu.VMEM((DIFF_HEADS // 2, PAST_LEN, 2 * PAIR), BF16),
            pltpu.VMEM((DIFF_HEADS // 2, DEC_SEQ, 2 * PAIR), BF16),
            pltpu.VMEM((NA_HEADS, NA_DR_ROWS, GRID_W, PAIR), F32),
        ],
        compiler_params=_cparams(("arbitrary", "arbitrary")),
        name="lat_mix",
    )(log_g, lam, q_arr, kv_arr, x2d, mod_l, w_out_bf, gnw, gnb, dnw, lnw, lnb,
      cache_na_k, cache_na_v, state_ret, cache_diff_k, cache_diff_v, bias_tab, cos_tab, sin_tab)


MOE_PART = 2048
MOE_TILE = 512
MOE_EB = 4
MOE_CH = 48
MOE_STEPS = N_EXPERTS // MOE_EB
MOE_ROUTE_ROWS = 40


def _route_transposed(lt):
    shape = lt.shape
    r = lax.broadcasted_iota(jnp.int32, shape, 0).astype(F32)
    ninf = -jnp.inf
    is_g = jnp.where(r >= N_EXPERTS, jnp.where(r < N_EXPERTS + N_GROUPS, 1.0, 0.0), 0.0) > 0.5
    gl = jnp.where(is_g, lt, ninf)
    gmax = jnp.max(gl, axis=0, keepdims=True)
    gsel = jnp.min(jnp.where(gl == gmax, r - N_EXPERTS, 1e9), axis=0, keepdims=True)
    gsum = jnp.sum(jnp.where(is_g, jnp.exp(gl - gmax), 0.0), axis=0, keepdims=True)
    gw = 1.0 / gsum
    lo = gsel * EXPERTS_PER_GROUP
    is_e = jnp.where(r >= lo, jnp.where(r < lo + EXPERTS_PER_GROUP, 1.0, 0.0), 0.0) > 0.5
    el = jnp.where(is_e, lt, ninf)
    v1 = jnp.max(el, axis=0, keepdims=True)
    i1 = jnp.min(jnp.where(el == v1, r, 1e9), axis=0, keepdims=True)
    el2 = jnp.where(r == i1, ninf, el)
    v2 = jnp.max(el2, axis=0, keepdims=True)
    i2 = jnp.min(jnp.where(el2 == v2, r, 1e9), axis=0, keepdims=True)
    t = jnp.exp(v2 - v1)
    w1 = gw / (1.0 + t)
    w2 = gw * t / (1.0 + t)
    first = r == i1
    second = r == i2
    gates = jnp.where(first, w1, 0.0) + jnp.where(second, w2, 0.0)
    member = jnp.where(first, 1.0, jnp.where(second, 1.0, 0.0))
    return gates, member


def _moe_kernel(x_ref, mod_ref, wr_ref, wg_ref, wu_ref, wd_ref, lnw_ref, lnb_ref, out_ref,
                h_ref, rank_ref, gate_ref, sel_ref, xs_ref, ys_ref):
    s = pl.program_id(1)
    n_tiles = MOE_PART // MOE_TILE

    @pl.when(s == 0)
    def _():
        m = mod_ref[...]
        h = x_ref[...] * (1.0 + m[4:5]) + m[3:4]
        h_hi = h.astype(BF16)
        h_ref[...] = h_hi
        h_lo = (h - h_hi.astype(F32)).astype(BF16)
        w = wr_ref[...]
        w_hi = w.astype(BF16)
        w_lo = (w - w_hi.astype(F32)).astype(BF16)
        lt = _dot_nt(w_hi, h_hi) + (_dot_nt(w_hi, h_lo) + _dot_nt(w_lo, h_hi))
        gates, member = _route_transposed(lt)
        before = jnp.where(lax.broadcasted_iota(jnp.int32, (MOE_TILE, MOE_TILE), 0)
                           < lax.broadcasted_iota(jnp.int32, (MOE_TILE, MOE_TILE), 1), 1.0, 0.0).astype(BF16)
        rank_ref[...] = jnp.full(rank_ref.shape, -1.0, F32)
        gate_ref[...] = jnp.zeros(gate_ref.shape, F32)
        for t in range(n_tiles):
            c0 = t * MOE_TILE
            mem_t = member[0:N_EXPERTS, c0:c0 + MOE_TILE]
            cnt = _dot(mem_t.astype(BF16), before)
            rank = jnp.where(mem_t > 0.5, cnt, -1.0)
            for st in range(MOE_STEPS):
                rank_ref[st, 0:MOE_EB, c0:c0 + MOE_TILE] = rank[st * MOE_EB:(st + 1) * MOE_EB]
                gate_ref[st, 0:MOE_EB, c0:c0 + MOE_TILE] = gates[st * MOE_EB:(st + 1) * MOE_EB, c0:c0 + MOE_TILE]
        out_ref[...] = jnp.zeros(out_ref.shape, F32)

    ranks = rank_ref[s, 0:MOE_EB, :]
    gts = gate_ref[s, 0:MOE_EB, :]
    n_chunks = ((jnp.max(ranks) + 0.5) * (1.0 / MOE_CH)).astype(jnp.int32) + 1

    def chunk_body(k, carry):
        slot = (lax.broadcasted_iota(jnp.int32, (MOE_CH, MOE_TILE), 0) + k * MOE_CH).astype(F32)
        row_gate = [[] for _ in range(MOE_EB)]
        for t in range(n_tiles):
            c0 = t * MOE_TILE
            onehots = []
            for i in range(MOE_EB):
                hit = ranks[i:i + 1, c0:c0 + MOE_TILE] == slot
                onehots.append(jnp.where(hit, 1.0, 0.0).astype(BF16))
                row_gate[i].append(jnp.sum(jnp.where(hit, gts[i:i + 1, c0:c0 + MOE_TILE], 0.0),
                                           axis=1, keepdims=True))
            sel = jnp.concatenate(onehots, axis=0)
            sel_ref[t] = sel
            xs = _dot(sel, h_ref[c0:c0 + MOE_TILE, :]).astype(BF16)
            for i in range(MOE_EB):
                xs_ref[i, t * MOE_CH:(t + 1) * MOE_CH, :] = xs[i * MOE_CH:(i + 1) * MOE_CH]
        for i in range(MOE_EB):
            xi = xs_ref[i]
            a = _dot(xi, wg_ref[i])
            u = _dot(xi, wu_ref[i])
            hm = (_silu(a) * u * jnp.concatenate(row_gate[i], axis=0)).astype(BF16)
            ys_ref[i] = _dot(hm, wd_ref[i]).astype(BF16)
        for t in range(n_tiles):
            c0 = t * MOE_TILE
            y = jnp.concatenate([ys_ref[i, t * MOE_CH:(t + 1) * MOE_CH, :] for i in range(MOE_EB)], axis=0)
            out_ref[c0:c0 + MOE_TILE, :] += _dot_tn(sel_ref[t], y)
        return carry

    lax.fori_loop(0, n_chunks, chunk_body, 0)

    @pl.when(s == MOE_STEPS - 1)
    def _():
        m = mod_ref[...]
        z = ALPHA * x_ref[...] + m[5:6] * out_ref[...]
        out_ref[...] = _layer_norm_rows(z, lnw_ref[...], lnb_ref[...])


def _moe(x2d, mod_l, mod_row_fn, wr_t, wg_bf, wu_bf, wd_bf, lnw, lnb, layer, name):
    n = x2d.shape[0]
    row = lambda p, s: (p, 0)
    wspec = lambda shape: pl.BlockSpec((None, MOE_EB) + shape, lambda p, s: (layer, s, 0, 0))
    return pl.pallas_call(
        _moe_kernel,
        grid=(n // MOE_PART, MOE_STEPS),
        in_specs=[
            pl.BlockSpec((MOE_PART, D_MODEL), row, pipeline_mode=pl.Buffered(1)),
            pl.BlockSpec((None, None, 6, D_MODEL), lambda p, s: (layer, mod_row_fn(p * MOE_PART), 0, 0)),
            pl.BlockSpec((None, MOE_ROUTE_ROWS, D_MODEL), lambda p, s: (layer, 0, 0)),
            wspec((D_MODEL, EXPERT_FF)),
            wspec((D_MODEL, EXPERT_FF)),
            wspec((EXPERT_FF, D_MODEL)),
            pl.BlockSpec((None, 1, D_MODEL), lambda *_: (layer, 0, 0)),
            pl.BlockSpec((None, 1, D_MODEL), lambda *_: (layer, 0, 0)),
        ],
        out_specs=pl.BlockSpec((MOE_PART, D_MODEL), row),
        out_shape=jax.ShapeDtypeStruct((n, D_MODEL), F32),
        scratch_shapes=[
            pltpu.VMEM((MOE_PART, D_MODEL), BF16),
            pltpu.VMEM((MOE_STEPS, 8, MOE_PART), F32),
            pltpu.VMEM((MOE_STEPS, 8, MOE_PART), F32),
            pltpu.VMEM((MOE_PART // MOE_TILE, MOE_EB * MOE_CH, MOE_TILE), BF16),
            pltpu.VMEM((MOE_EB, MOE_PART // MOE_TILE * MOE_CH, D_MODEL), BF16),
            pltpu.VMEM((MOE_EB, MOE_PART // MOE_TILE * MOE_CH, D_MODEL), BF16),
        ],
        compiler_params=_cparams(("arbitrary", "arbitrary")),
        name=name,
    )(x2d, mod_l, wr_t, wg_bf, wu_bf, wd_bf, lnw, lnb)


def _na_bias_tables(rel_bias):
    pad_c = GRID_W - NA_WIN_COLS
    padded = jnp.pad(rel_bias.astype(F32) * LOG2E, ((0, 0), (0, 0), (0, 0), (pad_c, pad_c)), mode="edge")
    return jnp.pad(padded, ((0, 0), (0, 0), (0, 1), (0, 1)))


def _rope_tables():
    n = DIFF_D // 4
    lane = np.arange(DIFF_W)
    d = lane % DIFF_D
    use_col = d >= DIFF_D // 2
    e = d % (DIFF_D // 2)
    f = e % n
    first = e < n
    t = np.arange(DEC_SEQ)
    pos = np.where(use_col[None, :], (t % GRID_W)[:, None], (t // GRID_W)[:, None]).astype(np.float64)
    freqs = ROPE_BASE ** (-np.arange(n, dtype=np.float64) / n)
    ang = pos * freqs[f][None, :]
    sign = np.where(first, -1.0, 1.0)
    return jnp.asarray(np.cos(ang), F32), jnp.asarray(np.sin(ang) * sign[None, :], F32)


def kernel(x_prompt, x_sample, c, cache_na_k, cache_na_v, state_ret, cache_diff_k, cache_diff_v, c_ctx,
           w_mod, b_mod, w_in, na_rel_bias, ret_decay, ret_gn_w, ret_gn_b, diff_lambda, diff_norm_w,
           w_out, ln1_w, ln1_b, router_group, router_expert, exp_w_gate, exp_w_up, exp_w_down,
           ln2_w, ln2_b):
    n_ctx = BATCH * SEQ
    n_lat = DEC_BATCH * DEC_SEQ
    x_ctx = x_prompt.reshape(n_ctx, D_MODEL)
    x_lat = x_sample.reshape(n_lat, D_MODEL)

    cond_t = jnp.concatenate([c_ctx[:, None], c.T, jnp.zeros((D_MODEL, 8 - 1 - DEC_BATCH), F32)], axis=1)
    mod = _modulation(cond_t, w_mod, b_mod)

    cache_na_k = cache_na_k.reshape(DEC_BATCH, DEPTH, PAST_LEN, NA_W)
    cache_na_v = cache_na_v.reshape(DEC_BATCH, DEPTH, PAST_LEN, NA_W)
    cache_diff_k = cache_diff_k.reshape(DEC_BATCH, DEPTH, PAST_LEN, DIFF_W)
    cache_diff_v = cache_diff_v.reshape(DEC_BATCH, DEPTH, PAST_LEN, DIFF_W)
    cos_tab, sin_tab = _rope_tables()
    st0 = state_ret.astype(F32).reshape(DEC_BATCH, DEPTH, 2, RET_HEADS // 2, 2, HEAD_DIM, 1, HEAD_DIM)
    st0 = (st0 * jnp.eye(2, dtype=F32)[:, None, :, None]).reshape(
        DEC_BATCH, DEPTH, 2, RET_HEADS // 2, PAIR, PAIR)

    ctx_row = lambda r: 0
    lat_row = lambda r: 1 + r // DEC_SEQ

    wg_bf = exp_w_gate.astype(BF16)
    wu_bf = exp_w_up.astype(BF16)
    wd_bf = exp_w_down.astype(BF16)

    w_in_bf = w_in.astype(BF16)
    w_out_bf = w_out.astype(BF16)
    wr_t = jnp.concatenate([jnp.swapaxes(router_expert, 1, 2), jnp.swapaxes(router_group, 1, 2),
                            jnp.zeros((DEPTH, MOE_ROUTE_ROWS - N_EXPERTS - N_GROUPS, D_MODEL), F32)], axis=1)
    log_g_all = jax.nn.log_sigmoid(ret_decay.astype(F32)) * LOG2E
    lp = diff_lambda.astype(F32)
    lam_dyn = jnp.exp(jnp.sum(lp[:, 0] * lp[:, 1], axis=-1)) - jnp.exp(jnp.sum(lp[:, 2] * lp[:, 3], axis=-1))

    bias_tab = _na_bias_tables(na_rel_bias)
    gnw = ret_gn_w.reshape(DEPTH, 1, RET_W)
    gnb = ret_gn_b.reshape(DEPTH, 1, RET_W)
    dnw = diff_norm_w.reshape(DEPTH, 1, DIFF_W)
    l1w = ln1_w.reshape(DEPTH, 1, D_MODEL)
    l1b = ln1_b.reshape(DEPTH, 1, D_MODEL)
    l2w = ln2_w.reshape(DEPTH, 1, D_MODEL)
    l2b = ln2_b.reshape(DEPTH, 1, D_MODEL)

    caches = None
    states = None
    for l in range(DEPTH):
        lam_init = 0.8 - 0.6 * math.exp(-0.3 * l)
        lam = (lam_dyn[l] + lam_init).reshape(1)
        log_g = log_g_all[l]

        q_c, kv_c, *caches = _project(x_ctx, mod, w_in_bf, ctx_row, l, emit_caches=True, caches=caches)
        x1_c, states = _ctx_mix(q_c, kv_c, x_ctx, mod, w_out_bf, log_g, lam, gnw, gnb, dnw, l1w, l1b,
                                states, l, lam_init)
        x_ctx = _moe(x1_c, mod, ctx_row, wr_t, wg_bf, wu_bf, wd_bf, l2w, l2b, l, "moe_ctx")

        q_l, kv_l = _project(x_lat, mod, w_in_bf, lat_row, l)
        x1_l = _lat_mix(q_l, kv_l, x_lat, mod, w_out_bf, log_g, lam, gnw, gnb, dnw, l1w, l1b,
                        cache_na_k, cache_na_v, st0, cache_diff_k, cache_diff_v,
                        bias_tab, cos_tab, sin_tab, l, lam_init)
        x_lat = _moe(x1_l, mod, lat_row, wr_t, wg_bf, wu_bf, wd_bf, l2w, l2b, l, "moe_lat")

    new_na_k, new_na_v, new_diff_k, new_diff_v = caches
    return (x_ctx.reshape(BATCH, SEQ, D_MODEL), x_lat.reshape(DEC_BATCH, DEC_SEQ, D_MODEL),
            new_na_k.reshape(BATCH, DEPTH, SEQ, NA_HEADS, HEAD_DIM),
            new_na_v.reshape(BATCH, DEPTH, SEQ, NA_HEADS, HEAD_DIM),
            states,
            new_diff_k.reshape(BATCH, DEPTH, SEQ, DIFF_HEADS, 2 * DIFF_D),
            new_diff_v.reshape(BATCH, DEPTH, SEQ, DIFF_HEADS, DIFF_DV))
```

```python
import functools
import math

import numpy as np
import jax
import jax.numpy as jnp
from jax import lax
from jax.experimental import pallas as pl
from jax.experimental.pallas import tpu as pltpu

D_MODEL = 1024
BATCH = 32
SEQ = 256
DEPTH = 2
DEC_BATCH = 2
DEC_SEQ = 2048
PAST_LEN = 512
GRID_W = 64
HEAD_DIM = 64
NA_HEADS = 6
NA_WIN_ROWS = 8
NA_WIN_COLS = 16
RET_HEADS = 6
DIFF_HEADS = 4
DIFF_D = 32
DIFF_DV = 64
NA_W = NA_HEADS * HEAD_DIM
RET_W = RET_HEADS * HEAD_DIM
DIFF_W = DIFF_HEADS * DIFF_DV
MIX_W = NA_W + RET_W + DIFF_W
IN_COLS = 3 * NA_W + 4 * RET_W + 3 * DIFF_W
N_GROUPS = 4
EXPERTS_PER_GROUP = 8
N_EXPERTS = N_GROUPS * EXPERTS_PER_GROUP
EXPERT_FF = 256
ROPE_BASE = 10000.0
LN_EPS = 1e-5
NEG_INF = -1e30
ALPHA = (2.0 * DEPTH) ** 0.25
LOG2E = math.log2(math.e)

F32 = jnp.float32
BF16 = jnp.bfloat16

Q_NA, Q_RET, Q_GATE, Q_DIFF = 0, NA_W, NA_W + RET_W, NA_W + 2 * RET_W
Q_COLS = NA_W + 2 * RET_W + DIFF_W
KV_NAK, KV_NAV = 0, NA_W
KV_RETK, KV_RETV = 2 * NA_W, 2 * NA_W + RET_W
KV_DK, KV_DV = 2 * NA_W + 2 * RET_W, 2 * NA_W + 2 * RET_W + DIFF_W
KV_COLS = 2 * NA_W + 2 * RET_W + 2 * DIFF_W
MIX_RET, MIX_DIFF = NA_W, NA_W + RET_W

NA_Q_TILE = 256
NA_KEY_ROWS = 12
NA_KEYS = NA_KEY_ROWS * GRID_W
CTX_PROBLEMS = NA_HEADS + 2 * DIFF_HEADS + RET_HEADS
GRID_ROWS = DEC_SEQ // GRID_W
PAIR = 2 * HEAD_DIM
NA_DR_PAD = 8
NA_DR_ROWS = 32

VMEM_LIMIT = 60 * 1024 * 1024


def _cparams(sem):
    return pltpu.CompilerParams(dimension_semantics=sem, vmem_limit_bytes=VMEM_LIMIT)


def _dot(a, b):
    return jnp.dot(a, b, preferred_element_type=F32)


def _dot_nt(a, b):
    return lax.dot_general(a, b, (((1,), (1,)), ((), ())), preferred_element_type=F32)


def _dot_tn(a, b):
    return lax.dot_general(a, b, (((0,), (0,)), ((), ())), preferred_element_type=F32)


def _silu(x):
    return x / (1.0 + jnp.exp(-x))


def _layer_norm_rows(z, w, b):
    mu = jnp.mean(z, axis=-1, keepdims=True)
    zc = z - mu
    var = jnp.mean(zc * zc, axis=-1, keepdims=True)
    return zc * lax.rsqrt(var + LN_EPS) * w + b


def _mod_kernel(cond_ref, w_ref, b_ref, o_ref):
    s = _silu(cond_ref[...])
    w = w_ref[...]
    n_cond = 1 + DEC_BATCH
    rows = [jnp.sum(s[:, r:r + 1] * w, axis=0, keepdims=True) for r in range(n_cond)]
    rows.append(jnp.zeros((8 - n_cond, w.shape[1]), F32))
    o_ref[...] = jnp.concatenate(rows, axis=0) + b_ref[...]


def _modulation(cond_t, w_mod, b_mod):
    nj = 6
    out = pl.pallas_call(
        _mod_kernel,
        grid=(DEPTH, nj),
        in_specs=[
            pl.BlockSpec((D_MODEL, 8), lambda l, j: (0, 0)),
            pl.BlockSpec((None, D_MODEL, D_MODEL), lambda l, j: (l, 0, j)),
            pl.BlockSpec((None, 1, D_MODEL), lambda l, j: (l, 0, j)),
        ],
        out_specs=pl.BlockSpec((None, 8, D_MODEL), lambda l, j: (l, 0, j)),
        out_shape=jax.ShapeDtypeStruct((DEPTH, 8, 6 * D_MODEL), F32),
        compiler_params=_cparams(("arbitrary", "arbitrary")),
        name="modulation",
    )(cond_t, w_mod, b_mod.reshape(DEPTH, 1, 6 * D_MODEL))
    return out.reshape(DEPTH, 8, 6, D_MODEL)


def _proj_kernel(x_ref, mod_ref, w_ref, *refs):
    if len(refs) == 10:
        refs = refs[4:]
    q_ref, kv_ref = refs[:2]
    cache_refs = refs[2:]
    m = mod_ref[...]
    h = x_ref[...] * (1.0 + m[1:2]) + m[0:1]
    p = _dot(h.astype(BF16), w_ref[...])
    o = 0
    na_q = p[:, o:o + NA_W] * (HEAD_DIM ** -0.5 * LOG2E); o += NA_W
    na_k = p[:, o:o + NA_W]; o += NA_W
    na_v = p[:, o:o + NA_W]; o += NA_W
    ret_q = p[:, o:o + RET_W]; o += RET_W
    ret_k = p[:, o:o + RET_W] * (HEAD_DIM ** -0.5); o += RET_W
    ret_v = p[:, o:o + RET_W]; o += RET_W
    ret_g = p[:, o:o + RET_W]; o += RET_W
    dq = p[:, o:o + DIFF_W] * (DIFF_D ** -0.5 * LOG2E); o += DIFF_W
    dk = p[:, o:o + DIFF_W]; o += DIFF_W
    dv = p[:, o:o + DIFF_W]
    q_ref[:, Q_NA:Q_NA + NA_W] = na_q.astype(BF16)
    q_ref[:, Q_RET:Q_RET + RET_W] = ret_q.astype(BF16)
    q_ref[:, Q_GATE:Q_GATE + RET_W] = ret_g.astype(BF16)
    q_ref[:, Q_DIFF:Q_DIFF + DIFF_W] = dq.astype(BF16)
    kv_ref[:, KV_NAK:KV_NAK + NA_W] = na_k.astype(BF16)
    kv_ref[:, KV_NAV:KV_NAV + NA_W] = na_v.astype(BF16)
    kv_ref[:, KV_RETK:KV_RETK + RET_W] = ret_k.astype(BF16)
    kv_ref[:, KV_RETV:KV_RETV + RET_W] = ret_v.astype(BF16)
    kv_ref[:, KV_DK:KV_DK + DIFF_W] = dk.astype(BF16)
    kv_ref[:, KV_DV:KV_DV + DIFF_W] = dv.astype(BF16)
    nb = x_ref.shape[0] // SEQ
    for ref, val in zip(cache_refs, (na_k, na_v, dk, dv)):
        val = val.reshape(nb, SEQ, val.shape[-1])
        if len(ref.shape) == 4:
            ref[:, 0] = val
            ref[:, 1:] = jnp.zeros((nb, DEPTH - 1) + val.shape[1:], F32)
        else:
            ref[...] = val


def _project(x2d, mod, w_in_bf, mod_row_fn, layer, emit_caches=False, caches=None, tm=512):
    n = x2d.shape[0]
    row = lambda i: (i, 0)
    out_shape = [jax.ShapeDtypeStruct((n, Q_COLS), BF16), jax.ShapeDtypeStruct((n, KV_COLS), BF16)]
    out_specs = [pl.BlockSpec((tm, Q_COLS), row), pl.BlockSpec((tm, KV_COLS), row)]
    in_specs = [
        pl.BlockSpec((tm, D_MODEL), row),
        pl.BlockSpec((None, None, 6, D_MODEL), lambda i: (layer, mod_row_fn(i * tm), 0, 0)),
        pl.BlockSpec((None, D_MODEL, IN_COLS), lambda i: (layer, 0, 0)),
    ]
    args = [x2d, mod, w_in_bf]
    aliases = {}
    if emit_caches:
        for k, w in enumerate((NA_W, NA_W, DIFF_W, DIFF_W)):
            out_shape.append(jax.ShapeDtypeStruct((BATCH, DEPTH, SEQ, w), F32))
            if caches is None:
                out_specs.append(pl.BlockSpec((tm // SEQ, DEPTH, SEQ, w), lambda i: (i, 0, 0, 0)))
            else:
                in_specs.append(pl.BlockSpec(memory_space=pl.ANY))
                args.append(caches[k])
                aliases[3 + k] = 2 + k
                out_specs.append(pl.BlockSpec((tm // SEQ, None, SEQ, w), lambda i: (i, layer, 0, 0)))
    return pl.pallas_call(
        _proj_kernel,
        grid=(n // tm,),
        in_specs=in_specs,
        out_specs=out_specs,
        out_shape=out_shape,
        input_output_aliases=aliases,
        compiler_params=_cparams(("arbitrary",)),
        name="proj_ctx" if emit_caches else "proj_lat",
    )(*args)


def _lane_group(rows, width):
    return lax.broadcasted_iota(jnp.int32, (rows, PAIR), 1) // width


def _keep_group(x, groups, g):
    return jnp.where(groups == g, x, jnp.zeros_like(x))


def _with_ones(v_pair):
    return jnp.concatenate([v_pair, jnp.ones(v_pair.shape, v_pair.dtype)], axis=1)


def _normalise(r):
    return r[:, :PAIR] * (1.0 / r[:, PAIR:PAIR + 1])


def _attend(qm, k_list, vx_list, bias=None):
    scores = [_dot_nt(qm, k) for k in k_list]
    if bias is not None:
        scores[0] = scores[0] + bias
    m = None
    for s in scores:
        mi = jnp.max(s, axis=-1, keepdims=True)
        m = mi if m is None else jnp.maximum(m, mi)
    r = None
    for s, vx in zip(scores, vx_list):
        ri = _dot(jnp.exp2(s - m).astype(BF16), vx)
        r = ri if r is None else r + ri
    return _normalise(r)


def _group_mean_matrix():
    r = lax.broadcasted_iota(jnp.int32, (PAIR, PAIR), 0) // HEAD_DIM
    c = lax.broadcasted_iota(jnp.int32, (PAIR, PAIR), 1) // HEAD_DIM
    return jnp.where(r == c, 1.0 / HEAD_DIM, 0.0).astype(BF16)


def _group_mean(x, gm):
    hi = x.astype(BF16)
    lo = (x - hi.astype(F32)).astype(BF16)
    return _dot(hi, gm) + _dot(lo, gm)


def _head_layer_norm(o, gm):
    oc = o - _group_mean(o, gm)
    return oc * lax.rsqrt(_group_mean(oc * oc, gm) + LN_EPS)


def _head_rms_norm(o, gm):
    return o * lax.rsqrt(_group_mean(o * o, gm) + LN_EPS)


def _pair_scalar(ref, row, j, groups):
    return jnp.where(groups == 0, ref[row, 2 * j], ref[row, 2 * j + 1])


def _decay_table(lg_ref, dm_ref):
    T = SEQ
    i = lax.broadcasted_iota(jnp.int32, (T, T), 0).astype(F32)
    j = lax.broadcasted_iota(jnp.int32, (T, T), 1).astype(F32)
    d = i - j
    for h in range(RET_HEADS):
        dm_ref[h * T:(h + 1) * T, :] = jnp.exp2(jnp.where(d >= 0, d * lg_ref[0, h], (-d) * lg_ref[1, h]))


def _out_proj_post_norm(mix_ref, wout_ref, x_ref, mod_ref, lnw_ref, lnb_ref, x1_ref):
    y = _dot(mix_ref[...].astype(BF16), wout_ref[...])
    m = mod_ref[...]
    z = ALPHA * x_ref[...] + m[2:3] * y
    x1_ref[...] = _layer_norm_rows(z, lnw_ref[...], lnb_ref[...])


def _ctx_mix_kernel(lg_ref, lam_ref, q_ref, kv_ref, x_ref, mod_ref, wout_ref, gnw_ref, gnb_ref,
                    dnw_ref, lnw_ref, lnb_ref, *rest, one_minus_lam_init):
    if len(rest) == 8:
        rest = rest[1:]
    x1_ref, st_ref, mix_ref, dm_ref, s_ref, e_ref, o_ref = rest
    if len(st_ref.shape) == 5:
        st_ref[1:] = jnp.zeros((DEPTH - 1,) + st_ref.shape[1:], F32)
        st_ref = st_ref.at[0]
    T = SEQ
    n_soft = NA_HEADS + 2 * DIFF_HEADS

    @pl.when(pl.program_id(0) == 0)
    def _():
        _decay_table(lg_ref, dm_ref)

    half = _lane_group(T, HEAD_DIM)
    quarter = _lane_group(T, DIFF_D)
    n_na, n_diff, n_ret = NA_HEADS // 2, DIFF_HEADS // 2, RET_HEADS // 2

    for j in range(n_na):
        q = q_ref[:, Q_NA + j * PAIR:Q_NA + (j + 1) * PAIR]
        k = kv_ref[:, KV_NAK + j * PAIR:KV_NAK + (j + 1) * PAIR]
        for g in range(2):
            r0 = (2 * j + g) * T
            s_ref[r0:r0 + T, :] = _dot_nt(_keep_group(q, half, g), k)
    for j in range(n_diff):
        q = q_ref[:, Q_DIFF + j * PAIR:Q_DIFF + (j + 1) * PAIR]
        k = kv_ref[:, KV_DK + j * PAIR:KV_DK + (j + 1) * PAIR]
        for g in range(4):
            r0 = (NA_HEADS + 4 * j + g) * T
            s_ref[r0:r0 + T, :] = _dot_nt(_keep_group(q, quarter, g), k)
    for j in range(n_ret):
        q = q_ref[:, Q_RET + j * PAIR:Q_RET + (j + 1) * PAIR]
        k = kv_ref[:, KV_RETK + j * PAIR:KV_RETK + (j + 1) * PAIR]
        for g in range(2):
            r0 = (n_soft + 2 * j + g) * T
            s_ref[r0:r0 + T, :] = _dot_nt(_keep_group(q, half, g), k)

    s = s_ref[0:n_soft * T, :]
    e_ref[0:n_soft * T, :] = jnp.exp2(s - jnp.max(s, axis=-1, keepdims=True)).astype(BF16)
    e_ref[n_soft * T:, :] = (s_ref[n_soft * T:, :] * dm_ref[...]).astype(BF16)

    def pv(n, vx):
        return _normalise(_dot(e_ref[n * T:(n + 1) * T, :], vx))

    for j in range(n_na):
        vx = _with_ones(kv_ref[:, KV_NAV + j * PAIR:KV_NAV + (j + 1) * PAIR])
        mix_ref[:, j * PAIR:(j + 1) * PAIR] = jnp.where(half == 0, pv(2 * j, vx), pv(2 * j + 1, vx))
    lam = lam_ref[0]
    for j in range(n_diff):
        vx = _with_ones(kv_ref[:, KV_DV + j * PAIR:KV_DV + (j + 1) * PAIR])
        n0 = NA_HEADS + 4 * j
        head_a = pv(n0, vx) - lam * pv(n0 + 1, vx)
        head_b = pv(n0 + 2, vx) - lam * pv(n0 + 3, vx)
        o_ref[(n_ret + j) * T:(n_ret + j + 1) * T, :] = jnp.where(half == 0, head_a, head_b)
    jj = lax.broadcasted_iota(jnp.int32, (T, PAIR), 0).astype(F32)
    for j in range(n_ret):
        kf = kv_ref[:, KV_RETK + j * PAIR:KV_RETK + (j + 1) * PAIR].astype(F32)
        v = kv_ref[:, KV_RETV + j * PAIR:KV_RETV + (j + 1) * PAIR]
        n0 = n_soft + 2 * j
        o_ref[j * T:(j + 1) * T, :] = jnp.where(half == 0, _dot(e_ref[n0 * T:(n0 + 1) * T, :], v),
                                                _dot(e_ref[(n0 + 1) * T:(n0 + 2) * T, :], v))
        k_fwd = (kf * jnp.exp2((T - 1.0 - jj) * _pair_scalar(lg_ref, 0, j, half))).astype(BF16)
        k_bwd = (kf * jnp.exp2(jj * _pair_scalar(lg_ref, 1, j, half))).astype(BF16)
        for d, kd in enumerate((k_fwd, k_bwd)):
            st = _dot_tn(kd, v)
            st_ref[d, 2 * j] = st[0:HEAD_DIM, 0:HEAD_DIM]
            st_ref[d, 2 * j + 1] = st[HEAD_DIM:PAIR, HEAD_DIM:PAIR]

    gm = _group_mean_matrix()
    rn = _head_layer_norm(o_ref[0:n_ret * T, :], gm)
    dn = _head_rms_norm(o_ref[n_ret * T:, :], gm)
    for j in range(n_ret):
        c = j * PAIR
        g = q_ref[:, Q_GATE + c:Q_GATE + c + PAIR].astype(F32)
        mix_ref[:, MIX_RET + c:MIX_RET + c + PAIR] = (
            (rn[j * T:(j + 1) * T] * gnw_ref[:, c:c + PAIR] + gnb_ref[:, c:c + PAIR]) * _silu(g))
    for j in range(n_diff):
        c = j * PAIR
        mix_ref[:, MIX_DIFF + c:MIX_DIFF + c + PAIR] = (
            dn[j * T:(j + 1) * T] * dnw_ref[:, c:c + PAIR] * one_minus_lam_init)

    _out_proj_post_norm(mix_ref, wout_ref, x_ref, mod_ref, lnw_ref, lnb_ref, x1_ref)


def _ctx_mix(q_arr, kv_arr, x2d, mod, w_out_bf, log_g, lam, gnw, gnb, dnw, lnw, lnb, states, layer,
             lam_init):
    row = lambda b: (b, 0)
    smem = pl.BlockSpec(memory_space=pltpu.SMEM)
    st_shape = (BATCH, DEPTH, 2, RET_HEADS, HEAD_DIM, HEAD_DIM)
    in_specs = [
        smem, smem,
        pl.BlockSpec((SEQ, Q_COLS), row),
        pl.BlockSpec((SEQ, KV_COLS), row),
        pl.BlockSpec((SEQ, D_MODEL), row),
        pl.BlockSpec((None, None, 6, D_MODEL), lambda b: (layer, 0, 0, 0)),
        pl.BlockSpec((None, MIX_W, D_MODEL), lambda b: (layer, 0, 0)),
        pl.BlockSpec((None, 1, RET_W), lambda *_: (layer, 0, 0)),
        pl.BlockSpec((None, 1, RET_W), lambda *_: (layer, 0, 0)),
        pl.BlockSpec((None, 1, DIFF_W), lambda *_: (layer, 0, 0)),
        pl.BlockSpec((None, 1, D_MODEL), lambda *_: (layer, 0, 0)),
        pl.BlockSpec((None, 1, D_MODEL), lambda *_: (layer, 0, 0)),
    ]
    args = [log_g, lam, q_arr, kv_arr, x2d, mod, w_out_bf, gnw, gnb, dnw, lnw, lnb]
    if states is None:
        st_spec = pl.BlockSpec((None,) + st_shape[1:], lambda b: (b, 0, 0, 0, 0, 0))
        aliases = {}
    else:
        in_specs.append(pl.BlockSpec(memory_space=pl.ANY))
        args.append(states)
        st_spec = pl.BlockSpec((None, None) + st_shape[2:], lambda b: (b, layer, 0, 0, 0, 0))
        aliases = {12: 1}
    return pl.pallas_call(
        functools.partial(_ctx_mix_kernel, one_minus_lam_init=1.0 - lam_init),
        grid=(BATCH,),
        in_specs=in_specs,
        out_specs=[pl.BlockSpec((SEQ, D_MODEL), row), st_spec],
        out_shape=[
            jax.ShapeDtypeStruct((BATCH * SEQ, D_MODEL), F32),
            jax.ShapeDtypeStruct(st_shape, F32),
        ],
        input_output_aliases=aliases,
        scratch_shapes=[
            pltpu.VMEM((SEQ, MIX_W), F32),
            pltpu.VMEM((RET_HEADS * SEQ, SEQ), F32),
            pltpu.VMEM((CTX_PROBLEMS * SEQ, SEQ), F32),
            pltpu.VMEM((CTX_PROBLEMS * SEQ, SEQ), BF16),
            pltpu.VMEM(((RET_HEADS + DIFF_HEADS) // 2 * SEQ, PAIR), F32),
        ],
        compiler_params=_cparams(("arbitrary",)),
        name="ctx_mix",
    )(*args)


def _rope(x, cos, sin_signed):
    n, w = x.shape
    lane = lax.broadcasted_iota(jnp.int32, (n, w), 1)
    first = (lane % 16) < 8
    partner = jnp.where(first, pltpu.roll(x, w - 8, 1), pltpu.roll(x, 8, 1))
    return x * cos + partner * sin_signed


def _na_bias_blocks(rb_ref, bias_ref):
    n_dr = 2 * NA_WIN_ROWS - 1
    lane = lax.broadcasted_iota(jnp.int32, (GRID_W, PAIR), 1)
    qc = lax.broadcasted_iota(jnp.int32, (GRID_W, PAIR), 0)
    kc = lane % GRID_W
    start = jnp.clip(qc - NA_WIN_COLS // 2, 0, GRID_W - NA_WIN_COLS)
    col_ok = jnp.where(kc >= start, jnp.where(kc < start + NA_WIN_COLS, 1.0, 0.0), 0.0) > 0.5
    zero = jnp.zeros((GRID_W, PAIR), F32)
    for h in range(NA_HEADS):
        left, right = [], []
        for dr in range(n_dr):
            row = jnp.broadcast_to(rb_ref[h, dr:dr + 1, :], (GRID_W, PAIR))
            left.append(pltpu.roll(row, GRID_W + 1, 1, stride=1, stride_axis=0))
            right.append(pltpu.roll(row, 1, 1, stride=1, stride_axis=0))
        for i in range(NA_DR_ROWS):
            d0 = i - NA_DR_PAD
            a = left[d0] if 0 <= d0 < n_dr else zero
            b = right[d0 + 1] if 0 <= d0 + 1 < n_dr else zero
            bias_ref[h, i] = jnp.where(col_ok, jnp.where(lane < GRID_W, a, b), NEG_INF)


def _lat_mix_kernel(lg_ref, lam_ref, q_ref, kv_ref, x_ref, mod_ref, wout_ref, gnw_ref, gnb_ref,
                    dnw_ref, lnw_ref, lnb_ref, cnak_ref, cnav_ref, st0_ref, cdk_ref, cdv_ref,
                    rb_ref, cos_ref, sin_ref, x1_ref, mix_ref, kr_ref, dm_ref, sf_ref, sb_ref,
                    kc_na_ref, vxc_na_ref, vx_na_ref, kc_d_ref, vxc_d_ref, vx_d_ref, bias_ref, *,
                    one_minus_lam_init):
    TQ = NA_Q_TILE
    T = DEC_SEQ
    n_qt = T // TQ
    n_na, n_diff, n_ret = NA_HEADS // 2, DIFF_HEADS // 2, RET_HEADS // 2
    qt = pl.program_id(1)
    q0 = pl.multiple_of(qt * TQ, TQ)
    half = _lane_group(TQ, HEAD_DIM)
    quarter = _lane_group(TQ, DIFF_D)

    @pl.when(qt == 0)
    def _():
        kr = _rope(kv_ref[:, KV_DK:KV_DK + DIFF_W].astype(F32), cos_ref[...], sin_ref[...])
        kr_ref[...] = kr.astype(BF16)
        kc_na_ref[...] = cnak_ref[...].astype(BF16)
        kc_d_ref[...] = cdk_ref[...].astype(BF16)
        for j in range(n_na):
            vxc_na_ref[j] = _with_ones(cnav_ref[:, j * PAIR:(j + 1) * PAIR].astype(BF16))
            vx_na_ref[j] = _with_ones(kv_ref[:, KV_NAV + j * PAIR:KV_NAV + (j + 1) * PAIR])
        for j in range(n_diff):
            vxc_d_ref[j] = _with_ones(cdv_ref[:, j * PAIR:(j + 1) * PAIR].astype(BF16))
            vx_d_ref[j] = _with_ones(kv_ref[:, KV_DV + j * PAIR:KV_DV + (j + 1) * PAIR])
        _decay_table(lg_ref, dm_ref)
        _na_bias_blocks(rb_ref, bias_ref)
        jl = lax.broadcasted_iota(jnp.int32, (TQ, PAIR), 0).astype(F32)
        rows = lax.broadcasted_iota(jnp.int32, (PAIR, PAIR), 0) // HEAD_DIM
        cols = lax.broadcasted_iota(jnp.int32, (PAIR, PAIR), 1) // HEAD_DIM
        for j in range(n_ret):
            lf = _pair_scalar(lg_ref, 0, j, half)
            lb = _pair_scalar(lg_ref, 1, j, half)
            dec_f = jnp.exp2((TQ - 1.0 - jl) * lf)
            dec_b = jnp.exp2(jl * lb)
            tile_f = jnp.exp2(float(TQ) * _pair_scalar(lg_ref, 0, j, rows))
            tile_b = jnp.exp2(float(TQ) * _pair_scalar(lg_ref, 1, j, rows))
            loc_f, loc_b = [], []
            for t in range(n_qt):
                kf = kv_ref[t * TQ:(t + 1) * TQ, KV_RETK + j * PAIR:KV_RETK + (j + 1) * PAIR].astype(F32)
                v = kv_ref[t * TQ:(t + 1) * TQ, KV_RETV + j * PAIR:KV_RETV + (j + 1) * PAIR]
                loc_f.append(jnp.where(rows == cols, _dot_tn((kf * dec_f).astype(BF16), v), 0.0))
                loc_b.append(jnp.where(rows == cols, _dot_tn((kf * dec_b).astype(BF16), v), 0.0))
            state = st0_ref[0, j]
            for t in range(n_qt):
                sf_ref[t, j] = state.astype(BF16)
                state = state * tile_f + loc_f[t]
            state = st0_ref[1, j]
            for t in reversed(range(n_qt)):
                sb_ref[t, j] = state.astype(BF16)
                state = state * tile_b + loc_b[t]

    q_rows = TQ // GRID_W
    r0 = qt * q_rows
    ks = jnp.clip(r0 - NA_WIN_ROWS // 2, 0, GRID_ROWS - NA_KEY_ROWS)
    k0 = pl.multiple_of(ks * GRID_W, 256)
    key_row = lax.broadcasted_iota(jnp.int32, (1, NA_KEYS), 1) // GRID_W + ks
    row_masks = []
    for a in range(q_rows):
        start = jnp.clip(r0 + a - NA_WIN_ROWS // 2, 0, GRID_ROWS - NA_WIN_ROWS)
        inside = jnp.where(key_row >= start, jnp.where(key_row < start + NA_WIN_ROWS, 1.0, 0.0), 0.0)
        row_masks.append(jnp.where(inside > 0.5, 0.0, NEG_INF))

    def window_bias(h):
        rows = []
        for a in range(q_rows):
            base = ks - r0 - a + (NA_WIN_ROWS - 1) + NA_DR_PAD
            blocks = [bias_ref[h, base + 2 * p] for p in range(NA_KEY_ROWS // 2)]
            rows.append(jnp.concatenate(blocks, axis=1) + row_masks[a])
        return jnp.concatenate(rows, axis=0)

    for j in range(n_na):
        c = j * PAIR
        q = q_ref[:, Q_NA + c:Q_NA + c + PAIR]
        ks_list = [kv_ref[pl.ds(k0, NA_KEYS), KV_NAK + c:KV_NAK + c + PAIR], kc_na_ref[:, c:c + PAIR]]
        vx_list = [vx_na_ref[j, pl.ds(k0, NA_KEYS), :], vxc_na_ref[j]]
        o = [_attend(_keep_group(q, half, g), ks_list, vx_list, window_bias(2 * j + g)) for g in range(2)]
        mix_ref[:, c:c + PAIR] = jnp.where(half == 0, o[0], o[1])

    gm = _group_mean_matrix()
    ii = lax.broadcasted_iota(jnp.int32, (TQ, PAIR), 0).astype(F32)
    for j in range(n_ret):
        c = j * PAIR
        q = q_ref[:, Q_RET + c:Q_RET + c + PAIR]
        k = kv_ref[pl.ds(q0, TQ), KV_RETK + c:KV_RETK + c + PAIR]
        v = kv_ref[pl.ds(q0, TQ), KV_RETV + c:KV_RETV + c + PAIR]
        inner = []
        for g in range(2):
            h = 2 * j + g
            sc = _dot_nt(_keep_group(q, half, g), k) * dm_ref[h * TQ:(h + 1) * TQ, :]
            inner.append(_dot(sc.astype(BF16), v))
        qf = q.astype(F32)
        q_fwd = (qf * jnp.exp2((ii + 1.0) * _pair_scalar(lg_ref, 0, j, half))).astype(BF16)
        q_bwd = (qf * jnp.exp2((TQ - ii) * _pair_scalar(lg_ref, 1, j, half))).astype(BF16)
        o = jnp.where(half == 0, inner[0], inner[1]) + _dot(q_fwd, sf_ref[qt, j]) + _dot(q_bwd, sb_ref[qt, j])
        gate = q_ref[:, Q_GATE + c:Q_GATE + c + PAIR].astype(F32)
        mix_ref[:, MIX_RET + c:MIX_RET + c + PAIR] = (
            (_head_layer_norm(o, gm) * gnw_ref[:, c:c + PAIR] + gnb_ref[:, c:c + PAIR]) * _silu(gate))

    lam = lam_ref[0]
    qr = _rope(q_ref[:, Q_DIFF:Q_DIFF + DIFF_W].astype(F32),
               cos_ref[pl.ds(q0, TQ), :], sin_ref[pl.ds(q0, TQ), :]).astype(BF16)
    for j in range(n_diff):
        c = j * PAIR
        q = qr[:, c:c + PAIR]
        ks_list = [kc_d_ref[:, c:c + PAIR], kr_ref[:, c:c + PAIR]]
        vx_list = [vxc_d_ref[j], vx_d_ref[j]]
        o = [_attend(_keep_group(q, quarter, g), ks_list, vx_list) for g in range(4)]
        blk = jnp.where(half == 0, o[0] - lam * o[1], o[2] - lam * o[3])
        mix_ref[:, MIX_DIFF + c:MIX_DIFF + c + PAIR] = (
            _head_rms_norm(blk, gm) * dnw_ref[:, c:c + PAIR] * one_minus_lam_init)

    _out_proj_post_norm(mix_ref, wout_ref, x_ref, mod_ref, lnw_ref, lnb_ref, x1_ref)


def _lat_mix(q_arr, kv_arr, x2d, mod_l, w_out_bf, log_g, lam, gnw, gnb, dnw, lnw, lnb,
             cache_na_k, cache_na_v, state_ret, cache_diff_k, cache_diff_v, bias_tab, cos_tab,
             sin_tab, layer, lam_init):
    nq = DEC_SEQ // NA_Q_TILE
    const2 = lambda b, t: (0, 0)
    smem = pl.BlockSpec(memory_space=pltpu.SMEM)
    qrow = lambda b, t: (b * nq + t, 0)

    return pl.pallas_call(
        functools.partial(_lat_mix_kernel, one_minus_lam_init=1.0 - lam_init),
        grid=(DEC_BATCH, nq),
        in_specs=[
            smem, smem,
            pl.BlockSpec((NA_Q_TILE, Q_COLS), qrow),
            pl.BlockSpec((DEC_SEQ, KV_COLS), lambda b, t: (b, 0), pipeline_mode=pl.Buffered(1)),
            pl.BlockSpec((NA_Q_TILE, D_MODEL), qrow),
            pl.BlockSpec((None, None, 6, D_MODEL), lambda b, t: (layer, b + 1, 0, 0)),
            pl.BlockSpec((None, MIX_W, D_MODEL), lambda b, t: (layer, 0, 0)),
            pl.BlockSpec((None, 1, RET_W), lambda *_: (layer, 0, 0)),
            pl.BlockSpec((None, 1, RET_W), lambda *_: (layer, 0, 0)),
            pl.BlockSpec((None, 1, DIFF_W), lambda *_: (layer, 0, 0)),
            pl.BlockSpec((None, 1, D_MODEL), lambda *_: (layer, 0, 0)),
            pl.BlockSpec((None, 1, D_MODEL), lambda *_: (layer, 0, 0)),
            pl.BlockSpec((None, None, PAST_LEN, NA_W), lambda b, t: (b, layer, 0, 0),
                         pipeline_mode=pl.Buffered(1)),
            pl.BlockSpec((None, None, PAST_LEN, NA_W), lambda b, t: (b, layer, 0, 0),
                         pipeline_mode=pl.Buffered(1)),
            pl.BlockSpec((None, None, 2, RET_HEADS // 2, PAIR, PAIR),
                         lambda b, t: (b, layer, 0, 0, 0, 0)),
            pl.BlockSpec((None, None, PAST_LEN, DIFF_W), lambda b, t: (b, layer, 0, 0),
                         pipeline_mode=pl.Buffered(1)),
            pl.BlockSpec((None, None, PAST_LEN, DIFF_W), lambda b, t: (b, layer, 0, 0),
                         pipeline_mode=pl.Buffered(1)),
            pl.BlockSpec((None, NA_HEADS, 2 * NA_WIN_ROWS, PAIR), lambda b, t: (layer, 0, 0, 0)),
            pl.BlockSpec((DEC_SEQ, DIFF_W), const2),
            pl.BlockSpec((DEC_SEQ, DIFF_W), const2),
        ],
        out_specs=pl.BlockSpec((NA_Q_TILE, D_MODEL), qrow),
        out_shape=jax.ShapeDtypeStruct((DEC_BATCH * DEC_SEQ, D_MODEL), F32),
        scratch_shapes=[
            pltpu.VMEM((NA_Q_TILE, MIX_W), F32),
            pltpu.VMEM((DEC_SEQ, DIFF_W), BF16),
            pltpu.VMEM((RET_HEADS * NA_Q_TILE, NA_Q_TILE), F32),
            pltpu.VMEM((DEC_SEQ // NA_Q_TILE, RET_HEADS // 2, PAIR, PAIR), BF16),
            pltpu.VMEM((DEC_SEQ // NA_Q_TILE, RET_HEADS // 2, PAIR, PAIR), BF16),
            pltpu.VMEM((PAST_LEN, NA_W), BF16),
            pltpu.VMEM((NA_HEADS // 2, PAST_LEN, 2 * PAIR), BF16),
            pltpu.VMEM((NA_HEADS // 2, DEC_SEQ, 2 * PAIR), BF16),
            pltpu.VMEM((PAST_LEN, DIFF_W), BF16),
            pltpu.VMEM((DIFF_HEADS // 2, PAST_LEN, 2 * PAIR), BF16),
            pltpu.VMEM((DIFF_HEADS // 2, DEC_SEQ, 2 * PAIR), BF16),
            pltpu.VMEM((NA_HEADS, NA_DR_ROWS, GRID_W, PAIR), F32),
        ],
        compiler_params=_cparams(("arbitrary", "arbitrary")),
        name="lat_mix",
    )(log_g, lam, q_arr, kv_arr, x2d, mod_l, w_out_bf, gnw, gnb, dnw, lnw, lnb,
      cache_na_k, cache_na_v, state_ret, cache_diff_k, cache_diff_v, bias_tab, cos_tab, sin_tab)


MOE_PART = 2048
MOE_TILE = 512
MOE_EB = 4
MOE_CH = 48
MOE_STEPS = N_EXPERTS // MOE_EB
MOE_ROUTE_ROWS = 40


def _route_transposed(lt):
    shape = lt.shape
    r = lax.broadcasted_iota(jnp.int32, shape, 0).astype(F32)
    ninf = -jnp.inf
    is_g = jnp.where(r >= N_EXPERTS, jnp.where(r < N_EXPERTS + N_GROUPS, 1.0, 0.0), 0.0) > 0.5
    gl = jnp.where(is_g, lt, ninf)
    gmax = jnp.max(gl, axis=0, keepdims=True)
    gsel = jnp.min(jnp.where(gl == gmax, r - N_EXPERTS, 1e9), axis=0, keepdims=True)
    gsum = jnp.sum(jnp.where(is_g, jnp.exp(gl - gmax), 0.0), axis=0, keepdims=True)
    gw = 1.0 / gsum
    lo = gsel * EXPERTS_PER_GROUP
    is_e = jnp.where(r >= lo, jnp.where(r < lo + EXPERTS_PER_GROUP, 1.0, 0.0), 0.0) > 0.5
    el = jnp.where(is_e, lt, ninf)
    v1 = jnp.max(el, axis=0, keepdims=True)
    i1 = jnp.min(jnp.where(el == v1, r, 1e9), axis=0, keepdims=True)
    el2 = jnp.where(r == i1, ninf, el)
    v2 = jnp.max(el2, axis=0, keepdims=True)
    i2 = jnp.min(jnp.where(el2 == v2, r, 1e9), axis=0, keepdims=True)
    t = jnp.exp(v2 - v1)
    w1 = gw / (1.0 + t)
    w2 = gw * t / (1.0 + t)
    first = r == i1
    second = r == i2
    gates = jnp.where(first, w1, 0.0) + jnp.where(second, w2, 0.0)
    member = jnp.where(first, 1.0, jnp.where(second, 1.0, 0.0))
    return gates, member


def _moe_kernel(x_ref, mod_ref, wr_ref, wg_ref, wu_ref, wd_ref, lnw_ref, lnb_ref, out_ref,
                h_ref, rank_ref, gate_ref, sel_ref, xs_ref, ys_ref):
    s = pl.program_id(1)
    n_tiles = MOE_PART // MOE_TILE

    @pl.when(s == 0)
    def _():
        m = mod_ref[...]
        h = x_ref[...] * (1.0 + m[4:5]) + m[3:4]
        h_hi = h.astype(BF16)
        h_ref[...] = h_hi
        h_lo = (h - h_hi.astype(F32)).astype(BF16)
        w = wr_ref[...]
        w_hi = w.astype(BF16)
        w_lo = (w - w_hi.astype(F32)).astype(BF16)
        lt = _dot_nt(w_hi, h_hi) + (_dot_nt(w_hi, h_lo) + _dot_nt(w_lo, h_hi))
        gates, member = _route_transposed(lt)
        before = jnp.where(lax.broadcasted_iota(jnp.int32, (MOE_TILE, MOE_TILE), 0)
                           < lax.broadcasted_iota(jnp.int32, (MOE_TILE, MOE_TILE), 1), 1.0, 0.0).astype(BF16)
        rank_ref[...] = jnp.full(rank_ref.shape, -1.0, F32)
        gate_ref[...] = jnp.zeros(gate_ref.shape, F32)
        for t in range(n_tiles):
            c0 = t * MOE_TILE
            mem_t = member[0:N_EXPERTS, c0:c0 + MOE_TILE]
            cnt = _dot(mem_t.astype(BF16), before)
            rank = jnp.where(mem_t > 0.5, cnt, -1.0)
            for st in range(MOE_STEPS):
                rank_ref[st, 0:MOE_EB, c0:c0 + MOE_TILE] = rank[st * MOE_EB:(st + 1) * MOE_EB]
                gate_ref[st, 0:MOE_EB, c0:c0 + MOE_TILE] = gates[st * MOE_EB:(st + 1) * MOE_EB, c0:c0 + MOE_TILE]
        out_ref[...] = jnp.zeros(out_ref.shape, F32)

    ranks = rank_ref[s, 0:MOE_EB, :]
    gts = gate_ref[s, 0:MOE_EB, :]
    n_chunks = ((jnp.max(ranks) + 0.5) * (1.0 / MOE_CH)).astype(jnp.int32) + 1

    def chunk_body(k, carry):
        slot = (lax.broadcasted_iota(jnp.int32, (MOE_CH, MOE_TILE), 0) + k * MOE_CH).astype(F32)
        row_gate = [[] for _ in range(MOE_EB)]
        for t in range(n_tiles):
            c0 = t * MOE_TILE
            onehots = []
            for i in range(MOE_EB):
                hit = ranks[i:i + 1, c0:c0 + MOE_TILE] == slot
                onehots.append(jnp.where(hit, 1.0, 0.0).astype(BF16))
                row_gate[i].append(jnp.sum(jnp.where(hit, gts[i:i + 1, c0:c0 + MOE_TILE], 0.0),
                                           axis=1, keepdims=True))
            sel = jnp.concatenate(onehots, axis=0)
            sel_ref[t] = sel
            xs = _dot(sel, h_ref[c0:c0 + MOE_TILE, :]).astype(BF16)
            for i in range(MOE_EB):
                xs_ref[i, t * MOE_CH:(t + 1) * MOE_CH, :] = xs[i * MOE_CH:(i + 1) * MOE_CH]
        for i in range(MOE_EB):
            xi = xs_ref[i]
            a = _dot(xi, wg_ref[i].astype(BF16))
            u = _dot(xi, wu_ref[i].astype(BF16))
            hm = (_silu(a) * u * jnp.concatenate(row_gate[i], axis=0)).astype(BF16)
            ys_ref[i] = _dot(hm, wd_ref[i].astype(BF16)).astype(BF16)
        for t in range(n_tiles):
            c0 = t * MOE_TILE
            y = jnp.concatenate([ys_ref[i, t * MOE_CH:(t + 1) * MOE_CH, :] for i in range(MOE_EB)], axis=0)
            out_ref[c0:c0 + MOE_TILE, :] += _dot_tn(sel_ref[t], y)
        return carry

    lax.fori_loop(0, n_chunks, chunk_body, 0)

    @pl.when(s == MOE_STEPS - 1)
    def _():
        m = mod_ref[...]
        z = ALPHA * x_ref[...] + m[5:6] * out_ref[...]
        out_ref[...] = _layer_norm_rows(z, lnw_ref[...], lnb_ref[...])


def _moe(x2d, mod_l, mod_row_fn, wr_t, wg_bf, wu_bf, wd_bf, lnw, lnb, layer, name):
    n = x2d.shape[0]
    row = lambda p, s: (p, 0)
    wspec = lambda shape: pl.BlockSpec((None, MOE_EB) + shape, lambda p, s: (layer, s, 0, 0))
    return pl.pallas_call(
        _moe_kernel,
        grid=(n // MOE_PART, MOE_STEPS),
        in_specs=[
            pl.BlockSpec((MOE_PART, D_MODEL), row, pipeline_mode=pl.Buffered(1)),
            pl.BlockSpec((None, None, 6, D_MODEL), lambda p, s: (layer, mod_row_fn(p * MOE_PART), 0, 0)),
            pl.BlockSpec((None, MOE_ROUTE_ROWS, D_MODEL), lambda p, s: (layer, 0, 0)),
            wspec((D_MODEL, EXPERT_FF)),
            wspec((D_MODEL, EXPERT_FF)),
            wspec((EXPERT_FF, D_MODEL)),
            pl.BlockSpec((None, 1, D_MODEL), lambda *_: (layer, 0, 0)),
            pl.BlockSpec((None, 1, D_MODEL), lambda *_: (layer, 0, 0)),
        ],
        out_specs=pl.BlockSpec((MOE_PART, D_MODEL), row, pipeline_mode=pl.Buffered(1)),
        out_shape=jax.ShapeDtypeStruct((n, D_MODEL), F32),
        scratch_shapes=[
            pltpu.VMEM((MOE_PART, D_MODEL), BF16),
            pltpu.VMEM((MOE_STEPS, 8, MOE_PART), F32),
            pltpu.VMEM((MOE_STEPS, 8, MOE_PART), F32),
            pltpu.VMEM((MOE_PART // MOE_TILE, MOE_EB * MOE_CH, MOE_TILE), BF16),
            pltpu.VMEM((MOE_EB, MOE_PART // MOE_TILE * MOE_CH, D_MODEL), BF16),
            pltpu.VMEM((MOE_EB, MOE_PART // MOE_TILE * MOE_CH, D_MODEL), BF16),
        ],
        compiler_params=_cparams(("arbitrary", "arbitrary")),
        name=name,
    )(x2d, mod_l, wr_t, wg_bf, wu_bf, wd_bf, lnw, lnb)


def _na_bias_tables(rel_bias):
    pad_c = GRID_W - NA_WIN_COLS
    padded = jnp.pad(rel_bias.astype(F32) * LOG2E, ((0, 0), (0, 0), (0, 0), (pad_c, pad_c)), mode="edge")
    return jnp.pad(padded, ((0, 0), (0, 0), (0, 1), (0, 1)))


def _rope_tables():
    n = DIFF_D // 4
    lane = np.arange(DIFF_W)
    d = lane % DIFF_D
    use_col = d >= DIFF_D // 2
    e = d % (DIFF_D // 2)
    f = e % n
    first = e < n
    t = np.arange(DEC_SEQ)
    pos = np.where(use_col[None, :], (t % GRID_W)[:, None], (t // GRID_W)[:, None]).astype(np.float64)
    freqs = ROPE_BASE ** (-np.arange(n, dtype=np.float64) / n)
    ang = pos * freqs[f][None, :]
    sign = np.where(first, -1.0, 1.0)
    return jnp.asarray(np.cos(ang), F32), jnp.asarray(np.sin(ang) * sign[None, :], F32)


def kernel(x_prompt, x_sample, c, cache_na_k, cache_na_v, state_ret, cache_diff_k, cache_diff_v, c_ctx,
           w_mod, b_mod, w_in, na_rel_bias, ret_decay, ret_gn_w, ret_gn_b, diff_lambda, diff_norm_w,
           w_out, ln1_w, ln1_b, router_group, router_expert, exp_w_gate, exp_w_up, exp_w_down,
           ln2_w, ln2_b):
    n_ctx = BATCH * SEQ
    n_lat = DEC_BATCH * DEC_SEQ
    x_ctx = x_prompt.reshape(n_ctx, D_MODEL)
    x_lat = x_sample.reshape(n_lat, D_MODEL)

    cond_t = jnp.concatenate([c_ctx[:, None], c.T, jnp.zeros((D_MODEL, 8 - 1 - DEC_BATCH), F32)], axis=1)
    mod = _modulation(cond_t, w_mod, b_mod)

    cache_na_k = cache_na_k.reshape(DEC_BATCH, DEPTH, PAST_LEN, NA_W)
    cache_na_v = cache_na_v.reshape(DEC_BATCH, DEPTH, PAST_LEN, NA_W)
    cache_diff_k = cache_diff_k.reshape(DEC_BATCH, DEPTH, PAST_LEN, DIFF_W)
    cache_diff_v = cache_diff_v.reshape(DEC_BATCH, DEPTH, PAST_LEN, DIFF_W)
    cos_tab, sin_tab = _rope_tables()
    st0 = state_ret.astype(F32).reshape(DEC_BATCH, DEPTH, 2, RET_HEADS // 2, 2, HEAD_DIM, 1, HEAD_DIM)
    st0 = (st0 * jnp.eye(2, dtype=F32)[:, None, :, None]).reshape(
        DEC_BATCH, DEPTH, 2, RET_HEADS // 2, PAIR, PAIR)

    ctx_row = lambda r: 0
    lat_row = lambda r: 1 + r // DEC_SEQ

    wg_bf, wu_bf, wd_bf = exp_w_gate, exp_w_up, exp_w_down

    w_in_bf = w_in.astype(BF16)
    w_out_bf = w_out.astype(BF16)
    wr_t = jnp.concatenate([jnp.swapaxes(router_expert, 1, 2), jnp.swapaxes(router_group, 1, 2),
                            jnp.zeros((DEPTH, MOE_ROUTE_ROWS - N_EXPERTS - N_GROUPS, D_MODEL), F32)], axis=1)
    log_g_all = jax.nn.log_sigmoid(ret_decay.astype(F32)) * LOG2E
    lp = diff_lambda.astype(F32)
    lam_dyn = jnp.exp(jnp.sum(lp[:, 0] * lp[:, 1], axis=-1)) - jnp.exp(jnp.sum(lp[:, 2] * lp[:, 3], axis=-1))

    bias_tab = _na_bias_tables(na_rel_bias)
    gnw = ret_gn_w.reshape(DEPTH, 1, RET_W)
    gnb = ret_gn_b.reshape(DEPTH, 1, RET_W)
    dnw = diff_norm_w.reshape(DEPTH, 1, DIFF_W)
    l1w = ln1_w.reshape(DEPTH, 1, D_MODEL)
    l1b = ln1_b.reshape(DEPTH, 1, D_MODEL)
    l2w = ln2_w.reshape(DEPTH, 1, D_MODEL)
    l2b = ln2_b.reshape(DEPTH, 1, D_MODEL)

    caches = None
    states = None
    for l in range(DEPTH):
        lam_init = 0.8 - 0.6 * math.exp(-0.3 * l)
        lam = (lam_dyn[l] + lam_init).reshape(1)
        log_g = log_g_all[l]

        q_c, kv_c, *caches = _project(x_ctx, mod, w_in_bf, ctx_row, l, emit_caches=True, caches=caches)
        x1_c, states = _ctx_mix(q_c, kv_c, x_ctx, mod, w_out_bf, log_g, lam, gnw, gnb, dnw, l1w, l1b,
                                states, l, lam_init)
        x_ctx = _moe(x1_c, mod, ctx_row, wr_t, wg_bf, wu_bf, wd_bf, l2w, l2b, l, "moe_ctx")

        q_l, kv_l = _project(x_lat, mod, w_in_bf, lat_row, l)
        x1_l = _lat_mix(q_l, kv_l, x_lat, mod, w_out_bf, log_g, lam, gnw, gnb, dnw, l1w, l1b,
                        cache_na_k, cache_na_v, st0, cache_diff_k, cache_diff_v,
                        bias_tab, cos_tab, sin_tab, l, lam_init)
        x_lat = _moe(x1_l, mod, lat_row, wr_t, wg_bf, wu_bf, wd_bf, l2w, l2b, l, "moe_lat")

    new_na_k, new_na_v, new_diff_k, new_diff_v = caches
    return (x_ctx.reshape(BATCH, SEQ, D_MODEL), x_lat.reshape(DEC_BATCH, DEC_SEQ, D_MODEL),
            new_na_k.reshape(BATCH, DEPTH, SEQ, NA_HEADS, HEAD_DIM),
            new_na_v.reshape(BATCH, DEPTH, SEQ, NA_HEADS, HEAD_DIM),
            states,
            new_diff_k.reshape(BATCH, DEPTH, SEQ, DIFF_HEADS, 2 * DIFF_D),
            new_diff_v.reshape(BATCH, DEPTH, SEQ, DIFF_HEADS, DIFF_DV))
```

```python
import functools
import math

import numpy as np
import jax
import jax.numpy as jnp
from jax import lax
from jax.experimental import pallas as pl
from jax.experimental.pallas import tpu as pltpu

D_MODEL = 1024
BATCH = 32
SEQ = 256
DEPTH = 2
DEC_BATCH = 2
DEC_SEQ = 2048
PAST_LEN = 512
GRID_W = 64
HEAD_DIM = 64
NA_HEADS = 6
NA_WIN_ROWS = 8
NA_WIN_COLS = 16
RET_HEADS = 6
DIFF_HEADS = 4
DIFF_D = 32
DIFF_DV = 64
NA_W = NA_HEADS * HEAD_DIM
RET_W = RET_HEADS * HEAD_DIM
DIFF_W = DIFF_HEADS * DIFF_DV
MIX_W = NA_W + RET_W + DIFF_W
IN_COLS = 3 * NA_W + 4 * RET_W + 3 * DIFF_W
N_GROUPS = 4
EXPERTS_PER_GROUP = 8
N_EXPERTS = N_GROUPS * EXPERTS_PER_GROUP
EXPERT_FF = 256
ROPE_BASE = 10000.0
LN_EPS = 1e-5
NEG_INF = -1e30
ALPHA = (2.0 * DEPTH) ** 0.25
LOG2E = math.log2(math.e)

F32 = jnp.float32
BF16 = jnp.bfloat16

Q_NA, Q_RET, Q_GATE, Q_DIFF = 0, NA_W, NA_W + RET_W, NA_W + 2 * RET_W
Q_COLS = NA_W + 2 * RET_W + DIFF_W
KV_NAK, KV_NAV = 0, NA_W
KV_RETK, KV_RETV = 2 * NA_W, 2 * NA_W + RET_W
KV_DK, KV_DV = 2 * NA_W + 2 * RET_W, 2 * NA_W + 2 * RET_W + DIFF_W
KV_COLS = 2 * NA_W + 2 * RET_W + 2 * DIFF_W
MIX_RET, MIX_DIFF = NA_W, NA_W + RET_W

NA_Q_TILE = 256
NA_KEY_ROWS = 12
NA_KEYS = NA_KEY_ROWS * GRID_W
CTX_PROBLEMS = NA_HEADS + 2 * DIFF_HEADS + RET_HEADS
GRID_ROWS = DEC_SEQ // GRID_W
PAIR = 2 * HEAD_DIM
NA_DR_PAD = 8
NA_DR_ROWS = 32

VMEM_LIMIT = 60 * 1024 * 1024


def _cparams(sem):
    return pltpu.CompilerParams(dimension_semantics=sem, vmem_limit_bytes=VMEM_LIMIT)


def _dot(a, b):
    return jnp.dot(a, b, preferred_element_type=F32)


def _dot_nt(a, b):
    return lax.dot_general(a, b, (((1,), (1,)), ((), ())), preferred_element_type=F32)


def _dot_tn(a, b):
    return lax.dot_general(a, b, (((0,), (0,)), ((), ())), preferred_element_type=F32)


def _silu(x):
    return x / (1.0 + jnp.exp(-x))


def _layer_norm_rows(z, w, b):
    mu = jnp.mean(z, axis=-1, keepdims=True)
    zc = z - mu
    var = jnp.mean(zc * zc, axis=-1, keepdims=True)
    return zc * lax.rsqrt(var + LN_EPS) * w + b


def _mod_kernel(cond_ref, w_ref, b_ref, o_ref):
    s = _silu(cond_ref[...])
    w = w_ref[...]
    n_cond = 1 + DEC_BATCH
    rows = [jnp.sum(s[:, r:r + 1] * w, axis=0, keepdims=True) for r in range(n_cond)]
    rows.append(jnp.zeros((8 - n_cond, w.shape[1]), F32))
    o_ref[...] = jnp.concatenate(rows, axis=0) + b_ref[...]


def _modulation(cond_t, w_mod, b_mod):
    nj = 6
    out = pl.pallas_call(
        _mod_kernel,
        grid=(DEPTH, nj),
        in_specs=[
            pl.BlockSpec((D_MODEL, 8), lambda l, j: (0, 0)),
            pl.BlockSpec((None, D_MODEL, D_MODEL), lambda l, j: (l, 0, j)),
            pl.BlockSpec((None, 1, D_MODEL), lambda l, j: (l, 0, j)),
        ],
        out_specs=pl.BlockSpec((None, 8, D_MODEL), lambda l, j: (l, 0, j)),
        out_shape=jax.ShapeDtypeStruct((DEPTH, 8, 6 * D_MODEL), F32),
        compiler_params=_cparams(("arbitrary", "arbitrary")),
        name="modulation",
    )(cond_t, w_mod, b_mod.reshape(DEPTH, 1, 6 * D_MODEL))
    return out.reshape(DEPTH, 8, 6, D_MODEL)


def _proj_kernel(x_ref, mod_ref, w_ref, *refs):
    if len(refs) == 10:
        refs = refs[4:]
    q_ref, kv_ref = refs[:2]
    cache_refs = refs[2:]
    m = mod_ref[...]
    h = x_ref[...] * (1.0 + m[1:2]) + m[0:1]
    p = _dot(h.astype(BF16), w_ref[...])
    o = 0
    na_q = p[:, o:o + NA_W] * (HEAD_DIM ** -0.5 * LOG2E); o += NA_W
    na_k = p[:, o:o + NA_W]; o += NA_W
    na_v = p[:, o:o + NA_W]; o += NA_W
    ret_q = p[:, o:o + RET_W]; o += RET_W
    ret_k = p[:, o:o + RET_W] * (HEAD_DIM ** -0.5); o += RET_W
    ret_v = p[:, o:o + RET_W]; o += RET_W
    ret_g = p[:, o:o + RET_W]; o += RET_W
    dq = p[:, o:o + DIFF_W] * (DIFF_D ** -0.5 * LOG2E); o += DIFF_W
    dk = p[:, o:o + DIFF_W]; o += DIFF_W
    dv = p[:, o:o + DIFF_W]
    q_ref[:, Q_NA:Q_NA + NA_W] = na_q.astype(BF16)
    q_ref[:, Q_RET:Q_RET + RET_W] = ret_q.astype(BF16)
    q_ref[:, Q_GATE:Q_GATE + RET_W] = ret_g.astype(BF16)
    q_ref[:, Q_DIFF:Q_DIFF + DIFF_W] = dq.astype(BF16)
    kv_ref[:, KV_NAK:KV_NAK + NA_W] = na_k.astype(BF16)
    kv_ref[:, KV_NAV:KV_NAV + NA_W] = na_v.astype(BF16)
    kv_ref[:, KV_RETK:KV_RETK + RET_W] = ret_k.astype(BF16)
    kv_ref[:, KV_RETV:KV_RETV + RET_W] = ret_v.astype(BF16)
    kv_ref[:, KV_DK:KV_DK + DIFF_W] = dk.astype(BF16)
    kv_ref[:, KV_DV:KV_DV + DIFF_W] = dv.astype(BF16)
    nb = x_ref.shape[0] // SEQ
    for ref, val in zip(cache_refs, (na_k, na_v, dk, dv)):
        val = val.reshape(nb, SEQ, val.shape[-1])
        if len(ref.shape) == 4:
            ref[:, 0] = val
            ref[:, 1:] = jnp.zeros((nb, DEPTH - 1) + val.shape[1:], F32)
        else:
            ref[...] = val


def _project(x2d, mod, w_in_bf, mod_row_fn, layer, emit_caches=False, caches=None, tm=512):
    n = x2d.shape[0]
    row = lambda i: (i, 0)
    out_shape = [jax.ShapeDtypeStruct((n, Q_COLS), BF16), jax.ShapeDtypeStruct((n, KV_COLS), BF16)]
    out_specs = [pl.BlockSpec((tm, Q_COLS), row), pl.BlockSpec((tm, KV_COLS), row)]
    in_specs = [
        pl.BlockSpec((tm, D_MODEL), row),
        pl.BlockSpec((None, None, 6, D_MODEL), lambda i: (layer, mod_row_fn(i * tm), 0, 0)),
        pl.BlockSpec((None, D_MODEL, IN_COLS), lambda i: (layer, 0, 0)),
    ]
    args = [x2d, mod, w_in_bf]
    aliases = {}
    if emit_caches:
        for k, w in enumerate((NA_W, NA_W, DIFF_W, DIFF_W)):
            out_shape.append(jax.ShapeDtypeStruct((BATCH, DEPTH, SEQ, w), F32))
            if caches is None:
                out_specs.append(pl.BlockSpec((tm // SEQ, DEPTH, SEQ, w), lambda i: (i, 0, 0, 0)))
            else:
                in_specs.append(pl.BlockSpec(memory_space=pl.ANY))
                args.append(caches[k])
                aliases[3 + k] = 2 + k
                out_specs.append(pl.BlockSpec((tm // SEQ, None, SEQ, w), lambda i: (i, layer, 0, 0)))
    return pl.pallas_call(
        _proj_kernel,
        grid=(n // tm,),
        in_specs=in_specs,
        out_specs=out_specs,
        out_shape=out_shape,
        input_output_aliases=aliases,
        compiler_params=_cparams(("arbitrary",)),
        name="proj_ctx" if emit_caches else "proj_lat",
    )(*args)


def _lane_group(rows, width):
    return lax.broadcasted_iota(jnp.int32, (rows, PAIR), 1) // width


def _keep_group(x, groups, g):
    return jnp.where(groups == g, x, jnp.zeros_like(x))


def _with_ones(v_pair):
    return jnp.concatenate([v_pair, jnp.ones(v_pair.shape, v_pair.dtype)], axis=1)


def _normalise(r):
    return r[:, :PAIR] * (1.0 / r[:, PAIR:PAIR + 1])


def _attend(qm, k_list, vx_list, bias=None):
    scores = [_dot_nt(qm, k) for k in k_list]
    if bias is not None:
        scores[0] = scores[0] + bias
    m = None
    for s in scores:
        mi = jnp.max(s, axis=-1, keepdims=True)
        m = mi if m is None else jnp.maximum(m, mi)
    r = None
    for s, vx in zip(scores, vx_list):
        ri = _dot(jnp.exp2(s - m).astype(BF16), vx)
        r = ri if r is None else r + ri
    return _normalise(r)


def _group_mean_matrix():
    r = lax.broadcasted_iota(jnp.int32, (PAIR, PAIR), 0) // HEAD_DIM
    c = lax.broadcasted_iota(jnp.int32, (PAIR, PAIR), 1) // HEAD_DIM
    return jnp.where(r == c, 1.0 / HEAD_DIM, 0.0).astype(BF16)


def _group_mean(x, gm):
    hi = x.astype(BF16)
    lo = (x - hi.astype(F32)).astype(BF16)
    return _dot(hi, gm) + _dot(lo, gm)


def _head_layer_norm(o, gm):
    oc = o - _group_mean(o, gm)
    return oc * lax.rsqrt(_group_mean(oc * oc, gm) + LN_EPS)


def _head_rms_norm(o, gm):
    return o * lax.rsqrt(_group_mean(o * o, gm) + LN_EPS)


def _pair_scalar(ref, row, j, groups):
    return jnp.where(groups == 0, ref[row, 2 * j], ref[row, 2 * j + 1])


def _decay_table(lg_ref, dm_ref):
    T = SEQ
    i = lax.broadcasted_iota(jnp.int32, (T, T), 0).astype(F32)
    j = lax.broadcasted_iota(jnp.int32, (T, T), 1).astype(F32)
    d = i - j
    for h in range(RET_HEADS):
        dm_ref[h * T:(h + 1) * T, :] = jnp.exp2(jnp.where(d >= 0, d * lg_ref[0, h], (-d) * lg_ref[1, h]))


def _out_proj_post_norm(mix_ref, wout_ref, x_ref, mod_ref, lnw_ref, lnb_ref, x1_ref):
    y = _dot(mix_ref[...].astype(BF16), wout_ref[...])
    m = mod_ref[...]
    z = ALPHA * x_ref[...] + m[2:3] * y
    x1_ref[...] = _layer_norm_rows(z, lnw_ref[...], lnb_ref[...])


def _ctx_mix_kernel(lg_ref, lam_ref, q_ref, kv_ref, x_ref, mod_ref, wout_ref, gnw_ref, gnb_ref,
                    dnw_ref, lnw_ref, lnb_ref, *rest, one_minus_lam_init):
    if len(rest) == 8:
        rest = rest[1:]
    x1_ref, st_ref, mix_ref, dm_ref, s_ref, e_ref, o_ref = rest
    if len(st_ref.shape) == 5:
        st_ref[1:] = jnp.zeros((DEPTH - 1,) + st_ref.shape[1:], F32)
        st_ref = st_ref.at[0]
    T = SEQ
    n_soft = NA_HEADS + 2 * DIFF_HEADS

    @pl.when(pl.program_id(0) == 0)
    def _():
        _decay_table(lg_ref, dm_ref)

    half = _lane_group(T, HEAD_DIM)
    quarter = _lane_group(T, DIFF_D)
    n_na, n_diff, n_ret = NA_HEADS // 2, DIFF_HEADS // 2, RET_HEADS // 2

    for j in range(n_na):
        q = q_ref[:, Q_NA + j * PAIR:Q_NA + (j + 1) * PAIR]
        k = kv_ref[:, KV_NAK + j * PAIR:KV_NAK + (j + 1) * PAIR]
        for g in range(2):
            r0 = (2 * j + g) * T
            s_ref[r0:r0 + T, :] = _dot_nt(_keep_group(q, half, g), k)
    for j in range(n_diff):
        q = q_ref[:, Q_DIFF + j * PAIR:Q_DIFF + (j + 1) * PAIR]
        k = kv_ref[:, KV_DK + j * PAIR:KV_DK + (j + 1) * PAIR]
        for g in range(4):
            r0 = (NA_HEADS + 4 * j + g) * T
            s_ref[r0:r0 + T, :] = _dot_nt(_keep_group(q, quarter, g), k)
    for j in range(n_ret):
        q = q_ref[:, Q_RET + j * PAIR:Q_RET + (j + 1) * PAIR]
        k = kv_ref[:, KV_RETK + j * PAIR:KV_RETK + (j + 1) * PAIR]
        for g in range(2):
            r0 = (n_soft + 2 * j + g) * T
            s_ref[r0:r0 + T, :] = _dot_nt(_keep_group(q, half, g), k)

    s = s_ref[0:n_soft * T, :]
    e_ref[0:n_soft * T, :] = jnp.exp2(s - jnp.max(s, axis=-1, keepdims=True)).astype(BF16)
    e_ref[n_soft * T:, :] = (s_ref[n_soft * T:, :] * dm_ref[...]).astype(BF16)

    def pv(n, vx):
        return _normalise(_dot(e_ref[n * T:(n + 1) * T, :], vx))

    for j in range(n_na):
        vx = _with_ones(kv_ref[:, KV_NAV + j * PAIR:KV_NAV + (j + 1) * PAIR])
        mix_ref[:, j * PAIR:(j + 1) * PAIR] = jnp.where(half == 0, pv(2 * j, vx), pv(2 * j + 1, vx))
    lam = lam_ref[0]
    for j in range(n_diff):
        vx = _with_ones(kv_ref[:, KV_DV + j * PAIR:KV_DV + (j + 1) * PAIR])
        n0 = NA_HEADS + 4 * j
        head_a = pv(n0, vx) - lam * pv(n0 + 1, vx)
        head_b = pv(n0 + 2, vx) - lam * pv(n0 + 3, vx)
        o_ref[(n_ret + j) * T:(n_ret + j + 1) * T, :] = jnp.where(half == 0, head_a, head_b)
    jj = lax.broadcasted_iota(jnp.int32, (T, PAIR), 0).astype(F32)
    for j in range(n_ret):
        kf = kv_ref[:, KV_RETK + j * PAIR:KV_RETK + (j + 1) * PAIR].astype(F32)
        v = kv_ref[:, KV_RETV + j * PAIR:KV_RETV + (j + 1) * PAIR]
        n0 = n_soft + 2 * j
        o_ref[j * T:(j + 1) * T, :] = jnp.where(half == 0, _dot(e_ref[n0 * T:(n0 + 1) * T, :], v),
                                                _dot(e_ref[(n0 + 1) * T:(n0 + 2) * T, :], v))
        k_fwd = (kf * jnp.exp2((T - 1.0 - jj) * _pair_scalar(lg_ref, 0, j, half))).astype(BF16)
        k_bwd = (kf * jnp.exp2(jj * _pair_scalar(lg_ref, 1, j, half))).astype(BF16)
        for d, kd in enumerate((k_fwd, k_bwd)):
            st = _dot_tn(kd, v)
            st_ref[d, 2 * j] = st[0:HEAD_DIM, 0:HEAD_DIM]
            st_ref[d, 2 * j + 1] = st[HEAD_DIM:PAIR, HEAD_DIM:PAIR]

    gm = _group_mean_matrix()
    rn = _head_layer_norm(o_ref[0:n_ret * T, :], gm)
    dn = _head_rms_norm(o_ref[n_ret * T:, :], gm)
    for j in range(n_ret):
        c = j * PAIR
        g = q_ref[:, Q_GATE + c:Q_GATE + c + PAIR].astype(F32)
        mix_ref[:, MIX_RET + c:MIX_RET + c + PAIR] = (
            (rn[j * T:(j + 1) * T] * gnw_ref[:, c:c + PAIR] + gnb_ref[:, c:c + PAIR]) * _silu(g))
    for j in range(n_diff):
        c = j * PAIR
        mix_ref[:, MIX_DIFF + c:MIX_DIFF + c + PAIR] = (
            dn[j * T:(j + 1) * T] * dnw_ref[:, c:c + PAIR] * one_minus_lam_init)

    _out_proj_post_norm(mix_ref, wout_ref, x_ref, mod_ref, lnw_ref, lnb_ref, x1_ref)


def _ctx_mix(q_arr, kv_arr, x2d, mod, w_out_bf, log_g, lam, gnw, gnb, dnw, lnw, lnb, states, layer,
             lam_init):
    row = lambda b: (b, 0)
    smem = pl.BlockSpec(memory_space=pltpu.SMEM)
    st_shape = (BATCH, DEPTH, 2, RET_HEADS, HEAD_DIM, HEAD_DIM)
    in_specs = [
        smem, smem,
        pl.BlockSpec((SEQ, Q_COLS), row),
        pl.BlockSpec((SEQ, KV_COLS), row),
        pl.BlockSpec((SEQ, D_MODEL), row),
        pl.BlockSpec((None, None, 6, D_MODEL), lambda b: (layer, 0, 0, 0)),
        pl.BlockSpec((None, MIX_W, D_MODEL), lambda b: (layer, 0, 0)),
        pl.BlockSpec((None, 1, RET_W), lambda *_: (layer, 0, 0)),
        pl.BlockSpec((None, 1, RET_W), lambda *_: (layer, 0, 0)),
        pl.BlockSpec((None, 1, DIFF_W), lambda *_: (layer, 0, 0)),
        pl.BlockSpec((None, 1, D_MODEL), lambda *_: (layer, 0, 0)),
        pl.BlockSpec((None, 1, D_MODEL), lambda *_: (layer, 0, 0)),
    ]
    args = [log_g, lam, q_arr, kv_arr, x2d, mod, w_out_bf, gnw, gnb, dnw, lnw, lnb]
    if states is None:
        st_spec = pl.BlockSpec((None,) + st_shape[1:], lambda b: (b, 0, 0, 0, 0, 0))
        aliases = {}
    else:
        in_specs.append(pl.BlockSpec(memory_space=pl.ANY))
        args.append(states)
        st_spec = pl.BlockSpec((None, None) + st_shape[2:], lambda b: (b, layer, 0, 0, 0, 0))
        aliases = {12: 1}
    return pl.pallas_call(
        functools.partial(_ctx_mix_kernel, one_minus_lam_init=1.0 - lam_init),
        grid=(BATCH,),
        in_specs=in_specs,
        out_specs=[pl.BlockSpec((SEQ, D_MODEL), row), st_spec],
        out_shape=[
            jax.ShapeDtypeStruct((BATCH * SEQ, D_MODEL), F32),
            jax.ShapeDtypeStruct(st_shape, F32),
        ],
        input_output_aliases=aliases,
        scratch_shapes=[
            pltpu.VMEM((SEQ, MIX_W), F32),
            pltpu.VMEM((RET_HEADS * SEQ, SEQ), F32),
            pltpu.VMEM((CTX_PROBLEMS * SEQ, SEQ), F32),
            pltpu.VMEM((CTX_PROBLEMS * SEQ, SEQ), BF16),
            pltpu.VMEM(((RET_HEADS + DIFF_HEADS) // 2 * SEQ, PAIR), F32),
        ],
        compiler_params=_cparams(("arbitrary",)),
        name="ctx_mix",
    )(*args)


def _rope(x, cos, sin_signed):
    n, w = x.shape
    lane = lax.broadcasted_iota(jnp.int32, (n, w), 1)
    first = (lane % 16) < 8
    partner = jnp.where(first, pltpu.roll(x, w - 8, 1), pltpu.roll(x, 8, 1))
    return x * cos + partner * sin_signed


def _na_bias_blocks(rb_ref, bias_ref):
    n_dr = 2 * NA_WIN_ROWS - 1
    lane = lax.broadcasted_iota(jnp.int32, (GRID_W, PAIR), 1)
    qc = lax.broadcasted_iota(jnp.int32, (GRID_W, PAIR), 0)
    kc = lane % GRID_W
    start = jnp.clip(qc - NA_WIN_COLS // 2, 0, GRID_W - NA_WIN_COLS)
    col_ok = jnp.where(kc >= start, jnp.where(kc < start + NA_WIN_COLS, 1.0, 0.0), 0.0) > 0.5
    zero = jnp.zeros((GRID_W, PAIR), F32)
    for h in range(NA_HEADS):
        left, right = [], []
        for dr in range(n_dr):
            row = jnp.broadcast_to(rb_ref[h, dr:dr + 1, :], (GRID_W, PAIR))
            left.append(pltpu.roll(row, GRID_W + 1, 1, stride=1, stride_axis=0))
            right.append(pltpu.roll(row, 1, 1, stride=1, stride_axis=0))
        for i in range(NA_DR_ROWS):
            d0 = i - NA_DR_PAD
            a = left[d0] if 0 <= d0 < n_dr else zero
            b = right[d0 + 1] if 0 <= d0 + 1 < n_dr else zero
            bias_ref[h, i] = jnp.where(col_ok, jnp.where(lane < GRID_W, a, b), NEG_INF)


def _lat_mix_kernel(lg_ref, lam_ref, q_ref, kv_ref, x_ref, mod_ref, wout_ref, gnw_ref, gnb_ref,
                    dnw_ref, lnw_ref, lnb_ref, cnak_ref, cnav_ref, st0_ref, cdk_ref, cdv_ref,
                    rb_ref, cos_ref, sin_ref, x1_ref, mix_ref, kr_ref, dm_ref, sf_ref, sb_ref,
                    kc_na_ref, vxc_na_ref, vx_na_ref, kc_d_ref, vxc_d_ref, vx_d_ref, bias_ref, *,
                    one_minus_lam_init):
    TQ = NA_Q_TILE
    T = DEC_SEQ
    n_qt = T // TQ
    n_na, n_diff, n_ret = NA_HEADS // 2, DIFF_HEADS // 2, RET_HEADS // 2
    qt = pl.program_id(1)
    q0 = pl.multiple_of(qt * TQ, TQ)
    half = _lane_group(TQ, HEAD_DIM)
    quarter = _lane_group(TQ, DIFF_D)

    @pl.when(qt == 0)
    def _():
        kr = _rope(kv_ref[:, KV_DK:KV_DK + DIFF_W].astype(F32), cos_ref[...], sin_ref[...])
        kr_ref[...] = kr.astype(BF16)
        kc_na_ref[...] = cnak_ref[...].astype(BF16)
        kc_d_ref[...] = cdk_ref[...].astype(BF16)
        for j in range(n_na):
            vxc_na_ref[j] = _with_ones(cnav_ref[:, j * PAIR:(j + 1) * PAIR].astype(BF16))
            vx_na_ref[j] = _with_ones(kv_ref[:, KV_NAV + j * PAIR:KV_NAV + (j + 1) * PAIR])
        for j in range(n_diff):
            vxc_d_ref[j] = _with_ones(cdv_ref[:, j * PAIR:(j + 1) * PAIR].astype(BF16))
            vx_d_ref[j] = _with_ones(kv_ref[:, KV_DV + j * PAIR:KV_DV + (j + 1) * PAIR])
        _decay_table(lg_ref, dm_ref)
        _na_bias_blocks(rb_ref, bias_ref)
        jl = lax.broadcasted_iota(jnp.int32, (TQ, PAIR), 0).astype(F32)
        rows = lax.broadcasted_iota(jnp.int32, (PAIR, PAIR), 0) // HEAD_DIM
        cols = lax.broadcasted_iota(jnp.int32, (PAIR, PAIR), 1) // HEAD_DIM
        for j in range(n_ret):
            lf = _pair_scalar(lg_ref, 0, j, half)
            lb = _pair_scalar(lg_ref, 1, j, half)
            dec_f = jnp.exp2((TQ - 1.0 - jl) * lf)
            dec_b = jnp.exp2(jl * lb)
            tile_f = jnp.exp2(float(TQ) * _pair_scalar(lg_ref, 0, j, rows))
            tile_b = jnp.exp2(float(TQ) * _pair_scalar(lg_ref, 1, j, rows))
            loc_f, loc_b = [], []
            for t in range(n_qt):
                kf = kv_ref[t * TQ:(t + 1) * TQ, KV_RETK + j * PAIR:KV_RETK + (j + 1) * PAIR].astype(F32)
                v = kv_ref[t * TQ:(t + 1) * TQ, KV_RETV + j * PAIR:KV_RETV + (j + 1) * PAIR]
                loc_f.append(jnp.where(rows == cols, _dot_tn((kf * dec_f).astype(BF16), v), 0.0))
                loc_b.append(jnp.where(rows == cols, _dot_tn((kf * dec_b).astype(BF16), v), 0.0))
            state = st0_ref[0, j]
            for t in range(n_qt):
                sf_ref[t, j] = state.astype(BF16)
                state = state * tile_f + loc_f[t]
            state = st0_ref[1, j]
            for t in reversed(range(n_qt)):
                sb_ref[t, j] = state.astype(BF16)
                state = state * tile_b + loc_b[t]

    q_rows = TQ // GRID_W
    r0 = qt * q_rows
    ks = jnp.clip(r0 - NA_WIN_ROWS // 2, 0, GRID_ROWS - NA_KEY_ROWS)
    k0 = pl.multiple_of(ks * GRID_W, 256)
    key_row = lax.broadcasted_iota(jnp.int32, (1, NA_KEYS), 1) // GRID_W + ks
    row_masks = []
    for a in range(q_rows):
        start = jnp.clip(r0 + a - NA_WIN_ROWS // 2, 0, GRID_ROWS - NA_WIN_ROWS)
        inside = jnp.where(key_row >= start, jnp.where(key_row < start + NA_WIN_ROWS, 1.0, 0.0), 0.0)
        row_masks.append(jnp.where(inside > 0.5, 0.0, NEG_INF))

    def window_bias(h):
        rows = []
        for a in range(q_rows):
            base = ks - r0 - a + (NA_WIN_ROWS - 1) + NA_DR_PAD
            blocks = [bias_ref[h, base + 2 * p] for p in range(NA_KEY_ROWS // 2)]
            rows.append(jnp.concatenate(blocks, axis=1) + row_masks[a])
        return jnp.concatenate(rows, axis=0)

    for j in range(n_na):
        c = j * PAIR
        q = q_ref[:, Q_NA + c:Q_NA + c + PAIR]
        ks_list = [kv_ref[pl.ds(k0, NA_KEYS), KV_NAK + c:KV_NAK + c + PAIR], kc_na_ref[:, c:c + PAIR]]
        vx_list = [vx_na_ref[j, pl.ds(k0, NA_KEYS), :], vxc_na_ref[j]]
        o = [_attend(_keep_group(q, half, g), ks_list, vx_list, window_bias(2 * j + g)) for g in range(2)]
        mix_ref[:, c:c + PAIR] = jnp.where(half == 0, o[0], o[1])

    gm = _group_mean_matrix()
    ii = lax.broadcasted_iota(jnp.int32, (TQ, PAIR), 0).astype(F32)
    for j in range(n_ret):
        c = j * PAIR
        q = q_ref[:, Q_RET + c:Q_RET + c + PAIR]
        k = kv_ref[pl.ds(q0, TQ), KV_RETK + c:KV_RETK + c + PAIR]
        v = kv_ref[pl.ds(q0, TQ), KV_RETV + c:KV_RETV + c + PAIR]
        inner = []
        for g in range(2):
            h = 2 * j + g
            sc = _dot_nt(_keep_group(q, half, g), k) * dm_ref[h * TQ:(h + 1) * TQ, :]
            inner.append(_dot(sc.astype(BF16), v))
        qf = q.astype(F32)
        q_fwd = (qf * jnp.exp2((ii + 1.0) * _pair_scalar(lg_ref, 0, j, half))).astype(BF16)
        q_bwd = (qf * jnp.exp2((TQ - ii) * _pair_scalar(lg_ref, 1, j, half))).astype(BF16)
        o = jnp.where(half == 0, inner[0], inner[1]) + _dot(q_fwd, sf_ref[qt, j]) + _dot(q_bwd, sb_ref[qt, j])
        gate = q_ref[:, Q_GATE + c:Q_GATE + c + PAIR].astype(F32)
        mix_ref[:, MIX_RET + c:MIX_RET + c + PAIR] = (
            (_head_layer_norm(o, gm) * gnw_ref[:, c:c + PAIR] + gnb_ref[:, c:c + PAIR]) * _silu(gate))

    lam = lam_ref[0]
    qr = _rope(q_ref[:, Q_DIFF:Q_DIFF + DIFF_W].astype(F32),
               cos_ref[pl.ds(q0, TQ), :], sin_ref[pl.ds(q0, TQ), :]).astype(BF16)
    for j in range(n_diff):
        c = j * PAIR
        q = qr[:, c:c + PAIR]
        ks_list = [kc_d_ref[:, c:c + PAIR], kr_ref[:, c:c + PAIR]]
        vx_list = [vxc_d_ref[j], vx_d_ref[j]]
        o = [_attend(_keep_group(q, quarter, g), ks_list, vx_list) for g in range(4)]
        blk = jnp.where(half == 0, o[0] - lam * o[1], o[2] - lam * o[3])
        mix_ref[:, MIX_DIFF + c:MIX_DIFF + c + PAIR] = (
            _head_rms_norm(blk, gm) * dnw_ref[:, c:c + PAIR] * one_minus_lam_init)

    _out_proj_post_norm(mix_ref, wout_ref, x_ref, mod_ref, lnw_ref, lnb_ref, x1_ref)


def _lat_mix(q_arr, kv_arr, x2d, mod_l, w_out_bf, log_g, lam, gnw, gnb, dnw, lnw, lnb,
             cache_na_k, cache_na_v, state_ret, cache_diff_k, cache_diff_v, bias_tab, cos_tab,
             sin_tab, layer, lam_init):
    nq = DEC_SEQ // NA_Q_TILE
    const2 = lambda b, t: (0, 0)
    smem = pl.BlockSpec(memory_space=pltpu.SMEM)
    qrow = lambda b, t: (b * nq + t, 0)

    return pl.pallas_call(
        functools.partial(_lat_mix_kernel, one_minus_lam_init=1.0 - lam_init),
        grid=(DEC_BATCH, nq),
        in_specs=[
            smem, smem,
            pl.BlockSpec((NA_Q_TILE, Q_COLS), qrow),
            pl.BlockSpec((DEC_SEQ, KV_COLS), lambda b, t: (b, 0), pipeline_mode=pl.Buffered(1)),
            pl.BlockSpec((NA_Q_TILE, D_MODEL), qrow),
            pl.BlockSpec((None, None, 6, D_MODEL), lambda b, t: (layer, b + 1, 0, 0)),
            pl.BlockSpec((None, MIX_W, D_MODEL), lambda b, t: (layer, 0, 0)),
            pl.BlockSpec((None, 1, RET_W), lambda *_: (layer, 0, 0)),
            pl.BlockSpec((None, 1, RET_W), lambda *_: (layer, 0, 0)),
            pl.BlockSpec((None, 1, DIFF_W), lambda *_: (layer, 0, 0)),
            pl.BlockSpec((None, 1, D_MODEL), lambda *_: (layer, 0, 0)),
            pl.BlockSpec((None, 1, D_MODEL), lambda *_: (layer, 0, 0)),
            pl.BlockSpec((None, None, PAST_LEN, NA_W), lambda b, t: (b, layer, 0, 0),
                         pipeline_mode=pl.Buffered(1)),
            pl.BlockSpec((None, None, PAST_LEN, NA_W), lambda b, t: (b, layer, 0, 0),
                         pipeline_mode=pl.Buffered(1)),
            pl.BlockSpec((None, None, 2, RET_HEADS // 2, PAIR, PAIR),
                         lambda b, t: (b, layer, 0, 0, 0, 0)),
            pl.BlockSpec((None, None, PAST_LEN, DIFF_W), lambda b, t: (b, layer, 0, 0),
                         pipeline_mode=pl.Buffered(1)),
            pl.BlockSpec((None, None, PAST_LEN, DIFF_W), lambda b, t: (b, layer, 0, 0),
                         pipeline_mode=pl.Buffered(1)),
            pl.BlockSpec((None, NA_HEADS, 2 * NA_WIN_ROWS, PAIR), lambda b, t: (layer, 0, 0, 0)),
            pl.BlockSpec((DEC_SEQ, DIFF_W), const2),
            pl.BlockSpec((DEC_SEQ, DIFF_W), const2),
        ],
        out_specs=pl.BlockSpec((NA_Q_TILE, D_MODEL), qrow),
        out_shape=jax.ShapeDtypeStruct((DEC_BATCH * DEC_SEQ, D_MODEL), F32),
        scratch_shapes=[
            pltpu.VMEM((NA_Q_TILE, MIX_W), F32),
            pltpu.VMEM((DEC_SEQ, DIFF_W), BF16),
            pltpu.VMEM((RET_HEADS * NA_Q_TILE, NA_Q_TILE), F32),
            pltpu.VMEM((DEC_SEQ // NA_Q_TILE, RET_HEADS // 2, PAIR, PAIR), BF16),
            pltpu.VMEM((DEC_SEQ // NA_Q_TILE, RET_HEADS // 2, PAIR, PAIR), BF16),
            pltpu.VMEM((PAST_LEN, NA_W), BF16),
            pltpu.VMEM((NA_HEADS // 2, PAST_LEN, 2 * PAIR), BF16),
            pltpu.VMEM((NA_HEADS // 2, DEC_SEQ, 2 * PAIR), BF16),
            pltpu.VMEM((PAST_LEN, DIFF_W), BF16),
            pltpu.VMEM((DIFF_HEADS // 2, PAST_LEN, 2 * PAIR), BF16),
            pltpu.VMEM((DIFF_HEADS // 2, DEC_SEQ, 2 * PAIR), BF16),
            pltpu.VMEM((NA_HEADS, NA_DR_ROWS, GRID_W, PAIR), F32),
        ],
        compiler_params=_cparams(("arbitrary", "arbitrary")),
        name="lat_mix",
    )(log_g, lam, q_arr, kv_arr, x2d, mod_l, w_out_bf, gnw, gnb, dnw, lnw, lnb,
      cache_na_k, cache_na_v, state_ret, cache_diff_k, cache_diff_v, bias_tab, cos_tab, sin_tab)


MOE_PART = 2048
MOE_TILE = 256
MOE_EB = 4
MOE_CH = 32
MOE_STEPS = N_EXPERTS // MOE_EB
MOE_ROUTE_ROWS = 40


def _route_transposed(lt):
    shape = lt.shape
    r = lax.broadcasted_iota(jnp.int32, shape, 0).astype(F32)
    ninf = -jnp.inf
    is_g = jnp.where(r >= N_EXPERTS, jnp.where(r < N_EXPERTS + N_GROUPS, 1.0, 0.0), 0.0) > 0.5
    gl = jnp.where(is_g, lt, ninf)
    gmax = jnp.max(gl, axis=0, keepdims=True)
    gsel = jnp.min(jnp.where(gl == gmax, r - N_EXPERTS, 1e9), axis=0, keepdims=True)
    gsum = jnp.sum(jnp.where(is_g, jnp.exp(gl - gmax), 0.0), axis=0, keepdims=True)
    gw = 1.0 / gsum
    lo = gsel * EXPERTS_PER_GROUP
    is_e = jnp.where(r >= lo, jnp.where(r < lo + EXPERTS_PER_GROUP, 1.0, 0.0), 0.0) > 0.5
    el = jnp.where(is_e, lt, ninf)
    v1 = jnp.max(el, axis=0, keepdims=True)
    i1 = jnp.min(jnp.where(el == v1, r, 1e9), axis=0, keepdims=True)
    el2 = jnp.where(r == i1, ninf, el)
    v2 = jnp.max(el2, axis=0, keepdims=True)
    i2 = jnp.min(jnp.where(el2 == v2, r, 1e9), axis=0, keepdims=True)
    t = jnp.exp(v2 - v1)
    w1 = gw / (1.0 + t)
    w2 = gw * t / (1.0 + t)
    first = r == i1
    second = r == i2
    gates = jnp.where(first, w1, 0.0) + jnp.where(second, w2, 0.0)
    member = jnp.where(first, 1.0, jnp.where(second, 1.0, 0.0))
    return gates, member


def _moe_kernel(x_ref, mod_ref, wr_ref, wg_ref, wu_ref, wd_ref, lnw_ref, lnb_ref, out_ref,
                h_ref, rank_ref, gate_ref, sel_ref, xs_ref, ys_ref):
    s = pl.program_id(1)
    n_tiles = MOE_PART // MOE_TILE

    @pl.when(s == 0)
    def _():
        m = mod_ref[...]
        h = x_ref[...] * (1.0 + m[4:5]) + m[3:4]
        h_hi = h.astype(BF16)
        h_ref[...] = h_hi
        h_lo = (h - h_hi.astype(F32)).astype(BF16)
        w = wr_ref[...]
        w_hi = w.astype(BF16)
        w_lo = (w - w_hi.astype(F32)).astype(BF16)
        lt = _dot_nt(w_hi, h_hi) + (_dot_nt(w_hi, h_lo) + _dot_nt(w_lo, h_hi))
        gates, member = _route_transposed(lt)
        before = jnp.where(lax.broadcasted_iota(jnp.int32, (MOE_TILE, MOE_TILE), 0)
                           < lax.broadcasted_iota(jnp.int32, (MOE_TILE, MOE_TILE), 1), 1.0, 0.0).astype(BF16)
        rank_ref[...] = jnp.full(rank_ref.shape, -1.0, F32)
        gate_ref[...] = jnp.zeros(gate_ref.shape, F32)
        for t in range(n_tiles):
            c0 = t * MOE_TILE
            mem_t = member[0:N_EXPERTS, c0:c0 + MOE_TILE]
            cnt = _dot(mem_t.astype(BF16), before)
            rank = jnp.where(mem_t > 0.5, cnt, -1.0)
            for st in range(MOE_STEPS):
                rank_ref[st, 0:MOE_EB, c0:c0 + MOE_TILE] = rank[st * MOE_EB:(st + 1) * MOE_EB]
                gate_ref[st, 0:MOE_EB, c0:c0 + MOE_TILE] = gates[st * MOE_EB:(st + 1) * MOE_EB, c0:c0 + MOE_TILE]
        out_ref[...] = jnp.zeros(out_ref.shape, F32)

    ranks = rank_ref[s, 0:MOE_EB, :]
    gts = gate_ref[s, 0:MOE_EB, :]
    n_chunks = ((jnp.max(ranks) + 0.5) * (1.0 / MOE_CH)).astype(jnp.int32) + 1

    def chunk_body(k, carry):
        slot = (lax.broadcasted_iota(jnp.int32, (MOE_CH, MOE_TILE), 0) + k * MOE_CH).astype(F32)
        row_gate = [[] for _ in range(MOE_EB)]
        for t in range(n_tiles):
            c0 = t * MOE_TILE
            onehots = []
            for i in range(MOE_EB):
                hit = ranks[i:i + 1, c0:c0 + MOE_TILE] == slot
                onehots.append(jnp.where(hit, 1.0, 0.0).astype(BF16))
                row_gate[i].append(jnp.sum(jnp.where(hit, gts[i:i + 1, c0:c0 + MOE_TILE], 0.0),
                                           axis=1, keepdims=True))
            sel = jnp.concatenate(onehots, axis=0)
            sel_ref[t] = sel
            xs = _dot(sel, h_ref[c0:c0 + MOE_TILE, :]).astype(BF16)
            for i in range(MOE_EB):
                xs_ref[i, t * MOE_CH:(t + 1) * MOE_CH, :] = xs[i * MOE_CH:(i + 1) * MOE_CH]
        for i in range(MOE_EB):
            xi = xs_ref[i]
            a = _dot(xi, wg_ref[i].astype(BF16))
            u = _dot(xi, wu_ref[i].astype(BF16))
            hm = (_silu(a) * u * jnp.concatenate(row_gate[i], axis=0)).astype(BF16)
            ys_ref[i] = _dot(hm, wd_ref[i].astype(BF16)).astype(BF16)
        for t in range(n_tiles):
            c0 = t * MOE_TILE
            y = jnp.concatenate([ys_ref[i, t * MOE_CH:(t + 1) * MOE_CH, :] for i in range(MOE_EB)], axis=0)
            out_ref[c0:c0 + MOE_TILE, :] += _dot_tn(sel_ref[t], y)
        return carry

    lax.fori_loop(0, n_chunks, chunk_body, 0)

    @pl.when(s == MOE_STEPS - 1)
    def _():
        m = mod_ref[...]
        z = ALPHA * x_ref[...] + m[5:6] * out_ref[...]
        out_ref[...] = _layer_norm_rows(z, lnw_ref[...], lnb_ref[...])


def _moe(x2d, mod_l, mod_row_fn, wr_t, wg_bf, wu_bf, wd_bf, lnw, lnb, layer, name):
    n = x2d.shape[0]
    row = lambda p, s: (p, 0)
    wspec = lambda shape: pl.BlockSpec((None, MOE_EB) + shape, lambda p, s: (layer, s, 0, 0))
    return pl.pallas_call(
        _moe_kernel,
        grid=(n // MOE_PART, MOE_STEPS),
        in_specs=[
            pl.BlockSpec((MOE_PART, D_MODEL), row, pipeline_mode=pl.Buffered(1)),
            pl.BlockSpec((None, None, 6, D_MODEL), lambda p, s: (layer, mod_row_fn(p * MOE_PART), 0, 0)),
            pl.BlockSpec((None, MOE_ROUTE_ROWS, D_MODEL), lambda p, s: (layer, 0, 0)),
            wspec((D_MODEL, EXPERT_FF)),
            wspec((D_MODEL, EXPERT_FF)),
            wspec((EXPERT_FF, D_MODEL)),
            pl.BlockSpec((None, 1, D_MODEL), lambda *_: (layer, 0, 0)),
            pl.BlockSpec((None, 1, D_MODEL), lambda *_: (layer, 0, 0)),
        ],
        out_specs=pl.BlockSpec((MOE_PART, D_MODEL), row, pipeline_mode=pl.Buffered(1)),
        out_shape=jax.ShapeDtypeStruct((n, D_MODEL), F32),
        scratch_shapes=[
            pltpu.VMEM((MOE_PART, D_MODEL), BF16),
            pltpu.VMEM((MOE_STEPS, 8, MOE_PART), F32),
            pltpu.VMEM((MOE_STEPS, 8, MOE_PART), F32),
            pltpu.VMEM((MOE_PART // MOE_TILE, MOE_EB * MOE_CH, MOE_TILE), BF16),
            pltpu.VMEM((MOE_EB, MOE_PART // MOE_TILE * MOE_CH, D_MODEL), BF16),
            pltpu.VMEM((MOE_EB, MOE_PART // MOE_TILE * MOE_CH, D_MODEL), BF16),
        ],
        compiler_params=_cparams(("arbitrary", "arbitrary")),
        name=name,
    )(x2d, mod_l, wr_t, wg_bf, wu_bf, wd_bf, lnw, lnb)


def _na_bias_tables(rel_bias):
    pad_c = GRID_W - NA_WIN_COLS
    padded = jnp.pad(rel_bias.astype(F32) * LOG2E, ((0, 0), (0, 0), (0, 0), (pad_c, pad_c)), mode="edge")
    return jnp.pad(padded, ((0, 0), (0, 0), (0, 1), (0, 1)))


def _rope_tables():
    n = DIFF_D // 4
    lane = np.arange(DIFF_W)
    d = lane % DIFF_D
    use_col = d >= DIFF_D // 2
    e = d % (DIFF_D // 2)
    f = e % n
    first = e < n
    t = np.arange(DEC_SEQ)
    pos = np.where(use_col[None, :], (t % GRID_W)[:, None], (t // GRID_W)[:, None]).astype(np.float64)
    freqs = ROPE_BASE ** (-np.arange(n, dtype=np.float64) / n)
    ang = pos * freqs[f][None, :]
    sign = np.where(first, -1.0, 1.0)
    return jnp.asarray(np.cos(ang), F32), jnp.asarray(np.sin(ang) * sign[None, :], F32)


def kernel(x_prompt, x_sample, c, cache_na_k, cache_na_v, state_ret, cache_diff_k, cache_diff_v, c_ctx,
           w_mod, b_mod, w_in, na_rel_bias, ret_decay, ret_gn_w, ret_gn_b, diff_lambda, diff_norm_w,
           w_out, ln1_w, ln1_b, router_group, router_expert, exp_w_gate, exp_w_up, exp_w_down,
           ln2_w, ln2_b):
    n_ctx = BATCH * SEQ
    n_lat = DEC_BATCH * DEC_SEQ
    x_ctx = x_prompt.reshape(n_ctx, D_MODEL)
    x_lat = x_sample.reshape(n_lat, D_MODEL)

    cond_t = jnp.concatenate([c_ctx[:, None], c.T, jnp.zeros((D_MODEL, 8 - 1 - DEC_BATCH), F32)], axis=1)
    mod = _modulation(cond_t, w_mod, b_mod)

    cache_na_k = cache_na_k.reshape(DEC_BATCH, DEPTH, PAST_LEN, NA_W)
    cache_na_v = cache_na_v.reshape(DEC_BATCH, DEPTH, PAST_LEN, NA_W)
    cache_diff_k = cache_diff_k.reshape(DEC_BATCH, DEPTH, PAST_LEN, DIFF_W)
    cache_diff_v = cache_diff_v.reshape(DEC_BATCH, DEPTH, PAST_LEN, DIFF_W)
    cos_tab, sin_tab = _rope_tables()
    st0 = state_ret.astype(F32).reshape(DEC_BATCH, DEPTH, 2, RET_HEADS // 2, 2, HEAD_DIM, 1, HEAD_DIM)
    st0 = (st0 * jnp.eye(2, dtype=F32)[:, None, :, None]).reshape(
        DEC_BATCH, DEPTH, 2, RET_HEADS // 2, PAIR, PAIR)

    ctx_row = lambda r: 0
    lat_row = lambda r: 1 + r // DEC_SEQ

    wg_bf, wu_bf, wd_bf = exp_w_gate, exp_w_up, exp_w_down

    w_in_bf = w_in.astype(BF16)
    w_out_bf = w_out.astype(BF16)
    wr_t = jnp.concatenate([jnp.swapaxes(router_expert, 1, 2), jnp.swapaxes(router_group, 1, 2),
                            jnp.zeros((DEPTH, MOE_ROUTE_ROWS - N_EXPERTS - N_GROUPS, D_MODEL), F32)], axis=1)
    log_g_all = jax.nn.log_sigmoid(ret_decay.astype(F32)) * LOG2E
    lp = diff_lambda.astype(F32)
    lam_dyn = jnp.exp(jnp.sum(lp[:, 0] * lp[:, 1], axis=-1)) - jnp.exp(jnp.sum(lp[:, 2] * lp[:, 3], axis=-1))

    bias_tab = _na_bias_tables(na_rel_bias)
    gnw = ret_gn_w.reshape(DEPTH, 1, RET_W)
    gnb = ret_gn_b.reshape(DEPTH, 1, RET_W)
    dnw = diff_norm_w.reshape(DEPTH, 1, DIFF_W)
    l1w = ln1_w.reshape(DEPTH, 1, D_MODEL)
    l1b = ln1_b.reshape(DEPTH, 1, D_MODEL)
    l2w = ln2_w.reshape(DEPTH, 1, D_MODEL)
    l2b = ln2_b.reshape(DEPTH, 1, D_MODEL)

    caches = None
    states = None
    for l in range(DEPTH):
        lam_init = 0.8 - 0.6 * math.exp(-0.3 * l)
        lam = (lam_dyn[l] + lam_init).reshape(1)
        log_g = log_g_all[l]

        q_c, kv_c, *caches = _project(x_ctx, mod, w_in_bf, ctx_row, l, emit_caches=True, caches=caches)
        x1_c, states = _ctx_mix(q_c, kv_c, x_ctx, mod, w_out_bf, log_g, lam, gnw, gnb, dnw, l1w, l1b,
                                states, l, lam_init)
        x_ctx = _moe(x1_c, mod, ctx_row, wr_t, wg_bf, wu_bf, wd_bf, l2w, l2b, l, "moe_ctx")

        q_l, kv_l = _project(x_lat, mod, w_in_bf, lat_row, l)
        x1_l = _lat_mix(q_l, kv_l, x_lat, mod, w_out_bf, log_g, lam, gnw, gnb, dnw, l1w, l1b,
                        cache_na_k, cache_na_v, st0, cache_diff_k, cache_diff_v,
                        bias_tab, cos_tab, sin_tab, l, lam_init)
        x_lat = _moe(x1_l, mod, lat_row, wr_t, wg_bf, wu_bf, wd_bf, l2w, l2b, l, "moe_lat")

    new_na_k, new_na_v, new_diff_k, new_diff_v = caches
    return (x_ctx.reshape(BATCH, SEQ, D_MODEL), x_lat.reshape(DEC_BATCH, DEC_SEQ, D_MODEL),
            new_na_k.reshape(BATCH, DEPTH, SEQ, NA_HEADS, HEAD_DIM),
            new_na_v.reshape(BATCH, DEPTH, SEQ, NA_HEADS, HEAD_DIM),
            states,
            new_diff_k.reshape(BATCH, DEPTH, SEQ, DIFF_HEADS, 2 * DIFF_D),
            new_diff_v.reshape(BATCH, DEPTH, SEQ, DIFF_HEADS, DIFF_DV))
```

```python
import functools
import math

import numpy as np
import jax
import jax.numpy as jnp
from jax import lax
from jax.experimental import pallas as pl
from jax.experimental.pallas import tpu as pltpu

D_MODEL = 1024
BATCH = 32
SEQ = 256
DEPTH = 2
DEC_BATCH = 2
DEC_SEQ = 2048
PAST_LEN = 512
GRID_W = 64
HEAD_DIM = 64
NA_HEADS = 6
NA_WIN_ROWS = 8
NA_WIN_COLS = 16
RET_HEADS = 6
DIFF_HEADS = 4
DIFF_D = 32
DIFF_DV = 64
NA_W = NA_HEADS * HEAD_DIM
RET_W = RET_HEADS * HEAD_DIM
DIFF_W = DIFF_HEADS * DIFF_DV
MIX_W = NA_W + RET_W + DIFF_W
IN_COLS = 3 * NA_W + 4 * RET_W + 3 * DIFF_W
N_GROUPS = 4
EXPERTS_PER_GROUP = 8
N_EXPERTS = N_GROUPS * EXPERTS_PER_GROUP
EXPERT_FF = 256
ROPE_BASE = 10000.0
LN_EPS = 1e-5
NEG_INF = -1e30
ALPHA = (2.0 * DEPTH) ** 0.25
LOG2E = math.log2(math.e)

F32 = jnp.float32
BF16 = jnp.bfloat16

Q_NA, Q_RET, Q_GATE, Q_DIFF = 0, NA_W, NA_W + RET_W, NA_W + 2 * RET_W
Q_COLS = NA_W + 2 * RET_W + DIFF_W
KV_NAK, KV_NAV = 0, NA_W
KV_RETK, KV_RETV = 2 * NA_W, 2 * NA_W + RET_W
KV_DK, KV_DV = 2 * NA_W + 2 * RET_W, 2 * NA_W + 2 * RET_W + DIFF_W
KV_COLS = 2 * NA_W + 2 * RET_W + 2 * DIFF_W
MIX_RET, MIX_DIFF = NA_W, NA_W + RET_W

NA_Q_TILE = 256
NA_KEY_ROWS = 12
NA_KEYS = NA_KEY_ROWS * GRID_W
CTX_PROBLEMS = NA_HEADS + 2 * DIFF_HEADS + RET_HEADS
GRID_ROWS = DEC_SEQ // GRID_W
CTX_NB = 2
PAIR = 2 * HEAD_DIM
NA_DR_PAD = 8
NA_DR_ROWS = 32

VMEM_LIMIT = 60 * 1024 * 1024


def _cparams(sem):
    return pltpu.CompilerParams(dimension_semantics=sem, vmem_limit_bytes=VMEM_LIMIT)


def _dot(a, b):
    return jnp.dot(a, b, preferred_element_type=F32)


def _dot_nt(a, b):
    return lax.dot_general(a, b, (((1,), (1,)), ((), ())), preferred_element_type=F32)


def _dot_tn(a, b):
    return lax.dot_general(a, b, (((0,), (0,)), ((), ())), preferred_element_type=F32)


def _silu(x):
    return x / (1.0 + jnp.exp(-x))


def _layer_norm_rows(z, w, b):
    mu = jnp.mean(z, axis=-1, keepdims=True)
    zc = z - mu
    var = jnp.mean(zc * zc, axis=-1, keepdims=True)
    return zc * lax.rsqrt(var + LN_EPS) * w + b


def _mod_kernel(cond_ref, w_ref, b_ref, o_ref):
    s = _silu(cond_ref[...])
    w = w_ref[...]
    n_cond = 1 + DEC_BATCH
    rows = [jnp.sum(s[:, r:r + 1] * w, axis=0, keepdims=True) for r in range(n_cond)]
    rows.append(jnp.zeros((8 - n_cond, w.shape[1]), F32))
    o_ref[...] = jnp.concatenate(rows, axis=0) + b_ref[...]


def _modulation(cond_t, w_mod, b_mod):
    nj = 6
    out = pl.pallas_call(
        _mod_kernel,
        grid=(DEPTH, nj),
        in_specs=[
            pl.BlockSpec((D_MODEL, 8), lambda l, j: (0, 0)),
            pl.BlockSpec((None, D_MODEL, D_MODEL), lambda l, j: (l, 0, j)),
            pl.BlockSpec((None, 1, D_MODEL), lambda l, j: (l, 0, j)),
        ],
        out_specs=pl.BlockSpec((None, 8, D_MODEL), lambda l, j: (l, 0, j)),
        out_shape=jax.ShapeDtypeStruct((DEPTH, 8, 6 * D_MODEL), F32),
        compiler_params=_cparams(("arbitrary", "arbitrary")),
        name="modulation",
    )(cond_t, w_mod, b_mod.reshape(DEPTH, 1, 6 * D_MODEL))
    return out.reshape(DEPTH, 8, 6, D_MODEL)


def _proj_kernel(x_ref, mod_ref, w_ref, *refs):
    if len(refs) == 10:
        refs = refs[4:]
    q_ref, kv_ref = refs[:2]
    cache_refs = refs[2:]
    m = mod_ref[...]
    h = x_ref[...] * (1.0 + m[1:2]) + m[0:1]
    p = _dot(h.astype(BF16), w_ref[...])
    o = 0
    na_q = p[:, o:o + NA_W] * (HEAD_DIM ** -0.5 * LOG2E); o += NA_W
    na_k = p[:, o:o + NA_W]; o += NA_W
    na_v = p[:, o:o + NA_W]; o += NA_W
    ret_q = p[:, o:o + RET_W]; o += RET_W
    ret_k = p[:, o:o + RET_W] * (HEAD_DIM ** -0.5); o += RET_W
    ret_v = p[:, o:o + RET_W]; o += RET_W
    ret_g = p[:, o:o + RET_W]; o += RET_W
    dq = p[:, o:o + DIFF_W] * (DIFF_D ** -0.5 * LOG2E); o += DIFF_W
    dk = p[:, o:o + DIFF_W]; o += DIFF_W
    dv = p[:, o:o + DIFF_W]
    q_ref[:, Q_NA:Q_NA + NA_W] = na_q.astype(BF16)
    q_ref[:, Q_RET:Q_RET + RET_W] = ret_q.astype(BF16)
    q_ref[:, Q_GATE:Q_GATE + RET_W] = ret_g.astype(BF16)
    q_ref[:, Q_DIFF:Q_DIFF + DIFF_W] = dq.astype(BF16)
    kv_ref[:, KV_NAK:KV_NAK + NA_W] = na_k.astype(BF16)
    kv_ref[:, KV_NAV:KV_NAV + NA_W] = na_v.astype(BF16)
    kv_ref[:, KV_RETK:KV_RETK + RET_W] = ret_k.astype(BF16)
    kv_ref[:, KV_RETV:KV_RETV + RET_W] = ret_v.astype(BF16)
    kv_ref[:, KV_DK:KV_DK + DIFF_W] = dk.astype(BF16)
    kv_ref[:, KV_DV:KV_DV + DIFF_W] = dv.astype(BF16)
    nb = x_ref.shape[0] // SEQ
    for ref, val in zip(cache_refs, (na_k, na_v, dk, dv)):
        val = val.reshape(nb, SEQ, val.shape[-1])
        if len(ref.shape) == 4:
            ref[:, 0] = val
            ref[:, 1:] = jnp.zeros((nb, DEPTH - 1) + val.shape[1:], F32)
        else:
            ref[...] = val


def _project(x2d, mod, w_in_bf, mod_row_fn, layer, emit_caches=False, caches=None, tm=512):
    n = x2d.shape[0]
    row = lambda i: (i, 0)
    out_shape = [jax.ShapeDtypeStruct((n, Q_COLS), BF16), jax.ShapeDtypeStruct((n, KV_COLS), BF16)]
    out_specs = [pl.BlockSpec((tm, Q_COLS), row), pl.BlockSpec((tm, KV_COLS), row)]
    in_specs = [
        pl.BlockSpec((tm, D_MODEL), row),
        pl.BlockSpec((None, None, 6, D_MODEL), lambda i: (layer, mod_row_fn(i * tm), 0, 0)),
        pl.BlockSpec((None, D_MODEL, IN_COLS), lambda i: (layer, 0, 0)),
    ]
    args = [x2d, mod, w_in_bf]
    aliases = {}
    if emit_caches:
        for k, w in enumerate((NA_W, NA_W, DIFF_W, DIFF_W)):
            out_shape.append(jax.ShapeDtypeStruct((BATCH, DEPTH, SEQ, w), F32))
            if caches is None:
                out_specs.append(pl.BlockSpec((tm // SEQ, DEPTH, SEQ, w), lambda i: (i, 0, 0, 0)))
            else:
                in_specs.append(pl.BlockSpec(memory_space=pl.ANY))
                args.append(caches[k])
                aliases[3 + k] = 2 + k
                out_specs.append(pl.BlockSpec((tm // SEQ, None, SEQ, w), lambda i: (i, layer, 0, 0)))
    return pl.pallas_call(
        _proj_kernel,
        grid=(n // tm,),
        in_specs=in_specs,
        out_specs=out_specs,
        out_shape=out_shape,
        input_output_aliases=aliases,
        compiler_params=_cparams(("arbitrary",)),
        name="proj_ctx" if emit_caches else "proj_lat",
    )(*args)


def _lane_group(rows, width):
    return lax.broadcasted_iota(jnp.int32, (rows, PAIR), 1) // width


def _keep_group(x, groups, g):
    return jnp.where(groups == g, x, jnp.zeros_like(x))


def _with_ones(v_pair):
    return jnp.concatenate([v_pair, jnp.ones(v_pair.shape, v_pair.dtype)], axis=1)


def _normalise(r):
    return r[:, :PAIR] * (1.0 / r[:, PAIR:PAIR + 1])


def _attend(qm, k_list, vx_list, bias=None):
    scores = [_dot_nt(qm, k) for k in k_list]
    if bias is not None:
        scores[0] = scores[0] + bias
    m = None
    for s in scores:
        mi = jnp.max(s, axis=-1, keepdims=True)
        m = mi if m is None else jnp.maximum(m, mi)
    r = None
    for s, vx in zip(scores, vx_list):
        ri = _dot(jnp.exp2(s - m).astype(BF16), vx)
        r = ri if r is None else r + ri
    return _normalise(r)


def _group_mean_matrix():
    r = lax.broadcasted_iota(jnp.int32, (PAIR, PAIR), 0) // HEAD_DIM
    c = lax.broadcasted_iota(jnp.int32, (PAIR, PAIR), 1) // HEAD_DIM
    return jnp.where(r == c, 1.0 / HEAD_DIM, 0.0).astype(BF16)


def _group_mean(x, gm):
    hi = x.astype(BF16)
    lo = (x - hi.astype(F32)).astype(BF16)
    return _dot(hi, gm) + _dot(lo, gm)


def _head_layer_norm(o, gm):
    oc = o - _group_mean(o, gm)
    return oc * lax.rsqrt(_group_mean(oc * oc, gm) + LN_EPS)


def _head_rms_norm(o, gm):
    return o * lax.rsqrt(_group_mean(o * o, gm) + LN_EPS)


def _pair_scalar(ref, row, j, groups):
    return jnp.where(groups == 0, ref[row, 2 * j], ref[row, 2 * j + 1])


def _decay_table(lg_ref, dm_ref):
    T = SEQ
    i = lax.broadcasted_iota(jnp.int32, (T, T), 0).astype(F32)
    j = lax.broadcasted_iota(jnp.int32, (T, T), 1).astype(F32)
    d = i - j
    for h in range(RET_HEADS):
        dm_ref[h * T:(h + 1) * T, :] = jnp.exp2(jnp.where(d >= 0, d * lg_ref[0, h], (-d) * lg_ref[1, h]))


def _out_proj_post_norm(mix_ref, wout_ref, x_ref, mod_ref, lnw_ref, lnb_ref, x1_ref):
    y = _dot(mix_ref[...].astype(BF16), wout_ref[...])
    m = mod_ref[...]
    z = ALPHA * x_ref[...] + m[2:3] * y
    x1_ref[...] = _layer_norm_rows(z, lnw_ref[...], lnb_ref[...])


def _ctx_mix_kernel(lg_ref, lam_ref, q_ref, kv_ref, x_ref, mod_ref, wout_ref, gnw_ref, gnb_ref,
                    dnw_ref, lnw_ref, lnb_ref, *rest, one_minus_lam_init):
    if len(rest) == 8:
        rest = rest[1:]
    x1_ref, st_ref, mix_ref, dm_ref, s_ref, e_ref, o_ref = rest
    NB = CTX_NB
    if len(st_ref.shape) == 6:
        st_ref[:, 1:] = jnp.zeros((NB, DEPTH - 1) + st_ref.shape[2:], F32)
        st_refs = [st_ref.at[b, 0] for b in range(NB)]
    else:
        st_refs = [st_ref.at[b] for b in range(NB)]
    T = SEQ
    n_soft = NA_HEADS + 2 * DIFF_HEADS
    blk = lambda n, b: n * NB + b
    seq = lambda b: slice(b * T, (b + 1) * T)

    @pl.when(pl.program_id(0) == 0)
    def _():
        _decay_table(lg_ref, dm_ref)

    half = _lane_group(T, HEAD_DIM)
    quarter = _lane_group(T, DIFF_D)
    n_na, n_diff, n_ret = NA_HEADS // 2, DIFF_HEADS // 2, RET_HEADS // 2

    def put_scores(n, b, qm, k):
        r0 = blk(n, b) * T
        s_ref[r0:r0 + T, :] = _dot_nt(qm, k)

    for b in range(NB):
        for j in range(n_na):
            q = q_ref[seq(b), Q_NA + j * PAIR:Q_NA + (j + 1) * PAIR]
            k = kv_ref[seq(b), KV_NAK + j * PAIR:KV_NAK + (j + 1) * PAIR]
            for g in range(2):
                put_scores(2 * j + g, b, _keep_group(q, half, g), k)
        for j in range(n_diff):
            q = q_ref[seq(b), Q_DIFF + j * PAIR:Q_DIFF + (j + 1) * PAIR]
            k = kv_ref[seq(b), KV_DK + j * PAIR:KV_DK + (j + 1) * PAIR]
            for g in range(4):
                put_scores(NA_HEADS + 4 * j + g, b, _keep_group(q, quarter, g), k)
        for j in range(n_ret):
            q = q_ref[seq(b), Q_RET + j * PAIR:Q_RET + (j + 1) * PAIR]
            k = kv_ref[seq(b), KV_RETK + j * PAIR:KV_RETK + (j + 1) * PAIR]
            for g in range(2):
                put_scores(n_soft + 2 * j + g, b, _keep_group(q, half, g), k)

    n_sm = n_soft * NB * T
    s = s_ref[0:n_sm, :]
    e_ref[0:n_sm, :] = jnp.exp2(s - jnp.max(s, axis=-1, keepdims=True)).astype(BF16)
    for h in range(RET_HEADS):
        r0 = blk(n_soft + h, 0) * T
        dm = dm_ref[h * T:(h + 1) * T, :]
        for b in range(NB):
            e_ref[r0 + b * T:r0 + (b + 1) * T, :] = (s_ref[r0 + b * T:r0 + (b + 1) * T, :] * dm).astype(BF16)

    def probs(n, b):
        r0 = blk(n, b) * T
        return e_ref[r0:r0 + T, :]

    def pv(n, b, vx):
        return _normalise(_dot(probs(n, b), vx))

    lam = lam_ref[0]
    jj = lax.broadcasted_iota(jnp.int32, (T, PAIR), 0).astype(F32)
    for b in range(NB):
        for j in range(n_na):
            vx = _with_ones(kv_ref[seq(b), KV_NAV + j * PAIR:KV_NAV + (j + 1) * PAIR])
            mix_ref[seq(b), j * PAIR:(j + 1) * PAIR] = jnp.where(half == 0, pv(2 * j, b, vx), pv(2 * j + 1, b, vx))
        for j in range(n_diff):
            vx = _with_ones(kv_ref[seq(b), KV_DV + j * PAIR:KV_DV + (j + 1) * PAIR])
            n0 = NA_HEADS + 4 * j
            head_a = pv(n0, b, vx) - lam * pv(n0 + 1, b, vx)
            head_b = pv(n0 + 2, b, vx) - lam * pv(n0 + 3, b, vx)
            r0 = blk(n_ret + j, b) * T
            o_ref[r0:r0 + T, :] = jnp.where(half == 0, head_a, head_b)
        for j in range(n_ret):
            kf = kv_ref[seq(b), KV_RETK + j * PAIR:KV_RETK + (j + 1) * PAIR].astype(F32)
            v = kv_ref[seq(b), KV_RETV + j * PAIR:KV_RETV + (j + 1) * PAIR]
            n0 = n_soft + 2 * j
            r0 = blk(j, b) * T
            o_ref[r0:r0 + T, :] = jnp.where(half == 0, _dot(probs(n0, b), v), _dot(probs(n0 + 1, b), v))
            k_fwd = (kf * jnp.exp2((T - 1.0 - jj) * _pair_scalar(lg_ref, 0, j, half))).astype(BF16)
            k_bwd = (kf * jnp.exp2(jj * _pair_scalar(lg_ref, 1, j, half))).astype(BF16)
            for d, kd in enumerate((k_fwd, k_bwd)):
                st = _dot_tn(kd, v)
                st_refs[b][d, 2 * j] = st[0:HEAD_DIM, 0:HEAD_DIM]
                st_refs[b][d, 2 * j + 1] = st[HEAD_DIM:PAIR, HEAD_DIM:PAIR]

    gm = _group_mean_matrix()
    n_rn = n_ret * NB * T
    rn = _head_layer_norm(o_ref[0:n_rn, :], gm)
    dn = _head_rms_norm(o_ref[n_rn:, :], gm)
    for b in range(NB):
        for j in range(n_ret):
            c = j * PAIR
            g = q_ref[seq(b), Q_GATE + c:Q_GATE + c + PAIR].astype(F32)
            r0 = blk(j, b) * T
            mix_ref[seq(b), MIX_RET + c:MIX_RET + c + PAIR] = (
                (rn[r0:r0 + T] * gnw_ref[:, c:c + PAIR] + gnb_ref[:, c:c + PAIR]) * _silu(g))
        for j in range(n_diff):
            c = j * PAIR
            r0 = blk(j, b) * T
            mix_ref[seq(b), MIX_DIFF + c:MIX_DIFF + c + PAIR] = (
                dn[r0:r0 + T] * dnw_ref[:, c:c + PAIR] * one_minus_lam_init)

    _out_proj_post_norm(mix_ref, wout_ref, x_ref, mod_ref, lnw_ref, lnb_ref, x1_ref)


def _ctx_mix(q_arr, kv_arr, x2d, mod, w_out_bf, log_g, lam, gnw, gnb, dnw, lnw, lnb, states, layer,
             lam_init):
    row = lambda b: (b, 0)
    smem = pl.BlockSpec(memory_space=pltpu.SMEM)
    st_shape = (BATCH, DEPTH, 2, RET_HEADS, HEAD_DIM, HEAD_DIM)
    in_specs = [
        smem, smem,
        pl.BlockSpec((CTX_NB * SEQ, Q_COLS), row),
        pl.BlockSpec((CTX_NB * SEQ, KV_COLS), row),
        pl.BlockSpec((CTX_NB * SEQ, D_MODEL), row),
        pl.BlockSpec((None, None, 6, D_MODEL), lambda b: (layer, 0, 0, 0)),
        pl.BlockSpec((None, MIX_W, D_MODEL), lambda b: (layer, 0, 0)),
        pl.BlockSpec((None, 1, RET_W), lambda *_: (layer, 0, 0)),
        pl.BlockSpec((None, 1, RET_W), lambda *_: (layer, 0, 0)),
        pl.BlockSpec((None, 1, DIFF_W), lambda *_: (layer, 0, 0)),
        pl.BlockSpec((None, 1, D_MODEL), lambda *_: (layer, 0, 0)),
        pl.BlockSpec((None, 1, D_MODEL), lambda *_: (layer, 0, 0)),
    ]
    args = [log_g, lam, q_arr, kv_arr, x2d, mod, w_out_bf, gnw, gnb, dnw, lnw, lnb]
    if states is None:
        st_spec = pl.BlockSpec((CTX_NB,) + st_shape[1:], lambda b: (b, 0, 0, 0, 0, 0))
        aliases = {}
    else:
        in_specs.append(pl.BlockSpec(memory_space=pl.ANY))
        args.append(states)
        st_spec = pl.BlockSpec((CTX_NB, None) + st_shape[2:], lambda b: (b, layer, 0, 0, 0, 0))
        aliases = {12: 1}
    return pl.pallas_call(
        functools.partial(_ctx_mix_kernel, one_minus_lam_init=1.0 - lam_init),
        grid=(BATCH // CTX_NB,),
        in_specs=in_specs,
        out_specs=[pl.BlockSpec((CTX_NB * SEQ, D_MODEL), row), st_spec],
        out_shape=[
            jax.ShapeDtypeStruct((BATCH * SEQ, D_MODEL), F32),
            jax.ShapeDtypeStruct(st_shape, F32),
        ],
        input_output_aliases=aliases,
        scratch_shapes=[
            pltpu.VMEM((CTX_NB * SEQ, MIX_W), F32),
            pltpu.VMEM((RET_HEADS * SEQ, SEQ), F32),
            pltpu.VMEM((CTX_NB * CTX_PROBLEMS * SEQ, SEQ), F32),
            pltpu.VMEM((CTX_NB * CTX_PROBLEMS * SEQ, SEQ), BF16),
            pltpu.VMEM((CTX_NB * (RET_HEADS + DIFF_HEADS) // 2 * SEQ, PAIR), F32),
        ],
        compiler_params=_cparams(("arbitrary",)),
        name="ctx_mix",
    )(*args)


def _rope(x, cos, sin_signed):
    n, w = x.shape
    lane = lax.broadcasted_iota(jnp.int32, (n, w), 1)
    first = (lane % 16) < 8
    partner = jnp.where(first, pltpu.roll(x, w - 8, 1), pltpu.roll(x, 8, 1))
    return x * cos + partner * sin_signed


def _na_bias_blocks(rb_ref, bias_ref):
    n_dr = 2 * NA_WIN_ROWS - 1
    lane = lax.broadcasted_iota(jnp.int32, (GRID_W, PAIR), 1)
    qc = lax.broadcasted_iota(jnp.int32, (GRID_W, PAIR), 0)
    kc = lane % GRID_W
    start = jnp.clip(qc - NA_WIN_COLS // 2, 0, GRID_W - NA_WIN_COLS)
    col_ok = jnp.where(kc >= start, jnp.where(kc < start + NA_WIN_COLS, 1.0, 0.0), 0.0) > 0.5
    zero = jnp.zeros((GRID_W, PAIR), F32)
    for h in range(NA_HEADS):
        left, right = [], []
        for dr in range(n_dr):
            row = jnp.broadcast_to(rb_ref[h, dr:dr + 1, :], (GRID_W, PAIR))
            left.append(pltpu.roll(row, GRID_W + 1, 1, stride=1, stride_axis=0))
            right.append(pltpu.roll(row, 1, 1, stride=1, stride_axis=0))
        for i in range(NA_DR_ROWS):
            d0 = i - NA_DR_PAD
            a = left[d0] if 0 <= d0 < n_dr else zero
            b = right[d0 + 1] if 0 <= d0 + 1 < n_dr else zero
            bias_ref[h, i] = jnp.where(col_ok, jnp.where(lane < GRID_W, a, b), NEG_INF)


def _lat_mix_kernel(lg_ref, lam_ref, q_ref, kv_ref, x_ref, mod_ref, wout_ref, gnw_ref, gnb_ref,
                    dnw_ref, lnw_ref, lnb_ref, cnak_ref, cnav_ref, st0_ref, cdk_ref, cdv_ref,
                    rb_ref, cos_ref, sin_ref, x1_ref, mix_ref, kr_ref, dm_ref, sf_ref, sb_ref,
                    kc_na_ref, vxc_na_ref, vx_na_ref, kc_d_ref, vxc_d_ref, vx_d_ref, bias_ref, *,
                    one_minus_lam_init):
    TQ = NA_Q_TILE
    T = DEC_SEQ
    n_qt = T // TQ
    n_na, n_diff, n_ret = NA_HEADS // 2, DIFF_HEADS // 2, RET_HEADS // 2
    qt = pl.program_id(1)
    q0 = pl.multiple_of(qt * TQ, TQ)
    half = _lane_group(TQ, HEAD_DIM)
    quarter = _lane_group(TQ, DIFF_D)

    @pl.when(qt == 0)
    def _():
        kr = _rope(kv_ref[:, KV_DK:KV_DK + DIFF_W].astype(F32), cos_ref[...], sin_ref[...])
        kr_ref[...] = kr.astype(BF16)
        kc_na_ref[...] = cnak_ref[...].astype(BF16)
        kc_d_ref[...] = cdk_ref[...].astype(BF16)
        for j in range(n_na):
            vxc_na_ref[j] = _with_ones(cnav_ref[:, j * PAIR:(j + 1) * PAIR].astype(BF16))
            vx_na_ref[j] = _with_ones(kv_ref[:, KV_NAV + j * PAIR:KV_NAV + (j + 1) * PAIR])
        for j in range(n_diff):
            vxc_d_ref[j] = _with_ones(cdv_ref[:, j * PAIR:(j + 1) * PAIR].astype(BF16))
            vx_d_ref[j] = _with_ones(kv_ref[:, KV_DV + j * PAIR:KV_DV + (j + 1) * PAIR])
        _decay_table(lg_ref, dm_ref)
        _na_bias_blocks(rb_ref, bias_ref)
        jl = lax.broadcasted_iota(jnp.int32, (TQ, PAIR), 0).astype(F32)
        rows = lax.broadcasted_iota(jnp.int32, (PAIR, PAIR), 0) // HEAD_DIM
        cols = lax.broadcasted_iota(jnp.int32, (PAIR, PAIR), 1) // HEAD_DIM
        for j in range(n_ret):
            lf = _pair_scalar(lg_ref, 0, j, half)
            lb = _pair_scalar(lg_ref, 1, j, half)
            dec_f = jnp.exp2((TQ - 1.0 - jl) * lf)
            dec_b = jnp.exp2(jl * lb)
            tile_f = jnp.exp2(float(TQ) * _pair_scalar(lg_ref, 0, j, rows))
            tile_b = jnp.exp2(float(TQ) * _pair_scalar(lg_ref, 1, j, rows))
            loc_f, loc_b = [], []
            for t in range(n_qt):
                kf = kv_ref[t * TQ:(t + 1) * TQ, KV_RETK + j * PAIR:KV_RETK + (j + 1) * PAIR].astype(F32)
                v = kv_ref[t * TQ:(t + 1) * TQ, KV_RETV + j * PAIR:KV_RETV + (j + 1) * PAIR]
                loc_f.append(jnp.where(rows == cols, _dot_tn((kf * dec_f).astype(BF16), v), 0.0))
                loc_b.append(jnp.where(rows == cols, _dot_tn((kf * dec_b).astype(BF16), v), 0.0))
            state = st0_ref[0, j]
            for t in range(n_qt):
                sf_ref[t, j] = state.astype(BF16)
                state = state * tile_f + loc_f[t]
            state = st0_ref[1, j]
            for t in reversed(range(n_qt)):
                sb_ref[t, j] = state.astype(BF16)
                state = state * tile_b + loc_b[t]

    q_rows = TQ // GRID_W
    r0 = qt * q_rows
    ks = jnp.clip(r0 - NA_WIN_ROWS // 2, 0, GRID_ROWS - NA_KEY_ROWS)
    k0 = pl.multiple_of(ks * GRID_W, 256)
    key_row = lax.broadcasted_iota(jnp.int32, (1, NA_KEYS), 1) // GRID_W + ks
    row_masks = []
    for a in range(q_rows):
        start = jnp.clip(r0 + a - NA_WIN_ROWS // 2, 0, GRID_ROWS - NA_WIN_ROWS)
        inside = jnp.where(key_row >= start, jnp.where(key_row < start + NA_WIN_ROWS, 1.0, 0.0), 0.0)
        row_masks.append(jnp.where(inside > 0.5, 0.0, NEG_INF))

    def window_bias(h):
        rows = []
        for a in range(q_rows):
            base = ks - r0 - a + (NA_WIN_ROWS - 1) + NA_DR_PAD
            blocks = [bias_ref[h, base + 2 * p] for p in range(NA_KEY_ROWS // 2)]
            rows.append(jnp.concatenate(blocks, axis=1) + row_masks[a])
        return jnp.concatenate(rows, axis=0)

    for j in range(n_na):
        c = j * PAIR
        q = q_ref[:, Q_NA + c:Q_NA + c + PAIR]
        ks_list = [kv_ref[pl.ds(k0, NA_KEYS), KV_NAK + c:KV_NAK + c + PAIR], kc_na_ref[:, c:c + PAIR]]
        vx_list = [vx_na_ref[j, pl.ds(k0, NA_KEYS), :], vxc_na_ref[j]]
        o = [_attend(_keep_group(q, half, g), ks_list, vx_list, window_bias(2 * j + g)) for g in range(2)]
        mix_ref[:, c:c + PAIR] = jnp.where(half == 0, o[0], o[1])

    gm = _group_mean_matrix()
    ii = lax.broadcasted_iota(jnp.int32, (TQ, PAIR), 0).astype(F32)
    for j in range(n_ret):
        c = j * PAIR
        q = q_ref[:, Q_RET + c:Q_RET + c + PAIR]
        k = kv_ref[pl.ds(q0, TQ), KV_RETK + c:KV_RETK + c + PAIR]
        v = kv_ref[pl.ds(q0, TQ), KV_RETV + c:KV_RETV + c + PAIR]
        inner = []
        for g in range(2):
            h = 2 * j + g
            sc = _dot_nt(_keep_group(q, half, g), k) * dm_ref[h * TQ:(h + 1) * TQ, :]
            inner.append(_dot(sc.astype(BF16), v))
        qf = q.astype(F32)
        q_fwd = (qf * jnp.exp2((ii + 1.0) * _pair_scalar(lg_ref, 0, j, half))).astype(BF16)
        q_bwd = (qf * jnp.exp2((TQ - ii) * _pair_scalar(lg_ref, 1, j, half))).astype(BF16)
        o = jnp.where(half == 0, inner[0], inner[1]) + _dot(q_fwd, sf_ref[qt, j]) + _dot(q_bwd, sb_ref[qt, j])
        gate = q_ref[:, Q_GATE + c:Q_GATE + c + PAIR].astype(F32)
        mix_ref[:, MIX_RET + c:MIX_RET + c + PAIR] = (
            (_head_layer_norm(o, gm) * gnw_ref[:, c:c + PAIR] + gnb_ref[:, c:c + PAIR]) * _silu(gate))

    lam = lam_ref[0]
    qr = _rope(q_ref[:, Q_DIFF:Q_DIFF + DIFF_W].astype(F32),
               cos_ref[pl.ds(q0, TQ), :], sin_ref[pl.ds(q0, TQ), :]).astype(BF16)
    for j in range(n_diff):
        c = j * PAIR
        q = qr[:, c:c + PAIR]
        ks_list = [kc_d_ref[:, c:c + PAIR], kr_ref[:, c:c + PAIR]]
        vx_list = [vxc_d_ref[j], vx_d_ref[j]]
        o = [_attend(_keep_group(q, quarter, g), ks_list, vx_list) for g in range(4)]
        blk = jnp.where(half == 0, o[0] - lam * o[1], o[2] - lam * o[3])
        mix_ref[:, MIX_DIFF + c:MIX_DIFF + c + PAIR] = (
            _head_rms_norm(blk, gm) * dnw_ref[:, c:c + PAIR] * one_minus_lam_init)

    _out_proj_post_norm(mix_ref, wout_ref, x_ref, mod_ref, lnw_ref, lnb_ref, x1_ref)


def _lat_mix(q_arr, kv_arr, x2d, mod_l, w_out_bf, log_g, lam, gnw, gnb, dnw, lnw, lnb,
             cache_na_k, cache_na_v, state_ret, cache_diff_k, cache_diff_v, bias_tab, cos_tab,
             sin_tab, layer, lam_init):
    nq = DEC_SEQ // NA_Q_TILE
    const2 = lambda b, t: (0, 0)
    smem = pl.BlockSpec(memory_space=pltpu.SMEM)
    qrow = lambda b, t: (b * nq + t, 0)

    return pl.pallas_call(
        functools.partial(_lat_mix_kernel, one_minus_lam_init=1.0 - lam_init),
        grid=(DEC_BATCH, nq),
        in_specs=[
            smem, smem,
            pl.BlockSpec((NA_Q_TILE, Q_COLS), qrow),
            pl.BlockSpec((DEC_SEQ, KV_COLS), lambda b, t: (b, 0), pipeline_mode=pl.Buffered(1)),
            pl.BlockSpec((NA_Q_TILE, D_MODEL), qrow),
            pl.BlockSpec((None, None, 6, D_MODEL), lambda b, t: (layer, b + 1, 0, 0)),
            pl.BlockSpec((None, MIX_W, D_MODEL), lambda b, t: (layer, 0, 0)),
            pl.BlockSpec((None, 1, RET_W), lambda *_: (layer, 0, 0)),
            pl.BlockSpec((None, 1, RET_W), lambda *_: (layer, 0, 0)),
            pl.BlockSpec((None, 1, DIFF_W), lambda *_: (layer, 0, 0)),
            pl.BlockSpec((None, 1, D_MODEL), lambda *_: (layer, 0, 0)),
            pl.BlockSpec((None, 1, D_MODEL), lambda *_: (layer, 0, 0)),
            pl.BlockSpec((None, None, PAST_LEN, NA_W), lambda b, t: (b, layer, 0, 0),
                         pipeline_mode=pl.Buffered(1)),
            pl.BlockSpec((None, None, PAST_LEN, NA_W), lambda b, t: (b, layer, 0, 0),
                         pipeline_mode=pl.Buffered(1)),
            pl.BlockSpec((None, None, 2, RET_HEADS // 2, PAIR, PAIR),
                         lambda b, t: (b, layer, 0, 0, 0, 0)),
            pl.BlockSpec((None, None, PAST_LEN, DIFF_W), lambda b, t: (b, layer, 0, 0),
                         pipeline_mode=pl.Buffered(1)),
            pl.BlockSpec((None, None, PAST_LEN, DIFF_W), lambda b, t: (b, layer, 0, 0),
                         pipeline_mode=pl.Buffered(1)),
            pl.BlockSpec((None, NA_HEADS, 2 * NA_WIN_ROWS, PAIR), lambda b, t: (layer, 0, 0, 0)),
            pl.BlockSpec((DEC_SEQ, DIFF_W), const2),
            pl.BlockSpec((DEC_SEQ, DIFF_W), const2),
        ],
        out_specs=pl.BlockSpec((NA_Q_TILE, D_MODEL), qrow),
        out_shape=jax.ShapeDtypeStruct((DEC_BATCH * DEC_SEQ, D_MODEL), F32),
        scratch_shapes=[
            pltpu.VMEM((NA_Q_TILE, MIX_W), F32),
            pltpu.VMEM((DEC_SEQ, DIFF_W), BF16),
            pltpu.VMEM((RET_HEADS * NA_Q_TILE, NA_Q_TILE), F32),
            pltpu.VMEM((DEC_SEQ // NA_Q_TILE, RET_HEADS // 2, PAIR, PAIR), BF16),
            pltpu.VMEM((DEC_SEQ // NA_Q_TILE, RET_HEADS // 2, PAIR, PAIR), BF16),
            pltpu.VMEM((PAST_LEN, NA_W), BF16),
            pltpu.VMEM((NA_HEADS // 2, PAST_LEN, 2 * PAIR), BF16),
            pltpu.VMEM((NA_HEADS // 2, DEC_SEQ, 2 * PAIR), BF16),
            pltpu.VMEM((PAST_LEN, DIFF_W), BF16),
            pltpu.VMEM((DIFF_HEADS // 2, PAST_LEN, 2 * PAIR), BF16),
            pltpu.VMEM((DIFF_HEADS // 2, DEC_SEQ, 2 * PAIR), BF16),
            pltpu.VMEM((NA_HEADS, NA_DR_ROWS, GRID_W, PAIR), F32),
        ],
        compiler_params=_cparams(("arbitrary", "arbitrary")),
        name="lat_mix",
    )(log_g, lam, q_arr, kv_arr, x2d, mod_l, w_out_bf, gnw, gnb, dnw, lnw, lnb,
      cache_na_k, cache_na_v, state_ret, cache_diff_k, cache_diff_v, bias_tab, cos_tab, sin_tab)


MOE_PART = 2048
MOE_TILE = 256
MOE_EB = 4
MOE_CH = 32
MOE_STEPS = N_EXPERTS // MOE_EB
MOE_ROUTE_ROWS = 40


def _route_transposed(lt):
    shape = lt.shape
    r = lax.broadcasted_iota(jnp.int32, shape, 0).astype(F32)
    ninf = -jnp.inf
    is_g = jnp.where(r >= N_EXPERTS, jnp.where(r < N_EXPERTS + N_GROUPS, 1.0, 0.0), 0.0) > 0.5
    gl = jnp.where(is_g, lt, ninf)
    gmax = jnp.max(gl, axis=0, keepdims=True)
    gsel = jnp.min(jnp.where(gl == gmax, r - N_EXPERTS, 1e9), axis=0, keepdims=True)
    gsum = jnp.sum(jnp.where(is_g, jnp.exp(gl - gmax), 0.0), axis=0, keepdims=True)
    gw = 1.0 / gsum
    lo = gsel * EXPERTS_PER_GROUP
    is_e = jnp.where(r >= lo, jnp.where(r < lo + EXPERTS_PER_GROUP, 1.0, 0.0), 0.0) > 0.5
    el = jnp.where(is_e, lt, ninf)
    v1 = jnp.max(el, axis=0, keepdims=True)
    i1 = jnp.min(jnp.where(el == v1, r, 1e9), axis=0, keepdims=True)
    el2 = jnp.where(r == i1, ninf, el)
    v2 = jnp.max(el2, axis=0, keepdims=True)
    i2 = jnp.min(jnp.where(el2 == v2, r, 1e9), axis=0, keepdims=True)
    t = jnp.exp(v2 - v1)
    w1 = gw / (1.0 + t)
    w2 = gw * t / (1.0 + t)
    first = r == i1
    second = r == i2
    gates = jnp.where(first, w1, 0.0) + jnp.where(second, w2, 0.0)
    member = jnp.where(first, 1.0, jnp.where(second, 1.0, 0.0))
    return gates, member


def _moe_kernel(x_ref, mod_ref, wr_ref, wg_ref, wu_ref, wd_ref, lnw_ref, lnb_ref, out_ref,
                h_ref, rank_ref, gate_ref, sel_ref, xs_ref, ys_ref):
    s = pl.program_id(1)
    n_tiles = MOE_PART // MOE_TILE

    @pl.when(s == 0)
    def _():
        m = mod_ref[...]
        h = x_ref[...] * (1.0 + m[4:5]) + m[3:4]
        h_hi = h.astype(BF16)
        h_ref[...] = h_hi
        h_lo = (h - h_hi.astype(F32)).astype(BF16)
        w = wr_ref[...]
        w_hi = w.astype(BF16)
        w_lo = (w - w_hi.astype(F32)).astype(BF16)
        lt = _dot_nt(w_hi, h_hi) + (_dot_nt(w_hi, h_lo) + _dot_nt(w_lo, h_hi))
        gates, member = _route_transposed(lt)
        before = jnp.where(lax.broadcasted_iota(jnp.int32, (MOE_TILE, MOE_TILE), 0)
                           < lax.broadcasted_iota(jnp.int32, (MOE_TILE, MOE_TILE), 1), 1.0, 0.0).astype(BF16)
        rank_ref[...] = jnp.full(rank_ref.shape, -1.0, F32)
        gate_ref[...] = jnp.zeros(gate_ref.shape, F32)
        for t in range(n_tiles):
            c0 = t * MOE_TILE
            mem_t = member[0:N_EXPERTS, c0:c0 + MOE_TILE]
            cnt = _dot(mem_t.astype(BF16), before)
            rank = jnp.where(mem_t > 0.5, cnt, -1.0)
            for st in range(MOE_STEPS):
                rank_ref[st, 0:MOE_EB, c0:c0 + MOE_TILE] = rank[st * MOE_EB:(st + 1) * MOE_EB]
                gate_ref[st, 0:MOE_EB, c0:c0 + MOE_TILE] = gates[st * MOE_EB:(st + 1) * MOE_EB, c0:c0 + MOE_TILE]
        out_ref[...] = jnp.zeros(out_ref.shape, F32)

    ranks = rank_ref[s, 0:MOE_EB, :]
    gts = gate_ref[s, 0:MOE_EB, :]
    n_chunks = ((jnp.max(ranks) + 0.5) * (1.0 / MOE_CH)).astype(jnp.int32) + 1

    def chunk_body(k, carry):
        slot = (lax.broadcasted_iota(jnp.int32, (MOE_CH, MOE_TILE), 0) + k * MOE_CH).astype(F32)
        row_gate = [[] for _ in range(MOE_EB)]
        for t in range(n_tiles):
            c0 = t * MOE_TILE
            onehots = []
            for i in range(MOE_EB):
                hit = ranks[i:i + 1, c0:c0 + MOE_TILE] == slot
                onehots.append(jnp.where(hit, 1.0, 0.0).astype(BF16))
                row_gate[i].append(jnp.sum(jnp.where(hit, gts[i:i + 1, c0:c0 + MOE_TILE], 0.0),
                                           axis=1, keepdims=True))
            sel = jnp.concatenate(onehots, axis=0)
            sel_ref[t] = sel
            xs = _dot(sel, h_ref[c0:c0 + MOE_TILE, :]).astype(BF16)
            for i in range(MOE_EB):
                xs_ref[i, t * MOE_CH:(t + 1) * MOE_CH, :] = xs[i * MOE_CH:(i + 1) * MOE_CH]
        for i in range(MOE_EB):
            xi = xs_ref[i]
            a = _dot(xi, wg_ref[i].astype(BF16))
            u = _dot(xi, wu_ref[i].astype(BF16))
            hm = (_silu(a) * u * jnp.concatenate(row_gate[i], axis=0)).astype(BF16)
            ys_ref[i] = _dot(hm, wd_ref[i].astype(BF16)).astype(BF16)
        for t in range(n_tiles):
            c0 = t * MOE_TILE
            y = jnp.concatenate([ys_ref[i, t * MOE_CH:(t + 1) * MOE_CH, :] for i in range(MOE_EB)], axis=0)
            out_ref[c0:c0 + MOE_TILE, :] += _dot_tn(sel_ref[t], y)
        return carry

    lax.fori_loop(0, n_chunks, chunk_body, 0)

    @pl.when(s == MOE_STEPS - 1)
    def _():
        m = mod_ref[...]
        z = ALPHA * x_ref[...] + m[5:6] * out_ref[...]
        out_ref[...] = _layer_norm_rows(z, lnw_ref[...], lnb_ref[...])


def _moe(x2d, mod_l, mod_row_fn, wr_t, wg_bf, wu_bf, wd_bf, lnw, lnb, layer, name):
    n = x2d.shape[0]
    row = lambda p, s: (p, 0)
    wspec = lambda shape: pl.BlockSpec((None, MOE_EB) + shape, lambda p, s: (layer, s, 0, 0))
    return pl.pallas_call(
        _moe_kernel,
        grid=(n // MOE_PART, MOE_STEPS),
        in_specs=[
            pl.BlockSpec((MOE_PART, D_MODEL), row, pipeline_mode=pl.Buffered(1)),
            pl.BlockSpec((None, None, 6, D_MODEL), lambda p, s: (layer, mod_row_fn(p * MOE_PART), 0, 0)),
            pl.BlockSpec((None, MOE_ROUTE_ROWS, D_MODEL), lambda p, s: (layer, 0, 0)),
            wspec((D_MODEL, EXPERT_FF)),
            wspec((D_MODEL, EXPERT_FF)),
            wspec((EXPERT_FF, D_MODEL)),
            pl.BlockSpec((None, 1, D_MODEL), lambda *_: (layer, 0, 0)),
            pl.BlockSpec((None, 1, D_MODEL), lambda *_: (layer, 0, 0)),
        ],
        out_specs=pl.BlockSpec((MOE_PART, D_MODEL), row, pipeline_mode=pl.Buffered(1)),
        out_shape=jax.ShapeDtypeStruct((n, D_MODEL), F32),
        scratch_shapes=[
            pltpu.VMEM((MOE_PART, D_MODEL), BF16),
            pltpu.VMEM((MOE_STEPS, 8, MOE_PART), F32),
            pltpu.VMEM((MOE_STEPS, 8, MOE_PART), F32),
            pltpu.VMEM((MOE_PART // MOE_TILE, MOE_EB * MOE_CH, MOE_TILE), BF16),
            pltpu.VMEM((MOE_EB, MOE_PART // MOE_TILE * MOE_CH, D_MODEL), BF16),
            pltpu.VMEM((MOE_EB, MOE_PART // MOE_TILE * MOE_CH, D_MODEL), BF16),
        ],
        compiler_params=_cparams(("arbitrary", "arbitrary")),
        name=name,
    )(x2d, mod_l, wr_t, wg_bf, wu_bf, wd_bf, lnw, lnb)


def _na_bias_tables(rel_bias):
    pad_c = GRID_W - NA_WIN_COLS
    padded = jnp.pad(rel_bias.astype(F32) * LOG2E, ((0, 0), (0, 0), (0, 0), (pad_c, pad_c)), mode="edge")
    return jnp.pad(padded, ((0, 0), (0, 0), (0, 1), (0, 1)))


def _rope_tables():
    n = DIFF_D // 4
    lane = np.arange(DIFF_W)
    d = lane % DIFF_D
    use_col = d >= DIFF_D // 2
    e = d % (DIFF_D // 2)
    f = e % n
    first = e < n
    t = np.arange(DEC_SEQ)
    pos = np.where(use_col[None, :], (t % GRID_W)[:, None], (t // GRID_W)[:, None]).astype(np.float64)
    freqs = ROPE_BASE ** (-np.arange(n, dtype=np.float64) / n)
    ang = pos * freqs[f][None, :]
    sign = np.where(first, -1.0, 1.0)
    return jnp.asarray(np.cos(ang), F32), jnp.asarray(np.sin(ang) * sign[None, :], F32)


def kernel(x_prompt, x_sample, c, cache_na_k, cache_na_v, state_ret, cache_diff_k, cache_diff_v, c_ctx,
           w_mod, b_mod, w_in, na_rel_bias, ret_decay, ret_gn_w, ret_gn_b, diff_lambda, diff_norm_w,
           w_out, ln1_w, ln1_b, router_group, router_expert, exp_w_gate, exp_w_up, exp_w_down,
           ln2_w, ln2_b):
    n_ctx = BATCH * SEQ
    n_lat = DEC_BATCH * DEC_SEQ
    x_ctx = x_prompt.reshape(n_ctx, D_MODEL)
    x_lat = x_sample.reshape(n_lat, D_MODEL)

    cond_t = jnp.concatenate([c_ctx[:, None], c.T, jnp.zeros((D_MODEL, 8 - 1 - DEC_BATCH), F32)], axis=1)
    mod = _modulation(cond_t, w_mod, b_mod)

    cache_na_k = cache_na_k.reshape(DEC_BATCH, DEPTH, PAST_LEN, NA_W)
    cache_na_v = cache_na_v.reshape(DEC_BATCH, DEPTH, PAST_LEN, NA_W)
    cache_diff_k = cache_diff_k.reshape(DEC_BATCH, DEPTH, PAST_LEN, DIFF_W)
    cache_diff_v = cache_diff_v.reshape(DEC_BATCH, DEPTH, PAST_LEN, DIFF_W)
    cos_tab, sin_tab = _rope_tables()
    st0 = state_ret.astype(F32).reshape(DEC_BATCH, DEPTH, 2, RET_HEADS // 2, 2, HEAD_DIM, 1, HEAD_DIM)
    st0 = (st0 * jnp.eye(2, dtype=F32)[:, None, :, None]).reshape(
        DEC_BATCH, DEPTH, 2, RET_HEADS // 2, PAIR, PAIR)

    ctx_row = lambda r: 0
    lat_row = lambda r: 1 + r // DEC_SEQ

    wg_bf, wu_bf, wd_bf = exp_w_gate, exp_w_up, exp_w_down

    w_in_bf = w_in.astype(BF16)
    w_out_bf = w_out.astype(BF16)
    wr_t = jnp.concatenate([jnp.swapaxes(router_expert, 1, 2), jnp.swapaxes(router_group, 1, 2),
                            jnp.zeros((DEPTH, MOE_ROUTE_ROWS - N_EXPERTS - N_GROUPS, D_MODEL), F32)], axis=1)
    log_g_all = jax.nn.log_sigmoid(ret_decay.astype(F32)) * LOG2E
    lp = diff_lambda.astype(F32)
    lam_dyn = jnp.exp(jnp.sum(lp[:, 0] * lp[:, 1], axis=-1)) - jnp.exp(jnp.sum(lp[:, 2] * lp[:, 3], axis=-1))

    bias_tab = _na_bias_tables(na_rel_bias)
    gnw = ret_gn_w.reshape(DEPTH, 1, RET_W)
    gnb = ret_gn_b.reshape(DEPTH, 1, RET_W)
    dnw = diff_norm_w.reshape(DEPTH, 1, DIFF_W)
    l1w = ln1_w.reshape(DEPTH, 1, D_MODEL)
    l1b = ln1_b.reshape(DEPTH, 1, D_MODEL)
    l2w = ln2_w.reshape(DEPTH, 1, D_MODEL)
    l2b = ln2_b.reshape(DEPTH, 1, D_MODEL)

    caches = None
    states = None
    for l in range(DEPTH):
        lam_init = 0.8 - 0.6 * math.exp(-0.3 * l)
        lam = (lam_dyn[l] + lam_init).reshape(1)
        log_g = log_g_all[l]

        q_c, kv_c, *caches = _project(x_ctx, mod, w_in_bf, ctx_row, l, emit_caches=True, caches=caches)
        x1_c, states = _ctx_mix(q_c, kv_c, x_ctx, mod, w_out_bf, log_g, lam, gnw, gnb, dnw, l1w, l1b,
                                states, l, lam_init)
        x_ctx = _moe(x1_c, mod, ctx_row, wr_t, wg_bf, wu_bf, wd_bf, l2w, l2b, l, "moe_ctx")

        q_l, kv_l = _project(x_lat, mod, w_in_bf, lat_row, l)
        x1_l = _lat_mix(q_l, kv_l, x_lat, mod, w_out_bf, log_g, lam, gnw, gnb, dnw, l1w, l1b,
                        cache_na_k, cache_na_v, st0, cache_diff_k, cache_diff_v,
                        bias_tab, cos_tab, sin_tab, l, lam_init)
        x_lat = _moe(x1_l, mod, lat_row, wr_t, wg_bf, wu_bf, wd_bf, l2w, l2b, l, "moe_lat")

    new_na_k, new_na_v, new_diff_k, new_diff_v = caches
    return (x_ctx.reshape(BATCH, SEQ, D_MODEL), x_lat.reshape(DEC_BATCH, DEC_SEQ, D_MODEL),
            new_na_k.reshape(BATCH, DEPTH, SEQ, NA_HEADS, HEAD_DIM),
            new_na_v.reshape(BATCH, DEPTH, SEQ, NA_HEADS, HEAD_DIM),
            states,
            new_diff_k.reshape(BATCH, DEPTH, SEQ, DIFF_HEADS, 2 * DIFF_D),
            new_diff_v.reshape(BATCH, DEPTH, SEQ, DIFF_HEADS, DIFF_DV))
```
